```python
import math
import jax, jax.numpy as jnp
from jax import lax
import numpy as np

D_MODEL = 1024
BATCH = 8
SEQ = 2048
DEPTH = 2

D_MIX = D_MODEL
GROUP_W = D_MIX // 4
CHUNK = 128

GM_HEADS = 4
GM_HEAD_DIM = GROUP_W // GM_HEADS

SSM_D_INNER = GROUP_W
SSM_HEAD_DIM = 64
SSM_HEADS = SSM_D_INNER // SSM_HEAD_DIM
SSM_GROUPS = 2
SSM_D_STATE = 128
SSM_CONV_K = 4
SSM_CONV_DIM = SSM_D_INNER + 2 * SSM_GROUPS * SSM_D_STATE
SSM_CHUNK = CHUNK

POOL_WINDOWS = (2, 4, 8, 16)
POOL_GROUPS = len(POOL_WINDOWS)
POOL_GROUP_DIM = GROUP_W // POOL_GROUPS

DA_HEADS = 4
DA_V_DIM = GROUP_W // DA_HEADS
DA_QK_DIM = DA_V_DIM // 2
Q_BLOCK = 128

D_FF = -(-8 * D_MODEL // (3 * 256)) * 256

GM_IN = 2 * GROUP_W
SSM_IN = SSM_D_INNER + SSM_CONV_DIM + SSM_HEADS
POOL_IN = GROUP_W
DA_IN = 3 * GROUP_W
D_IN = GM_IN + SSM_IN + POOL_IN + DA_IN
IN_SPLITS = (GM_IN, GM_IN + SSM_IN, GM_IN + SSM_IN + POOL_IN)

RMS_EPS = 1e-6

kernel_name = 'hybrid_parallel_heads_gmlp_ssd_pool_diffattn'


def rms_norm(x, w):
    xf = x.astype(jnp.float32)
    y = xf * lax.rsqrt(jnp.mean(xf * xf, axis=-1, keepdims=True) + RMS_EPS)
    return (y * w.astype(jnp.float32)).astype(x.dtype)


def gmlp_mixer(h, norm_w, ws, bs):
    b, L, _ = h.shape
    h = jax.nn.gelu(h)
    u, v = jnp.split(h, 2, axis=-1)
    v = rms_norm(v.reshape(b, L, GM_HEADS, GM_HEAD_DIM), norm_w)
    v = v.reshape(b, L // CHUNK, CHUNK, GM_HEADS, GM_HEAD_DIM)
    causal = jnp.tril(jnp.ones((CHUNK, CHUNK), dtype=bool))
    ws = jnp.where(causal[None], ws, jnp.zeros_like(ws))
    s = jnp.einsum('hts,bcshd->bcthd', ws, v) + bs.T[None, None, :, :, None]
    return u * s.reshape(b, L, GROUP_W)


def ssd_scan(x, dt, A, B, C):
    b, L, H, P = x.shape
    N = B.shape[-1]
    nc = L // SSM_CHUNK
    rep = H // SSM_GROUPS
    B = jnp.repeat(B, rep, axis=2).reshape(b, nc, SSM_CHUNK, H, N)
    C = jnp.repeat(C, rep, axis=2).reshape(b, nc, SSM_CHUNK, H, N)
    xdt = (x * dt[..., None]).reshape(b, nc, SSM_CHUNK, H, P)
    a_cs = jnp.cumsum((dt * A).reshape(b, nc, SSM_CHUNK, H), axis=2)
    causal = jnp.tril(jnp.ones((SSM_CHUNK, SSM_CHUNK), dtype=bool))[None, None, :, :, None]
    seg = a_cs[:, :, :, None, :] - a_cs[:, :, None, :, :]
    decay = jnp.exp(jnp.where(causal, seg, -jnp.inf))
    cb = jnp.einsum('bclhn,bcshn->bclsh', C, B)
    y_diag = jnp.einsum('bclsh,bcshp->bclhp', cb * decay, xdt)
    decay_to_end = jnp.exp(a_cs[:, :, -1:, :] - a_cs)
    states = jnp.einsum('bclhn,bclh,bclhp->bchpn', B, decay_to_end, xdt)
    chunk_decay = jnp.exp(a_cs[:, :, -1, :])

    def step(carry, inp):
        st, dec = inp
        return carry * dec[:, :, None, None] + st, carry

    init = jnp.zeros((b, H, P, N), dtype=states.dtype)
    _, prev_states = lax.scan(step, init, (jnp.moveaxis(states, 1, 0), jnp.moveaxis(chunk_decay, 1, 0)))
    prev_states = jnp.moveaxis(prev_states, 0, 1)
    y_off = jnp.einsum('bclhn,bchpn,bclh->bclhp', C, prev_states, jnp.exp(a_cs))
    return (y_diag + y_off).reshape(b, L, H, P)


def ssd_mixer(h, conv_w, conv_b, dt_bias, a_log, d_skip, norm_w):
    b, L, _ = h.shape
    z, xbc, dt = jnp.split(h, [SSM_D_INNER, SSM_D_INNER + SSM_CONV_DIM], axis=-1)
    xbc = lax.conv_general_dilated(
        xbc, conv_w.T[:, None, :], window_strides=(1,), padding=[(SSM_CONV_K - 1, 0)],
        dimension_numbers=('NWC', 'WIO', 'NWC'), feature_group_count=SSM_CONV_DIM)
    xbc = jax.nn.silu(xbc + conv_b)
    xs, Bm, Cm = jnp.split(xbc, [SSM_D_INNER, SSM_D_INNER + SSM_GROUPS * SSM_D_STATE], axis=-1)
    xs = xs.reshape(b, L, SSM_HEADS, SSM_HEAD_DIM).astype(jnp.float32)
    Bm = Bm.reshape(b, L, SSM_GROUPS, SSM_D_STATE).astype(jnp.float32)
    Cm = Cm.reshape(b, L, SSM_GROUPS, SSM_D_STATE).astype(jnp.float32)
    dt = jax.nn.softplus(dt.astype(jnp.float32) + dt_bias.astype(jnp.float32))
    A = -jnp.exp(a_log.astype(jnp.float32))
    y = ssd_scan(xs, dt, A, Bm, Cm) + d_skip.astype(jnp.float32)[:, None] * xs
    y = y.reshape(b, L, SSM_D_INNER) * jax.nn.silu(z.astype(jnp.float32))
    y = rms_norm(y.reshape(b, L, SSM_GROUPS, SSM_D_INNER // SSM_GROUPS),
                 norm_w.reshape(SSM_GROUPS, SSM_D_INNER // SSM_GROUPS))
    return y.reshape(b, L, SSM_D_INNER)


def pool_mixer(h, w, scale):
    b, L, _ = h.shape
    hf = h.astype(jnp.float32).reshape(b, L, POOL_GROUPS, POOL_GROUP_DIM)
    cs = jnp.concatenate([jnp.zeros((b, 1, POOL_GROUPS, POOL_GROUP_DIM), jnp.float32),
                          jnp.cumsum(hf, axis=1)], axis=1)
    hi = jnp.arange(1, L + 1)
    outs = []
    for g, win in enumerate(POOL_WINDOWS):
        lo = jnp.maximum(hi - win, 0)
        cs_g = cs[:, :, g]
        window_sum = cs_g[:, 1:] - jnp.take(cs_g, lo, axis=1)
        count = (hi - lo).astype(jnp.float32)[None, :, None]
        outs.append(window_sum / count - hf[:, :, g])
    p = jnp.stack(outs, axis=2)
    y = jnp.einsum('blgd,gde->blge', p, w.astype(jnp.float32)) * scale.astype(jnp.float32).reshape(POOL_GROUPS, POOL_GROUP_DIM)
    return y.reshape(b, L, GROUP_W)


def diff_attn_mixer(h, q_norm_w, k_norm_w, lq1, lk1, lq2, lk2, subln_w, layer_idx):
    b, L, _ = h.shape
    q, k, v = jnp.split(h, 3, axis=-1)
    q = rms_norm(q.reshape(b, L, DA_HEADS, 2, DA_QK_DIM), q_norm_w)
    k = rms_norm(k.reshape(b, L, DA_HEADS, 2, DA_QK_DIM), k_norm_w)
    v = v.reshape(b, L, DA_HEADS, DA_V_DIM)
    lam_init = 0.8 - 0.6 * math.exp(-0.3 * layer_idx)
    lam = (jnp.exp(jnp.sum(lq1.astype(jnp.float32) * lk1.astype(jnp.float32)))
           - jnp.exp(jnp.sum(lq2.astype(jnp.float32) * lk2.astype(jnp.float32))) + lam_init)
    slopes = jnp.exp2(-8.0 * jnp.arange(1, DA_HEADS + 1, dtype=jnp.float32) / DA_HEADS)
    kpos = jnp.arange(L)
    nb = L // Q_BLOCK
    q_blocks = jnp.moveaxis(q.reshape(b, nb, Q_BLOCK, DA_HEADS, 2, DA_QK_DIM), 1, 0)
    starts = jnp.arange(nb) * Q_BLOCK
    sm_scale = DA_QK_DIM ** -0.5

    def block(args):
        qi, start = args
        s = jnp.einsum('bqhcd,bkhcd->bhcqk', qi, k).astype(jnp.float32) * sm_scale
        dist = (start + jnp.arange(Q_BLOCK))[:, None] - kpos[None, :]
        bias = jnp.where(dist >= 0, -slopes[:, None, None] * dist.astype(jnp.float32), -jnp.inf)
        p = jax.nn.softmax(s + bias[None, :, None], axis=-1)
        a = p[:, :, 0] - lam * p[:, :, 1]
        return jnp.einsum('bhqk,bkhd->bqhd', a.astype(v.dtype), v)

    o = lax.map(block, (q_blocks, starts))
    o = jnp.moveaxis(o, 0, 1).reshape(b, L, DA_HEADS, DA_V_DIM)
    o = rms_norm(o, subln_w) * (1.0 - lam_init)
    return o.reshape(b, L, GROUP_W)


def setup_inputs(seed: int = 0) -> dict:
    key = jax.random.key(seed)
    ks = jax.random.split(key, 32)
    f32 = jnp.float32
    nrm = lambda k, shape, s: jax.random.normal(k, shape, f32) * s
    dt0 = jnp.exp(jax.random.uniform(ks[7], (DEPTH, SSM_HEADS), f32, math.log(1e-3), math.log(1e-1)))
    return {
        'x': jax.random.normal(ks[0], (BATCH, SEQ, D_MODEL), f32),
        'norm1_w': 1.0 + nrm(ks[1], (DEPTH, D_MODEL), 0.05),
        'w_in': nrm(ks[2], (DEPTH, D_MODEL, D_IN), D_MODEL ** -0.5),
        'gm_norm_w': 1.0 + nrm(ks[3], (DEPTH, GM_HEADS, GM_HEAD_DIM), 0.05),
        'gm_ws': nrm(ks[4], (DEPTH, GM_HEADS, CHUNK, CHUNK), CHUNK ** -0.5),
        'gm_bs': 1.0 + nrm(ks[5], (DEPTH, GM_HEADS, CHUNK), 0.05),
        'ssm_conv_w': nrm(ks[6], (DEPTH, SSM_CONV_DIM, SSM_CONV_K), SSM_CONV_K ** -0.5),
        'ssm_conv_b': nrm(ks[8], (DEPTH, SSM_CONV_DIM), 0.02),
        'ssm_dt_bias': dt0 + jnp.log(-jnp.expm1(-dt0)),
        'ssm_a_log': jnp.log(jax.random.uniform(ks[9], (DEPTH, SSM_HEADS), f32, 1.0, 16.0)),
        'ssm_d': 1.0 + nrm(ks[10], (DEPTH, SSM_HEADS), 0.1),
        'ssm_norm_w': 1.0 + nrm(ks[11], (DEPTH, SSM_D_INNER), 0.05),
        'pool_w': nrm(ks[12], (DEPTH, POOL_GROUPS, POOL_GROUP_DIM, POOL_GROUP_DIM), POOL_GROUP_DIM ** -0.5),
        'pool_scale': 1.0 + nrm(ks[13], (DEPTH, GROUP_W), 0.1),
        'da_q_norm_w': 1.0 + nrm(ks[14], (DEPTH, DA_QK_DIM), 0.05),
        'da_k_norm_w': 1.0 + nrm(ks[15], (DEPTH, DA_QK_DIM), 0.05),
        'da_lambda_q1': nrm(ks[16], (DEPTH, DA_QK_DIM), 0.1),
        'da_lambda_k1': nrm(ks[17], (DEPTH, DA_QK_DIM), 0.1),
        'da_lambda_q2': nrm(ks[18], (DEPTH, DA_QK_DIM), 0.1),
        'da_lambda_k2': nrm(ks[19], (DEPTH, DA_QK_DIM), 0.1),
        'da_subln_w': 1.0 + nrm(ks[20], (DEPTH, DA_V_DIM), 0.05),
        'w_out': nrm(ks[21], (DEPTH, D_MIX, D_MODEL), D_MIX ** -0.5),
        'norm2_w': 1.0 + nrm(ks[22], (DEPTH, D_MODEL), 0.05),
        'ffn_w_gate': nrm(ks[23], (DEPTH, D_MODEL, D_FF), D_MODEL ** -0.5),
        'ffn_w_up': nrm(ks[24], (DEPTH, D_MODEL, D_FF), D_MODEL ** -0.5),
        'ffn_w_down': nrm(ks[25], (DEPTH, D_FF, D_MODEL), D_FF ** -0.5),
    }


def reference(x, norm1_w, w_in, gm_norm_w, gm_ws, gm_bs, ssm_conv_w, ssm_conv_b, ssm_dt_bias,
              ssm_a_log, ssm_d, ssm_norm_w, pool_w, pool_scale, da_q_norm_w, da_k_norm_w,
              da_lambda_q1, da_lambda_k1, da_lambda_q2, da_lambda_k2, da_subln_w, w_out,
              norm2_w, ffn_w_gate, ffn_w_up, ffn_w_down):
    for i in range(DEPTH):
        h = rms_norm(x, norm1_w[i])
        proj = h @ w_in[i]
        pa, pb, pc, pd = jnp.split(proj, IN_SPLITS, axis=-1)
        ya = gmlp_mixer(pa, gm_norm_w[i], gm_ws[i], gm_bs[i])
        yb = ssd_mixer(pb, ssm_conv_w[i], ssm_conv_b[i], ssm_dt_bias[i], ssm_a_log[i], ssm_d[i], ssm_norm_w[i])
        yc = pool_mixer(pc, pool_w[i], pool_scale[i])
        yd = diff_attn_mixer(pd, da_q_norm_w[i], da_k_norm_w[i], da_lambda_q1[i], da_lambda_k1[i],
                             da_lambda_q2[i], da_lambda_k2[i], da_subln_w[i], i)
        mix = jnp.concatenate([ya.astype(x.dtype), yb.astype(x.dtype), yc.astype(x.dtype), yd.astype(x.dtype)], axis=-1)
        x = x + mix @ w_out[i]
        h = rms_norm(x, norm2_w[i])
        x = x + (jax.nn.silu(h @ ffn_w_gate[i]) * (h @ ffn_w_up[i])) @ ffn_w_down[i]
    return x
```

```python
import functools
import math

import jax
import jax.numpy as jnp
from jax import lax
from jax.experimental import pallas as pl
from jax.experimental.pallas import tpu as pltpu

F32 = jnp.float32
BF16 = jnp.bfloat16

D_MODEL = 1024
GROUP_W = 256
CHUNK = 128
HEAD_DIM = 64
N_HEADS = 4
SSM_GROUPS = 2
SSM_D_STATE = 128
SSM_CONV_K = 4
SSM_CONV_DIM = GROUP_W + 2 * SSM_GROUPS * SSM_D_STATE
POOL_WINDOWS = (2, 4, 8, 16)
DA_QK_DIM = 32
D_FF = 2816
RMS_EPS = 1e-6
NEG_BIG = -1e30

LANES = 128
VMEM_LIMIT = 56 * 1024 * 1024

COL_PA = 0
COL_V = 512
COL_Z = 1024
COL_PC = 1280
COL_XBC = 1536
COL_Q = 2304
COL_K = 2560
PROJ_W = 2816
DT_W = LANES
IN_CHUNK = 256

ATT_TQ = 256
ATT_TK = 256


def _cparams(sem):
    return pltpu.CompilerParams(dimension_semantics=sem, vmem_limit_bytes=VMEM_LIMIT)


def _sigmoid(x):
    return 1.0 / (1.0 + jnp.exp(-x))


def _dot(a, b):
    return jnp.dot(a, b, preferred_element_type=F32)


def _dot_nt(a, b):
    return lax.dot_general(a, b, (((1,), (1,)), ((), ())), preferred_element_type=F32)


def _inproj_kernel(x_ref, nw_ref, w_ref, proj_ref, dt_ref):
    x = x_ref[...]
    ms = jnp.mean(x * x, axis=-1, keepdims=True)
    h = (x * lax.rsqrt(ms + RMS_EPS) * nw_ref[...]).astype(BF16)
    for n0 in range(0, PROJ_W, IN_CHUNK):
        proj_ref[:, n0:n0 + IN_CHUNK] = _dot(h, w_ref[:, n0:n0 + IN_CHUNK]).astype(BF16)
    dt_ref[...] = _dot(h, w_ref[:, PROJ_W:PROJ_W + DT_W])


def _inproj(x2, nw, w, tm):
    m = x2.shape[0]
    return pl.pallas_call(
        _inproj_kernel,
        grid=(m // tm,),
        in_specs=[
            pl.BlockSpec((tm, D_MODEL), lambda i: (i, 0)),
            pl.BlockSpec((1, D_MODEL), lambda i: (0, 0)),
            pl.BlockSpec((D_MODEL, PROJ_W + DT_W), lambda i: (0, 0)),
        ],
        out_specs=[
            pl.BlockSpec((tm, PROJ_W), lambda i: (i, 0)),
            pl.BlockSpec((tm, DT_W), lambda i: (i, 0)),
        ],
        out_shape=[
            jax.ShapeDtypeStruct((m, PROJ_W), BF16),
            jax.ShapeDtypeStruct((m, DT_W), F32),
        ],
        compiler_params=_cparams(("parallel",)),
        name="inproj",
    )(x2, nw, w)


def _head_id(shape, width):
    lane = lax.broadcasted_iota(jnp.int32, shape, 1)
    return lax.shift_right_logical(lane, int(math.log2(width)))


def _gmlp_kernel(pa_ref, ws_ref, bsm_ref, nw_ref, g_ref, out_ref):
    t = pa_ref.shape[0]
    hact = jax.nn.gelu(pa_ref[...].astype(F32), approximate=True)
    u = hact[:, :GROUP_W]
    v = hact[:, GROUP_W:]
    ms = _dot((v * v).astype(BF16), g_ref[...])
    vn = (v * lax.rsqrt(ms + RMS_EPS) * nw_ref[...]).astype(BF16)
    row = lax.broadcasted_iota(jnp.int32, (CHUNK, CHUNK), 0)
    col = lax.broadcasted_iota(jnp.int32, (CHUNK, CHUNK), 1)
    wcat = jnp.concatenate(
        [jnp.where(row >= col, ws_ref[h], 0.0) for h in range(N_HEADS)], axis=1).astype(BF16)
    hid = _head_id((CHUNK, GROUP_W), HEAD_DIM)
    for c in range(t // CHUNK):
        vc = vn[c * CHUNK:(c + 1) * CHUNK]
        vstack = jnp.concatenate(
            [jnp.where(hid == h, vc, jnp.zeros_like(vc)) for h in range(N_HEADS)], axis=0)
        s = _dot(wcat, vstack) + bsm_ref[...]
        out_ref[c * CHUNK:(c + 1) * CHUNK, :] = (u[c * CHUNK:(c + 1) * CHUNK] * s).astype(BF16)


def _gmlp(proj, ws, bsm, nw, g64, t):
    m = proj.shape[0]
    return pl.pallas_call(
        _gmlp_kernel,
        grid=(m // t,),
        in_specs=[
            pl.BlockSpec((t, 2 * GROUP_W), lambda i: (i, COL_PA // (2 * GROUP_W))),
            pl.BlockSpec((N_HEADS, CHUNK, CHUNK), lambda i: (0, 0, 0)),
            pl.BlockSpec((CHUNK, GROUP_W), lambda i: (0, 0)),
            pl.BlockSpec((1, GROUP_W), lambda i: (0, 0)),
            pl.BlockSpec((GROUP_W, GROUP_W), lambda i: (0, 0)),
        ],
        out_specs=pl.BlockSpec((t, GROUP_W), lambda i: (i, 0)),
        out_shape=jax.ShapeDtypeStruct((m, GROUP_W), BF16),
        compiler_params=_cparams(("parallel",)),
        name="gmlp",
    )(proj, ws, bsm, nw, g64)


def _pool_kernel(pc_ref, wbd_ref, scale_ref, out_ref):
    h = pc_ref[...].astype(F32)
    shape = h.shape
    row = lax.broadcasted_iota(jnp.int32, shape, 0)

    def shifted(x, k):
        return jnp.where(row >= k, pltpu.roll(x, k, 0), 0.0)

    s2 = h + shifted(h, 1)
    s4 = s2 + shifted(s2, 2)
    s8 = s4 + shifted(s4, 4)
    s16 = s8 + shifted(s8, 8)
    gid = _head_id(shape, HEAD_DIM)
    wsum = jnp.where(gid == 0, s2, jnp.where(gid == 1, s4, jnp.where(gid == 2, s8, s16)))
    win = jnp.where(gid == 0, POOL_WINDOWS[0],
                    jnp.where(gid == 1, POOL_WINDOWS[1],
                              jnp.where(gid == 2, POOL_WINDOWS[2], POOL_WINDOWS[3])))
    cnt = jnp.minimum(win, row + 1).astype(F32)
    p = wsum / cnt - h
    y = _dot(p.astype(BF16), wbd_ref[...]) * scale_ref[...]
    out_ref[...] = y.astype(BF16)


def _pool(proj, wbd, scale, b, seq):
    m = proj.shape[0]
    return pl.pallas_call(
        _pool_kernel,
        grid=(b,),
        in_specs=[
            pl.BlockSpec((seq, GROUP_W), lambda i: (i, COL_PC // GROUP_W)),
            pl.BlockSpec((GROUP_W, GROUP_W), lambda i: (0, 0)),
            pl.BlockSpec((1, GROUP_W), lambda i: (0, 0)),
        ],
        out_specs=pl.BlockSpec((seq, GROUP_W), lambda i: (i, 0)),
        out_shape=jax.ShapeDtypeStruct((m, GROUP_W), BF16),
        compiler_params=_cparams(("parallel",)),
        name="pool",
    )(proj, wbd, scale)


def _split3(a):
    a1 = a.astype(BF16)
    r1 = a - a1.astype(F32)
    a2 = r1.astype(BF16)
    r2 = r1 - a2.astype(F32)
    return a1, a2, r2.astype(BF16)


def _ssd_kernel(z_ref, xbc_ref, dt_ref, cw_ref, cb_ref, dtb_ref, alog_ref, dexp_ref, nw_ref, g_ref,
                out_ref, st_ref, tail_ref):
    c = pl.program_id(1)

    @pl.when(c == 0)
    def _():
        st_ref[...] = jnp.zeros_like(st_ref)
        tail_ref[...] = jnp.zeros_like(tail_ref)

    xraw = xbc_ref[...].astype(F32)
    ext = jnp.concatenate([tail_ref[...], xraw], axis=0)
    tail_ref[...] = xraw[CHUNK - 8:CHUNK]
    acc = cb_ref[...] + cw_ref[0:1, :] * ext[5:5 + CHUNK]
    for k in range(1, SSM_CONV_K):
        acc = acc + cw_ref[k:k + 1, :] * ext[5 + k:5 + k + CHUNK]
    xc = acc * _sigmoid(acc)
    xs = xc[:, :GROUP_W]
    bm = xc[:, GROUP_W:2 * GROUP_W]
    cm = xc[:, 2 * GROUP_W:]

    t = dt_ref[...] + dtb_ref[...]
    dt = jnp.maximum(t, 0.0) + jnp.log(1.0 + jnp.exp(-jnp.abs(t)))
    a = dt * (-jnp.exp(alog_ref[...]))

    row = lax.broadcasted_iota(jnp.int32, (CHUNK, CHUNK), 0)
    col = lax.broadcasted_iota(jnp.int32, (CHUNK, CHUNK), 1)
    causal = row >= col
    ltri = jnp.where(causal, 1.0, 0.0).astype(BF16)
    a1, a2, a3 = _split3(a)
    acs = _dot(ltri, a1) + _dot(ltri, a2) + _dot(ltri, a3)
    acs_t = acs.T

    hid = _head_id((CHUNK, GROUP_W), HEAD_DIM)

    def expand(c4):
        return jnp.where(hid == 0, c4[:, 0:1],
                         jnp.where(hid == 1, c4[:, 1:2],
                                   jnp.where(hid == 2, c4[:, 2:3], c4[:, 3:4])))

    xdt = xs * expand(dt)
    eacs_e = expand(jnp.exp(acs))
    dte_e = expand(jnp.exp(acs[CHUNK - 1:CHUNK, :] - acs))
    xdt_b = xdt.astype(BF16)

    ms, yoff, bgs = [], [], []
    for g in range(SSM_GROUPS):
        bg = bm[:, g * SSM_D_STATE:(g + 1) * SSM_D_STATE]
        cg = cm[:, g * SSM_D_STATE:(g + 1) * SSM_D_STATE].astype(BF16)
        bgs.append(bg)
        cb = _dot_nt(cg, bg.astype(BF16))
        for hh in range(N_HEADS // SSM_GROUPS):
            h = g * (N_HEADS // SSM_GROUPS) + hh
            seg = acs[:, h:h + 1] - acs_t[h:h + 1, :]
            dec = jnp.exp(jnp.where(causal, seg, NEG_BIG))
            ms.append((cb * dec).astype(BF16))
        yoff.append(_dot(cg, st_ref[g].astype(BF16)))
    mcat = jnp.concatenate(ms, axis=1)
    xstack = jnp.concatenate(
        [jnp.where(hid == h, xdt_b, jnp.zeros_like(xdt_b)) for h in range(N_HEADS)], axis=0)
    y = _dot(mcat, xstack) + jnp.concatenate(yoff, axis=1) * eacs_e + dexp_ref[...] * xs

    w = (xdt * dte_e).astype(BF16)
    cdl = eacs_e[CHUNK - 1:CHUNK, :]
    for g in range(SSM_GROUPS):
        lo, hi = g * SSM_D_STATE, (g + 1) * SSM_D_STATE
        snew = _dot(bgs[g].T.astype(BF16), w[:, lo:hi])
        st_ref[g] = st_ref[g] * cdl[:, lo:hi] + snew

    z = z_ref[...].astype(F32)
    y = y * (z * _sigmoid(z))
    msq = _dot((y * y).astype(BF16), g_ref[...])
    out_ref[...] = (y * lax.rsqrt(msq + RMS_EPS) * nw_ref[...]).astype(BF16)


def _ssd(proj, dtraw, cw, cb, dtb, alog, dexp, nw, g128, b, seq):
    m = proj.shape[0]
    nc = seq // CHUNK
    const = lambda i, c: (0, 0)
    return pl.pallas_call(
        _ssd_kernel,
        grid=(b, nc),
        in_specs=[
            pl.BlockSpec((CHUNK, GROUP_W), lambda i, c: (i * nc + c, COL_Z // GROUP_W)),
            pl.BlockSpec((CHUNK, SSM_CONV_DIM), lambda i, c: (i * nc + c, COL_XBC // SSM_CONV_DIM)),
            pl.BlockSpec((CHUNK, DT_W), lambda i, c: (i * nc + c, 0)),
            pl.BlockSpec((SSM_CONV_K, SSM_CONV_DIM), const),
            pl.BlockSpec((1, SSM_CONV_DIM), const),
            pl.BlockSpec((1, DT_W), const),
            pl.BlockSpec((1, DT_W), const),
            pl.BlockSpec((1, GROUP_W), const),
            pl.BlockSpec((1, GROUP_W), const),
            pl.BlockSpec((GROUP_W, GROUP_W), const),
        ],
        out_specs=pl.BlockSpec((CHUNK, GROUP_W), lambda i, c: (i * nc + c, 0)),
        out_shape=jax.ShapeDtypeStruct((m, GROUP_W), BF16),
        scratch_shapes=[
            pltpu.VMEM((SSM_GROUPS, SSM_D_STATE, LANES), F32),
            pltpu.VMEM((8, SSM_CONV_DIM), F32),
        ],
        compiler_params=_cparams(("parallel", "arbitrary")),
        name="ssd",
    )(proj, proj, dtraw, cw, cb, dtb, alog, dexp, nw, g128)


def _attn_kernel(q_ref, k_ref, v_ref, qnw_ref, knw_ref, lq1_ref, lk1_ref, lq2_ref, lk2_ref, subw_ref,
                 g_ref, out_ref, kn_ref, qs_ref, m_ref, l_ref, acc_ref, *, lam_init):
    qi = pl.program_id(1)
    tq, tk = ATT_TQ, ATT_TK

    @pl.when(qi == 0)
    def _():
        k = k_ref[...].astype(F32)
        ms = _dot((k * k).astype(BF16), g_ref[...])
        kn_ref[...] = (k * lax.rsqrt(ms + RMS_EPS) * knw_ref[...]).astype(BF16)

    q = q_ref[...].astype(F32)
    ms = _dot((q * q).astype(BF16), g_ref[...])
    qn = (q * lax.rsqrt(ms + RMS_EPS) * (qnw_ref[...] * (DA_QK_DIM ** -0.5))).astype(BF16)
    grp = _head_id((tq, GROUP_W), DA_QK_DIM)
    for h in range(N_HEADS):
        qs_ref[h, 0:tq, :] = jnp.where(grp == 2 * h, qn, jnp.zeros_like(qn))
        qs_ref[h, tq:2 * tq, :] = jnp.where(grp == 2 * h + 1, qn, jnp.zeros_like(qn))
    m_ref[...] = jnp.full(m_ref.shape, NEG_BIG, F32)
    l_ref[...] = jnp.zeros_like(l_ref)
    acc_ref[...] = jnp.zeros_like(acc_ref)

    rowi = lax.broadcasted_iota(jnp.int32, (tq, tk), 0)
    coli = lax.broadcasted_iota(jnp.int32, (tq, tk), 1)

    def body(j, carry):
        k0 = pl.multiple_of(j * tk, tk)
        kt = kn_ref[pl.ds(k0, tk), :]
        dist = rowi - coli + (qi - j) * tq
        negd = jnp.where(dist >= 0, -dist.astype(F32), NEG_BIG)
        negd2 = jnp.concatenate([negd, negd], axis=0)
        for h in range(N_HEADS):
            slope = 2.0 ** (-8.0 * (h + 1) / N_HEADS)
            s = _dot_nt(qs_ref[h], kt) + slope * negd2
            m_prev = m_ref[h]
            m_new = jnp.maximum(m_prev, jnp.max(s, axis=1, keepdims=True))
            alpha = jnp.exp(m_prev - m_new)
            p = jnp.exp(s - jnp.concatenate([m_new] * (tk // LANES), axis=1))
            l_ref[h] = alpha * l_ref[h] + jnp.sum(p, axis=1, keepdims=True)
            m_ref[h] = m_new
            vt = v_ref[pl.ds(k0, tk), h * LANES:(h + 1) * LANES]
            acc_ref[h] = acc_ref[h] * alpha + _dot(p.astype(BF16), vt)
        return carry

    lax.fori_loop(0, qi + 1, body, 0)

    def lane_sum(x):
        return jnp.broadcast_to(jnp.sum(x, axis=1, keepdims=True), x.shape)

    lam = (jnp.exp(lane_sum(lq1_ref[...] * lk1_ref[...]))
           - jnp.exp(lane_sum(lq2_ref[...] * lk2_ref[...])) + lam_init)
    outs = []
    for h in range(N_HEADS):
        o = acc_ref[h] / l_ref[h]
        oh = o[0:tq] - lam * o[tq:2 * tq]
        ms = jnp.sum(oh * oh, axis=1, keepdims=True) * (1.0 / HEAD_DIM)
        outs.append(oh * lax.rsqrt(ms + RMS_EPS) * (subw_ref[...] * (1.0 - lam_init)))
    for g in range(N_HEADS // 2):
        pair = outs[2 * g] + pltpu.roll(outs[2 * g + 1], HEAD_DIM, 1)
        out_ref[:, g * LANES:(g + 1) * LANES] = pair.astype(BF16)


def _attn(proj, qnw, knw, lq1, lk1, lq2, lk2, subw, g32, b, seq, lam_init):
    m = proj.shape[0]
    nq = seq // ATT_TQ
    const = lambda i, j: (0, 0)
    return pl.pallas_call(
        functools.partial(_attn_kernel, lam_init=lam_init),
        grid=(b, nq),
        in_specs=[
            pl.BlockSpec((ATT_TQ, GROUP_W), lambda i, j: (i * nq + j, COL_Q // GROUP_W)),
            pl.BlockSpec((seq, GROUP_W), lambda i, j: (i, COL_K // GROUP_W)),
            pl.BlockSpec((seq, 2 * GROUP_W), lambda i, j: (i, COL_V // (2 * GROUP_W))),
            pl.BlockSpec((1, GROUP_W), const),
            pl.BlockSpec((1, GROUP_W), const),
            pl.BlockSpec((1, LANES), const),
            pl.BlockSpec((1, LANES), const),
            pl.BlockSpec((1, LANES), const),
            pl.BlockSpec((1, LANES), const),
            pl.BlockSpec((1, LANES), const),
            pl.BlockSpec((GROUP_W, GROUP_W), const),
        ],
        out_specs=pl.BlockSpec((ATT_TQ, GROUP_W), lambda i, j: (i * nq + j, 0)),
        out_shape=jax.ShapeDtypeStruct((m, GROUP_W), BF16),
        scratch_shapes=[
            pltpu.VMEM((seq, GROUP_W), BF16),
            pltpu.VMEM((N_HEADS, 2 * ATT_TQ, GROUP_W), BF16),
            pltpu.VMEM((N_HEADS, 2 * ATT_TQ, LANES), F32),
            pltpu.VMEM((N_HEADS, 2 * ATT_TQ, LANES), F32),
            pltpu.VMEM((N_HEADS, 2 * ATT_TQ, LANES), F32),
        ],
        compiler_params=_cparams(("parallel", "arbitrary")),
        name="diffattn",
    )(proj, proj, proj, qnw, knw, lq1, lk1, lq2, lk2, subw, g32)


def _ffn_kernel(x_ref, ya_ref, yb_ref, yc_ref, yd_ref, wo_ref, nw_ref, wg_ref, wu_ref, wd_ref,
                out_ref, x1_ref, h_ref, acc_ref):
    f = pl.program_id(1)

    @pl.when(f == 0)
    def _():
        x1 = x_ref[...]
        for i, y_ref in enumerate((ya_ref, yb_ref, yc_ref, yd_ref)):
            x1 = x1 + _dot(y_ref[...], wo_ref[i * GROUP_W:(i + 1) * GROUP_W, :])
        x1_ref[...] = x1
        ms = jnp.mean(x1 * x1, axis=-1, keepdims=True)
        h_ref[...] = (x1 * lax.rsqrt(ms + RMS_EPS) * nw_ref[...]).astype(BF16)
        acc_ref[...] = jnp.zeros_like(acc_ref)

    h = h_ref[...]
    g = _dot(h, wg_ref[...])
    u = _dot(h, wu_ref[...])
    act = (g * _sigmoid(g) * u).astype(BF16)
    acc_ref[...] += _dot(act, wd_ref[...])

    @pl.when(f == pl.num_programs(1) - 1)
    def _():
        out_ref[...] = x1_ref[...] + acc_ref[...]


def _ffn(x2, ya, yb, yc, yd, wo, nw, wg, wu, wd, tm, fc):
    m = x2.shape[0]
    nf = D_FF // fc
    row = lambda i, f: (i, 0)
    const = lambda i, f: (0, 0)
    return pl.pallas_call(
        _ffn_kernel,
        grid=(m // tm, nf),
        in_specs=[
            pl.BlockSpec((tm, D_MODEL), row),
            pl.BlockSpec((tm, GROUP_W), row),
            pl.BlockSpec((tm, GROUP_W), row),
            pl.BlockSpec((tm, GROUP_W), row),
            pl.BlockSpec((tm, GROUP_W), row),
            pl.BlockSpec((D_MODEL, D_MODEL), const),
            pl.BlockSpec((1, D_MODEL), const),
            pl.BlockSpec((D_MODEL, fc), lambda i, f: (0, f)),
            pl.BlockSpec((D_MODEL, fc), lambda i, f: (0, f)),
            pl.BlockSpec((fc, D_MODEL), lambda i, f: (f, 0)),
        ],
        out_specs=pl.BlockSpec((tm, D_MODEL), row),
        out_shape=jax.ShapeDtypeStruct((m, D_MODEL), F32),
        scratch_shapes=[
            pltpu.VMEM((tm, D_MODEL), F32),
            pltpu.VMEM((tm, D_MODEL), BF16),
            pltpu.VMEM((tm, D_MODEL), F32),
        ],
        compiler_params=_cparams(("parallel", "arbitrary")),
        name="outproj_ffn",
    )(x2, ya, yb, yc, yd, wo, nw, wg, wu, wd)


def _block_diag_mean(width, group):
    idx = jnp.arange(width) // group
    return jnp.where(idx[:, None] == idx[None, :], 1.0 / group, 0.0).astype(BF16)


def _pad_lanes(v, width):
    v = v.reshape(1, -1).astype(F32)
    return jnp.pad(v, ((0, 0), (0, width - v.shape[1])))


def _reorder_w_in(w):
    o = 2 * GROUP_W
    z = w[:, o:o + GROUP_W]
    xbc = w[:, o + GROUP_W:o + GROUP_W + SSM_CONV_DIM]
    dt0 = o + GROUP_W + SSM_CONV_DIM
    dt = w[:, dt0:dt0 + N_HEADS]
    pc0 = dt0 + N_HEADS
    pc = w[:, pc0:pc0 + GROUP_W]
    q = w[:, pc0 + GROUP_W:pc0 + 2 * GROUP_W]
    k = w[:, pc0 + 2 * GROUP_W:pc0 + 3 * GROUP_W]
    v = w[:, pc0 + 3 * GROUP_W:pc0 + 4 * GROUP_W]
    vpad = jnp.pad(v.reshape(D_MODEL, N_HEADS, HEAD_DIM), ((0, 0), (0, 0), (0, LANES - HEAD_DIM)))
    vpad = vpad.reshape(D_MODEL, N_HEADS * LANES)
    dtpad = jnp.pad(dt, ((0, 0), (0, DT_W - N_HEADS)))
    return jnp.concatenate([w[:, :o], vpad, z, pc, xbc, q, k, dtpad], axis=1).astype(BF16)


def kernel(x, norm1_w, w_in, gm_norm_w, gm_ws, gm_bs, ssm_conv_w, ssm_conv_b, ssm_dt_bias, ssm_a_log, ssm_d, ssm_norm_w, pool_w, pool_scale, da_q_norm_w, da_k_norm_w, da_lambda_q1, da_lambda_k1, da_lambda_q2, da_lambda_k2, da_subln_w, w_out, norm2_w, ffn_w_gate, ffn_w_up, ffn_w_down):
    b, seq, d = x.shape
    depth = w_in.shape[0]
    assert d == D_MODEL and seq % ATT_TQ == 0 and seq % CHUNK == 0
    m = b * seq
    tm = 512 if m % 512 == 0 else ATT_TQ
    fc = D_FF // 2

    g64 = _block_diag_mean(GROUP_W, HEAD_DIM)
    g128 = _block_diag_mean(GROUP_W, SSM_D_STATE)
    g32 = _block_diag_mean(GROUP_W, DA_QK_DIM)

    x2 = x.reshape(m, d)
    for i in range(depth):
        proj, dtraw = _inproj(x2, norm1_w[i].reshape(1, d), _reorder_w_in(w_in[i]), tm)

        bsm = jnp.repeat(gm_bs[i].T, HEAD_DIM, axis=1)
        ya = _gmlp(proj, gm_ws[i], bsm, gm_norm_w[i].reshape(1, GROUP_W), g64, tm)

        yb = _ssd(proj, dtraw, ssm_conv_w[i].T, ssm_conv_b[i].reshape(1, -1),
                  _pad_lanes(ssm_dt_bias[i], DT_W), _pad_lanes(ssm_a_log[i], DT_W),
                  jnp.repeat(ssm_d[i], HEAD_DIM).reshape(1, GROUP_W),
                  ssm_norm_w[i].reshape(1, GROUP_W), g128, b, seq)

        wbd = jax.scipy.linalg.block_diag(*[pool_w[i, g] for g in range(len(POOL_WINDOWS))]).astype(BF16)
        yc = _pool(proj, wbd, pool_scale[i].reshape(1, GROUP_W), b, seq)

        lam_init = 0.8 - 0.6 * math.exp(-0.3 * i)
        yd = _attn(proj, jnp.tile(da_q_norm_w[i], GROUP_W // DA_QK_DIM).reshape(1, GROUP_W),
                   jnp.tile(da_k_norm_w[i], GROUP_W // DA_QK_DIM).reshape(1, GROUP_W),
                   _pad_lanes(da_lambda_q1[i], LANES), _pad_lanes(da_lambda_k1[i], LANES),
                   _pad_lanes(da_lambda_q2[i], LANES), _pad_lanes(da_lambda_k2[i], LANES),
                   _pad_lanes(da_subln_w[i], LANES), g32, b, seq, lam_init)

        x2 = _ffn(x2, ya, yb, yc, yd, w_out[i].astype(BF16), norm2_w[i].reshape(1, d),
                  ffn_w_gate[i].astype(BF16), ffn_w_up[i].astype(BF16), ffn_w_down[i].astype(BF16),
                  tm, fc)
    return x2.reshape(b, seq, d)
```

```python
import functools
import math

import jax
import jax.numpy as jnp
from jax import lax
from jax.experimental import pallas as pl
from jax.experimental.pallas import tpu as pltpu

F32 = jnp.float32
BF16 = jnp.bfloat16

D_MODEL = 1024
GROUP_W = 256
CHUNK = 128
HEAD_DIM = 64
N_HEADS = 4
SSM_GROUPS = 2
SSM_D_STATE = 128
SSM_CONV_K = 4
SSM_CONV_DIM = GROUP_W + 2 * SSM_GROUPS * SSM_D_STATE
POOL_WINDOWS = (2, 4, 8, 16)
DA_QK_DIM = 32
D_FF = 2816
RMS_EPS = 1e-6
NEG_BIG = -1e30

LANES = 128
VMEM_LIMIT = 56 * 1024 * 1024

COL_PA = 0
COL_Z = 512
COL_PC = 768
COL_Q = 1024
COL_K = 1280
COL_XBC = 1536
COL_V = 2304
PROJ_W = 2560
DT_W = LANES
IN_CHUNK = 256

ATT_TQ = 256
ATT_TK = 256


def _cparams(sem):
    return pltpu.CompilerParams(dimension_semantics=sem, vmem_limit_bytes=VMEM_LIMIT)


def _sigmoid(x):
    return 1.0 / (1.0 + jnp.exp(-x))


def _dot(a, b):
    return jnp.dot(a, b, preferred_element_type=F32)


def _dot_nt(a, b):
    return lax.dot_general(a, b, (((1,), (1,)), ((), ())), preferred_element_type=F32)


def _inproj_kernel(x_ref, nw_ref, w_ref, proj_ref, dt_ref):
    x = x_ref[...]
    ms = jnp.mean(x * x, axis=-1, keepdims=True)
    h = (x * lax.rsqrt(ms + RMS_EPS) * nw_ref[...]).astype(BF16)
    for n0 in range(0, PROJ_W, IN_CHUNK):
        proj_ref[:, n0:n0 + IN_CHUNK] = _dot(h, w_ref[:, n0:n0 + IN_CHUNK]).astype(BF16)
    dt_ref[...] = _dot(h, w_ref[:, PROJ_W:PROJ_W + DT_W])


def _inproj(x2, nw, w, tm):
    m = x2.shape[0]
    return pl.pallas_call(
        _inproj_kernel,
        grid=(m // tm,),
        in_specs=[
            pl.BlockSpec((tm, D_MODEL), lambda i: (i, 0)),
            pl.BlockSpec((1, D_MODEL), lambda i: (0, 0)),
            pl.BlockSpec((D_MODEL, PROJ_W + DT_W), lambda i: (0, 0)),
        ],
        out_specs=[
            pl.BlockSpec((tm, PROJ_W), lambda i: (i, 0)),
            pl.BlockSpec((tm, DT_W), lambda i: (i, 0)),
        ],
        out_shape=[
            jax.ShapeDtypeStruct((m, PROJ_W), BF16),
            jax.ShapeDtypeStruct((m, DT_W), F32),
        ],
        compiler_params=_cparams(("parallel",)),
        name="inproj",
    )(x2, nw, w)


def _head_id(shape, width):
    lane = lax.broadcasted_iota(jnp.int32, shape, 1)
    return lax.shift_right_logical(lane, int(math.log2(width)))


def _gmlp_kernel(pa_ref, ws_ref, bsm_ref, nw_ref, g_ref, out_ref):
    t = pa_ref.shape[0]
    hact = jax.nn.gelu(pa_ref[...].astype(F32), approximate=True)
    u = hact[:, :GROUP_W]
    v = hact[:, GROUP_W:]
    ms = _dot((v * v).astype(BF16), g_ref[...])
    vn = (v * lax.rsqrt(ms + RMS_EPS) * nw_ref[...]).astype(BF16)
    row = lax.broadcasted_iota(jnp.int32, (CHUNK, CHUNK), 0)
    col = lax.broadcasted_iota(jnp.int32, (CHUNK, CHUNK), 1)
    wcat = jnp.concatenate(
        [jnp.where(row >= col, ws_ref[h], 0.0) for h in range(N_HEADS)], axis=1).astype(BF16)
    hid = _head_id((CHUNK, GROUP_W), HEAD_DIM)
    for c in range(t // CHUNK):
        vc = vn[c * CHUNK:(c + 1) * CHUNK]
        vstack = jnp.concatenate(
            [jnp.where(hid == h, vc, jnp.zeros_like(vc)) for h in range(N_HEADS)], axis=0)
        s = _dot(wcat, vstack) + bsm_ref[...]
        out_ref[c * CHUNK:(c + 1) * CHUNK, :] = (u[c * CHUNK:(c + 1) * CHUNK] * s).astype(BF16)


def _gmlp(proj, ws, bsm, nw, g64, t):
    m = proj.shape[0]
    return pl.pallas_call(
        _gmlp_kernel,
        grid=(m // t,),
        in_specs=[
            pl.BlockSpec((t, 2 * GROUP_W), lambda i: (i, COL_PA // (2 * GROUP_W))),
            pl.BlockSpec((N_HEADS, CHUNK, CHUNK), lambda i: (0, 0, 0)),
            pl.BlockSpec((CHUNK, GROUP_W), lambda i: (0, 0)),
            pl.BlockSpec((1, GROUP_W), lambda i: (0, 0)),
            pl.BlockSpec((GROUP_W, GROUP_W), lambda i: (0, 0)),
        ],
        out_specs=pl.BlockSpec((t, GROUP_W), lambda i: (i, 0)),
        out_shape=jax.ShapeDtypeStruct((m, GROUP_W), BF16),
        compiler_params=_cparams(("parallel",)),
        name="gmlp",
    )(proj, ws, bsm, nw, g64)


def _pool_kernel(pc_ref, wbd_ref, scale_ref, out_ref):
    h = pc_ref[...].astype(F32)
    shape = h.shape
    row = lax.broadcasted_iota(jnp.int32, shape, 0)

    def shifted(x, k):
        return jnp.where(row >= k, pltpu.roll(x, k, 0), 0.0)

    s2 = h + shifted(h, 1)
    s4 = s2 + shifted(s2, 2)
    s8 = s4 + shifted(s4, 4)
    s16 = s8 + shifted(s8, 8)
    gid = _head_id(shape, HEAD_DIM)
    wsum = jnp.where(gid == 0, s2, jnp.where(gid == 1, s4, jnp.where(gid == 2, s8, s16)))
    win = jnp.where(gid == 0, POOL_WINDOWS[0],
                    jnp.where(gid == 1, POOL_WINDOWS[1],
                              jnp.where(gid == 2, POOL_WINDOWS[2], POOL_WINDOWS[3])))
    cnt = jnp.minimum(win, row + 1).astype(F32)
    p = wsum / cnt - h
    y = _dot(p.astype(BF16), wbd_ref[...]) * scale_ref[...]
    out_ref[...] = y.astype(BF16)


def _pool(proj, wbd, scale, b, seq):
    m = proj.shape[0]
    return pl.pallas_call(
        _pool_kernel,
        grid=(b,),
        in_specs=[
            pl.BlockSpec((seq, GROUP_W), lambda i: (i, COL_PC // GROUP_W)),
            pl.BlockSpec((GROUP_W, GROUP_W), lambda i: (0, 0)),
            pl.BlockSpec((1, GROUP_W), lambda i: (0, 0)),
        ],
        out_specs=pl.BlockSpec((seq, GROUP_W), lambda i: (i, 0)),
        out_shape=jax.ShapeDtypeStruct((m, GROUP_W), BF16),
        compiler_params=_cparams(("parallel",)),
        name="pool",
    )(proj, wbd, scale)


def _split3(a):
    a1 = a.astype(BF16)
    r1 = a - a1.astype(F32)
    a2 = r1.astype(BF16)
    r2 = r1 - a2.astype(F32)
    return a1, a2, r2.astype(BF16)


def _ssd_kernel(z_ref, xbc_ref, dt_ref, cw_ref, cb_ref, dtb_ref, alog_ref, dexp_ref, nw_ref, g_ref,
                out_ref, st_ref, tail_ref):
    c = pl.program_id(1)

    @pl.when(c == 0)
    def _():
        st_ref[...] = jnp.zeros_like(st_ref)
        tail_ref[...] = jnp.zeros_like(tail_ref)

    xraw = xbc_ref[...].astype(F32)
    ext = jnp.concatenate([tail_ref[...], xraw], axis=0)
    tail_ref[...] = xraw[CHUNK - 8:CHUNK]
    acc = cb_ref[...] + cw_ref[0:1, :] * ext[5:5 + CHUNK]
    for k in range(1, SSM_CONV_K):
        acc = acc + cw_ref[k:k + 1, :] * ext[5 + k:5 + k + CHUNK]
    xc = acc * _sigmoid(acc)
    xs = xc[:, :GROUP_W]
    bm = xc[:, GROUP_W:2 * GROUP_W]
    cm = xc[:, 2 * GROUP_W:]

    t = dt_ref[...] + dtb_ref[...]
    dt = jnp.maximum(t, 0.0) + jnp.log(1.0 + jnp.exp(-jnp.abs(t)))
    a = dt * (-jnp.exp(alog_ref[...]))

    row = lax.broadcasted_iota(jnp.int32, (CHUNK, CHUNK), 0)
    col = lax.broadcasted_iota(jnp.int32, (CHUNK, CHUNK), 1)
    causal = row >= col
    ltri = jnp.where(causal, 1.0, 0.0).astype(BF16)
    a1, a2, a3 = _split3(a)
    acs = _dot(ltri, a1) + _dot(ltri, a2) + _dot(ltri, a3)
    acs_t = acs.T

    hid = _head_id((CHUNK, GROUP_W), HEAD_DIM)

    def expand(c4):
        return jnp.where(hid == 0, c4[:, 0:1],
                         jnp.where(hid == 1, c4[:, 1:2],
                                   jnp.where(hid == 2, c4[:, 2:3], c4[:, 3:4])))

    xdt = xs * expand(dt)
    eacs_e = expand(jnp.exp(acs))
    dte_e = expand(jnp.exp(acs[CHUNK - 1:CHUNK, :] - acs))
    xdt_b = xdt.astype(BF16)

    ms, yoff, bgs = [], [], []
    for g in range(SSM_GROUPS):
        bg = bm[:, g * SSM_D_STATE:(g + 1) * SSM_D_STATE]
        cg = cm[:, g * SSM_D_STATE:(g + 1) * SSM_D_STATE].astype(BF16)
        bgs.append(bg)
        cb = _dot_nt(cg, bg.astype(BF16))
        for hh in range(N_HEADS // SSM_GROUPS):
            h = g * (N_HEADS // SSM_GROUPS) + hh
            seg = acs[:, h:h + 1] - acs_t[h:h + 1, :]
            dec = jnp.exp(jnp.where(causal, seg, NEG_BIG))
            ms.append((cb * dec).astype(BF16))
        yoff.append(_dot(cg, st_ref[g].astype(BF16)))
    mcat = jnp.concatenate(ms, axis=1)
    xstack = jnp.concatenate(
        [jnp.where(hid == h, xdt_b, jnp.zeros_like(xdt_b)) for h in range(N_HEADS)], axis=0)
    y = _dot(mcat, xstack) + jnp.concatenate(yoff, axis=1) * eacs_e + dexp_ref[...] * xs

    w = (xdt * dte_e).astype(BF16)
    cdl = eacs_e[CHUNK - 1:CHUNK, :]
    for g in range(SSM_GROUPS):
        lo, hi = g * SSM_D_STATE, (g + 1) * SSM_D_STATE
        snew = _dot(bgs[g].T.astype(BF16), w[:, lo:hi])
        st_ref[g] = st_ref[g] * cdl[:, lo:hi] + snew

    z = z_ref[...].astype(F32)
    y = y * (z * _sigmoid(z))
    msq = _dot((y * y).astype(BF16), g_ref[...])
    out_ref[...] = (y * lax.rsqrt(msq + RMS_EPS) * nw_ref[...]).astype(BF16)


def _ssd(proj, dtraw, cw, cb, dtb, alog, dexp, nw, g128, b, seq):
    m = proj.shape[0]
    nc = seq // CHUNK
    const = lambda i, c: (0, 0)
    return pl.pallas_call(
        _ssd_kernel,
        grid=(b, nc),
        in_specs=[
            pl.BlockSpec((CHUNK, GROUP_W), lambda i, c: (i * nc + c, COL_Z // GROUP_W)),
            pl.BlockSpec((CHUNK, SSM_CONV_DIM), lambda i, c: (i * nc + c, COL_XBC // SSM_CONV_DIM)),
            pl.BlockSpec((CHUNK, DT_W), lambda i, c: (i * nc + c, 0)),
            pl.BlockSpec((SSM_CONV_K, SSM_CONV_DIM), const),
            pl.BlockSpec((1, SSM_CONV_DIM), const),
            pl.BlockSpec((1, DT_W), const),
            pl.BlockSpec((1, DT_W), const),
            pl.BlockSpec((1, GROUP_W), const),
            pl.BlockSpec((1, GROUP_W), const),
            pl.BlockSpec((GROUP_W, GROUP_W), const),
        ],
        out_specs=pl.BlockSpec((CHUNK, GROUP_W), lambda i, c: (i * nc + c, 0)),
        out_shape=jax.ShapeDtypeStruct((m, GROUP_W), BF16),
        scratch_shapes=[
            pltpu.VMEM((SSM_GROUPS, SSM_D_STATE, LANES), F32),
            pltpu.VMEM((8, SSM_CONV_DIM), F32),
        ],
        compiler_params=_cparams(("parallel", "arbitrary")),
        name="ssd",
    )(proj, proj, dtraw, cw, cb, dtb, alog, dexp, nw, g128)


def _alibi_slope(h):
    return 2.0 ** (-8.0 * (h + 1) / N_HEADS)


def _head_slot(x, h):
    base = x[:, (h // 2) * LANES:(h // 2 + 1) * LANES]
    return pltpu.roll(base, HEAD_DIM, 1) if h % 2 else base


def _attn_kernel(q_ref, k_ref, v_ref, qnw_ref, knw_ref, lq1_ref, lk1_ref, lq2_ref, lk2_ref, subw_ref,
                 g_ref, out_ref, ka_ref, vt_ref, qs_ref, acc_ref, s0_ref, *, lam_init):
    qi = pl.program_id(1)
    tq, tk = ATT_TQ, ATT_TK
    seq = k_ref.shape[0]

    def aug_lanes(pos, slope, key_side):
        lane = lax.broadcasted_iota(jnp.int32, pos.shape, 1)
        hi = (slope * LANES) * lax.shift_right_logical(pos, 7).astype(F32)
        lo = slope * (pos & (LANES - 1)).astype(F32)
        one = jnp.ones_like(hi)
        c = (one, one, hi, lo) if key_side else (-hi, -lo, one, one)
        return jnp.where(lane == 64, c[0], jnp.where(lane == 65, c[1],
                         jnp.where(lane == 66, c[2], jnp.where(lane == 67, c[3], 0.0))))

    @pl.when(qi == 0)
    def _():
        k = k_ref[...].astype(F32)
        ms = _dot((k * k).astype(BF16), g_ref[...])
        kn = k * lax.rsqrt(ms + RMS_EPS) * knw_ref[...]
        pos = lax.broadcasted_iota(jnp.int32, (seq, LANES), 0)
        lane = lax.broadcasted_iota(jnp.int32, (seq, LANES), 1)
        for h in range(N_HEADS):
            aug = aug_lanes(pos, _alibi_slope(h), True)
            ka_ref[h] = jnp.where(lane < HEAD_DIM, _head_slot(kn, h), aug).astype(BF16)
        for j in range(seq // tk):
            vt_ref[j] = v_ref[j * tk:(j + 1) * tk, :].astype(F32).T.astype(BF16)

    q = q_ref[...].astype(F32)
    ms = _dot((q * q).astype(BF16), g_ref[...])
    qn = q * lax.rsqrt(ms + RMS_EPS) * (qnw_ref[...] * (DA_QK_DIM ** -0.5))
    pos = qi * tq + lax.broadcasted_iota(jnp.int32, (tq, LANES), 0)
    lane = lax.broadcasted_iota(jnp.int32, (tq, LANES), 1)
    for h in range(N_HEADS):
        base = _head_slot(qn, h)
        aug = aug_lanes(pos, _alibi_slope(h), False)
        qs_ref[h, 0:tq, :] = jnp.where(lane < DA_QK_DIM, base,
                                       jnp.where(lane < HEAD_DIM, 0.0, aug)).astype(BF16)
        qs_ref[h, tq:2 * tq, :] = jnp.where(lane < DA_QK_DIM, 0.0,
                                            jnp.where(lane < HEAD_DIM, base, aug)).astype(BF16)
    acc_ref[...] = jnp.zeros_like(acc_ref)

    def scores(h, j):
        k0 = pl.multiple_of(j * tk, tk)
        return _dot_nt(ka_ref[h, pl.ds(k0, tk), :], qs_ref[h])

    def tile(j, ms_, ls_, masked):
        vt = vt_ref[j]
        new_m, new_l = [], []
        s_next = s0_ref[...]
        for h in range(N_HEADS):
            s = s_next
            if h + 1 < N_HEADS:
                s_next = scores(h + 1, j)
            elif not masked:
                s0_ref[...] = scores(0, j + 1)
            if masked:
                kk = lax.broadcasted_iota(jnp.int32, (tk, 2 * tq), 0)
                qq = lax.broadcasted_iota(jnp.int32, (tk, 2 * tq), 1) & (tq - 1)
                s = jnp.where(kk <= qq, s, NEG_BIG)
            m_new = jnp.maximum(ms_[h], jnp.max(s, axis=0, keepdims=True))
            alpha = jnp.exp(ms_[h] - m_new)
            p = jnp.exp(s - m_new)
            new_l.append(alpha * ls_[h] + jnp.sum(p, axis=0, keepdims=True))
            new_m.append(m_new)
            pv = _dot(vt[h * HEAD_DIM:(h + 1) * HEAD_DIM, :], p.astype(BF16))
            acc_ref[h] = acc_ref[h] * alpha + pv
        return tuple(new_m), tuple(new_l)

    s0_ref[...] = scores(0, 0)
    m0 = tuple(jnp.full((1, 2 * tq), NEG_BIG, F32) for _ in range(N_HEADS))
    l0 = tuple(jnp.zeros((1, 2 * tq), F32) for _ in range(N_HEADS))
    m1, l1 = lax.fori_loop(0, qi, lambda j, c: tile(j, c[0], c[1], False), (m0, l0))
    _, l2 = tile(qi, m1, l1, True)

    def lane_sum(x):
        return jnp.broadcast_to(jnp.sum(x, axis=1, keepdims=True), x.shape)

    lam = (jnp.exp(lane_sum(lq1_ref[...] * lk1_ref[...]))
           - jnp.exp(lane_sum(lq2_ref[...] * lk2_ref[...])) + lam_init)
    lam = jnp.concatenate([lam] * (tq // LANES), axis=1)
    outs = []
    for h in range(N_HEADS):
        o = acc_ref[h] / l2[h]
        oh = o[:, 0:tq] - lam * o[:, tq:2 * tq]
        ms = jnp.mean(oh * oh, axis=0, keepdims=True)
        outs.append(oh * lax.rsqrt(ms + RMS_EPS) * (subw_ref[...] * (1.0 - lam_init)))
    out_ref[...] = jnp.concatenate(outs, axis=0).T.astype(BF16)


def _attn(proj, qnw, knw, lq1, lk1, lq2, lk2, subw, g32, b, seq, lam_init):
    m = proj.shape[0]
    assert ATT_TQ == ATT_TK
    nq = seq // ATT_TQ
    const = lambda i, j: (0, 0)
    return pl.pallas_call(
        functools.partial(_attn_kernel, lam_init=lam_init),
        grid=(b, nq),
        in_specs=[
            pl.BlockSpec((ATT_TQ, GROUP_W), lambda i, j: (i * nq + j, COL_Q // GROUP_W)),
            pl.BlockSpec((seq, GROUP_W), lambda i, j: (i, COL_K // GROUP_W)),
            pl.BlockSpec((seq, GROUP_W), lambda i, j: (i, COL_V // GROUP_W)),
            pl.BlockSpec((1, GROUP_W), const),
            pl.BlockSpec((1, GROUP_W), const),
            pl.BlockSpec((1, LANES), const),
            pl.BlockSpec((1, LANES), const),
            pl.BlockSpec((1, LANES), const),
            pl.BlockSpec((1, LANES), const),
            pl.BlockSpec((HEAD_DIM, ATT_TQ), const),
            pl.BlockSpec((GROUP_W, GROUP_W), const),
        ],
        out_specs=pl.BlockSpec((ATT_TQ, GROUP_W), lambda i, j: (i * nq + j, 0)),
        out_shape=jax.ShapeDtypeStruct((m, GROUP_W), BF16),
        scratch_shapes=[
            pltpu.VMEM((N_HEADS, seq, LANES), BF16),
            pltpu.VMEM((seq // ATT_TK, GROUP_W, ATT_TK), BF16),
            pltpu.VMEM((N_HEADS, 2 * ATT_TQ, LANES), BF16),
            pltpu.VMEM((N_HEADS, HEAD_DIM, 2 * ATT_TQ), F32),
            pltpu.VMEM((ATT_TK, 2 * ATT_TQ), F32),
        ],
        compiler_params=_cparams(("parallel", "arbitrary")),
        name="diffattn",
    )(proj, proj, proj, qnw, knw, lq1, lk1, lq2, lk2, subw, g32)


def _ffn_kernel(x_ref, ya_ref, yb_ref, yc_ref, yd_ref, wo_ref, nw_ref, wg_ref, wu_ref, wd_ref,
                out_ref, x1_ref, h_ref, acc_ref):
    f = pl.program_id(1)

    @pl.when(f == 0)
    def _():
        x1 = x_ref[...]
        for i, y_ref in enumerate((ya_ref, yb_ref, yc_ref, yd_ref)):
            x1 = x1 + _dot(y_ref[...], wo_ref[i * GROUP_W:(i + 1) * GROUP_W, :])
        x1_ref[...] = x1
        ms = jnp.mean(x1 * x1, axis=-1, keepdims=True)
        h_ref[...] = (x1 * lax.rsqrt(ms + RMS_EPS) * nw_ref[...]).astype(BF16)
        acc_ref[...] = jnp.zeros_like(acc_ref)

    h = h_ref[...]
    g = _dot(h, wg_ref[...])
    u = _dot(h, wu_ref[...])
    act = (g * _sigmoid(g) * u).astype(BF16)
    acc_ref[...] += _dot(act, wd_ref[...])

    @pl.when(f == pl.num_programs(1) - 1)
    def _():
        out_ref[...] = x1_ref[...] + acc_ref[...]


def _ffn(x2, ya, yb, yc, yd, wo, nw, wg, wu, wd, tm, fc):
    m = x2.shape[0]
    nf = D_FF // fc
    row = lambda i, f: (i, 0)
    const = lambda i, f: (0, 0)
    return pl.pallas_call(
        _ffn_kernel,
        grid=(m // tm, nf),
        in_specs=[
            pl.BlockSpec((tm, D_MODEL), row),
            pl.BlockSpec((tm, GROUP_W), row),
            pl.BlockSpec((tm, GROUP_W), row),
            pl.BlockSpec((tm, GROUP_W), row),
            pl.BlockSpec((tm, GROUP_W), row),
            pl.BlockSpec((D_MODEL, D_MODEL), const),
            pl.BlockSpec((1, D_MODEL), const),
            pl.BlockSpec((D_MODEL, fc), lambda i, f: (0, f)),
            pl.BlockSpec((D_MODEL, fc), lambda i, f: (0, f)),
            pl.BlockSpec((fc, D_MODEL), lambda i, f: (f, 0)),
        ],
        out_specs=pl.BlockSpec((tm, D_MODEL), row),
        out_shape=jax.ShapeDtypeStruct((m, D_MODEL), F32),
        scratch_shapes=[
            pltpu.VMEM((tm, D_MODEL), F32),
            pltpu.VMEM((tm, D_MODEL), BF16),
            pltpu.VMEM((tm, D_MODEL), F32),
        ],
        compiler_params=_cparams(("parallel", "arbitrary")),
        name="outproj_ffn",
    )(x2, ya, yb, yc, yd, wo, nw, wg, wu, wd)


def _block_diag_mean(width, group):
    idx = jnp.arange(width) // group
    return jnp.where(idx[:, None] == idx[None, :], 1.0 / group, 0.0).astype(BF16)


def _pad_lanes(v, width):
    v = v.reshape(1, -1).astype(F32)
    return jnp.pad(v, ((0, 0), (0, width - v.shape[1])))


def _reorder_w_in(w):
    o = 2 * GROUP_W
    z = w[:, o:o + GROUP_W]
    xbc = w[:, o + GROUP_W:o + GROUP_W + SSM_CONV_DIM]
    dt0 = o + GROUP_W + SSM_CONV_DIM
    dt = w[:, dt0:dt0 + N_HEADS]
    pc0 = dt0 + N_HEADS
    pc = w[:, pc0:pc0 + GROUP_W]
    q = w[:, pc0 + GROUP_W:pc0 + 2 * GROUP_W]
    k = w[:, pc0 + 2 * GROUP_W:pc0 + 3 * GROUP_W]
    v = w[:, pc0 + 3 * GROUP_W:pc0 + 4 * GROUP_W]
    dtpad = jnp.pad(dt, ((0, 0), (0, DT_W - N_HEADS)))
    return jnp.concatenate([w[:, :o], z, pc, q, k, xbc, v, dtpad], axis=1).astype(BF16)


def kernel(x, norm1_w, w_in, gm_norm_w, gm_ws, gm_bs, ssm_conv_w, ssm_conv_b, ssm_dt_bias, ssm_a_log, ssm_d, ssm_norm_w, pool_w, pool_scale, da_q_norm_w, da_k_norm_w, da_lambda_q1, da_lambda_k1, da_lambda_q2, da_lambda_k2, da_subln_w, w_out, norm2_w, ffn_w_gate, ffn_w_up, ffn_w_down):
    b, seq, d = x.shape
    depth = w_in.shape[0]
    assert d == D_MODEL and seq % ATT_TQ == 0 and seq % CHUNK == 0
    m = b * seq
    tm = 512 if m % 512 == 0 else ATT_TQ
    fc = D_FF // 2

    g64 = _block_diag_mean(GROUP_W, HEAD_DIM)
    g128 = _block_diag_mean(GROUP_W, SSM_D_STATE)
    g32 = _block_diag_mean(GROUP_W, DA_QK_DIM)

    x2 = x.reshape(m, d)
    for i in range(depth):
        proj, dtraw = _inproj(x2, norm1_w[i].reshape(1, d), _reorder_w_in(w_in[i]), tm)

        bsm = jnp.repeat(gm_bs[i].T, HEAD_DIM, axis=1)
        ya = _gmlp(proj, gm_ws[i], bsm, gm_norm_w[i].reshape(1, GROUP_W), g64, tm)

        yb = _ssd(proj, dtraw, ssm_conv_w[i].T, ssm_conv_b[i].reshape(1, -1),
                  _pad_lanes(ssm_dt_bias[i], DT_W), _pad_lanes(ssm_a_log[i], DT_W),
                  jnp.repeat(ssm_d[i], HEAD_DIM).reshape(1, GROUP_W),
                  ssm_norm_w[i].reshape(1, GROUP_W), g128, b, seq)

        wbd = jax.scipy.linalg.block_diag(*[pool_w[i, g] for g in range(len(POOL_WINDOWS))]).astype(BF16)
        yc = _pool(proj, wbd, pool_scale[i].reshape(1, GROUP_W), b, seq)

        lam_init = 0.8 - 0.6 * math.exp(-0.3 * i)
        yd = _attn(proj, jnp.tile(da_q_norm_w[i], GROUP_W // DA_QK_DIM).reshape(1, GROUP_W),
                   jnp.tile(da_k_norm_w[i], GROUP_W // DA_QK_DIM).reshape(1, GROUP_W),
                   _pad_lanes(da_lambda_q1[i], LANES), _pad_lanes(da_lambda_k1[i], LANES),
                   _pad_lanes(da_lambda_q2[i], LANES), _pad_lanes(da_lambda_k2[i], LANES),
                   jnp.broadcast_to(da_subln_w[i].reshape(HEAD_DIM, 1), (HEAD_DIM, ATT_TQ)),
                   g32, b, seq, lam_init)

        x2 = _ffn(x2, ya, yb, yc, yd, w_out[i].astype(BF16), norm2_w[i].reshape(1, d),
                  ffn_w_gate[i].astype(BF16), ffn_w_up[i].astype(BF16), ffn_w_down[i].astype(BF16),
                  tm, fc)
    return x2.reshape(b, seq, d)
```

```python
import functools
import math

import jax
import jax.numpy as jnp
from jax import lax
from jax.experimental import pallas as pl
from jax.experimental.pallas import tpu as pltpu

F32 = jnp.float32
BF16 = jnp.bfloat16

D_MODEL = 1024
GROUP_W = 256
CHUNK = 128
HEAD_DIM = 64
N_HEADS = 4
SSM_GROUPS = 2
SSM_D_STATE = 128
SSM_CONV_K = 4
SSM_CONV_DIM = GROUP_W + 2 * SSM_GROUPS * SSM_D_STATE
POOL_WINDOWS = (2, 4, 8, 16)
DA_QK_DIM = 32
D_FF = 2816
RMS_EPS = 1e-6
NEG_BIG = -1e30

LANES = 128
VMEM_LIMIT = 56 * 1024 * 1024

COL_PA = 0
COL_Z = 512
COL_PC = 768
COL_Q = 1024
COL_K = 1280
COL_XBC = 1536
COL_V = 2304
PROJ_W = 2560
DT_W = LANES
IN_CHUNK = 256
FFN_CHUNK = 512

ATT_TQ = 256
ATT_TK = 256


def _cparams(sem):
    return pltpu.CompilerParams(dimension_semantics=sem, vmem_limit_bytes=VMEM_LIMIT)


def _sigmoid(x):
    return 1.0 / (1.0 + jnp.exp(-x))


def _dot(a, b):
    return jnp.dot(a, b, preferred_element_type=F32)


def _dot_nt(a, b):
    return lax.dot_general(a, b, (((1,), (1,)), ((), ())), preferred_element_type=F32)


def _inproj_kernel(x_ref, nw_ref, w_ref, proj_ref, dt_ref):
    x = x_ref[...]
    ms = jnp.mean(x * x, axis=-1, keepdims=True)
    h = (x * lax.rsqrt(ms + RMS_EPS) * nw_ref[...]).astype(BF16)
    for n0 in range(0, PROJ_W, IN_CHUNK):
        proj_ref[:, n0:n0 + IN_CHUNK] = _dot(h, w_ref[:, n0:n0 + IN_CHUNK]).astype(BF16)
    dt_ref[...] = _dot(h, w_ref[:, PROJ_W:PROJ_W + DT_W])


def _inproj(x2, nw, w, tm):
    m = x2.shape[0]
    return pl.pallas_call(
        _inproj_kernel,
        grid=(m // tm,),
        in_specs=[
            pl.BlockSpec((tm, D_MODEL), lambda i: (i, 0)),
            pl.BlockSpec((1, D_MODEL), lambda i: (0, 0)),
            pl.BlockSpec((D_MODEL, PROJ_W + DT_W), lambda i: (0, 0)),
        ],
        out_specs=[
            pl.BlockSpec((tm, PROJ_W), lambda i: (i, 0)),
            pl.BlockSpec((tm, DT_W), lambda i: (i, 0)),
        ],
        out_shape=[
            jax.ShapeDtypeStruct((m, PROJ_W), BF16),
            jax.ShapeDtypeStruct((m, DT_W), F32),
        ],
        compiler_params=_cparams(("parallel",)),
        name="inproj",
    )(x2, nw, w)


def _head_id(shape, width):
    lane = lax.broadcasted_iota(jnp.int32, shape, 1)
    return lax.shift_right_logical(lane, int(math.log2(width)))


def _gmlp_kernel(pa_ref, ws_ref, bsm_ref, nw_ref, g_ref, out_ref):
    t = pa_ref.shape[0]
    hact = jax.nn.gelu(pa_ref[...].astype(F32), approximate=True)
    u = hact[:, :GROUP_W]
    v = hact[:, GROUP_W:]
    ms = _dot((v * v).astype(BF16), g_ref[...])
    vn = (v * lax.rsqrt(ms + RMS_EPS) * nw_ref[...]).astype(BF16)
    row = lax.broadcasted_iota(jnp.int32, (CHUNK, CHUNK), 0)
    col = lax.broadcasted_iota(jnp.int32, (CHUNK, CHUNK), 1)
    wcat = jnp.concatenate(
        [jnp.where(row >= col, ws_ref[h], 0.0) for h in range(N_HEADS)], axis=1).astype(BF16)
    hid = _head_id((CHUNK, GROUP_W), HEAD_DIM)
    for c in range(t // CHUNK):
        vc = vn[c * CHUNK:(c + 1) * CHUNK]
        vstack = jnp.concatenate(
            [jnp.where(hid == h, vc, jnp.zeros_like(vc)) for h in range(N_HEADS)], axis=0)
        s = _dot(wcat, vstack) + bsm_ref[...]
        out_ref[c * CHUNK:(c + 1) * CHUNK, :] = (u[c * CHUNK:(c + 1) * CHUNK] * s).astype(BF16)


def _gmlp(proj, ws, bsm, nw, g64, t):
    m = proj.shape[0]
    return pl.pallas_call(
        _gmlp_kernel,
        grid=(m // t,),
        in_specs=[
            pl.BlockSpec((t, 2 * GROUP_W), lambda i: (i, COL_PA // (2 * GROUP_W))),
            pl.BlockSpec((N_HEADS, CHUNK, CHUNK), lambda i: (0, 0, 0)),
            pl.BlockSpec((CHUNK, GROUP_W), lambda i: (0, 0)),
            pl.BlockSpec((1, GROUP_W), lambda i: (0, 0)),
            pl.BlockSpec((GROUP_W, GROUP_W), lambda i: (0, 0)),
        ],
        out_specs=pl.BlockSpec((t, GROUP_W), lambda i: (i, 0)),
        out_shape=jax.ShapeDtypeStruct((m, GROUP_W), BF16),
        compiler_params=_cparams(("parallel",)),
        name="gmlp",
    )(proj, ws, bsm, nw, g64)


def _pool_kernel(pc_ref, wbd_ref, scale_ref, out_ref):
    h = pc_ref[...].astype(F32)
    shape = h.shape
    row = lax.broadcasted_iota(jnp.int32, shape, 0)

    def shifted(x, k):
        return jnp.where(row >= k, pltpu.roll(x, k, 0), 0.0)

    s2 = h + shifted(h, 1)
    s4 = s2 + shifted(s2, 2)
    s8 = s4 + shifted(s4, 4)
    s16 = s8 + shifted(s8, 8)
    gid = _head_id(shape, HEAD_DIM)
    wsum = jnp.where(gid == 0, s2, jnp.where(gid == 1, s4, jnp.where(gid == 2, s8, s16)))
    win = jnp.where(gid == 0, POOL_WINDOWS[0],
                    jnp.where(gid == 1, POOL_WINDOWS[1],
                              jnp.where(gid == 2, POOL_WINDOWS[2], POOL_WINDOWS[3])))
    cnt = jnp.minimum(win, row + 1).astype(F32)
    p = wsum / cnt - h
    y = _dot(p.astype(BF16), wbd_ref[...]) * scale_ref[...]
    out_ref[...] = y.astype(BF16)


def _pool(proj, wbd, scale, b, seq):
    m = proj.shape[0]
    return pl.pallas_call(
        _pool_kernel,
        grid=(b,),
        in_specs=[
            pl.BlockSpec((seq, GROUP_W), lambda i: (i, COL_PC // GROUP_W)),
            pl.BlockSpec((GROUP_W, GROUP_W), lambda i: (0, 0)),
            pl.BlockSpec((1, GROUP_W), lambda i: (0, 0)),
        ],
        out_specs=pl.BlockSpec((seq, GROUP_W), lambda i: (i, 0)),
        out_shape=jax.ShapeDtypeStruct((m, GROUP_W), BF16),
        compiler_params=_cparams(("parallel",)),
        name="pool",
    )(proj, wbd, scale)


def _split3(a):
    a1 = a.astype(BF16)
    r1 = a - a1.astype(F32)
    a2 = r1.astype(BF16)
    r2 = r1 - a2.astype(F32)
    return a1, a2, r2.astype(BF16)


def _ssd_kernel(z_ref, xbc_ref, dt_ref, cw_ref, cb_ref, dtb_ref, alog_ref, dexp_ref, nw_ref, g_ref,
                out_ref, st_ref, tail_ref):
    c = pl.program_id(1)

    @pl.when(c == 0)
    def _():
        st_ref[...] = jnp.zeros_like(st_ref)
        tail_ref[...] = jnp.zeros_like(tail_ref)

    xraw = xbc_ref[...].astype(F32)
    ext = jnp.concatenate([tail_ref[...], xraw], axis=0)
    tail_ref[...] = xraw[CHUNK - 8:CHUNK]
    acc = cb_ref[...] + cw_ref[0:1, :] * ext[5:5 + CHUNK]
    for k in range(1, SSM_CONV_K):
        acc = acc + cw_ref[k:k + 1, :] * ext[5 + k:5 + k + CHUNK]
    xc = acc * _sigmoid(acc)
    xs = xc[:, :GROUP_W]
    bm = xc[:, GROUP_W:2 * GROUP_W]
    cm = xc[:, 2 * GROUP_W:]

    t = dt_ref[...] + dtb_ref[...]
    dt = jnp.maximum(t, 0.0) + jnp.log(1.0 + jnp.exp(-jnp.abs(t)))
    a = dt * (-jnp.exp(alog_ref[...]))

    row = lax.broadcasted_iota(jnp.int32, (CHUNK, CHUNK), 0)
    col = lax.broadcasted_iota(jnp.int32, (CHUNK, CHUNK), 1)
    causal = row >= col
    ltri = jnp.where(causal, 1.0, 0.0).astype(BF16)
    a1, a2, a3 = _split3(a)
    acs = _dot(ltri, a1) + _dot(ltri, a2) + _dot(ltri, a3)
    acs_t = acs.T

    hid = _head_id((CHUNK, GROUP_W), HEAD_DIM)

    def expand(c4):
        return jnp.where(hid == 0, c4[:, 0:1],
                         jnp.where(hid == 1, c4[:, 1:2],
                                   jnp.where(hid == 2, c4[:, 2:3], c4[:, 3:4])))

    xdt = xs * expand(dt)
    eacs_e = expand(jnp.exp(acs))
    dte_e = expand(jnp.exp(acs[CHUNK - 1:CHUNK, :] - acs))
    xdt_b = xdt.astype(BF16)

    ms, yoff, bgs = [], [], []
    for g in range(SSM_GROUPS):
        bg = bm[:, g * SSM_D_STATE:(g + 1) * SSM_D_STATE]
        cg = cm[:, g * SSM_D_STATE:(g + 1) * SSM_D_STATE].astype(BF16)
        bgs.append(bg)
        cb = _dot_nt(cg, bg.astype(BF16))
        for hh in range(N_HEADS // SSM_GROUPS):
            h = g * (N_HEADS // SSM_GROUPS) + hh
            seg = acs[:, h:h + 1] - acs_t[h:h + 1, :]
            dec = jnp.exp(jnp.where(causal, seg, NEG_BIG))
            ms.append((cb * dec).astype(BF16))
        yoff.append(_dot(cg, st_ref[g].astype(BF16)))
    mcat = jnp.concatenate(ms, axis=1)
    xstack = jnp.concatenate(
        [jnp.where(hid == h, xdt_b, jnp.zeros_like(xdt_b)) for h in range(N_HEADS)], axis=0)
    y = _dot(mcat, xstack) + jnp.concatenate(yoff, axis=1) * eacs_e + dexp_ref[...] * xs

    w = (xdt * dte_e).astype(BF16)
    cdl = eacs_e[CHUNK - 1:CHUNK, :]
    for g in range(SSM_GROUPS):
        lo, hi = g * SSM_D_STATE, (g + 1) * SSM_D_STATE
        snew = _dot(bgs[g].T.astype(BF16), w[:, lo:hi])
        st_ref[g] = st_ref[g] * cdl[:, lo:hi] + snew

    z = z_ref[...].astype(F32)
    y = y * (z * _sigmoid(z))
    msq = _dot((y * y).astype(BF16), g_ref[...])
    out_ref[...] = (y * lax.rsqrt(msq + RMS_EPS) * nw_ref[...]).astype(BF16)


def _ssd(proj, dtraw, cw, cb, dtb, alog, dexp, nw, g128, b, seq):
    m = proj.shape[0]
    nc = seq // CHUNK
    const = lambda i, c: (0, 0)
    return pl.pallas_call(
        _ssd_kernel,
        grid=(b, nc),
        in_specs=[
            pl.BlockSpec((CHUNK, GROUP_W), lambda i, c: (i * nc + c, COL_Z // GROUP_W)),
            pl.BlockSpec((CHUNK, SSM_CONV_DIM), lambda i, c: (i * nc + c, COL_XBC // SSM_CONV_DIM)),
            pl.BlockSpec((CHUNK, DT_W), lambda i, c: (i * nc + c, 0)),
            pl.BlockSpec((SSM_CONV_K, SSM_CONV_DIM), const),
            pl.BlockSpec((1, SSM_CONV_DIM), const),
            pl.BlockSpec((1, DT_W), const),
            pl.BlockSpec((1, DT_W), const),
            pl.BlockSpec((1, GROUP_W), const),
            pl.BlockSpec((1, GROUP_W), const),
            pl.BlockSpec((GROUP_W, GROUP_W), const),
        ],
        out_specs=pl.BlockSpec((CHUNK, GROUP_W), lambda i, c: (i * nc + c, 0)),
        out_shape=jax.ShapeDtypeStruct((m, GROUP_W), BF16),
        scratch_shapes=[
            pltpu.VMEM((SSM_GROUPS, SSM_D_STATE, LANES), F32),
            pltpu.VMEM((8, SSM_CONV_DIM), F32),
        ],
        compiler_params=_cparams(("parallel", "arbitrary")),
        name="ssd",
    )(proj, proj, dtraw, cw, cb, dtb, alog, dexp, nw, g128)


def _alibi_slope(h):
    return 2.0 ** (-8.0 * (h + 1) / N_HEADS)


def _head_slot(x, h):
    base = x[:, (h // 2) * LANES:(h // 2 + 1) * LANES]
    return pltpu.roll(base, HEAD_DIM, 1) if h % 2 else base


def _attn_kernel(q_ref, k_ref, v_ref, qnw_ref, knw_ref, lq1_ref, lk1_ref, lq2_ref, lk2_ref, subw_ref,
                 g_ref, out_ref, ka_ref, vt_ref, qs_ref, acc_ref, s0_ref, *, lam_init):
    qi = pl.program_id(1)
    tq, tk = ATT_TQ, ATT_TK
    seq = k_ref.shape[0]

    def aug_lanes(pos, slope, key_side):
        lane = lax.broadcasted_iota(jnp.int32, pos.shape, 1)
        hi = (slope * LANES) * lax.shift_right_logical(pos, 7).astype(F32)
        lo = slope * (pos & (LANES - 1)).astype(F32)
        one = jnp.ones_like(hi)
        c = (one, one, hi, lo) if key_side else (-hi, -lo, one, one)
        return jnp.where(lane == 64, c[0], jnp.where(lane == 65, c[1],
                         jnp.where(lane == 66, c[2], jnp.where(lane == 67, c[3], 0.0))))

    @pl.when(qi == 0)
    def _():
        k = k_ref[...].astype(F32)
        ms = _dot((k * k).astype(BF16), g_ref[...])
        kn = k * lax.rsqrt(ms + RMS_EPS) * knw_ref[...]
        pos = lax.broadcasted_iota(jnp.int32, (seq, LANES), 0)
        lane = lax.broadcasted_iota(jnp.int32, (seq, LANES), 1)
        for h in range(N_HEADS):
            aug = aug_lanes(pos, _alibi_slope(h), True)
            ka_ref[h] = jnp.where(lane < HEAD_DIM, _head_slot(kn, h), aug).astype(BF16)
        for j in range(seq // tk):
            vt_ref[j] = v_ref[j * tk:(j + 1) * tk, :].astype(F32).T.astype(BF16)

    q = q_ref[...].astype(F32)
    ms = _dot((q * q).astype(BF16), g_ref[...])
    qn = q * lax.rsqrt(ms + RMS_EPS) * (qnw_ref[...] * (DA_QK_DIM ** -0.5))
    pos = qi * tq + lax.broadcasted_iota(jnp.int32, (tq, LANES), 0)
    lane = lax.broadcasted_iota(jnp.int32, (tq, LANES), 1)
    for h in range(N_HEADS):
        base = _head_slot(qn, h)
        aug = aug_lanes(pos, _alibi_slope(h), False)
        qs_ref[h, 0:tq, :] = jnp.where(lane < DA_QK_DIM, base,
                                       jnp.where(lane < HEAD_DIM, 0.0, aug)).astype(BF16)
        qs_ref[h, tq:2 * tq, :] = jnp.where(lane < DA_QK_DIM, 0.0,
                                            jnp.where(lane < HEAD_DIM, base, aug)).astype(BF16)
    acc_ref[...] = jnp.zeros_like(acc_ref)

    def scores(h, j):
        k0 = pl.multiple_of(j * tk, tk)
        return _dot_nt(ka_ref[h, pl.ds(k0, tk), :], qs_ref[h])

    def tile(j, ms_, ls_, masked):
        vt = vt_ref[j]
        new_m, new_l = [], []
        s_next = s0_ref[...]
        for h in range(N_HEADS):
            s = s_next
            if h + 1 < N_HEADS:
                s_next = scores(h + 1, j)
            elif not masked:
                s0_ref[...] = scores(0, j + 1)
            if masked:
                kk = lax.broadcasted_iota(jnp.int32, (tk, 2 * tq), 0)
                qq = lax.broadcasted_iota(jnp.int32, (tk, 2 * tq), 1) & (tq - 1)
                s = jnp.where(kk <= qq, s, NEG_BIG)
            m_new = jnp.maximum(ms_[h], jnp.max(s, axis=0, keepdims=True))
            alpha = jnp.exp(ms_[h] - m_new)
            p = jnp.exp(s - m_new)
            new_l.append(alpha * ls_[h] + jnp.sum(p, axis=0, keepdims=True))
            new_m.append(m_new)
            pv = _dot(vt[h * HEAD_DIM:(h + 1) * HEAD_DIM, :], p.astype(BF16))
            acc_ref[h] = acc_ref[h] * alpha + pv
        return tuple(new_m), tuple(new_l)

    s0_ref[...] = scores(0, 0)
    m0 = tuple(jnp.full((1, 2 * tq), NEG_BIG, F32) for _ in range(N_HEADS))
    l0 = tuple(jnp.zeros((1, 2 * tq), F32) for _ in range(N_HEADS))
    m1, l1 = lax.fori_loop(0, qi, lambda j, c: tile(j, c[0], c[1], False), (m0, l0))
    _, l2 = tile(qi, m1, l1, True)

    def lane_sum(x):
        return jnp.broadcast_to(jnp.sum(x, axis=1, keepdims=True), x.shape)

    lam = (jnp.exp(lane_sum(lq1_ref[...] * lk1_ref[...]))
           - jnp.exp(lane_sum(lq2_ref[...] * lk2_ref[...])) + lam_init)
    lam = jnp.concatenate([lam] * (tq // LANES), axis=1)
    outs = []
    for h in range(N_HEADS):
        o = acc_ref[h] / l2[h]
        oh = o[:, 0:tq] - lam * o[:, tq:2 * tq]
        ms = jnp.mean(oh * oh, axis=0, keepdims=True)
        outs.append(oh * lax.rsqrt(ms + RMS_EPS) * (subw_ref[...] * (1.0 - lam_init)))
    out_ref[...] = jnp.concatenate(outs, axis=0).T.astype(BF16)


def _attn(proj, qnw, knw, lq1, lk1, lq2, lk2, subw, g32, b, seq, lam_init):
    m = proj.shape[0]
    assert ATT_TQ == ATT_TK
    nq = seq // ATT_TQ
    const = lambda i, j: (0, 0)
    return pl.pallas_call(
        functools.partial(_attn_kernel, lam_init=lam_init),
        grid=(b, nq),
        in_specs=[
            pl.BlockSpec((ATT_TQ, GROUP_W), lambda i, j: (i * nq + j, COL_Q // GROUP_W)),
            pl.BlockSpec((seq, GROUP_W), lambda i, j: (i, COL_K // GROUP_W)),
            pl.BlockSpec((seq, GROUP_W), lambda i, j: (i, COL_V // GROUP_W)),
            pl.BlockSpec((1, GROUP_W), const),
            pl.BlockSpec((1, GROUP_W), const),
            pl.BlockSpec((1, LANES), const),
            pl.BlockSpec((1, LANES), const),
            pl.BlockSpec((1, LANES), const),
            pl.BlockSpec((1, LANES), const),
            pl.BlockSpec((HEAD_DIM, ATT_TQ), const),
            pl.BlockSpec((GROUP_W, GROUP_W), const),
        ],
        out_specs=pl.BlockSpec((ATT_TQ, GROUP_W), lambda i, j: (i * nq + j, 0)),
        out_shape=jax.ShapeDtypeStruct((m, GROUP_W), BF16),
        scratch_shapes=[
            pltpu.VMEM((N_HEADS, seq, LANES), BF16),
            pltpu.VMEM((seq // ATT_TK, GROUP_W, ATT_TK), BF16),
            pltpu.VMEM((N_HEADS, 2 * ATT_TQ, LANES), BF16),
            pltpu.VMEM((N_HEADS, HEAD_DIM, 2 * ATT_TQ), F32),
            pltpu.VMEM((ATT_TK, 2 * ATT_TQ), F32),
        ],
        compiler_params=_cparams(("parallel", "arbitrary")),
        name="diffattn",
    )(proj, proj, proj, qnw, knw, lq1, lk1, lq2, lk2, subw, g32)


def _ffn_kernel(x_ref, ya_ref, yb_ref, yc_ref, yd_ref, wo_ref, nw_ref, wg_ref, wu_ref, wd_ref,
                out_ref, act_ref):
    x1 = x_ref[...]
    for i, y_ref in enumerate((ya_ref, yb_ref, yc_ref, yd_ref)):
        x1 = x1 + _dot(y_ref[...], wo_ref[i * GROUP_W:(i + 1) * GROUP_W, :])
    ms = jnp.mean(x1 * x1, axis=-1, keepdims=True)
    h = (x1 * lax.rsqrt(ms + RMS_EPS) * nw_ref[...]).astype(BF16)
    for c0 in range(0, D_FF, FFN_CHUNK):
        c1 = min(c0 + FFN_CHUNK, D_FF)
        g = _dot(h, wg_ref[:, c0:c1])
        u = _dot(h, wu_ref[:, c0:c1])
        act_ref[:, c0:c1] = (g * _sigmoid(g) * u).astype(BF16)
    out_ref[...] = x1 + _dot(act_ref[...], wd_ref[...])


def _ffn(x2, ya, yb, yc, yd, wo, nw, wg, wu, wd, tm):
    m = x2.shape[0]
    row = lambda i: (i, 0)
    resident = lambda shape: pl.BlockSpec(shape, lambda i: (0, 0), pipeline_mode=pl.Buffered(1))
    return pl.pallas_call(
        _ffn_kernel,
        grid=(m // tm,),
        in_specs=[
            pl.BlockSpec((tm, D_MODEL), row),
            pl.BlockSpec((tm, GROUP_W), row),
            pl.BlockSpec((tm, GROUP_W), row),
            pl.BlockSpec((tm, GROUP_W), row),
            pl.BlockSpec((tm, GROUP_W), row),
            resident((D_MODEL, D_MODEL)),
            resident((1, D_MODEL)),
            resident((D_MODEL, D_FF)),
            resident((D_MODEL, D_FF)),
            resident((D_FF, D_MODEL)),
        ],
        out_specs=pl.BlockSpec((tm, D_MODEL), row),
        out_shape=jax.ShapeDtypeStruct((m, D_MODEL), F32),
        scratch_shapes=[pltpu.VMEM((tm, D_FF), BF16)],
        compiler_params=_cparams(("parallel",)),
        name="outproj_ffn",
    )(x2, ya, yb, yc, yd, wo, nw, wg, wu, wd)


def _block_diag_mean(width, group):
    idx = jnp.arange(width) // group
    return jnp.where(idx[:, None] == idx[None, :], 1.0 / group, 0.0).astype(BF16)


def _pad_lanes(v, width):
    v = v.reshape(1, -1).astype(F32)
    return jnp.pad(v, ((0, 0), (0, width - v.shape[1])))


def _reorder_w_in(w):
    o = 2 * GROUP_W
    z = w[:, o:o + GROUP_W]
    xbc = w[:, o + GROUP_W:o + GROUP_W + SSM_CONV_DIM]
    dt0 = o + GROUP_W + SSM_CONV_DIM
    dt = w[:, dt0:dt0 + N_HEADS]
    pc0 = dt0 + N_HEADS
    pc = w[:, pc0:pc0 + GROUP_W]
    q = w[:, pc0 + GROUP_W:pc0 + 2 * GROUP_W]
    k = w[:, pc0 + 2 * GROUP_W:pc0 + 3 * GROUP_W]
    v = w[:, pc0 + 3 * GROUP_W:pc0 + 4 * GROUP_W]
    dtpad = jnp.pad(dt, ((0, 0), (0, DT_W - N_HEADS)))
    return jnp.concatenate([w[:, :o], z, pc, q, k, xbc, v, dtpad], axis=1).astype(BF16)


def kernel(x, norm1_w, w_in, gm_norm_w, gm_ws, gm_bs, ssm_conv_w, ssm_conv_b, ssm_dt_bias, ssm_a_log, ssm_d, ssm_norm_w, pool_w, pool_scale, da_q_norm_w, da_k_norm_w, da_lambda_q1, da_lambda_k1, da_lambda_q2, da_lambda_k2, da_subln_w, w_out, norm2_w, ffn_w_gate, ffn_w_up, ffn_w_down):
    b, seq, d = x.shape
    depth = w_in.shape[0]
    assert d == D_MODEL and seq % ATT_TQ == 0 and seq % CHUNK == 0
    m = b * seq
    tm = 512 if m % 512 == 0 else ATT_TQ

    g64 = _block_diag_mean(GROUP_W, HEAD_DIM)
    g128 = _block_diag_mean(GROUP_W, SSM_D_STATE)
    g32 = _block_diag_mean(GROUP_W, DA_QK_DIM)

    x2 = x.reshape(m, d)
    for i in range(depth):
        proj, dtraw = _inproj(x2, norm1_w[i].reshape(1, d), _reorder_w_in(w_in[i]), tm)

        bsm = jnp.repeat(gm_bs[i].T, HEAD_DIM, axis=1)
        ya = _gmlp(proj, gm_ws[i], bsm, gm_norm_w[i].reshape(1, GROUP_W), g64, tm)

        yb = _ssd(proj, dtraw, ssm_conv_w[i].T, ssm_conv_b[i].reshape(1, -1),
                  _pad_lanes(ssm_dt_bias[i], DT_W), _pad_lanes(ssm_a_log[i], DT_W),
                  jnp.repeat(ssm_d[i], HEAD_DIM).reshape(1, GROUP_W),
                  ssm_norm_w[i].reshape(1, GROUP_W), g128, b, seq)

        wbd = jax.scipy.linalg.block_diag(*[pool_w[i, g] for g in range(len(POOL_WINDOWS))]).astype(BF16)
        yc = _pool(proj, wbd, pool_scale[i].reshape(1, GROUP_W), b, seq)

        lam_init = 0.8 - 0.6 * math.exp(-0.3 * i)
        yd = _attn(proj, jnp.tile(da_q_norm_w[i], GROUP_W // DA_QK_DIM).reshape(1, GROUP_W),
                   jnp.tile(da_k_norm_w[i], GROUP_W // DA_QK_DIM).reshape(1, GROUP_W),
                   _pad_lanes(da_lambda_q1[i], LANES), _pad_lanes(da_lambda_k1[i], LANES),
                   _pad_lanes(da_lambda_q2[i], LANES), _pad_lanes(da_lambda_k2[i], LANES),
                   jnp.broadcast_to(da_subln_w[i].reshape(HEAD_DIM, 1), (HEAD_DIM, ATT_TQ)),
                   g32, b, seq, lam_init)

        x2 = _ffn(x2, ya, yb, yc, yd, w_out[i].astype(BF16), norm2_w[i].reshape(1, d),
                  ffn_w_gate[i].astype(BF16), ffn_w_up[i].astype(BF16), ffn_w_down[i].astype(BF16),
                  tm)
    return x2.reshape(b, seq, d)
```

```python
import functools
import math

import jax
import jax.numpy as jnp
from jax import lax
from jax.experimental import pallas as pl
from jax.experimental.pallas import tpu as pltpu

F32 = jnp.float32
BF16 = jnp.bfloat16

D_MODEL = 1024
GROUP_W = 256
CHUNK = 128
HEAD_DIM = 64
N_HEADS = 4
SSM_GROUPS = 2
SSM_D_STATE = 128
SSM_CONV_K = 4
SSM_CONV_DIM = GROUP_W + 2 * SSM_GROUPS * SSM_D_STATE
POOL_WINDOWS = (2, 4, 8, 16)
DA_QK_DIM = 32
D_FF = 2816
RMS_EPS = 1e-6
NEG_BIG = -1e30

LANES = 128
VMEM_LIMIT = 56 * 1024 * 1024

COL_PA = 0
COL_Z = 512
COL_PC = 768
COL_Q = 1024
COL_K = 1280
COL_XBC = 1536
COL_V = 2304
PROJ_W = 2560
DT_W = LANES
IN_CHUNK = 256
FFN_CHUNK = 512

ATT_TQ = 512
ATT_TK = 512
VT_ROWS = 80


def _cparams(sem):
    return pltpu.CompilerParams(dimension_semantics=sem, vmem_limit_bytes=VMEM_LIMIT)


def _sigmoid(x):
    return 1.0 / (1.0 + jnp.exp(-x))


def _dot(a, b):
    return jnp.dot(a, b, preferred_element_type=F32)


def _dot_nt(a, b):
    return lax.dot_general(a, b, (((1,), (1,)), ((), ())), preferred_element_type=F32)


def _inproj_kernel(x_ref, nw_ref, w_ref, proj_ref, dt_ref):
    x = x_ref[...]
    ms = jnp.mean(x * x, axis=-1, keepdims=True)
    h = (x * lax.rsqrt(ms + RMS_EPS) * nw_ref[...]).astype(BF16)
    for n0 in range(0, PROJ_W, IN_CHUNK):
        proj_ref[:, n0:n0 + IN_CHUNK] = _dot(h, w_ref[:, n0:n0 + IN_CHUNK]).astype(BF16)
    dt_ref[...] = _dot(h, w_ref[:, PROJ_W:PROJ_W + DT_W])


def _inproj(x2, nw, w, tm):
    m = x2.shape[0]
    return pl.pallas_call(
        _inproj_kernel,
        grid=(m // tm,),
        in_specs=[
            pl.BlockSpec((tm, D_MODEL), lambda i: (i, 0)),
            pl.BlockSpec((1, D_MODEL), lambda i: (0, 0)),
            pl.BlockSpec((D_MODEL, PROJ_W + DT_W), lambda i: (0, 0)),
        ],
        out_specs=[
            pl.BlockSpec((tm, PROJ_W), lambda i: (i, 0)),
            pl.BlockSpec((tm, DT_W), lambda i: (i, 0)),
        ],
        out_shape=[
            jax.ShapeDtypeStruct((m, PROJ_W), BF16),
            jax.ShapeDtypeStruct((m, DT_W), F32),
        ],
        compiler_params=_cparams(("parallel",)),
        name="inproj",
    )(x2, nw, w)


def _head_id(shape, width):
    lane = lax.broadcasted_iota(jnp.int32, shape, 1)
    return lax.shift_right_logical(lane, int(math.log2(width)))


def _gmlp_kernel(pa_ref, ws_ref, bsm_ref, nw_ref, g_ref, out_ref):
    t = pa_ref.shape[0]
    hact = jax.nn.gelu(pa_ref[...].astype(F32), approximate=True)
    u = hact[:, :GROUP_W]
    v = hact[:, GROUP_W:]
    ms = _dot((v * v).astype(BF16), g_ref[...])
    vn = (v * lax.rsqrt(ms + RMS_EPS) * nw_ref[...]).astype(BF16)
    row = lax.broadcasted_iota(jnp.int32, (CHUNK, CHUNK), 0)
    col = lax.broadcasted_iota(jnp.int32, (CHUNK, CHUNK), 1)
    wcat = jnp.concatenate(
        [jnp.where(row >= col, ws_ref[h], 0.0) for h in range(N_HEADS)], axis=1).astype(BF16)
    hid = _head_id((CHUNK, GROUP_W), HEAD_DIM)
    for c in range(t // CHUNK):
        vc = vn[c * CHUNK:(c + 1) * CHUNK]
        vstack = jnp.concatenate(
            [jnp.where(hid == h, vc, jnp.zeros_like(vc)) for h in range(N_HEADS)], axis=0)
        s = _dot(wcat, vstack) + bsm_ref[...]
        out_ref[c * CHUNK:(c + 1) * CHUNK, :] = (u[c * CHUNK:(c + 1) * CHUNK] * s).astype(BF16)


def _gmlp(proj, ws, bsm, nw, g64, t):
    m = proj.shape[0]
    return pl.pallas_call(
        _gmlp_kernel,
        grid=(m // t,),
        in_specs=[
            pl.BlockSpec((t, 2 * GROUP_W), lambda i: (i, COL_PA // (2 * GROUP_W))),
            pl.BlockSpec((N_HEADS, CHUNK, CHUNK), lambda i: (0, 0, 0)),
            pl.BlockSpec((CHUNK, GROUP_W), lambda i: (0, 0)),
            pl.BlockSpec((1, GROUP_W), lambda i: (0, 0)),
            pl.BlockSpec((GROUP_W, GROUP_W), lambda i: (0, 0)),
        ],
        out_specs=pl.BlockSpec((t, GROUP_W), lambda i: (i, 0)),
        out_shape=jax.ShapeDtypeStruct((m, GROUP_W), BF16),
        compiler_params=_cparams(("parallel",)),
        name="gmlp",
    )(proj, ws, bsm, nw, g64)


def _pool_kernel(pc_ref, wbd_ref, scale_ref, out_ref):
    h = pc_ref[...].astype(F32)
    shape = h.shape
    row = lax.broadcasted_iota(jnp.int32, shape, 0)

    def shifted(x, k):
        return jnp.where(row >= k, pltpu.roll(x, k, 0), 0.0)

    s2 = h + shifted(h, 1)
    s4 = s2 + shifted(s2, 2)
    s8 = s4 + shifted(s4, 4)
    s16 = s8 + shifted(s8, 8)
    gid = _head_id(shape, HEAD_DIM)
    wsum = jnp.where(gid == 0, s2, jnp.where(gid == 1, s4, jnp.where(gid == 2, s8, s16)))
    win = jnp.where(gid == 0, POOL_WINDOWS[0],
                    jnp.where(gid == 1, POOL_WINDOWS[1],
                              jnp.where(gid == 2, POOL_WINDOWS[2], POOL_WINDOWS[3])))
    cnt = jnp.minimum(win, row + 1).astype(F32)
    p = wsum / cnt - h
    y = _dot(p.astype(BF16), wbd_ref[...]) * scale_ref[...]
    out_ref[...] = y.astype(BF16)


def _pool(proj, wbd, scale, b, seq):
    m = proj.shape[0]
    return pl.pallas_call(
        _pool_kernel,
        grid=(b,),
        in_specs=[
            pl.BlockSpec((seq, GROUP_W), lambda i: (i, COL_PC // GROUP_W)),
            pl.BlockSpec((GROUP_W, GROUP_W), lambda i: (0, 0)),
            pl.BlockSpec((1, GROUP_W), lambda i: (0, 0)),
        ],
        out_specs=pl.BlockSpec((seq, GROUP_W), lambda i: (i, 0)),
        out_shape=jax.ShapeDtypeStruct((m, GROUP_W), BF16),
        compiler_params=_cparams(("parallel",)),
        name="pool",
    )(proj, wbd, scale)


def _split3(a):
    a1 = a.astype(BF16)
    r1 = a - a1.astype(F32)
    a2 = r1.astype(BF16)
    r2 = r1 - a2.astype(F32)
    return a1, a2, r2.astype(BF16)


def _ssd_kernel(z_ref, xbc_ref, dt_ref, cw_ref, cb_ref, dtb_ref, alog_ref, dexp_ref, nw_ref, g_ref,
                out_ref, st_ref, tail_ref):
    c = pl.program_id(1)

    @pl.when(c == 0)
    def _():
        st_ref[...] = jnp.zeros_like(st_ref)
        tail_ref[...] = jnp.zeros_like(tail_ref)

    xraw = xbc_ref[...].astype(F32)
    ext = jnp.concatenate([tail_ref[...], xraw], axis=0)
    tail_ref[...] = xraw[CHUNK - 8:CHUNK]
    acc = cb_ref[...] + cw_ref[0:1, :] * ext[5:5 + CHUNK]
    for k in range(1, SSM_CONV_K):
        acc = acc + cw_ref[k:k + 1, :] * ext[5 + k:5 + k + CHUNK]
    xc = acc * _sigmoid(acc)
    xs = xc[:, :GROUP_W]
    bm = xc[:, GROUP_W:2 * GROUP_W]
    cm = xc[:, 2 * GROUP_W:]

    t = dt_ref[...] + dtb_ref[...]
    dt = jnp.maximum(t, 0.0) + jnp.log(1.0 + jnp.exp(-jnp.abs(t)))
    a = dt * (-jnp.exp(alog_ref[...]))

    row = lax.broadcasted_iota(jnp.int32, (CHUNK, CHUNK), 0)
    col = lax.broadcasted_iota(jnp.int32, (CHUNK, CHUNK), 1)
    causal = row >= col
    ltri = jnp.where(causal, 1.0, 0.0).astype(BF16)
    a1, a2, a3 = _split3(a)
    acs = _dot(ltri, a1) + _dot(ltri, a2) + _dot(ltri, a3)
    acs_t = acs.T

    hid = _head_id((CHUNK, GROUP_W), HEAD_DIM)

    def expand(c4):
        return jnp.where(hid == 0, c4[:, 0:1],
                         jnp.where(hid == 1, c4[:, 1:2],
                                   jnp.where(hid == 2, c4[:, 2:3], c4[:, 3:4])))

    xdt = xs * expand(dt)
    eacs_e = expand(jnp.exp(acs))
    dte_e = expand(jnp.exp(acs[CHUNK - 1:CHUNK, :] - acs))
    xdt_b = xdt.astype(BF16)

    ms, yoff, bgs = [], [], []
    for g in range(SSM_GROUPS):
        bg = bm[:, g * SSM_D_STATE:(g + 1) * SSM_D_STATE]
        cg = cm[:, g * SSM_D_STATE:(g + 1) * SSM_D_STATE].astype(BF16)
        bgs.append(bg)
        cb = _dot_nt(cg, bg.astype(BF16))
        for hh in range(N_HEADS // SSM_GROUPS):
            h = g * (N_HEADS // SSM_GROUPS) + hh
            seg = acs[:, h:h + 1] - acs_t[h:h + 1, :]
            dec = jnp.exp(jnp.where(causal, seg, NEG_BIG))
            ms.append((cb * dec).astype(BF16))
        yoff.append(_dot(cg, st_ref[g].astype(BF16)))
    mcat = jnp.concatenate(ms, axis=1)
    xstack = jnp.concatenate(
        [jnp.where(hid == h, xdt_b, jnp.zeros_like(xdt_b)) for h in range(N_HEADS)], axis=0)
    y = _dot(mcat, xstack) + jnp.concatenate(yoff, axis=1) * eacs_e + dexp_ref[...] * xs

    w = (xdt * dte_e).astype(BF16)
    cdl = eacs_e[CHUNK - 1:CHUNK, :]
    for g in range(SSM_GROUPS):
        lo, hi = g * SSM_D_STATE, (g + 1) * SSM_D_STATE
        snew = _dot(bgs[g].T.astype(BF16), w[:, lo:hi])
        st_ref[g] = st_ref[g] * cdl[:, lo:hi] + snew

    z = z_ref[...].astype(F32)
    y = y * (z * _sigmoid(z))
    msq = _dot((y * y).astype(BF16), g_ref[...])
    out_ref[...] = (y * lax.rsqrt(msq + RMS_EPS) * nw_ref[...]).astype(BF16)


def _ssd(proj, dtraw, cw, cb, dtb, alog, dexp, nw, g128, b, seq):
    m = proj.shape[0]
    nc = seq // CHUNK
    const = lambda i, c: (0, 0)
    return pl.pallas_call(
        _ssd_kernel,
        grid=(b, nc),
        in_specs=[
            pl.BlockSpec((CHUNK, GROUP_W), lambda i, c: (i * nc + c, COL_Z // GROUP_W)),
            pl.BlockSpec((CHUNK, SSM_CONV_DIM), lambda i, c: (i * nc + c, COL_XBC // SSM_CONV_DIM)),
            pl.BlockSpec((CHUNK, DT_W), lambda i, c: (i * nc + c, 0)),
            pl.BlockSpec((SSM_CONV_K, SSM_CONV_DIM), const),
            pl.BlockSpec((1, SSM_CONV_DIM), const),
            pl.BlockSpec((1, DT_W), const),
            pl.BlockSpec((1, DT_W), const),
            pl.BlockSpec((1, GROUP_W), const),
            pl.BlockSpec((1, GROUP_W), const),
            pl.BlockSpec((GROUP_W, GROUP_W), const),
        ],
        out_specs=pl.BlockSpec((CHUNK, GROUP_W), lambda i, c: (i * nc + c, 0)),
        out_shape=jax.ShapeDtypeStruct((m, GROUP_W), BF16),
        scratch_shapes=[
            pltpu.VMEM((SSM_GROUPS, SSM_D_STATE, LANES), F32),
            pltpu.VMEM((8, SSM_CONV_DIM), F32),
        ],
        compiler_params=_cparams(("parallel", "arbitrary")),
        name="ssd",
    )(proj, proj, dtraw, cw, cb, dtb, alog, dexp, nw, g128)


def _alibi_slope(h):
    return 2.0 ** (-8.0 * (h + 1) / N_HEADS)


def _head_slot(x, h):
    base = x[:, (h // 2) * LANES:(h // 2 + 1) * LANES]
    return pltpu.roll(base, HEAD_DIM, 1) if h % 2 else base


def _attn_kernel(q_ref, k_ref, v_ref, qnw_ref, knw_ref, lq1_ref, lk1_ref, lq2_ref, lk2_ref, subw_ref,
                 g_ref, out_ref, ka_ref, vt_ref, qs_ref, acc_ref, s0_ref, *, lam_init):
    qi = pl.program_id(1)
    tq, tk = ATT_TQ, ATT_TK
    seq = k_ref.shape[0]

    def aug_lanes(pos, slope, key_side):
        lane = lax.broadcasted_iota(jnp.int32, pos.shape, 1)
        hi = (slope * LANES) * lax.shift_right_logical(pos, 7).astype(F32)
        lo = slope * (pos & (LANES - 1)).astype(F32)
        one = jnp.ones_like(hi)
        c = (one, one, hi, lo) if key_side else (-hi, -lo, one, one)
        return jnp.where(lane == 64, c[0], jnp.where(lane == 65, c[1],
                         jnp.where(lane == 66, c[2], jnp.where(lane == 67, c[3], 0.0))))

    @pl.when(qi == 0)
    def _():
        k = k_ref[...].astype(F32)
        ms = _dot((k * k).astype(BF16), g_ref[...])
        kn = k * lax.rsqrt(ms + RMS_EPS) * knw_ref[...]
        pos = lax.broadcasted_iota(jnp.int32, (seq, LANES), 0)
        lane = lax.broadcasted_iota(jnp.int32, (seq, LANES), 1)
        for h in range(N_HEADS):
            aug = aug_lanes(pos, _alibi_slope(h), True)
            ka_ref[h] = jnp.where(lane < HEAD_DIM, _head_slot(kn, h), aug).astype(BF16)
        tail = (lax.broadcasted_iota(jnp.int32, (VT_ROWS - HEAD_DIM, tk), 0) == 0).astype(BF16)
        for j in range(seq // tk):
            vt = v_ref[j * tk:(j + 1) * tk, :].astype(F32).T.astype(BF16)
            for h in range(N_HEADS):
                vt_ref[j, h, 0:HEAD_DIM, :] = vt[h * HEAD_DIM:(h + 1) * HEAD_DIM, :]
                vt_ref[j, h, HEAD_DIM:VT_ROWS, :] = tail

    q = q_ref[...].astype(F32)
    ms = _dot((q * q).astype(BF16), g_ref[...])
    qn = q * lax.rsqrt(ms + RMS_EPS) * (qnw_ref[...] * (DA_QK_DIM ** -0.5))
    pos = qi * tq + lax.broadcasted_iota(jnp.int32, (tq, LANES), 0)
    lane = lax.broadcasted_iota(jnp.int32, (tq, LANES), 1)
    for h in range(N_HEADS):
        base = _head_slot(qn, h)
        aug = aug_lanes(pos, _alibi_slope(h), False)
        qs_ref[h, 0:tq, :] = jnp.where(lane < DA_QK_DIM, base,
                                       jnp.where(lane < HEAD_DIM, 0.0, aug)).astype(BF16)
        qs_ref[h, tq:2 * tq, :] = jnp.where(lane < DA_QK_DIM, 0.0,
                                            jnp.where(lane < HEAD_DIM, base, aug)).astype(BF16)
    acc_ref[...] = jnp.zeros_like(acc_ref)

    def scores(h, j):
        k0 = pl.multiple_of(j * tk, tk)
        return _dot_nt(ka_ref[h, pl.ds(k0, tk), :], qs_ref[h])

    def tile(j, ms_, masked):
        new_m = []
        s_next = s0_ref[...]
        for h in range(N_HEADS):
            s = s_next
            if h + 1 < N_HEADS:
                s_next = scores(h + 1, j)
            elif not masked:
                s0_ref[...] = scores(0, j + 1)
            if masked:
                kk = lax.broadcasted_iota(jnp.int32, (tk, 2 * tq), 0)
                qq = lax.broadcasted_iota(jnp.int32, (tk, 2 * tq), 1) & (tq - 1)
                s = jnp.where(kk <= qq, s, NEG_BIG)
            m_new = jnp.maximum(ms_[h], jnp.max(s, axis=0, keepdims=True))
            alpha = jnp.exp(ms_[h] - m_new)
            p = jnp.exp(s - m_new)
            new_m.append(m_new)
            acc_ref[h] = acc_ref[h] * alpha + _dot(vt_ref[j, h], p.astype(BF16))
        return tuple(new_m)

    s0_ref[...] = scores(0, 0)
    m0 = tuple(jnp.full((1, 2 * tq), NEG_BIG, F32) for _ in range(N_HEADS))
    m1 = lax.fori_loop(0, qi, lambda j, c: tile(j, c, False), m0)
    tile(qi, m1, True)

    def lane_sum(x):
        return jnp.broadcast_to(jnp.sum(x, axis=1, keepdims=True), x.shape)

    lam = (jnp.exp(lane_sum(lq1_ref[...] * lk1_ref[...]))
           - jnp.exp(lane_sum(lq2_ref[...] * lk2_ref[...])) + lam_init)
    lam = jnp.concatenate([lam] * (tq // LANES), axis=1)
    outs = []
    for h in range(N_HEADS):
        o = acc_ref[h, 0:HEAD_DIM, :] / acc_ref[h, HEAD_DIM:HEAD_DIM + 1, :]
        oh = o[:, 0:tq] - lam * o[:, tq:2 * tq]
        ms = jnp.mean(oh * oh, axis=0, keepdims=True)
        outs.append(oh * lax.rsqrt(ms + RMS_EPS) * (subw_ref[...] * (1.0 - lam_init)))
    out_ref[...] = jnp.concatenate(outs, axis=0).T.astype(BF16)


def _attn(proj, qnw, knw, lq1, lk1, lq2, lk2, subw, g32, b, seq, lam_init):
    m = proj.shape[0]
    assert ATT_TQ == ATT_TK
    nq = seq // ATT_TQ
    const = lambda i, j: (0, 0)
    return pl.pallas_call(
        functools.partial(_attn_kernel, lam_init=lam_init),
        grid=(b, nq),
        in_specs=[
            pl.BlockSpec((ATT_TQ, GROUP_W), lambda i, j: (i * nq + j, COL_Q // GROUP_W)),
            pl.BlockSpec((seq, GROUP_W), lambda i, j: (i, COL_K // GROUP_W)),
            pl.BlockSpec((seq, GROUP_W), lambda i, j: (i, COL_V // GROUP_W)),
            pl.BlockSpec((1, GROUP_W), const),
            pl.BlockSpec((1, GROUP_W), const),
            pl.BlockSpec((1, LANES), const),
            pl.BlockSpec((1, LANES), const),
            pl.BlockSpec((1, LANES), const),
            pl.BlockSpec((1, LANES), const),
            pl.BlockSpec((HEAD_DIM, ATT_TQ), const),
            pl.BlockSpec((GROUP_W, GROUP_W), const),
        ],
        out_specs=pl.BlockSpec((ATT_TQ, GROUP_W), lambda i, j: (i * nq + j, 0)),
        out_shape=jax.ShapeDtypeStruct((m, GROUP_W), BF16),
        scratch_shapes=[
            pltpu.VMEM((N_HEADS, seq, LANES), BF16),
            pltpu.VMEM((seq // ATT_TK, N_HEADS, VT_ROWS, ATT_TK), BF16),
            pltpu.VMEM((N_HEADS, 2 * ATT_TQ, LANES), BF16),
            pltpu.VMEM((N_HEADS, VT_ROWS, 2 * ATT_TQ), F32),
            pltpu.VMEM((ATT_TK, 2 * ATT_TQ), F32),
        ],
        compiler_params=_cparams(("parallel", "arbitrary")),
        name="diffattn",
    )(proj, proj, proj, qnw, knw, lq1, lk1, lq2, lk2, subw, g32)


def _ffn_kernel(x_ref, ya_ref, yb_ref, yc_ref, yd_ref, wo_ref, nw_ref, wg_ref, wu_ref, wd_ref,
                out_ref, act_ref):
    x1 = x_ref[...]
    for i, y_ref in enumerate((ya_ref, yb_ref, yc_ref, yd_ref)):
        x1 = x1 + _dot(y_ref[...], wo_ref[i * GROUP_W:(i + 1) * GROUP_W, :])
    ms = jnp.mean(x1 * x1, axis=-1, keepdims=True)
    h = (x1 * lax.rsqrt(ms + RMS_EPS) * nw_ref[...]).astype(BF16)
    for c0 in range(0, D_FF, FFN_CHUNK):
        c1 = min(c0 + FFN_CHUNK, D_FF)
        g = _dot(h, wg_ref[:, c0:c1])
        u = _dot(h, wu_ref[:, c0:c1])
        act_ref[:, c0:c1] = (g * _sigmoid(g) * u).astype(BF16)
    out_ref[...] = x1 + _dot(act_ref[...], wd_ref[...])


def _ffn(x2, ya, yb, yc, yd, wo, nw, wg, wu, wd, tm):
    m = x2.shape[0]
    row = lambda i: (i, 0)
    resident = lambda shape: pl.BlockSpec(shape, lambda i: (0, 0), pipeline_mode=pl.Buffered(1))
    return pl.pallas_call(
        _ffn_kernel,
        grid=(m // tm,),
        in_specs=[
            pl.BlockSpec((tm, D_MODEL), row),
            pl.BlockSpec((tm, GROUP_W), row),
            pl.BlockSpec((tm, GROUP_W), row),
            pl.BlockSpec((tm, GROUP_W), row),
            pl.BlockSpec((tm, GROUP_W), row),
            resident((D_MODEL, D_MODEL)),
            resident((1, D_MODEL)),
            resident((D_MODEL, D_FF)),
            resident((D_MODEL, D_FF)),
            resident((D_FF, D_MODEL)),
        ],
        out_specs=pl.BlockSpec((tm, D_MODEL), row),
        out_shape=jax.ShapeDtypeStruct((m, D_MODEL), F32),
        scratch_shapes=[pltpu.VMEM((tm, D_FF), BF16)],
        compiler_params=_cparams(("parallel",)),
        name="outproj_ffn",
    )(x2, ya, yb, yc, yd, wo, nw, wg, wu, wd)


def _block_diag_mean(width, group):
    idx = jnp.arange(width) // group
    return jnp.where(idx[:, None] == idx[None, :], 1.0 / group, 0.0).astype(BF16)


def _pad_lanes(v, width):
    v = v.reshape(1, -1).astype(F32)
    return jnp.pad(v, ((0, 0), (0, width - v.shape[1])))


def _reorder_w_in(w):
    o = 2 * GROUP_W
    z = w[:, o:o + GROUP_W]
    xbc = w[:, o + GROUP_W:o + GROUP_W + SSM_CONV_DIM]
    dt0 = o + GROUP_W + SSM_CONV_DIM
    dt = w[:, dt0:dt0 + N_HEADS]
    pc0 = dt0 + N_HEADS
    pc = w[:, pc0:pc0 + GROUP_W]
    q = w[:, pc0 + GROUP_W:pc0 + 2 * GROUP_W]
    k = w[:, pc0 + 2 * GROUP_W:pc0 + 3 * GROUP_W]
    v = w[:, pc0 + 3 * GROUP_W:pc0 + 4 * GROUP_W]
    dtpad = jnp.pad(dt, ((0, 0), (0, DT_W - N_HEADS)))
    return jnp.concatenate([w[:, :o], z, pc, q, k, xbc, v, dtpad], axis=1).astype(BF16)


def kernel(x, norm1_w, w_in, gm_norm_w, gm_ws, gm_bs, ssm_conv_w, ssm_conv_b, ssm_dt_bias, ssm_a_log, ssm_d, ssm_norm_w, pool_w, pool_scale, da_q_norm_w, da_k_norm_w, da_lambda_q1, da_lambda_k1, da_lambda_q2, da_lambda_k2, da_subln_w, w_out, norm2_w, ffn_w_gate, ffn_w_up, ffn_w_down):
    b, seq, d = x.shape
    depth = w_in.shape[0]
    assert d == D_MODEL and seq % ATT_TQ == 0 and seq % CHUNK == 0
    m = b * seq
    tm = 512 if m % 512 == 0 else ATT_TQ

    g64 = _block_diag_mean(GROUP_W, HEAD_DIM)
    g128 = _block_diag_mean(GROUP_W, SSM_D_STATE)
    g32 = _block_diag_mean(GROUP_W, DA_QK_DIM)

    x2 = x.reshape(m, d)
    for i in range(depth):
        proj, dtraw = _inproj(x2, norm1_w[i].reshape(1, d), _reorder_w_in(w_in[i]), tm)

        bsm = jnp.repeat(gm_bs[i].T, HEAD_DIM, axis=1)
        ya = _gmlp(proj, gm_ws[i], bsm, gm_norm_w[i].reshape(1, GROUP_W), g64, tm)

        yb = _ssd(proj, dtraw, ssm_conv_w[i].T, ssm_conv_b[i].reshape(1, -1),
                  _pad_lanes(ssm_dt_bias[i], DT_W), _pad_lanes(ssm_a_log[i], DT_W),
                  jnp.repeat(ssm_d[i], HEAD_DIM).reshape(1, GROUP_W),
                  ssm_norm_w[i].reshape(1, GROUP_W), g128, b, seq)

        wbd = jax.scipy.linalg.block_diag(*[pool_w[i, g] for g in range(len(POOL_WINDOWS))]).astype(BF16)
        yc = _pool(proj, wbd, pool_scale[i].reshape(1, GROUP_W), b, seq)

        lam_init = 0.8 - 0.6 * math.exp(-0.3 * i)
        yd = _attn(proj, jnp.tile(da_q_norm_w[i], GROUP_W // DA_QK_DIM).reshape(1, GROUP_W),
                   jnp.tile(da_k_norm_w[i], GROUP_W // DA_QK_DIM).reshape(1, GROUP_W),
                   _pad_lanes(da_lambda_q1[i], LANES), _pad_lanes(da_lambda_k1[i], LANES),
                   _pad_lanes(da_lambda_q2[i], LANES), _pad_lanes(da_lambda_k2[i], LANES),
                   jnp.broadcast_to(da_subln_w[i].reshape(HEAD_DIM, 1), (HEAD_DIM, ATT_TQ)),
                   g32, b, seq, lam_init)

        x2 = _ffn(x2, ya, yb, yc, yd, w_out[i].astype(BF16), norm2_w[i].reshape(1, d),
                  ffn_w_gate[i].astype(BF16), ffn_w_up[i].astype(BF16), ffn_w_down[i].astype(BF16),
                  tm)
    return x2.reshape(b, seq, d)
```

```python
import functools
import math

import jax
import jax.numpy as jnp
from jax import lax
from jax.experimental import pallas as pl
from jax.experimental.pallas import tpu as pltpu

F32 = jnp.float32
BF16 = jnp.bfloat16

D_MODEL = 1024
GROUP_W = 256
CHUNK = 128
HEAD_DIM = 64
N_HEADS = 4
SSM_GROUPS = 2
SSM_D_STATE = 128
SSM_CONV_K = 4
SSM_CONV_DIM = GROUP_W + 2 * SSM_GROUPS * SSM_D_STATE
POOL_WINDOWS = (2, 4, 8, 16)
DA_QK_DIM = 32
D_FF = 2816
RMS_EPS = 1e-6
NEG_BIG = -1e30

LANES = 128
VMEM_LIMIT = 56 * 1024 * 1024

COL_PA = 0
COL_Z = 512
COL_XBC = 768
COL_PC = 1536
COL_Q = 1792
COL_K = 2048
COL_V = 2304
PROJ_W = 2560
DT_W = LANES
D_IN = PROJ_W + N_HEADS
DT_COL = COL_PC
IN_CHUNK = 256
FFN_CHUNK = 512
SSD_CPS = 4

ATT_TQ = 512
ATT_TK = 512
VT_ROWS = 80


def _cparams(sem):
    return pltpu.CompilerParams(dimension_semantics=sem, vmem_limit_bytes=VMEM_LIMIT)


def _sigmoid(x):
    return 1.0 / (1.0 + jnp.exp(-x))


def _dot(a, b):
    return jnp.dot(a, b, preferred_element_type=F32)


def _dot_nt(a, b):
    return lax.dot_general(a, b, (((1,), (1,)), ((), ())), preferred_element_type=F32)


def _softplus(t):
    return jnp.maximum(t, 0.0) + jnp.log(1.0 + jnp.exp(-jnp.abs(t)))


def _inproj_kernel(x_ref, nw_ref, w_ref, dtb_ref, proj_ref, dt_ref, wb_ref):
    @pl.when(pl.program_id(0) == 0)
    def _():
        for n0 in range(0, DT_COL, 2 * IN_CHUNK):
            wb_ref[:, n0:n0 + 2 * IN_CHUNK] = w_ref[:, n0:n0 + 2 * IN_CHUNK].astype(BF16)
        tail = w_ref[:, DT_COL:D_IN]
        wb_ref[:, DT_COL:PROJ_W] = tail[:, N_HEADS:].astype(BF16)
        wb_ref[:, PROJ_W:PROJ_W + DT_W] = tail[:, 0:DT_W].astype(BF16)

    x = x_ref[...]
    ms = jnp.mean(x * x, axis=-1, keepdims=True)
    h = (x * lax.rsqrt(ms + RMS_EPS) * nw_ref[...]).astype(BF16)
    for n0 in range(0, PROJ_W, IN_CHUNK):
        y = _dot(h, wb_ref[:, n0:n0 + IN_CHUNK])
        if n0 < COL_Z:
            y = jax.nn.gelu(y, approximate=True)
        elif n0 < COL_XBC:
            y = y * _sigmoid(y)
        proj_ref[:, n0:n0 + IN_CHUNK] = y.astype(BF16)
    dt_ref[...] = _softplus(_dot(h, wb_ref[:, PROJ_W:PROJ_W + DT_W]) + dtb_ref[...])


def _inproj(x2, nw, w_in, layer, dtb, tm):
    m = x2.shape[0]
    return pl.pallas_call(
        _inproj_kernel,
        grid=(m // tm,),
        in_specs=[
            pl.BlockSpec((tm, D_MODEL), lambda i: (i, 0)),
            pl.BlockSpec((1, D_MODEL), lambda i: (0, 0)),
            pl.BlockSpec((None, D_MODEL, D_IN), lambda i: (layer, 0, 0), pipeline_mode=pl.Buffered(1)),
            pl.BlockSpec((1, DT_W), lambda i: (0, 0)),
        ],
        out_specs=[
            pl.BlockSpec((tm, PROJ_W), lambda i: (i, 0)),
            pl.BlockSpec((tm, DT_W), lambda i: (i, 0)),
        ],
        out_shape=[
            jax.ShapeDtypeStruct((m, PROJ_W), BF16),
            jax.ShapeDtypeStruct((m, DT_W), F32),
        ],
        scratch_shapes=[pltpu.VMEM((D_MODEL, PROJ_W + DT_W), BF16)],
        compiler_params=_cparams(("arbitrary",)),
        name="inproj",
    )(x2, nw, w_in, dtb)


def _head_id(shape, width):
    lane = lax.broadcasted_iota(jnp.int32, shape, 1)
    return lax.shift_right_logical(lane, int(math.log2(width)))


def _gmlp_kernel(pa_ref, ws_ref, bsm_ref, nw_ref, g_ref, out_ref):
    t = pa_ref.shape[0]
    hact = pa_ref[...].astype(F32)
    u = hact[:, :GROUP_W]
    v = hact[:, GROUP_W:]
    ms = _dot((v * v).astype(BF16), g_ref[...])
    vn = (v * lax.rsqrt(ms + RMS_EPS) * nw_ref[...]).astype(BF16)
    row = lax.broadcasted_iota(jnp.int32, (CHUNK, CHUNK), 0)
    col = lax.broadcasted_iota(jnp.int32, (CHUNK, CHUNK), 1)
    wcat = jnp.concatenate(
        [jnp.where(row >= col, ws_ref[h], 0.0) for h in range(N_HEADS)], axis=1).astype(BF16)
    hid = _head_id((CHUNK, GROUP_W), HEAD_DIM)
    for c in range(t // CHUNK):
        vc = vn[c * CHUNK:(c + 1) * CHUNK]
        vstack = jnp.concatenate(
            [jnp.where(hid == h, vc, jnp.zeros_like(vc)) for h in range(N_HEADS)], axis=0)
        s = _dot(wcat, vstack) + bsm_ref[...]
        out_ref[c * CHUNK:(c + 1) * CHUNK, :] = (u[c * CHUNK:(c + 1) * CHUNK] * s).astype(BF16)


def _gmlp(proj, ws, bsm, nw, g64, t):
    m = proj.shape[0]
    return pl.pallas_call(
        _gmlp_kernel,
        grid=(m // t,),
        in_specs=[
            pl.BlockSpec((t, 2 * GROUP_W), lambda i: (i, COL_PA // (2 * GROUP_W))),
            pl.BlockSpec((N_HEADS, CHUNK, CHUNK), lambda i: (0, 0, 0)),
            pl.BlockSpec((CHUNK, GROUP_W), lambda i: (0, 0)),
            pl.BlockSpec((1, GROUP_W), lambda i: (0, 0)),
            pl.BlockSpec((GROUP_W, GROUP_W), lambda i: (0, 0)),
        ],
        out_specs=pl.BlockSpec((t, GROUP_W), lambda i: (i, 0)),
        out_shape=jax.ShapeDtypeStruct((m, GROUP_W), BF16),
        compiler_params=_cparams(("parallel",)),
        name="gmlp",
    )(proj, ws, bsm, nw, g64)


def _pool_kernel(pc_ref, wbd_ref, scale_ref, out_ref):
    h = pc_ref[...].astype(F32)
    shape = h.shape
    row = lax.broadcasted_iota(jnp.int32, shape, 0)

    def shifted(x, k):
        return jnp.where(row >= k, pltpu.roll(x, k, 0), 0.0)

    s2 = h + shifted(h, 1)
    s4 = s2 + shifted(s2, 2)
    s8 = s4 + shifted(s4, 4)
    s16 = s8 + shifted(s8, 8)
    gid = _head_id(shape, HEAD_DIM)
    wsum = jnp.where(gid == 0, s2, jnp.where(gid == 1, s4, jnp.where(gid == 2, s8, s16)))
    win = jnp.where(gid == 0, POOL_WINDOWS[0],
                    jnp.where(gid == 1, POOL_WINDOWS[1],
                              jnp.where(gid == 2, POOL_WINDOWS[2], POOL_WINDOWS[3])))
    cnt = jnp.minimum(win, row + 1).astype(F32)
    p = wsum / cnt - h
    y = _dot(p.astype(BF16), wbd_ref[...]) * scale_ref[...]
    out_ref[...] = y.astype(BF16)


def _pool(proj, wbd, scale, b, seq):
    m = proj.shape[0]
    return pl.pallas_call(
        _pool_kernel,
        grid=(b,),
        in_specs=[
            pl.BlockSpec((seq, GROUP_W), lambda i: (i, COL_PC // GROUP_W)),
            pl.BlockSpec((GROUP_W, GROUP_W), lambda i: (0, 0)),
            pl.BlockSpec((1, GROUP_W), lambda i: (0, 0)),
        ],
        out_specs=pl.BlockSpec((seq, GROUP_W), lambda i: (i, 0)),
        out_shape=jax.ShapeDtypeStruct((m, GROUP_W), BF16),
        compiler_params=_cparams(("parallel",)),
        name="pool",
    )(proj, wbd, scale)


def _split3(a):
    a1 = a.astype(BF16)
    r1 = a - a1.astype(F32)
    a2 = r1.astype(BF16)
    r2 = r1 - a2.astype(F32)
    return a1, a2, r2.astype(BF16)


def _ssd_kernel(z_ref, xbc_ref, halo_ref, dt_ref, cw_ref, cb_ref, shift_ref, alog_ref, dexp_ref, nw_ref,
                g_ref, out_ref, st_ref):
    c = pl.program_id(1)

    @pl.when(c == 0)
    def _():
        st_ref[...] = jnp.zeros_like(st_ref)

    row = lax.broadcasted_iota(jnp.int32, (CHUNK, CHUNK), 0)
    col = lax.broadcasted_iota(jnp.int32, (CHUNK, CHUNK), 1)
    causal = row >= col
    ltri = jnp.where(causal, 1.0, 0.0).astype(BF16)
    hid = _head_id((CHUNK, GROUP_W), HEAD_DIM)
    neg_a = -jnp.exp(alog_ref[...])

    def expand(c4):
        return jnp.where(hid == 0, c4[:, 0:1],
                         jnp.where(hid == 1, c4[:, 1:2],
                                   jnp.where(hid == 2, c4[:, 2:3], c4[:, 3:4])))

    chunks = range(SSD_CPS)
    rows = [slice(ck * CHUNK, (ck + 1) * CHUNK) for ck in chunks]
    groups = [slice(g * SSM_D_STATE, (g + 1) * SSM_D_STATE) for g in range(SSM_GROUPS)]

    xs, bm, cm = [], [], []
    halo = jnp.where(c > 0, halo_ref[...], jnp.zeros_like(halo_ref))
    for ck in chunks:
        xcur = xbc_ref[rows[ck], :]
        acc = cb_ref[...] + cw_ref[SSM_CONV_K - 1:SSM_CONV_K, :] * xcur.astype(F32)
        for j in range(1, SSM_CONV_K):
            k = SSM_CONV_K - 1 - j
            if ck == 0:
                shifted = (_dot(shift_ref[j - 1, :, 0:CHUNK], halo)
                           + _dot(shift_ref[j - 1, :, CHUNK:2 * CHUNK], xcur))
            else:
                shifted = _dot(shift_ref[j - 1], xbc_ref[(ck - 1) * CHUNK:(ck + 1) * CHUNK, :])
            acc = acc + cw_ref[k:k + 1, :] * shifted
        xc = acc * _sigmoid(acc)
        xs.append(xc[:, :GROUP_W])
        bm.append(xc[:, GROUP_W:2 * GROUP_W])
        cm.append(xc[:, 2 * GROUP_W:].astype(BF16))

    acs, acs_t = [], []
    for ck in chunks:
        a1, a2, a3 = _split3(dt_ref[rows[ck], :] * neg_a)
        acs.append(_dot(ltri, a1) + _dot(ltri, a2) + _dot(ltri, a3))
        acs_t.append(acs[ck].T)

    xdt, eacs_e, mcat, snew = [], [], [], []
    for ck in chunks:
        xdt.append(xs[ck] * expand(dt_ref[rows[ck], :]))
        eacs_e.append(expand(jnp.exp(acs[ck])))
        dte_e = expand(jnp.exp(acs[ck][CHUNK - 1:CHUNK, :] - acs[ck]))
        ms = []
        for g in range(SSM_GROUPS):
            cb = _dot_nt(cm[ck][:, groups[g]], bm[ck][:, groups[g]].astype(BF16))
            for hh in range(N_HEADS // SSM_GROUPS):
                h = g * (N_HEADS // SSM_GROUPS) + hh
                seg = acs[ck][:, h:h + 1] - acs_t[ck][h:h + 1, :]
                ms.append((cb * jnp.exp(jnp.where(causal, seg, NEG_BIG))).astype(BF16))
        mcat.append(jnp.concatenate(ms, axis=1))
        w = (xdt[ck] * dte_e).astype(BF16)
        snew.append([_dot(bm[ck][:, groups[g]].T.astype(BF16), w[:, groups[g]])
                     for g in range(SSM_GROUPS)])

    state = [st_ref[g] for g in range(SSM_GROUPS)]
    entering = []
    for ck in chunks:
        entering.append([st.astype(BF16) for st in state])
        cdl = eacs_e[ck][CHUNK - 1:CHUNK, :]
        state = [state[g] * cdl[:, groups[g]] + snew[ck][g] for g in range(SSM_GROUPS)]
    for g in range(SSM_GROUPS):
        st_ref[g] = state[g]

    ys = []
    for ck in chunks:
        xdt_b = xdt[ck].astype(BF16)
        xstack = jnp.concatenate(
            [jnp.where(hid == h, xdt_b, jnp.zeros_like(xdt_b)) for h in range(N_HEADS)], axis=0)
        yoff = jnp.concatenate([_dot(cm[ck][:, groups[g]], entering[ck][g]) for g in range(SSM_GROUPS)], axis=1)
        y = _dot(mcat[ck], xstack) + yoff * eacs_e[ck] + dexp_ref[...] * xs[ck]
        ys.append(y * z_ref[rows[ck], :].astype(F32))

    for ck in chunks:
        msq = _dot((ys[ck] * ys[ck]).astype(BF16), g_ref[...])
        out_ref[rows[ck], :] = (ys[ck] * lax.rsqrt(msq + RMS_EPS) * nw_ref[...]).astype(BF16)


def _conv_shift_matrices():
    r = jnp.arange(CHUNK)[:, None]
    col = jnp.arange(2 * CHUNK)[None, :]
    return jnp.stack([(col == CHUNK + r - j) for j in range(1, SSM_CONV_K)]).astype(BF16)


def _ssd(proj, dt, cw, cb, shift, alog, dexp, nw, g128, b, seq):
    m = proj.shape[0]
    ts = SSD_CPS * CHUNK
    assert seq % ts == 0
    nc = seq // ts
    const = lambda i, c: (0, 0)
    return pl.pallas_call(
        _ssd_kernel,
        grid=(b, nc),
        in_specs=[
            pl.BlockSpec((ts, GROUP_W), lambda i, c: (i * nc + c, COL_Z // GROUP_W)),
            pl.BlockSpec((ts, SSM_CONV_DIM), lambda i, c: (i * nc + c, COL_XBC // SSM_CONV_DIM)),
            pl.BlockSpec((CHUNK, SSM_CONV_DIM),
                         lambda i, c: (jnp.maximum((i * nc + c) * SSD_CPS - 1, 0), COL_XBC // SSM_CONV_DIM)),
            pl.BlockSpec((ts, DT_W), lambda i, c: (i * nc + c, 0)),
            pl.BlockSpec((SSM_CONV_K, SSM_CONV_DIM), const),
            pl.BlockSpec((1, SSM_CONV_DIM), const),
            pl.BlockSpec((SSM_CONV_K - 1, CHUNK, 2 * CHUNK), lambda i, c: (0, 0, 0)),
            pl.BlockSpec((1, DT_W), const),
            pl.BlockSpec((1, GROUP_W), const),
            pl.BlockSpec((1, GROUP_W), const),
            pl.BlockSpec((GROUP_W, GROUP_W), const),
        ],
        out_specs=pl.BlockSpec((ts, GROUP_W), lambda i, c: (i * nc + c, 0)),
        out_shape=jax.ShapeDtypeStruct((m, GROUP_W), BF16),
        scratch_shapes=[pltpu.VMEM((SSM_GROUPS, SSM_D_STATE, LANES), F32)],
        compiler_params=_cparams(("parallel", "arbitrary")),
        name="ssd",
    )(proj, proj, proj, dt, cw, cb, shift, alog, dexp, nw, g128)


def _alibi_slope(h):
    return 2.0 ** (-8.0 * (h + 1) / N_HEADS)


def _head_slot(x, h):
    base = x[:, (h // 2) * LANES:(h // 2 + 1) * LANES]
    return pltpu.roll(base, HEAD_DIM, 1) if h % 2 else base


def _attn_kernel(q_ref, k_ref, v_ref, qnw_ref, knw_ref, lq1_ref, lk1_ref, lq2_ref, lk2_ref, subw_ref,
                 g_ref, out_ref, ka_ref, vt_ref, qs_ref, acc_ref, s0_ref, *, lam_init):
    qi = pl.program_id(1)
    tq, tk = ATT_TQ, ATT_TK
    seq = k_ref.shape[0]

    def aug_lanes(pos, slope, key_side):
        lane = lax.broadcasted_iota(jnp.int32, pos.shape, 1)
        hi = (slope * LANES) * lax.shift_right_logical(pos, 7).astype(F32)
        lo = slope * (pos & (LANES - 1)).astype(F32)
        one = jnp.ones_like(hi)
        c = (one, one, hi, lo) if key_side else (-hi, -lo, one, one)
        return jnp.where(lane == 64, c[0], jnp.where(lane == 65, c[1],
                         jnp.where(lane == 66, c[2], jnp.where(lane == 67, c[3], 0.0))))

    @pl.when(qi == 0)
    def _():
        k = k_ref[...].astype(F32)
        ms = _dot((k * k).astype(BF16), g_ref[...])
        kn = k * lax.rsqrt(ms + RMS_EPS) * knw_ref[...]
        pos = lax.broadcasted_iota(jnp.int32, (seq, LANES), 0)
        lane = lax.broadcasted_iota(jnp.int32, (seq, LANES), 1)
        for h in range(N_HEADS):
            aug = aug_lanes(pos, _alibi_slope(h), True)
            ka_ref[h] = jnp.where(lane < HEAD_DIM, _head_slot(kn, h), aug).astype(BF16)
        tail = (lax.broadcasted_iota(jnp.int32, (VT_ROWS - HEAD_DIM, tk), 0) == 0).astype(BF16)
        for j in range(seq // tk):
            vt = v_ref[j * tk:(j + 1) * tk, :].astype(F32).T.astype(BF16)
            for h in range(N_HEADS):
                vt_ref[j, h, 0:HEAD_DIM, :] = vt[h * HEAD_DIM:(h + 1) * HEAD_DIM, :]
                vt_ref[j, h, HEAD_DIM:VT_ROWS, :] = tail

    q = q_ref[...].astype(F32)
    ms = _dot((q * q).astype(BF16), g_ref[...])
    qn = q * lax.rsqrt(ms + RMS_EPS) * (qnw_ref[...] * (DA_QK_DIM ** -0.5))
    pos = qi * tq + lax.broadcasted_iota(jnp.int32, (tq, LANES), 0)
    lane = lax.broadcasted_iota(jnp.int32, (tq, LANES), 1)
    for h in range(N_HEADS):
        base = _head_slot(qn, h)
        aug = aug_lanes(pos, _alibi_slope(h), False)
        qs_ref[h, 0:tq, :] = jnp.where(lane < DA_QK_DIM, base,
                                       jnp.where(lane < HEAD_DIM, 0.0, aug)).astype(BF16)
        qs_ref[h, tq:2 * tq, :] = jnp.where(lane < DA_QK_DIM, 0.0,
                                            jnp.where(lane < HEAD_DIM, base, aug)).astype(BF16)
    acc_ref[...] = jnp.zeros_like(acc_ref)

    def scores(h, j):
        k0 = pl.multiple_of(j * tk, tk)
        return _dot_nt(ka_ref[h, pl.ds(k0, tk), :], qs_ref[h])

    def tile(j, ms_, masked):
        new_m = []
        s_next = s0_ref[...]
        for h in range(N_HEADS):
            s = s_next
            if h + 1 < N_HEADS:
                s_next = scores(h + 1, j)
            elif not masked:
                s0_ref[...] = scores(0, j + 1)
            if masked:
                kk = lax.broadcasted_iota(jnp.int32, (tk, 2 * tq), 0)
                qq = lax.broadcasted_iota(jnp.int32, (tk, 2 * tq), 1) & (tq - 1)
                s = jnp.where(kk <= qq, s, NEG_BIG)
            m_new = jnp.maximum(ms_[h], jnp.max(s, axis=0, keepdims=True))
            alpha = jnp.exp(ms_[h] - m_new)
            p = jnp.exp(s - m_new)
            new_m.append(m_new)
            acc_ref[h] = acc_ref[h] * alpha + _dot(vt_ref[j, h], p.astype(BF16))
        return tuple(new_m)

    s0_ref[...] = scores(0, 0)
    m0 = tuple(jnp.full((1, 2 * tq), NEG_BIG, F32) for _ in range(N_HEADS))
    m1 = lax.fori_loop(0, qi, lambda j, c: tile(j, c, False), m0)
    tile(qi, m1, True)

    def lane_sum(x):
        return jnp.broadcast_to(jnp.sum(x, axis=1, keepdims=True), x.shape)

    lam = (jnp.exp(lane_sum(lq1_ref[...] * lk1_ref[...]))
           - jnp.exp(lane_sum(lq2_ref[...] * lk2_ref[...])) + lam_init)
    lam = jnp.concatenate([lam] * (tq // LANES), axis=1)
    outs = []
    for h in range(N_HEADS):
        o = acc_ref[h, 0:HEAD_DIM, :] / acc_ref[h, HEAD_DIM:HEAD_DIM + 1, :]
        oh = o[:, 0:tq] - lam * o[:, tq:2 * tq]
        ms = jnp.mean(oh * oh, axis=0, keepdims=True)
        outs.append(oh * lax.rsqrt(ms + RMS_EPS) * (subw_ref[...] * (1.0 - lam_init)))
    out_ref[...] = jnp.concatenate(outs, axis=0).T.astype(BF16)


def _attn(proj, qnw, knw, lq1, lk1, lq2, lk2, subw, g32, b, seq, lam_init):
    m = proj.shape[0]
    assert ATT_TQ == ATT_TK
    nq = seq // ATT_TQ
    const = lambda i, j: (0, 0)
    return pl.pallas_call(
        functools.partial(_attn_kernel, lam_init=lam_init),
        grid=(b, nq),
        in_specs=[
            pl.BlockSpec((ATT_TQ, GROUP_W), lambda i, j: (i * nq + j, COL_Q // GROUP_W)),
            pl.BlockSpec((seq, GROUP_W), lambda i, j: (i, COL_K // GROUP_W)),
            pl.BlockSpec((seq, GROUP_W), lambda i, j: (i, COL_V // GROUP_W)),
            pl.BlockSpec((1, GROUP_W), const),
            pl.BlockSpec((1, GROUP_W), const),
            pl.BlockSpec((1, LANES), const),
            pl.BlockSpec((1, LANES), const),
            pl.BlockSpec((1, LANES), const),
            pl.BlockSpec((1, LANES), const),
            pl.BlockSpec((HEAD_DIM, ATT_TQ), const),
            pl.BlockSpec((GROUP_W, GROUP_W), const),
        ],
        out_specs=pl.BlockSpec((ATT_TQ, GROUP_W), lambda i, j: (i * nq + j, 0)),
        out_shape=jax.ShapeDtypeStruct((m, GROUP_W), BF16),
        scratch_shapes=[
            pltpu.VMEM((N_HEADS, seq, LANES), BF16),
            pltpu.VMEM((seq // ATT_TK, N_HEADS, VT_ROWS, ATT_TK), BF16),
            pltpu.VMEM((N_HEADS, 2 * ATT_TQ, LANES), BF16),
            pltpu.VMEM((N_HEADS, VT_ROWS, 2 * ATT_TQ), F32),
            pltpu.VMEM((ATT_TK, 2 * ATT_TQ), F32),
        ],
        compiler_params=_cparams(("parallel", "arbitrary")),
        name="diffattn",
    )(proj, proj, proj, qnw, knw, lq1, lk1, lq2, lk2, subw, g32)


def _ffn_kernel(x_ref, ya_ref, yb_ref, yc_ref, yd_ref, wo_ref, nw_ref, wg_ref, wu_ref, wd_ref,
                out_ref, act_ref):
    x1 = x_ref[...]
    for i, y_ref in enumerate((ya_ref, yb_ref, yc_ref, yd_ref)):
        x1 = x1 + _dot(y_ref[...], wo_ref[i * GROUP_W:(i + 1) * GROUP_W, :])
    ms = jnp.mean(x1 * x1, axis=-1, keepdims=True)
    h = (x1 * lax.rsqrt(ms + RMS_EPS) * nw_ref[...]).astype(BF16)
    for c0 in range(0, D_FF, FFN_CHUNK):
        c1 = min(c0 + FFN_CHUNK, D_FF)
        g = _dot(h, wg_ref[:, c0:c1])
        u = _dot(h, wu_ref[:, c0:c1])
        act_ref[:, c0:c1] = (g * _sigmoid(g) * u).astype(BF16)
    out_ref[...] = x1 + _dot(act_ref[...], wd_ref[...])


def _ffn(x2, ya, yb, yc, yd, wo, nw, wg, wu, wd, tm):
    m = x2.shape[0]
    row = lambda i: (i, 0)
    resident = lambda shape: pl.BlockSpec(shape, lambda i: (0, 0), pipeline_mode=pl.Buffered(1))
    return pl.pallas_call(
        _ffn_kernel,
        grid=(m // tm,),
        in_specs=[
            pl.BlockSpec((tm, D_MODEL), row),
            pl.BlockSpec((tm, GROUP_W), row),
            pl.BlockSpec((tm, GROUP_W), row),
            pl.BlockSpec((tm, GROUP_W), row),
            pl.BlockSpec((tm, GROUP_W), row),
            resident((D_MODEL, D_MODEL)),
            resident((1, D_MODEL)),
            resident((D_MODEL, D_FF)),
            resident((D_MODEL, D_FF)),
            resident((D_FF, D_MODEL)),
        ],
        out_specs=pl.BlockSpec((tm, D_MODEL), row),
        out_shape=jax.ShapeDtypeStruct((m, D_MODEL), F32),
        scratch_shapes=[pltpu.VMEM((tm, D_FF), BF16)],
        compiler_params=_cparams(("parallel",)),
        name="outproj_ffn",
    )(x2, ya, yb, yc, yd, wo, nw, wg, wu, wd)


def _block_diag_mean(width, group):
    idx = jnp.arange(width) // group
    return jnp.where(idx[:, None] == idx[None, :], 1.0 / group, 0.0).astype(BF16)


def _pad_lanes(v, width):
    v = v.reshape(1, -1).astype(F32)
    return jnp.pad(v, ((0, 0), (0, width - v.shape[1])))


def kernel(x, norm1_w, w_in, gm_norm_w, gm_ws, gm_bs, ssm_conv_w, ssm_conv_b, ssm_dt_bias, ssm_a_log, ssm_d, ssm_norm_w, pool_w, pool_scale, da_q_norm_w, da_k_norm_w, da_lambda_q1, da_lambda_k1, da_lambda_q2, da_lambda_k2, da_subln_w, w_out, norm2_w, ffn_w_gate, ffn_w_up, ffn_w_down):
    b, seq, d = x.shape
    depth = w_in.shape[0]
    assert d == D_MODEL and seq % ATT_TQ == 0 and seq % CHUNK == 0
    m = b * seq
    tm = 512 if m % 512 == 0 else ATT_TQ

    g64 = _block_diag_mean(GROUP_W, HEAD_DIM)
    g128 = _block_diag_mean(GROUP_W, SSM_D_STATE)
    g32 = _block_diag_mean(GROUP_W, DA_QK_DIM)

    shift = _conv_shift_matrices()

    assert w_in.shape[1:] == (D_MODEL, D_IN)
    x2 = x.reshape(m, d)
    for i in range(depth):
        proj, dt = _inproj(x2, norm1_w[i].reshape(1, d), w_in, i, _pad_lanes(ssm_dt_bias[i], DT_W), tm)

        bsm = jnp.repeat(gm_bs[i].T, HEAD_DIM, axis=1)
        ya = _gmlp(proj, gm_ws[i], bsm, gm_norm_w[i].reshape(1, GROUP_W), g64, tm)

        yb = _ssd(proj, dt, ssm_conv_w[i].T, ssm_conv_b[i].reshape(1, -1),
                  shift, _pad_lanes(ssm_a_log[i], DT_W),
                  jnp.repeat(ssm_d[i], HEAD_DIM).reshape(1, GROUP_W),
                  ssm_norm_w[i].reshape(1, GROUP_W), g128, b, seq)

        wbd = jax.scipy.linalg.block_diag(*[pool_w[i, g] for g in range(len(POOL_WINDOWS))]).astype(BF16)
        yc = _pool(proj, wbd, pool_scale[i].reshape(1, GROUP_W), b, seq)

        lam_init = 0.8 - 0.6 * math.exp(-0.3 * i)
        yd = _attn(proj, jnp.tile(da_q_norm_w[i], GROUP_W // DA_QK_DIM).reshape(1, GROUP_W),
                   jnp.tile(da_k_norm_w[i], GROUP_W // DA_QK_DIM).reshape(1, GROUP_W),
                   _pad_lanes(da_lambda_q1[i], LANES), _pad_lanes(da_lambda_k1[i], LANES),
                   _pad_lanes(da_lambda_q2[i], LANES), _pad_lanes(da_lambda_k2[i], LANES),
                   jnp.broadcast_to(da_subln_w[i].reshape(HEAD_DIM, 1), (HEAD_DIM, ATT_TQ)),
                   g32, b, seq, lam_init)

        x2 = _ffn(x2, ya, yb, yc, yd, w_out[i].astype(BF16), norm2_w[i].reshape(1, d),
                  ffn_w_gate[i].astype(BF16), ffn_w_up[i].astype(BF16), ffn_w_down[i].astype(BF16),
                  tm)
    return x2.reshape(b, seq, d)
```

```python
import functools
import math

import jax
import jax.numpy as jnp
from jax import lax
from jax.experimental import pallas as pl
from jax.experimental.pallas import tpu as pltpu

F32 = jnp.float32
BF16 = jnp.bfloat16

D_MODEL = 1024
GROUP_W = 256
CHUNK = 128
HEAD_DIM = 64
N_HEADS = 4
SSM_GROUPS = 2
SSM_D_STATE = 128
SSM_CONV_K = 4
SSM_CONV_DIM = GROUP_W + 2 * SSM_GROUPS * SSM_D_STATE
POOL_WINDOWS = (2, 4, 8, 16)
DA_QK_DIM = 32
D_FF = 2816
RMS_EPS = 1e-6
NEG_BIG = -1e30

LANES = 128
VMEM_LIMIT = 56 * 1024 * 1024

COL_PA = 0
COL_Z = 512
COL_XBC = 768
COL_PC = 1536
COL_Q = 1792
COL_K = 2048
COL_V = 2304
PROJ_W = 2560
DT_W = LANES
D_IN = PROJ_W + N_HEADS
DT_COL = COL_PC
IN_CHUNK = 256
FFN_CHUNK = 512
SSD_CPS = 4

ATT_TQ = 512
ATT_TK = 512
ATT_QB = 512
ATT_BOUND_MARGIN = 1.01
ATT_ONE_PASS_BOUND = 40.0
VT_ROWS = 80


def _cparams(sem):
    return pltpu.CompilerParams(dimension_semantics=sem, vmem_limit_bytes=VMEM_LIMIT)


def _sigmoid(x):
    return 1.0 / (1.0 + jnp.exp(-x))


def _dot(a, b):
    return jnp.dot(a, b, preferred_element_type=F32)


def _dot_nt(a, b):
    return lax.dot_general(a, b, (((1,), (1,)), ((), ())), preferred_element_type=F32)


def _softplus(t):
    return jnp.maximum(t, 0.0) + jnp.log(1.0 + jnp.exp(-jnp.abs(t)))


def _inproj_kernel(x_ref, nw_ref, w_ref, dtb_ref, proj_ref, dt_ref, wb_ref):
    @pl.when(pl.program_id(0) == 0)
    def _():
        for n0 in range(0, DT_COL, 2 * IN_CHUNK):
            wb_ref[:, n0:n0 + 2 * IN_CHUNK] = w_ref[:, n0:n0 + 2 * IN_CHUNK].astype(BF16)
        tail = w_ref[:, DT_COL:D_IN]
        wb_ref[:, DT_COL:PROJ_W] = tail[:, N_HEADS:].astype(BF16)
        wb_ref[:, PROJ_W:PROJ_W + DT_W] = tail[:, 0:DT_W].astype(BF16)

    x = x_ref[...]
    ms = jnp.mean(x * x, axis=-1, keepdims=True)
    h = (x * lax.rsqrt(ms + RMS_EPS) * nw_ref[...]).astype(BF16)
    for n0 in range(0, PROJ_W, IN_CHUNK):
        y = _dot(h, wb_ref[:, n0:n0 + IN_CHUNK])
        if n0 < COL_Z:
            y = jax.nn.gelu(y, approximate=True)
        elif n0 < COL_XBC:
            y = y * _sigmoid(y)
        proj_ref[:, n0:n0 + IN_CHUNK] = y.astype(BF16)
    dt_ref[...] = _softplus(_dot(h, wb_ref[:, PROJ_W:PROJ_W + DT_W]) + dtb_ref[...])


def _inproj(x2, nw, w_in, layer, dtb, tm):
    m = x2.shape[0]
    return pl.pallas_call(
        _inproj_kernel,
        grid=(m // tm,),
        in_specs=[
            pl.BlockSpec((tm, D_MODEL), lambda i: (i, 0)),
            pl.BlockSpec((1, D_MODEL), lambda i: (0, 0)),
            pl.BlockSpec((None, D_MODEL, D_IN), lambda i: (layer, 0, 0), pipeline_mode=pl.Buffered(1)),
            pl.BlockSpec((1, DT_W), lambda i: (0, 0)),
        ],
        out_specs=[
            pl.BlockSpec((tm, PROJ_W), lambda i: (i, 0)),
            pl.BlockSpec((tm, DT_W), lambda i: (i, 0)),
        ],
        out_shape=[
            jax.ShapeDtypeStruct((m, PROJ_W), BF16),
            jax.ShapeDtypeStruct((m, DT_W), F32),
        ],
        scratch_shapes=[pltpu.VMEM((D_MODEL, PROJ_W + DT_W), BF16)],
        compiler_params=_cparams(("arbitrary",)),
        name="inproj",
    )(x2, nw, w_in, dtb)


def _head_id(shape, width):
    lane = lax.broadcasted_iota(jnp.int32, shape, 1)
    return lax.shift_right_logical(lane, int(math.log2(width)))


def _gmlp_kernel(pa_ref, ws_ref, bsm_ref, nw_ref, g_ref, out_ref):
    t = pa_ref.shape[0]
    hact = pa_ref[...].astype(F32)
    u = hact[:, :GROUP_W]
    v = hact[:, GROUP_W:]
    ms = _dot((v * v).astype(BF16), g_ref[...])
    vn = (v * lax.rsqrt(ms + RMS_EPS) * nw_ref[...]).astype(BF16)
    row = lax.broadcasted_iota(jnp.int32, (CHUNK, CHUNK), 0)
    col = lax.broadcasted_iota(jnp.int32, (CHUNK, CHUNK), 1)
    wcat = jnp.concatenate(
        [jnp.where(row >= col, ws_ref[h], 0.0) for h in range(N_HEADS)], axis=1).astype(BF16)
    hid = _head_id((CHUNK, GROUP_W), HEAD_DIM)
    for c in range(t // CHUNK):
        vc = vn[c * CHUNK:(c + 1) * CHUNK]
        vstack = jnp.concatenate(
            [jnp.where(hid == h, vc, jnp.zeros_like(vc)) for h in range(N_HEADS)], axis=0)
        s = _dot(wcat, vstack) + bsm_ref[...]
        out_ref[c * CHUNK:(c + 1) * CHUNK, :] = (u[c * CHUNK:(c + 1) * CHUNK] * s).astype(BF16)


def _gmlp(proj, ws, bsm, nw, g64, t):
    m = proj.shape[0]
    return pl.pallas_call(
        _gmlp_kernel,
        grid=(m // t,),
        in_specs=[
            pl.BlockSpec((t, 2 * GROUP_W), lambda i: (i, COL_PA // (2 * GROUP_W))),
            pl.BlockSpec((N_HEADS, CHUNK, CHUNK), lambda i: (0, 0, 0)),
            pl.BlockSpec((CHUNK, GROUP_W), lambda i: (0, 0)),
            pl.BlockSpec((1, GROUP_W), lambda i: (0, 0)),
            pl.BlockSpec((GROUP_W, GROUP_W), lambda i: (0, 0)),
        ],
        out_specs=pl.BlockSpec((t, GROUP_W), lambda i: (i, 0)),
        out_shape=jax.ShapeDtypeStruct((m, GROUP_W), BF16),
        compiler_params=_cparams(("parallel",)),
        name="gmlp",
    )(proj, ws, bsm, nw, g64)


def _pool_kernel(pc_ref, wbd_ref, scale_ref, out_ref):
    h = pc_ref[...].astype(F32)
    shape = h.shape
    row = lax.broadcasted_iota(jnp.int32, shape, 0)

    def shifted(x, k):
        return jnp.where(row >= k, pltpu.roll(x, k, 0), 0.0)

    s2 = h + shifted(h, 1)
    s4 = s2 + shifted(s2, 2)
    s8 = s4 + shifted(s4, 4)
    s16 = s8 + shifted(s8, 8)
    gid = _head_id(shape, HEAD_DIM)
    wsum = jnp.where(gid == 0, s2, jnp.where(gid == 1, s4, jnp.where(gid == 2, s8, s16)))
    win = jnp.where(gid == 0, POOL_WINDOWS[0],
                    jnp.where(gid == 1, POOL_WINDOWS[1],
                              jnp.where(gid == 2, POOL_WINDOWS[2], POOL_WINDOWS[3])))
    cnt = jnp.minimum(win, row + 1).astype(F32)
    p = wsum / cnt - h
    y = _dot(p.astype(BF16), wbd_ref[...]) * scale_ref[...]
    out_ref[...] = y.astype(BF16)


def _pool(proj, wbd, scale, b, seq):
    m = proj.shape[0]
    return pl.pallas_call(
        _pool_kernel,
        grid=(b,),
        in_specs=[
            pl.BlockSpec((seq, GROUP_W), lambda i: (i, COL_PC // GROUP_W)),
            pl.BlockSpec((GROUP_W, GROUP_W), lambda i: (0, 0)),
            pl.BlockSpec((1, GROUP_W), lambda i: (0, 0)),
        ],
        out_specs=pl.BlockSpec((seq, GROUP_W), lambda i: (i, 0)),
        out_shape=jax.ShapeDtypeStruct((m, GROUP_W), BF16),
        compiler_params=_cparams(("parallel",)),
        name="pool",
    )(proj, wbd, scale)


def _split3(a):
    a1 = a.astype(BF16)
    r1 = a - a1.astype(F32)
    a2 = r1.astype(BF16)
    r2 = r1 - a2.astype(F32)
    return a1, a2, r2.astype(BF16)


def _ssd_kernel(z_ref, xbc_ref, halo_ref, dt_ref, cw_ref, cb_ref, shift_ref, alog_ref, dexp_ref, nw_ref,
                g_ref, out_ref, st_ref):
    c = pl.program_id(1)

    @pl.when(c == 0)
    def _():
        st_ref[...] = jnp.zeros_like(st_ref)

    row = lax.broadcasted_iota(jnp.int32, (CHUNK, CHUNK), 0)
    col = lax.broadcasted_iota(jnp.int32, (CHUNK, CHUNK), 1)
    causal = row >= col
    ltri = jnp.where(causal, 1.0, 0.0).astype(BF16)
    hid = _head_id((CHUNK, GROUP_W), HEAD_DIM)
    neg_a = -jnp.exp(alog_ref[...])

    def expand(c4):
        return jnp.where(hid == 0, c4[:, 0:1],
                         jnp.where(hid == 1, c4[:, 1:2],
                                   jnp.where(hid == 2, c4[:, 2:3], c4[:, 3:4])))

    chunks = range(SSD_CPS)
    rows = [slice(ck * CHUNK, (ck + 1) * CHUNK) for ck in chunks]
    groups = [slice(g * SSM_D_STATE, (g + 1) * SSM_D_STATE) for g in range(SSM_GROUPS)]

    xs, bm, cm = [], [], []
    halo = jnp.where(c > 0, halo_ref[...], jnp.zeros_like(halo_ref))
    for ck in chunks:
        xcur = xbc_ref[rows[ck], :]
        acc = cb_ref[...] + cw_ref[SSM_CONV_K - 1:SSM_CONV_K, :] * xcur.astype(F32)
        for j in range(1, SSM_CONV_K):
            k = SSM_CONV_K - 1 - j
            if ck == 0:
                shifted = (_dot(shift_ref[j - 1, :, 0:CHUNK], halo)
                           + _dot(shift_ref[j - 1, :, CHUNK:2 * CHUNK], xcur))
            else:
                shifted = _dot(shift_ref[j - 1], xbc_ref[(ck - 1) * CHUNK:(ck + 1) * CHUNK, :])
            acc = acc + cw_ref[k:k + 1, :] * shifted
        xc = acc * _sigmoid(acc)
        xs.append(xc[:, :GROUP_W])
        bm.append(xc[:, GROUP_W:2 * GROUP_W])
        cm.append(xc[:, 2 * GROUP_W:].astype(BF16))

    acs, acs_t = [], []
    for ck in chunks:
        a1, a2, a3 = _split3(dt_ref[rows[ck], :] * neg_a)
        acs.append(_dot(ltri, a1) + _dot(ltri, a2) + _dot(ltri, a3))
        acs_t.append(acs[ck].T)

    xdt, eacs_e, mcat, snew = [], [], [], []
    for ck in chunks:
        xdt.append(xs[ck] * expand(dt_ref[rows[ck], :]))
        eacs_e.append(expand(jnp.exp(acs[ck])))
        dte_e = expand(jnp.exp(acs[ck][CHUNK - 1:CHUNK, :] - acs[ck]))
        ms = []
        for g in range(SSM_GROUPS):
            cb = _dot_nt(cm[ck][:, groups[g]], bm[ck][:, groups[g]].astype(BF16))
            for hh in range(N_HEADS // SSM_GROUPS):
                h = g * (N_HEADS // SSM_GROUPS) + hh
                seg = acs[ck][:, h:h + 1] - acs_t[ck][h:h + 1, :]
                ms.append((cb * jnp.exp(jnp.where(causal, seg, NEG_BIG))).astype(BF16))
        mcat.append(jnp.concatenate(ms, axis=1))
        w = (xdt[ck] * dte_e).astype(BF16)
        snew.append([_dot(bm[ck][:, groups[g]].T.astype(BF16), w[:, groups[g]])
                     for g in range(SSM_GROUPS)])

    state = [st_ref[g] for g in range(SSM_GROUPS)]
    entering = []
    for ck in chunks:
        entering.append([st.astype(BF16) for st in state])
        cdl = eacs_e[ck][CHUNK - 1:CHUNK, :]
        state = [state[g] * cdl[:, groups[g]] + snew[ck][g] for g in range(SSM_GROUPS)]
    for g in range(SSM_GROUPS):
        st_ref[g] = state[g]

    ys = []
    for ck in chunks:
        xdt_b = xdt[ck].astype(BF16)
        xstack = jnp.concatenate(
            [jnp.where(hid == h, xdt_b, jnp.zeros_like(xdt_b)) for h in range(N_HEADS)], axis=0)
        yoff = jnp.concatenate([_dot(cm[ck][:, groups[g]], entering[ck][g]) for g in range(SSM_GROUPS)], axis=1)
        y = _dot(mcat[ck], xstack) + yoff * eacs_e[ck] + dexp_ref[...] * xs[ck]
        ys.append(y * z_ref[rows[ck], :].astype(F32))

    for ck in chunks:
        msq = _dot((ys[ck] * ys[ck]).astype(BF16), g_ref[...])
        out_ref[rows[ck], :] = (ys[ck] * lax.rsqrt(msq + RMS_EPS) * nw_ref[...]).astype(BF16)


def _conv_shift_matrices():
    r = jnp.arange(CHUNK)[:, None]
    col = jnp.arange(2 * CHUNK)[None, :]
    return jnp.stack([(col == CHUNK + r - j) for j in range(1, SSM_CONV_K)]).astype(BF16)


def _ssd(proj, dt, cw, cb, shift, alog, dexp, nw, g128, b, seq):
    m = proj.shape[0]
    ts = SSD_CPS * CHUNK
    assert seq % ts == 0
    nc = seq // ts
    const = lambda i, c: (0, 0)
    return pl.pallas_call(
        _ssd_kernel,
        grid=(b, nc),
        in_specs=[
            pl.BlockSpec((ts, GROUP_W), lambda i, c: (i * nc + c, COL_Z // GROUP_W)),
            pl.BlockSpec((ts, SSM_CONV_DIM), lambda i, c: (i * nc + c, COL_XBC // SSM_CONV_DIM)),
            pl.BlockSpec((CHUNK, SSM_CONV_DIM),
                         lambda i, c: (jnp.maximum((i * nc + c) * SSD_CPS - 1, 0), COL_XBC // SSM_CONV_DIM)),
            pl.BlockSpec((ts, DT_W), lambda i, c: (i * nc + c, 0)),
            pl.BlockSpec((SSM_CONV_K, SSM_CONV_DIM), const),
            pl.BlockSpec((1, SSM_CONV_DIM), const),
            pl.BlockSpec((SSM_CONV_K - 1, CHUNK, 2 * CHUNK), lambda i, c: (0, 0, 0)),
            pl.BlockSpec((1, DT_W), const),
            pl.BlockSpec((1, GROUP_W), const),
            pl.BlockSpec((1, GROUP_W), const),
            pl.BlockSpec((GROUP_W, GROUP_W), const),
        ],
        out_specs=pl.BlockSpec((ts, GROUP_W), lambda i, c: (i * nc + c, 0)),
        out_shape=jax.ShapeDtypeStruct((m, GROUP_W), BF16),
        scratch_shapes=[pltpu.VMEM((SSM_GROUPS, SSM_D_STATE, LANES), F32)],
        compiler_params=_cparams(("parallel", "arbitrary")),
        name="ssd",
    )(proj, proj, proj, dt, cw, cb, shift, alog, dexp, nw, g128)


def _alibi_slope(h):
    return 2.0 ** (-8.0 * (h + 1) / N_HEADS)


def _head_slot(x, h):
    base = x[:, (h // 2) * LANES:(h // 2 + 1) * LANES]
    return pltpu.roll(base, HEAD_DIM, 1) if h % 2 else base


def _attn_kernel(q_ref, k_ref, v_ref, qnw_ref, knw_ref, lq1_ref, lk1_ref, lq2_ref, lk2_ref, subw_ref,
                 g_ref, out_ref, ka_ref, vt_ref, qs_ref, acc_ref, s0_ref, kmax_ref, *, lam_init):
    qi = pl.program_id(1)
    tq, tk = ATT_TQ, ATT_TK
    seq = k_ref.shape[0]

    def aug_lanes(pos, slope, key_side):
        lane = lax.broadcasted_iota(jnp.int32, pos.shape, 1)
        hi = (slope * LANES) * lax.shift_right_logical(pos, 7).astype(F32)
        lo = slope * (pos & (LANES - 1)).astype(F32)
        one = jnp.ones_like(hi)
        c = (one, one, hi, lo) if key_side else (-hi, -lo, one, one)
        last = jnp.where(lane == 68, 1.0, 0.0) if key_side else 0.0
        return jnp.where(lane == 64, c[0], jnp.where(lane == 65, c[1],
                         jnp.where(lane == 66, c[2], jnp.where(lane == 67, c[3], last))))

    @pl.when(qi == 0)
    def _():
        k = k_ref[...].astype(F32)
        ms = _dot((k * k).astype(BF16), g_ref[...])
        kn = k * lax.rsqrt(ms + RMS_EPS) * knw_ref[...]
        ksq = _dot((kn * kn).astype(BF16), g_ref[...]) * DA_QK_DIM
        kmax_ref[...] = jnp.broadcast_to(jnp.max(ksq, axis=0, keepdims=True), kmax_ref.shape)
        pos = lax.broadcasted_iota(jnp.int32, (seq, LANES), 0)
        lane = lax.broadcasted_iota(jnp.int32, (seq, LANES), 1)
        for h in range(N_HEADS):
            aug = aug_lanes(pos, _alibi_slope(h), True)
            ka_ref[h] = jnp.where(lane < HEAD_DIM, _head_slot(kn, h), aug).astype(BF16)
        tail = (lax.broadcasted_iota(jnp.int32, (VT_ROWS - HEAD_DIM, tk), 0) == 0).astype(BF16)
        for j in range(seq // tk):
            vt = v_ref[j * tk:(j + 1) * tk, :].astype(F32).T.astype(BF16)
            for h in range(N_HEADS):
                vt_ref[j, h, 0:HEAD_DIM, :] = vt[h * HEAD_DIM:(h + 1) * HEAD_DIM, :]
                vt_ref[j, h, HEAD_DIM:VT_ROWS, :] = tail

    q = q_ref[...].astype(F32)
    ms = _dot((q * q).astype(BF16), g_ref[...])
    qn = q * lax.rsqrt(ms + RMS_EPS) * (qnw_ref[...] * (DA_QK_DIM ** -0.5))
    qsq = _dot((qn * qn).astype(BF16), g_ref[...]) * DA_QK_DIM
    bound = jnp.sqrt(qsq * kmax_ref[0:1, :]) * ATT_BOUND_MARGIN
    one_pass = jnp.max(bound) <= ATT_ONE_PASS_BOUND
    pos = qi * tq + lax.broadcasted_iota(jnp.int32, (tq, LANES), 0)
    lane = lax.broadcasted_iota(jnp.int32, (tq, LANES), 1)
    for h in range(N_HEADS):
        base = _head_slot(qn, h)
        bnd = _head_slot(bound, h)
        aug = aug_lanes(pos, _alibi_slope(h), False)
        for comp in range(2):
            shift = jnp.where(lane == 68, -bnd[:, comp * DA_QK_DIM:comp * DA_QK_DIM + 1], aug)
            own = (lane >= comp * DA_QK_DIM) & (lane < (comp + 1) * DA_QK_DIM)
            qs_ref[h, comp * tq:(comp + 1) * tq, :] = jnp.where(
                own, base, jnp.where(lane < HEAD_DIM, 0.0, shift)).astype(BF16)
    acc_ref[...] = jnp.zeros_like(acc_ref)

    qb = ATT_QB
    items = [(h, n) for h in range(N_HEADS) for n in range(2 * tq // qb)]

    def scores(item, j):
        h, n = item
        k0 = pl.multiple_of(j * tk, tk)
        return _dot_nt(ka_ref[h, pl.ds(k0, tk), :], qs_ref[h, n * qb:(n + 1) * qb, :])

    def tile(j, ms_, masked):
        new_m = []
        s_next = s0_ref[...]
        for i, (h, n) in enumerate(items):
            cols = slice(n * qb, (n + 1) * qb)
            s = s_next
            if i + 1 < len(items):
                s_next = scores(items[i + 1], j)
            elif not masked:
                s0_ref[...] = scores(items[0], j + 1)
            if masked:
                kk = lax.broadcasted_iota(jnp.int32, (tk, qb), 0)
                qq = (lax.broadcasted_iota(jnp.int32, (tk, qb), 1) + n * qb) & (tq - 1)
                s = jnp.where(kk <= qq, s, NEG_BIG)
            m_old = ms_[i]
            m_new = jnp.maximum(m_old, jnp.max(s, axis=0, keepdims=True))
            alpha = jnp.exp(m_old - m_new)
            p = jnp.exp(s - m_new)
            new_m.append(m_new)
            acc_ref[h, :, cols] = acc_ref[h, :, cols] * alpha + _dot(vt_ref[j, h], p.astype(BF16))
        return tuple(new_m)

    def tile_one_pass(j, masked):
        s_next = s0_ref[...]
        for i, (h, n) in enumerate(items):
            cols = slice(n * qb, (n + 1) * qb)
            s = s_next
            if i + 1 < len(items):
                s_next = scores(items[i + 1], j)
            elif not masked:
                s0_ref[...] = scores(items[0], j + 1)
            if masked:
                kk = lax.broadcasted_iota(jnp.int32, (tk, qb), 0)
                qq = (lax.broadcasted_iota(jnp.int32, (tk, qb), 1) + n * qb) & (tq - 1)
                s = jnp.where(kk <= qq, s, NEG_BIG)
            acc_ref[h, :, cols] += _dot(vt_ref[j, h], jnp.exp(s).astype(BF16))

    s0_ref[...] = scores(items[0], 0)

    @pl.when(one_pass)
    def _():
        def body(j, carry):
            tile_one_pass(j, False)
            return carry
        lax.fori_loop(0, qi, body, 0)
        tile_one_pass(qi, True)

    @pl.when(jnp.logical_not(one_pass))
    def _():
        m0 = tuple(jnp.full((1, qb), NEG_BIG, F32) for _ in items)
        m1 = lax.fori_loop(0, qi, lambda j, c: tile(j, c, False), m0)
        tile(qi, m1, True)

    def lane_sum(x):
        return jnp.broadcast_to(jnp.sum(x, axis=1, keepdims=True), x.shape)

    lam = (jnp.exp(lane_sum(lq1_ref[...] * lk1_ref[...]))
           - jnp.exp(lane_sum(lq2_ref[...] * lk2_ref[...])) + lam_init)
    lam = jnp.concatenate([lam] * (tq // LANES), axis=1)
    outs = []
    for h in range(N_HEADS):
        o = acc_ref[h, 0:HEAD_DIM, :] / acc_ref[h, HEAD_DIM:HEAD_DIM + 1, :]
        oh = o[:, 0:tq] - lam * o[:, tq:2 * tq]
        ms = jnp.mean(oh * oh, axis=0, keepdims=True)
        outs.append(oh * lax.rsqrt(ms + RMS_EPS) * (subw_ref[...] * (1.0 - lam_init)))
    out_ref[...] = jnp.concatenate(outs, axis=0).T.astype(BF16)


def _attn(proj, qnw, knw, lq1, lk1, lq2, lk2, subw, g32, b, seq, lam_init):
    m = proj.shape[0]
    assert ATT_TQ == ATT_TK
    nq = seq // ATT_TQ
    const = lambda i, j: (0, 0)
    return pl.pallas_call(
        functools.partial(_attn_kernel, lam_init=lam_init),
        grid=(b, nq),
        in_specs=[
            pl.BlockSpec((ATT_TQ, GROUP_W), lambda i, j: (i * nq + j, COL_Q // GROUP_W)),
            pl.BlockSpec((seq, GROUP_W), lambda i, j: (i, COL_K // GROUP_W)),
            pl.BlockSpec((seq, GROUP_W), lambda i, j: (i, COL_V // GROUP_W)),
            pl.BlockSpec((1, GROUP_W), const),
            pl.BlockSpec((1, GROUP_W), const),
            pl.BlockSpec((1, LANES), const),
            pl.BlockSpec((1, LANES), const),
            pl.BlockSpec((1, LANES), const),
            pl.BlockSpec((1, LANES), const),
            pl.BlockSpec((HEAD_DIM, ATT_TQ), const),
            pl.BlockSpec((GROUP_W, GROUP_W), const),
        ],
        out_specs=pl.BlockSpec((ATT_TQ, GROUP_W), lambda i, j: (i * nq + j, 0)),
        out_shape=jax.ShapeDtypeStruct((m, GROUP_W), BF16),
        scratch_shapes=[
            pltpu.VMEM((N_HEADS, seq, LANES), BF16),
            pltpu.VMEM((seq // ATT_TK, N_HEADS, VT_ROWS, ATT_TK), BF16),
            pltpu.VMEM((N_HEADS, 2 * ATT_TQ, LANES), BF16),
            pltpu.VMEM((N_HEADS, VT_ROWS, 2 * ATT_TQ), F32),
            pltpu.VMEM((ATT_TK, ATT_QB), F32),
            pltpu.VMEM((8, GROUP_W), F32),
        ],
        compiler_params=_cparams(("parallel", "arbitrary")),
        name="diffattn",
    )(proj, proj, proj, qnw, knw, lq1, lk1, lq2, lk2, subw, g32)


def _ffn_kernel(x_ref, ya_ref, yb_ref, yc_ref, yd_ref, wo_ref, nw_ref, wg_ref, wu_ref, wd_ref,
                out_ref, act_ref):
    x1 = x_ref[...]
    for i, y_ref in enumerate((ya_ref, yb_ref, yc_ref, yd_ref)):
        x1 = x1 + _dot(y_ref[...], wo_ref[i * GROUP_W:(i + 1) * GROUP_W, :])
    ms = jnp.mean(x1 * x1, axis=-1, keepdims=True)
    h = (x1 * lax.rsqrt(ms + RMS_EPS) * nw_ref[...]).astype(BF16)
    for c0 in range(0, D_FF, FFN_CHUNK):
        c1 = min(c0 + FFN_CHUNK, D_FF)
        g = _dot(h, wg_ref[:, c0:c1])
        u = _dot(h, wu_ref[:, c0:c1])
        act_ref[:, c0:c1] = (g * _sigmoid(g) * u).astype(BF16)
    out_ref[...] = x1 + _dot(act_ref[...], wd_ref[...])


def _ffn(x2, ya, yb, yc, yd, wo, nw, wg, wu, wd, tm):
    m = x2.shape[0]
    row = lambda i: (i, 0)
    resident = lambda shape: pl.BlockSpec(shape, lambda i: (0, 0), pipeline_mode=pl.Buffered(1))
    return pl.pallas_call(
        _ffn_kernel,
        grid=(m // tm,),
        in_specs=[
            pl.BlockSpec((tm, D_MODEL), row),
            pl.BlockSpec((tm, GROUP_W), row),
            pl.BlockSpec((tm, GROUP_W), row),
            pl.BlockSpec((tm, GROUP_W), row),
            pl.BlockSpec((tm, GROUP_W), row),
            resident((D_MODEL, D_MODEL)),
            resident((1, D_MODEL)),
            resident((D_MODEL, D_FF)),
            resident((D_MODEL, D_FF)),
            resident((D_FF, D_MODEL)),
        ],
        out_specs=pl.BlockSpec((tm, D_MODEL), row),
        out_shape=jax.ShapeDtypeStruct((m, D_MODEL), F32),
        scratch_shapes=[pltpu.VMEM((tm, D_FF), BF16)],
        compiler_params=_cparams(("parallel",)),
        name="outproj_ffn",
    )(x2, ya, yb, yc, yd, wo, nw, wg, wu, wd)


def _block_diag_mean(width, group):
    idx = jnp.arange(width) // group
    return jnp.where(idx[:, None] == idx[None, :], 1.0 / group, 0.0).astype(BF16)


def _pad_lanes(v, width):
    v = v.reshape(1, -1).astype(F32)
    return jnp.pad(v, ((0, 0), (0, width - v.shape[1])))


def kernel(x, norm1_w, w_in, gm_norm_w, gm_ws, gm_bs, ssm_conv_w, ssm_conv_b, ssm_dt_bias, ssm_a_log, ssm_d, ssm_norm_w, pool_w, pool_scale, da_q_norm_w, da_k_norm_w, da_lambda_q1, da_lambda_k1, da_lambda_q2, da_lambda_k2, da_subln_w, w_out, norm2_w, ffn_w_gate, ffn_w_up, ffn_w_down):
    b, seq, d = x.shape
    depth = w_in.shape[0]
    assert d == D_MODEL and seq % ATT_TQ == 0 and seq % CHUNK == 0
    m = b * seq
    tm = 512 if m % 512 == 0 else ATT_TQ

    g64 = _block_diag_mean(GROUP_W, HEAD_DIM)
    g128 = _block_diag_mean(GROUP_W, SSM_D_STATE)
    g32 = _block_diag_mean(GROUP_W, DA_QK_DIM)

    shift = _conv_shift_matrices()

    assert w_in.shape[1:] == (D_MODEL, D_IN)
    x2 = x.reshape(m, d)
    for i in range(depth):
        proj, dt = _inproj(x2, norm1_w[i].reshape(1, d), w_in, i, _pad_lanes(ssm_dt_bias[i], DT_W), tm)

        bsm = jnp.repeat(gm_bs[i].T, HEAD_DIM, axis=1)
        ya = _gmlp(proj, gm_ws[i], bsm, gm_norm_w[i].reshape(1, GROUP_W), g64, seq if seq % tm == 0 else tm)

        yb = _ssd(proj, dt, ssm_conv_w[i].T, ssm_conv_b[i].reshape(1, -1),
                  shift, _pad_lanes(ssm_a_log[i], DT_W),
                  jnp.repeat(ssm_d[i], HEAD_DIM).reshape(1, GROUP_W),
                  ssm_norm_w[i].reshape(1, GROUP_W), g128, b, seq)

        wbd = jax.scipy.linalg.block_diag(*[pool_w[i, g] for g in range(len(POOL_WINDOWS))]).astype(BF16)
        yc = _pool(proj, wbd, pool_scale[i].reshape(1, GROUP_W), b, seq)

        lam_init = 0.8 - 0.6 * math.exp(-0.3 * i)
        yd = _attn(proj, jnp.tile(da_q_norm_w[i], GROUP_W // DA_QK_DIM).reshape(1, GROUP_W),
                   jnp.tile(da_k_norm_w[i], GROUP_W // DA_QK_DIM).reshape(1, GROUP_W),
                   _pad_lanes(da_lambda_q1[i], LANES), _pad_lanes(da_lambda_k1[i], LANES),
                   _pad_lanes(da_lambda_q2[i], LANES), _pad_lanes(da_lambda_k2[i], LANES),
                   jnp.broadcast_to(da_subln_w[i].reshape(HEAD_DIM, 1), (HEAD_DIM, ATT_TQ)),
                   g32, b, seq, lam_init)

        x2 = _ffn(x2, ya, yb, yc, yd, w_out[i].astype(BF16), norm2_w[i].reshape(1, d),
                  ffn_w_gate[i].astype(BF16), ffn_w_up[i].astype(BF16), ffn_w_down[i].astype(BF16),
                  tm)
    return x2.reshape(b, seq, d)
```

```python
import functools
import math

import jax
import jax.numpy as jnp
from jax import lax
from jax.experimental import pallas as pl
from jax.experimental.pallas import tpu as pltpu

F32 = jnp.float32
BF16 = jnp.bfloat16

D_MODEL = 1024
GROUP_W = 256
CHUNK = 128
HEAD_DIM = 64
N_HEADS = 4
SSM_GROUPS = 2
SSM_D_STATE = 128
SSM_CONV_K = 4
SSM_CONV_DIM = GROUP_W + 2 * SSM_GROUPS * SSM_D_STATE
POOL_WINDOWS = (2, 4, 8, 16)
DA_QK_DIM = 32
D_FF = 2816
RMS_EPS = 1e-6
NEG_BIG = -1e30

LANES = 128
VMEM_LIMIT = 56 * 1024 * 1024

COL_PA = 0
COL_Z = 512
COL_XBC = 768
COL_PC = 1536
COL_Q = 1792
COL_K = 2048
COL_V = 2304
PROJ_W = 2560
DT_W = LANES
D_IN = PROJ_W + N_HEADS
DT_COL = COL_PC
IN_CHUNK = 256
FFN_CHUNK = 512
SSD_CPS = 4

ATT_TQ = 512
ATT_TK = 512
ATT_QB = 512
ATT_BOUND_MARGIN = 1.01
ATT_ONE_PASS_BOUND = 40.0
VT_ROWS = 80


def _cparams(sem):
    return pltpu.CompilerParams(dimension_semantics=sem, vmem_limit_bytes=VMEM_LIMIT)


def _sigmoid(x):
    return 1.0 / (1.0 + jnp.exp(-x))


def _dot(a, b):
    return jnp.dot(a, b, preferred_element_type=F32)


def _dot_nt(a, b):
    return lax.dot_general(a, b, (((1,), (1,)), ((), ())), preferred_element_type=F32)


def _softplus(t):
    return jnp.maximum(t, 0.0) + jnp.log(1.0 + jnp.exp(-jnp.abs(t)))


def _inproj_kernel(x_ref, nw_ref, w_ref, dtb_ref, proj_ref, dt_ref, wb_ref):
    @pl.when(pl.program_id(0) == 0)
    def _():
        for n0 in range(0, DT_COL, 2 * IN_CHUNK):
            wb_ref[:, n0:n0 + 2 * IN_CHUNK] = w_ref[:, n0:n0 + 2 * IN_CHUNK].astype(BF16)
        tail = w_ref[:, DT_COL:D_IN]
        wb_ref[:, DT_COL:PROJ_W] = tail[:, N_HEADS:].astype(BF16)
        wb_ref[:, PROJ_W:PROJ_W + DT_W] = tail[:, 0:DT_W].astype(BF16)

    x = x_ref[...]
    ms = jnp.mean(x * x, axis=-1, keepdims=True)
    h = (x * lax.rsqrt(ms + RMS_EPS) * nw_ref[...]).astype(BF16)
    for n0 in range(0, PROJ_W, IN_CHUNK):
        y = _dot(h, wb_ref[:, n0:n0 + IN_CHUNK])
        if n0 < COL_Z:
            y = jax.nn.gelu(y, approximate=True)
        elif n0 < COL_XBC:
            y = y * _sigmoid(y)
        proj_ref[:, n0:n0 + IN_CHUNK] = y.astype(BF16)
    dt_ref[...] = _softplus(_dot(h, wb_ref[:, PROJ_W:PROJ_W + DT_W]) + dtb_ref[...])


def _per_layer(shape, layer):
    return pl.BlockSpec((None,) + tuple(shape), lambda *_: (layer,) + (0,) * len(shape))


def _inproj(x2, nw, w_in, layer, dtb, tm):
    m = x2.shape[0]
    return pl.pallas_call(
        _inproj_kernel,
        grid=(m // tm,),
        in_specs=[
            pl.BlockSpec((tm, D_MODEL), lambda i: (i, 0)),
            _per_layer((1, D_MODEL), layer),
            pl.BlockSpec((None, D_MODEL, D_IN), lambda i: (layer, 0, 0), pipeline_mode=pl.Buffered(1)),
            _per_layer((1, DT_W), layer),
        ],
        out_specs=[
            pl.BlockSpec((tm, PROJ_W), lambda i: (i, 0)),
            pl.BlockSpec((tm, DT_W), lambda i: (i, 0)),
        ],
        out_shape=[
            jax.ShapeDtypeStruct((m, PROJ_W), BF16),
            jax.ShapeDtypeStruct((m, DT_W), F32),
        ],
        scratch_shapes=[pltpu.VMEM((D_MODEL, PROJ_W + DT_W), BF16)],
        compiler_params=_cparams(("arbitrary",)),
        name="inproj",
    )(x2, nw, w_in, dtb)


def _head_id(shape, width):
    lane = lax.broadcasted_iota(jnp.int32, shape, 1)
    return lax.shift_right_logical(lane, int(math.log2(width)))


def _gmlp_kernel(pa_ref, ws_ref, bsm_ref, nw_ref, g_ref, out_ref):
    t = pa_ref.shape[0]
    hact = pa_ref[...].astype(F32)
    u = hact[:, :GROUP_W]
    v = hact[:, GROUP_W:]
    ms = _dot((v * v).astype(BF16), g_ref[...])
    vn = (v * lax.rsqrt(ms + RMS_EPS) * nw_ref[...]).astype(BF16)
    row = lax.broadcasted_iota(jnp.int32, (CHUNK, CHUNK), 0)
    col = lax.broadcasted_iota(jnp.int32, (CHUNK, CHUNK), 1)
    wcat = jnp.concatenate(
        [jnp.where(row >= col, ws_ref[h], 0.0) for h in range(N_HEADS)], axis=1).astype(BF16)
    hid = _head_id((CHUNK, GROUP_W), HEAD_DIM)
    for c in range(t // CHUNK):
        vc = vn[c * CHUNK:(c + 1) * CHUNK]
        vstack = jnp.concatenate(
            [jnp.where(hid == h, vc, jnp.zeros_like(vc)) for h in range(N_HEADS)], axis=0)
        s = _dot(wcat, vstack) + bsm_ref[...]
        out_ref[c * CHUNK:(c + 1) * CHUNK, :] = (u[c * CHUNK:(c + 1) * CHUNK] * s).astype(BF16)


def _gmlp(proj, ws, bsm, nw, g64, layer, t):
    m = proj.shape[0]
    return pl.pallas_call(
        _gmlp_kernel,
        grid=(m // t,),
        in_specs=[
            pl.BlockSpec((t, 2 * GROUP_W), lambda i: (i, COL_PA // (2 * GROUP_W))),
            _per_layer((N_HEADS, CHUNK, CHUNK), layer),
            _per_layer((CHUNK, GROUP_W), layer),
            _per_layer((1, GROUP_W), layer),
            pl.BlockSpec((GROUP_W, GROUP_W), lambda i: (0, 0)),
        ],
        out_specs=pl.BlockSpec((t, GROUP_W), lambda i: (i, 0)),
        out_shape=jax.ShapeDtypeStruct((m, GROUP_W), BF16),
        compiler_params=_cparams(("parallel",)),
        name="gmlp",
    )(proj, ws, bsm, nw, g64)


def _pool_kernel(pc_ref, wbd_ref, scale_ref, out_ref):
    h = pc_ref[...].astype(F32)
    shape = h.shape
    row = lax.broadcasted_iota(jnp.int32, shape, 0)

    def shifted(x, k):
        return jnp.where(row >= k, pltpu.roll(x, k, 0), 0.0)

    s2 = h + shifted(h, 1)
    s4 = s2 + shifted(s2, 2)
    s8 = s4 + shifted(s4, 4)
    s16 = s8 + shifted(s8, 8)
    gid = _head_id(shape, HEAD_DIM)
    wsum = jnp.where(gid == 0, s2, jnp.where(gid == 1, s4, jnp.where(gid == 2, s8, s16)))
    win = jnp.where(gid == 0, POOL_WINDOWS[0],
                    jnp.where(gid == 1, POOL_WINDOWS[1],
                              jnp.where(gid == 2, POOL_WINDOWS[2], POOL_WINDOWS[3])))
    cnt = jnp.minimum(win, row + 1).astype(F32)
    p = wsum / cnt - h
    y = _dot(p.astype(BF16), wbd_ref[...]) * scale_ref[...]
    out_ref[...] = y.astype(BF16)


def _pool(proj, wbd, scale, layer, b, seq):
    m = proj.shape[0]
    return pl.pallas_call(
        _pool_kernel,
        grid=(b,),
        in_specs=[
            pl.BlockSpec((seq, GROUP_W), lambda i: (i, COL_PC // GROUP_W)),
            _per_layer((GROUP_W, GROUP_W), layer),
            _per_layer((1, GROUP_W), layer),
        ],
        out_specs=pl.BlockSpec((seq, GROUP_W), lambda i: (i, 0)),
        out_shape=jax.ShapeDtypeStruct((m, GROUP_W), BF16),
        compiler_params=_cparams(("parallel",)),
        name="pool",
    )(proj, wbd, scale)


def _split3(a):
    a1 = a.astype(BF16)
    r1 = a - a1.astype(F32)
    a2 = r1.astype(BF16)
    r2 = r1 - a2.astype(F32)
    return a1, a2, r2.astype(BF16)


def _ssd_kernel(z_ref, xbc_ref, halo_ref, dt_ref, cw_ref, cb_ref, shift_ref, alog_ref, dexp_ref, nw_ref,
                g_ref, out_ref, st_ref):
    c = pl.program_id(1)

    @pl.when(c == 0)
    def _():
        st_ref[...] = jnp.zeros_like(st_ref)

    row = lax.broadcasted_iota(jnp.int32, (CHUNK, CHUNK), 0)
    col = lax.broadcasted_iota(jnp.int32, (CHUNK, CHUNK), 1)
    causal = row >= col
    ltri = jnp.where(causal, 1.0, 0.0).astype(BF16)
    hid = _head_id((CHUNK, GROUP_W), HEAD_DIM)
    neg_a = -jnp.exp(alog_ref[...])

    def expand(c4):
        return jnp.where(hid == 0, c4[:, 0:1],
                         jnp.where(hid == 1, c4[:, 1:2],
                                   jnp.where(hid == 2, c4[:, 2:3], c4[:, 3:4])))

    chunks = range(SSD_CPS)
    rows = [slice(ck * CHUNK, (ck + 1) * CHUNK) for ck in chunks]
    groups = [slice(g * SSM_D_STATE, (g + 1) * SSM_D_STATE) for g in range(SSM_GROUPS)]

    xs, bm, cm = [], [], []
    halo = jnp.where(c > 0, halo_ref[...], jnp.zeros_like(halo_ref))
    for ck in chunks:
        xcur = xbc_ref[rows[ck], :]
        acc = cb_ref[...] + cw_ref[SSM_CONV_K - 1:SSM_CONV_K, :] * xcur.astype(F32)
        for j in range(1, SSM_CONV_K):
            k = SSM_CONV_K - 1 - j
            if ck == 0:
                shifted = (_dot(shift_ref[j - 1, :, 0:CHUNK], halo)
                           + _dot(shift_ref[j - 1, :, CHUNK:2 * CHUNK], xcur))
            else:
                shifted = _dot(shift_ref[j - 1], xbc_ref[(ck - 1) * CHUNK:(ck + 1) * CHUNK, :])
            acc = acc + cw_ref[k:k + 1, :] * shifted
        xc = acc * _sigmoid(acc)
        xs.append(xc[:, :GROUP_W])
        bm.append(xc[:, GROUP_W:2 * GROUP_W])
        cm.append(xc[:, 2 * GROUP_W:].astype(BF16))

    acs, acs_t = [], []
    for ck in chunks:
        a1, a2, a3 = _split3(dt_ref[rows[ck], :] * neg_a)
        acs.append(_dot(ltri, a1) + _dot(ltri, a2) + _dot(ltri, a3))
        acs_t.append(acs[ck].T)

    xdt, eacs_e, mcat, snew = [], [], [], []
    for ck in chunks:
        xdt.append(xs[ck] * expand(dt_ref[rows[ck], :]))
        eacs_e.append(expand(jnp.exp(acs[ck])))
        dte_e = expand(jnp.exp(acs[ck][CHUNK - 1:CHUNK, :] - acs[ck]))
        ms = []
        for g in range(SSM_GROUPS):
            cb = _dot_nt(cm[ck][:, groups[g]], bm[ck][:, groups[g]].astype(BF16))
            for hh in range(N_HEADS // SSM_GROUPS):
                h = g * (N_HEADS // SSM_GROUPS) + hh
                seg = acs[ck][:, h:h + 1] - acs_t[ck][h:h + 1, :]
                ms.append((cb * jnp.exp(jnp.where(causal, seg, NEG_BIG))).astype(BF16))
        mcat.append(jnp.concatenate(ms, axis=1))
        w = (xdt[ck] * dte_e).astype(BF16)
        snew.append([_dot(bm[ck][:, groups[g]].T.astype(BF16), w[:, groups[g]])
                     for g in range(SSM_GROUPS)])

    state = [st_ref[g] for g in range(SSM_GROUPS)]
    entering = []
    for ck in chunks:
        entering.append([st.astype(BF16) for st in state])
        cdl = eacs_e[ck][CHUNK - 1:CHUNK, :]
        state = [state[g] * cdl[:, groups[g]] + snew[ck][g] for g in range(SSM_GROUPS)]
    for g in range(SSM_GROUPS):
        st_ref[g] = state[g]

    ys = []
    for ck in chunks:
        xdt_b = xdt[ck].astype(BF16)
        xstack = jnp.concatenate(
            [jnp.where(hid == h, xdt_b, jnp.zeros_like(xdt_b)) for h in range(N_HEADS)], axis=0)
        yoff = jnp.concatenate([_dot(cm[ck][:, groups[g]], entering[ck][g]) for g in range(SSM_GROUPS)], axis=1)
        y = _dot(mcat[ck], xstack) + yoff * eacs_e[ck] + dexp_ref[...] * xs[ck]
        ys.append(y * z_ref[rows[ck], :].astype(F32))

    for ck in chunks:
        msq = _dot((ys[ck] * ys[ck]).astype(BF16), g_ref[...])
        out_ref[rows[ck], :] = (ys[ck] * lax.rsqrt(msq + RMS_EPS) * nw_ref[...]).astype(BF16)


def _conv_shift_matrices():
    r = jnp.arange(CHUNK)[:, None]
    col = jnp.arange(2 * CHUNK)[None, :]
    return jnp.stack([(col == CHUNK + r - j) for j in range(1, SSM_CONV_K)]).astype(BF16)


def _ssd(proj, dt, cw, cb, shift, alog, dexp, nw, g128, layer, b, seq):
    m = proj.shape[0]
    ts = SSD_CPS * CHUNK
    assert seq % ts == 0
    nc = seq // ts
    const = lambda i, c: (0, 0)
    return pl.pallas_call(
        _ssd_kernel,
        grid=(b, nc),
        in_specs=[
            pl.BlockSpec((ts, GROUP_W), lambda i, c: (i * nc + c, COL_Z // GROUP_W)),
            pl.BlockSpec((ts, SSM_CONV_DIM), lambda i, c: (i * nc + c, COL_XBC // SSM_CONV_DIM)),
            pl.BlockSpec((CHUNK, SSM_CONV_DIM),
                         lambda i, c: (jnp.maximum((i * nc + c) * SSD_CPS - 1, 0), COL_XBC // SSM_CONV_DIM)),
            pl.BlockSpec((ts, DT_W), lambda i, c: (i * nc + c, 0)),
            _per_layer((SSM_CONV_K, SSM_CONV_DIM), layer),
            _per_layer((1, SSM_CONV_DIM), layer),
            pl.BlockSpec((SSM_CONV_K - 1, CHUNK, 2 * CHUNK), lambda i, c: (0, 0, 0)),
            _per_layer((1, DT_W), layer),
            _per_layer((1, GROUP_W), layer),
            _per_layer((1, GROUP_W), layer),
            pl.BlockSpec((GROUP_W, GROUP_W), const),
        ],
        out_specs=pl.BlockSpec((ts, GROUP_W), lambda i, c: (i * nc + c, 0)),
        out_shape=jax.ShapeDtypeStruct((m, GROUP_W), BF16),
        scratch_shapes=[pltpu.VMEM((SSM_GROUPS, SSM_D_STATE, LANES), F32)],
        compiler_params=_cparams(("parallel", "arbitrary")),
        name="ssd",
    )(proj, proj, proj, dt, cw, cb, shift, alog, dexp, nw, g128)


def _alibi_slope(h):
    return 2.0 ** (-8.0 * (h + 1) / N_HEADS)


def _head_slot(x, h):
    base = x[:, (h // 2) * LANES:(h // 2 + 1) * LANES]
    return pltpu.roll(base, HEAD_DIM, 1) if h % 2 else base


def _attn_kernel(q_ref, k_ref, v_ref, qnw_ref, knw_ref, lq1_ref, lk1_ref, lq2_ref, lk2_ref, subw_ref,
                 g_ref, out_ref, ka_ref, vt_ref, qs_ref, acc_ref, s0_ref, kmax_ref, *, lam_init):
    qi = pl.program_id(1)
    tq, tk = ATT_TQ, ATT_TK
    seq = k_ref.shape[0]

    def aug_lanes(pos, slope, key_side):
        lane = lax.broadcasted_iota(jnp.int32, pos.shape, 1)
        hi = (slope * LANES) * lax.shift_right_logical(pos, 7).astype(F32)
        lo = slope * (pos & (LANES - 1)).astype(F32)
        one = jnp.ones_like(hi)
        c = (one, one, hi, lo) if key_side else (-hi, -lo, one, one)
        last = jnp.where(lane == 68, 1.0, 0.0) if key_side else 0.0
        return jnp.where(lane == 64, c[0], jnp.where(lane == 65, c[1],
                         jnp.where(lane == 66, c[2], jnp.where(lane == 67, c[3], last))))

    @pl.when(qi == 0)
    def _():
        k = k_ref[...].astype(F32)
        ms = _dot((k * k).astype(BF16), g_ref[...])
        kn = k * lax.rsqrt(ms + RMS_EPS) * knw_ref[...]
        ksq = _dot((kn * kn).astype(BF16), g_ref[...]) * DA_QK_DIM
        kmax_ref[...] = jnp.broadcast_to(jnp.max(ksq, axis=0, keepdims=True), kmax_ref.shape)
        pos = lax.broadcasted_iota(jnp.int32, (seq, LANES), 0)
        lane = lax.broadcasted_iota(jnp.int32, (seq, LANES), 1)
        for h in range(N_HEADS):
            aug = aug_lanes(pos, _alibi_slope(h), True)
            ka_ref[h] = jnp.where(lane < HEAD_DIM, _head_slot(kn, h), aug).astype(BF16)
        tail = (lax.broadcasted_iota(jnp.int32, (VT_ROWS - HEAD_DIM, tk), 0) == 0).astype(BF16)
        for j in range(seq // tk):
            vt = v_ref[j * tk:(j + 1) * tk, :].astype(F32).T.astype(BF16)
            for h in range(N_HEADS):
                vt_ref[j, h, 0:HEAD_DIM, :] = vt[h * HEAD_DIM:(h + 1) * HEAD_DIM, :]
                vt_ref[j, h, HEAD_DIM:VT_ROWS, :] = tail

    q = q_ref[...].astype(F32)
    ms = _dot((q * q).astype(BF16), g_ref[...])
    qn = q * lax.rsqrt(ms + RMS_EPS) * (qnw_ref[...] * (DA_QK_DIM ** -0.5))
    qsq = _dot((qn * qn).astype(BF16), g_ref[...]) * DA_QK_DIM
    bound = jnp.sqrt(qsq * kmax_ref[0:1, :]) * ATT_BOUND_MARGIN
    one_pass = jnp.max(bound) <= ATT_ONE_PASS_BOUND
    pos = qi * tq + lax.broadcasted_iota(jnp.int32, (tq, LANES), 0)
    lane = lax.broadcasted_iota(jnp.int32, (tq, LANES), 1)
    for h in range(N_HEADS):
        base = _head_slot(qn, h)
        bnd = _head_slot(bound, h)
        aug = aug_lanes(pos, _alibi_slope(h), False)
        for comp in range(2):
            shift = jnp.where(lane == 68, -bnd[:, comp * DA_QK_DIM:comp * DA_QK_DIM + 1], aug)
            own = (lane >= comp * DA_QK_DIM) & (lane < (comp + 1) * DA_QK_DIM)
            qs_ref[h, comp * tq:(comp + 1) * tq, :] = jnp.where(
                own, base, jnp.where(lane < HEAD_DIM, 0.0, shift)).astype(BF16)
    acc_ref[...] = jnp.zeros_like(acc_ref)

    qb = ATT_QB
    items = [(h, n) for h in range(N_HEADS) for n in range(2 * tq // qb)]

    def scores(item, j):
        h, n = item
        k0 = pl.multiple_of(j * tk, tk)
        return _dot_nt(ka_ref[h, pl.ds(k0, tk), :], qs_ref[h, n * qb:(n + 1) * qb, :])

    def tile(j, ms_, masked):
        new_m = []
        s_next = s0_ref[...]
        for i, (h, n) in enumerate(items):
            cols = slice(n * qb, (n + 1) * qb)
            s = s_next
            if i + 1 < len(items):
                s_next = scores(items[i + 1], j)
            elif not masked:
                s0_ref[...] = scores(items[0], j + 1)
            if masked:
                kk = lax.broadcasted_iota(jnp.int32, (tk, qb), 0)
                qq = (lax.broadcasted_iota(jnp.int32, (tk, qb), 1) + n * qb) & (tq - 1)
                s = jnp.where(kk <= qq, s, NEG_BIG)
            m_old = ms_[i]
            m_new = jnp.maximum(m_old, jnp.max(s, axis=0, keepdims=True))
            alpha = jnp.exp(m_old - m_new)
            p = jnp.exp(s - m_new)
            new_m.append(m_new)
            acc_ref[h, :, cols] = acc_ref[h, :, cols] * alpha + _dot(vt_ref[j, h], p.astype(BF16))
        return tuple(new_m)

    def tile_one_pass(j, masked):
        s_next = s0_ref[...]
        for i, (h, n) in enumerate(items):
            cols = slice(n * qb, (n + 1) * qb)
            s = s_next
            if i + 1 < len(items):
                s_next = scores(items[i + 1], j)
            elif not masked:
                s0_ref[...] = scores(items[0], j + 1)
            if masked:
                kk = lax.broadcasted_iota(jnp.int32, (tk, qb), 0)
                qq = (lax.broadcasted_iota(jnp.int32, (tk, qb), 1) + n * qb) & (tq - 1)
                s = jnp.where(kk <= qq, s, NEG_BIG)
            acc_ref[h, :, cols] += _dot(vt_ref[j, h], jnp.exp(s).astype(BF16))

    s0_ref[...] = scores(items[0], 0)

    @pl.when(one_pass)
    def _():
        def body(j, carry):
            tile_one_pass(j, False)
            return carry
        lax.fori_loop(0, qi, body, 0)
        tile_one_pass(qi, True)

    @pl.when(jnp.logical_not(one_pass))
    def _():
        m0 = tuple(jnp.full((1, qb), NEG_BIG, F32) for _ in items)
        m1 = lax.fori_loop(0, qi, lambda j, c: tile(j, c, False), m0)
        tile(qi, m1, True)

    def lane_sum(x):
        return jnp.broadcast_to(jnp.sum(x, axis=1, keepdims=True), x.shape)

    lam = (jnp.exp(lane_sum(lq1_ref[...] * lk1_ref[...]))
           - jnp.exp(lane_sum(lq2_ref[...] * lk2_ref[...])) + lam_init)
    lam = jnp.concatenate([lam] * (tq // LANES), axis=1)
    outs = []
    for h in range(N_HEADS):
        o = acc_ref[h, 0:HEAD_DIM, :] / acc_ref[h, HEAD_DIM:HEAD_DIM + 1, :]
        oh = o[:, 0:tq] - lam * o[:, tq:2 * tq]
        ms = jnp.mean(oh * oh, axis=0, keepdims=True)
        outs.append(oh * lax.rsqrt(ms + RMS_EPS) * (subw_ref[...] * (1.0 - lam_init)))
    out_ref[...] = jnp.concatenate(outs, axis=0).T.astype(BF16)


def _attn(proj, qnw, knw, lq1, lk1, lq2, lk2, subw, g32, layer, b, seq, lam_init):
    m = proj.shape[0]
    assert ATT_TQ == ATT_TK
    nq = seq // ATT_TQ
    const = lambda i, j: (0, 0)
    return pl.pallas_call(
        functools.partial(_attn_kernel, lam_init=lam_init),
        grid=(b, nq),
        in_specs=[
            pl.BlockSpec((ATT_TQ, GROUP_W), lambda i, j: (i * nq + j, COL_Q // GROUP_W)),
            pl.BlockSpec((seq, GROUP_W), lambda i, j: (i, COL_K // GROUP_W)),
            pl.BlockSpec((seq, GROUP_W), lambda i, j: (i, COL_V // GROUP_W)),
            _per_layer((1, GROUP_W), layer),
            _per_layer((1, GROUP_W), layer),
            _per_layer((1, LANES), layer),
            _per_layer((1, LANES), layer),
            _per_layer((1, LANES), layer),
            _per_layer((1, LANES), layer),
            _per_layer((HEAD_DIM, ATT_TQ), layer),
            pl.BlockSpec((GROUP_W, GROUP_W), const),
        ],
        out_specs=pl.BlockSpec((ATT_TQ, GROUP_W), lambda i, j: (i * nq + j, 0)),
        out_shape=jax.ShapeDtypeStruct((m, GROUP_W), BF16),
        scratch_shapes=[
            pltpu.VMEM((N_HEADS, seq, LANES), BF16),
            pltpu.VMEM((seq // ATT_TK, N_HEADS, VT_ROWS, ATT_TK), BF16),
            pltpu.VMEM((N_HEADS, 2 * ATT_TQ, LANES), BF16),
            pltpu.VMEM((N_HEADS, VT_ROWS, 2 * ATT_TQ), F32),
            pltpu.VMEM((ATT_TK, ATT_QB), F32),
            pltpu.VMEM((8, GROUP_W), F32),
        ],
        compiler_params=_cparams(("parallel", "arbitrary")),
        name="diffattn",
    )(proj, proj, proj, qnw, knw, lq1, lk1, lq2, lk2, subw, g32)


def _ffn_kernel(x_ref, ya_ref, yb_ref, yc_ref, yd_ref, wo_ref, nw_ref, wg_ref, wu_ref, wd_ref,
                out_ref, act_ref):
    x1 = x_ref[...]
    for i, y_ref in enumerate((ya_ref, yb_ref, yc_ref, yd_ref)):
        x1 = x1 + _dot(y_ref[...], wo_ref[i * GROUP_W:(i + 1) * GROUP_W, :])
    ms = jnp.mean(x1 * x1, axis=-1, keepdims=True)
    h = (x1 * lax.rsqrt(ms + RMS_EPS) * nw_ref[...]).astype(BF16)
    for c0 in range(0, D_FF, FFN_CHUNK):
        c1 = min(c0 + FFN_CHUNK, D_FF)
        g = _dot(h, wg_ref[:, c0:c1])
        u = _dot(h, wu_ref[:, c0:c1])
        act_ref[:, c0:c1] = (g * _sigmoid(g) * u).astype(BF16)
    out_ref[...] = x1 + _dot(act_ref[...], wd_ref[...])


def _ffn(x2, ya, yb, yc, yd, wo, nw, wg, wu, wd, layer, tm):
    m = x2.shape[0]
    row = lambda i: (i, 0)
    resident = lambda shape: pl.BlockSpec((None,) + shape, lambda i: (layer, 0, 0),
                                          pipeline_mode=pl.Buffered(1))
    return pl.pallas_call(
        _ffn_kernel,
        grid=(m // tm,),
        in_specs=[
            pl.BlockSpec((tm, D_MODEL), row),
            pl.BlockSpec((tm, GROUP_W), row),
            pl.BlockSpec((tm, GROUP_W), row),
            pl.BlockSpec((tm, GROUP_W), row),
            pl.BlockSpec((tm, GROUP_W), row),
            resident((D_MODEL, D_MODEL)),
            _per_layer((1, D_MODEL), layer),
            resident((D_MODEL, D_FF)),
            resident((D_MODEL, D_FF)),
            resident((D_FF, D_MODEL)),
        ],
        out_specs=pl.BlockSpec((tm, D_MODEL), row),
        out_shape=jax.ShapeDtypeStruct((m, D_MODEL), F32),
        scratch_shapes=[pltpu.VMEM((tm, D_FF), BF16)],
        compiler_params=_cparams(("parallel",)),
        name="outproj_ffn",
    )(x2, ya, yb, yc, yd, wo, nw, wg, wu, wd)


def _block_diag_mean(width, group):
    idx = jnp.arange(width) // group
    return jnp.where(idx[:, None] == idx[None, :], 1.0 / group, 0.0).astype(BF16)


def _rows(v, width=None):
    v = v.reshape(v.shape[0], 1, -1).astype(F32)
    return v if width is None else jnp.pad(v, ((0, 0), (0, 0), (0, width - v.shape[2])))


def kernel(x, norm1_w, w_in, gm_norm_w, gm_ws, gm_bs, ssm_conv_w, ssm_conv_b, ssm_dt_bias, ssm_a_log, ssm_d, ssm_norm_w, pool_w, pool_scale, da_q_norm_w, da_k_norm_w, da_lambda_q1, da_lambda_k1, da_lambda_q2, da_lambda_k2, da_subln_w, w_out, norm2_w, ffn_w_gate, ffn_w_up, ffn_w_down):
    b, seq, d = x.shape
    depth = w_in.shape[0]
    assert d == D_MODEL and seq % ATT_TQ == 0 and seq % CHUNK == 0
    m = b * seq
    tm = 512 if m % 512 == 0 else ATT_TQ

    g64 = _block_diag_mean(GROUP_W, HEAD_DIM)
    g128 = _block_diag_mean(GROUP_W, SSM_D_STATE)
    g32 = _block_diag_mean(GROUP_W, DA_QK_DIM)

    shift = _conv_shift_matrices()
    wo_b, wg_b, wu_b, wd_b = (w.astype(BF16) for w in (w_out, ffn_w_gate, ffn_w_up, ffn_w_down))

    assert w_in.shape[1:] == (D_MODEL, D_IN)

    n1, n2 = _rows(norm1_w), _rows(norm2_w)
    dtb, alog = _rows(ssm_dt_bias, DT_W), _rows(ssm_a_log, DT_W)
    bsm = jnp.repeat(jnp.swapaxes(gm_bs, 1, 2), HEAD_DIM, axis=2)
    gmn = _rows(gm_norm_w)
    cw, cb = jnp.swapaxes(ssm_conv_w, 1, 2), _rows(ssm_conv_b)
    dexp, ssn = _rows(jnp.repeat(ssm_d, HEAD_DIM, axis=1)), _rows(ssm_norm_w)
    eye = jnp.eye(len(POOL_WINDOWS), dtype=pool_w.dtype)
    wbd = jnp.einsum('lgab,gh->lgahb', pool_w, eye).reshape(depth, GROUP_W, GROUP_W).astype(BF16)
    psc = _rows(pool_scale)
    qnw = _rows(jnp.tile(da_q_norm_w, (1, GROUP_W // DA_QK_DIM)))
    knw = _rows(jnp.tile(da_k_norm_w, (1, GROUP_W // DA_QK_DIM)))
    lams = [_rows(v, LANES) for v in (da_lambda_q1, da_lambda_k1, da_lambda_q2, da_lambda_k2)]
    subw = jnp.broadcast_to(da_subln_w[:, :, None], (depth, HEAD_DIM, ATT_TQ))

    x2 = x.reshape(m, d)
    for i in range(depth):
        proj, dt = _inproj(x2, n1, w_in, i, dtb, tm)
        ya = _gmlp(proj, gm_ws, bsm, gmn, g64, i, seq if seq % tm == 0 else tm)
        yb = _ssd(proj, dt, cw, cb, shift, alog, dexp, ssn, g128, i, b, seq)
        yc = _pool(proj, wbd, psc, i, b, seq)
        lam_init = 0.8 - 0.6 * math.exp(-0.3 * i)
        yd = _attn(proj, qnw, knw, *lams, subw, g32, i, b, seq, lam_init)
        x2 = _ffn(x2, ya, yb, yc, yd, wo_b, n2, wg_b, wu_b, wd_b, i, tm)
    return x2.reshape(b, seq, d)
```

```python
import functools
import math

import jax
import jax.numpy as jnp
from jax import lax
from jax.experimental import pallas as pl
from jax.experimental.pallas import tpu as pltpu

F32 = jnp.float32
BF16 = jnp.bfloat16

D_MODEL = 1024
GROUP_W = 256
CHUNK = 128
HEAD_DIM = 64
N_HEADS = 4
SSM_GROUPS = 2
SSM_D_STATE = 128
SSM_CONV_K = 4
SSM_CONV_DIM = GROUP_W + 2 * SSM_GROUPS * SSM_D_STATE
POOL_WINDOWS = (2, 4, 8, 16)
DA_QK_DIM = 32
D_FF = 2816
RMS_EPS = 1e-6
NEG_BIG = -1e30

LANES = 128
VMEM_LIMIT = 56 * 1024 * 1024

COL_PA = 0
COL_Z = 512
COL_XBC = 768
COL_PC = 1536
COL_Q = 1792
COL_K = 2048
COL_V = 2304
PROJ_W = 2560
DT_W = LANES
D_IN = PROJ_W + N_HEADS
DT_COL = COL_PC
IN_CHUNK = 256
FFN_CHUNK = 512
SSD_CPS = 4

ATT_TQ = 512
ATT_TK = 512
ATT_QB = 512
ATT_BOUND_MARGIN = 1.01
ATT_ONE_PASS_BOUND = 40.0
VT_ROWS = 80


def _cparams(sem):
    return pltpu.CompilerParams(dimension_semantics=sem, vmem_limit_bytes=VMEM_LIMIT)


def _sigmoid(x):
    return 1.0 / (1.0 + jnp.exp(-x))


def _dot(a, b):
    return jnp.dot(a, b, preferred_element_type=F32)


def _dot_nt(a, b):
    return lax.dot_general(a, b, (((1,), (1,)), ((), ())), preferred_element_type=F32)


def _softplus(t):
    return jnp.maximum(t, 0.0) + jnp.log(1.0 + jnp.exp(-jnp.abs(t)))


def _inproj_kernel(x_ref, nw_ref, w_ref, dtb_ref, proj_ref, dt_ref, wb_ref):
    @pl.when(pl.program_id(0) == 0)
    def _():
        for n0 in range(0, DT_COL, 2 * IN_CHUNK):
            wb_ref[:, n0:n0 + 2 * IN_CHUNK] = w_ref[:, n0:n0 + 2 * IN_CHUNK].astype(BF16)
        tail = w_ref[:, DT_COL:D_IN]
        wb_ref[:, DT_COL:PROJ_W] = tail[:, N_HEADS:].astype(BF16)
        wb_ref[:, PROJ_W:PROJ_W + DT_W] = tail[:, 0:DT_W].astype(BF16)

    x = x_ref[...]
    ms = jnp.mean(x * x, axis=-1, keepdims=True)
    h = (x * lax.rsqrt(ms + RMS_EPS) * nw_ref[...]).astype(BF16)
    for n0 in range(0, PROJ_W, IN_CHUNK):
        y = _dot(h, wb_ref[:, n0:n0 + IN_CHUNK])
        if n0 < COL_Z:
            y = jax.nn.gelu(y, approximate=True)
        elif n0 < COL_XBC:
            y = y * _sigmoid(y)
        proj_ref[:, n0:n0 + IN_CHUNK] = y.astype(BF16)
    dt_ref[...] = _softplus(_dot(h, wb_ref[:, PROJ_W:PROJ_W + DT_W]) + dtb_ref[...])


def _per_layer(shape, layer):
    return pl.BlockSpec((None,) + tuple(shape), lambda *_: (layer,) + (0,) * len(shape))


def _inproj(x2, nw, w_in, layer, dtb, tm):
    m = x2.shape[0]
    return pl.pallas_call(
        _inproj_kernel,
        grid=(m // tm,),
        in_specs=[
            pl.BlockSpec((tm, D_MODEL), lambda i: (i, 0)),
            _per_layer((1, D_MODEL), layer),
            pl.BlockSpec((None, D_MODEL, D_IN), lambda i: (layer, 0, 0), pipeline_mode=pl.Buffered(1)),
            _per_layer((1, DT_W), layer),
        ],
        out_specs=[
            pl.BlockSpec((tm, PROJ_W), lambda i: (i, 0)),
            pl.BlockSpec((tm, DT_W), lambda i: (i, 0)),
        ],
        out_shape=[
            jax.ShapeDtypeStruct((m, PROJ_W), BF16),
            jax.ShapeDtypeStruct((m, DT_W), F32),
        ],
        scratch_shapes=[pltpu.VMEM((D_MODEL, PROJ_W + DT_W), BF16)],
        compiler_params=_cparams(("arbitrary",)),
        name="inproj",
    )(x2, nw, w_in, dtb)


def _head_id(shape, width):
    lane = lax.broadcasted_iota(jnp.int32, shape, 1)
    return lax.shift_right_logical(lane, int(math.log2(width)))


def _gmlp_kernel(pa_ref, ws_ref, bsm_ref, nw_ref, g_ref, out_ref):
    t = pa_ref.shape[0]
    hact = pa_ref[...].astype(F32)
    u = hact[:, :GROUP_W]
    v = hact[:, GROUP_W:]
    ms = _dot((v * v).astype(BF16), g_ref[...])
    vn = (v * lax.rsqrt(ms + RMS_EPS) * nw_ref[...]).astype(BF16)
    row = lax.broadcasted_iota(jnp.int32, (CHUNK, CHUNK), 0)
    col = lax.broadcasted_iota(jnp.int32, (CHUNK, CHUNK), 1)
    wcat = jnp.concatenate(
        [jnp.where(row >= col, ws_ref[h], 0.0) for h in range(N_HEADS)], axis=1).astype(BF16)
    hid = _head_id((CHUNK, GROUP_W), HEAD_DIM)
    for c in range(t // CHUNK):
        vc = vn[c * CHUNK:(c + 1) * CHUNK]
        vstack = jnp.concatenate(
            [jnp.where(hid == h, vc, jnp.zeros_like(vc)) for h in range(N_HEADS)], axis=0)
        s = _dot(wcat, vstack) + bsm_ref[...]
        out_ref[c * CHUNK:(c + 1) * CHUNK, :] = (u[c * CHUNK:(c + 1) * CHUNK] * s).astype(BF16)


def _gmlp(proj, ws, bsm, nw, g64, layer, t):
    m = proj.shape[0]
    return pl.pallas_call(
        _gmlp_kernel,
        grid=(m // t,),
        in_specs=[
            pl.BlockSpec((t, 2 * GROUP_W), lambda i: (i, COL_PA // (2 * GROUP_W))),
            _per_layer((N_HEADS, CHUNK, CHUNK), layer),
            _per_layer((CHUNK, GROUP_W), layer),
            _per_layer((1, GROUP_W), layer),
            pl.BlockSpec((GROUP_W, GROUP_W), lambda i: (0, 0)),
        ],
        out_specs=pl.BlockSpec((t, GROUP_W), lambda i: (i, 0)),
        out_shape=jax.ShapeDtypeStruct((m, GROUP_W), BF16),
        compiler_params=_cparams(("parallel",)),
        name="gmlp",
    )(proj, ws, bsm, nw, g64)


def _pool_kernel(pc_ref, wbd_ref, scale_ref, out_ref):
    h = pc_ref[...].astype(F32)
    shape = h.shape
    row = lax.broadcasted_iota(jnp.int32, shape, 0)

    def shifted(x, k):
        return jnp.where(row >= k, pltpu.roll(x, k, 0), 0.0)

    s2 = h + shifted(h, 1)
    s4 = s2 + shifted(s2, 2)
    s8 = s4 + shifted(s4, 4)
    s16 = s8 + shifted(s8, 8)
    gid = _head_id(shape, HEAD_DIM)
    wsum = jnp.where(gid == 0, s2, jnp.where(gid == 1, s4, jnp.where(gid == 2, s8, s16)))
    win = jnp.where(gid == 0, POOL_WINDOWS[0],
                    jnp.where(gid == 1, POOL_WINDOWS[1],
                              jnp.where(gid == 2, POOL_WINDOWS[2], POOL_WINDOWS[3])))
    cnt = jnp.minimum(win, row + 1).astype(F32)
    p = wsum / cnt - h
    y = _dot(p.astype(BF16), wbd_ref[...]) * scale_ref[...]
    out_ref[...] = y.astype(BF16)


def _pool(proj, wbd, scale, layer, b, seq):
    m = proj.shape[0]
    return pl.pallas_call(
        _pool_kernel,
        grid=(b,),
        in_specs=[
            pl.BlockSpec((seq, GROUP_W), lambda i: (i, COL_PC // GROUP_W)),
            _per_layer((GROUP_W, GROUP_W), layer),
            _per_layer((1, GROUP_W), layer),
        ],
        out_specs=pl.BlockSpec((seq, GROUP_W), lambda i: (i, 0)),
        out_shape=jax.ShapeDtypeStruct((m, GROUP_W), BF16),
        compiler_params=_cparams(("parallel",)),
        name="pool",
    )(proj, wbd, scale)


def _split3(a):
    a1 = a.astype(BF16)
    r1 = a - a1.astype(F32)
    a2 = r1.astype(BF16)
    r2 = r1 - a2.astype(F32)
    return a1, a2, r2.astype(BF16)


def _ssd_kernel(z_ref, xbc_ref, halo_ref, dt_ref, cw_ref, cb_ref, shift_ref, alog_ref, dexp_ref, nw_ref,
                g_ref, out_ref, st_ref, between_phases=None):
    c = pl.program_id(1)

    @pl.when(c == 0)
    def _():
        st_ref[...] = jnp.zeros_like(st_ref)

    row = lax.broadcasted_iota(jnp.int32, (CHUNK, CHUNK), 0)
    col = lax.broadcasted_iota(jnp.int32, (CHUNK, CHUNK), 1)
    causal = row >= col
    ltri = jnp.where(causal, 1.0, 0.0).astype(BF16)
    hid = _head_id((CHUNK, GROUP_W), HEAD_DIM)
    neg_a = -jnp.exp(alog_ref[...])

    def expand(c4):
        return jnp.where(hid == 0, c4[:, 0:1],
                         jnp.where(hid == 1, c4[:, 1:2],
                                   jnp.where(hid == 2, c4[:, 2:3], c4[:, 3:4])))

    chunks = range(SSD_CPS)
    rows = [slice(ck * CHUNK, (ck + 1) * CHUNK) for ck in chunks]
    groups = [slice(g * SSM_D_STATE, (g + 1) * SSM_D_STATE) for g in range(SSM_GROUPS)]

    xs, bm, cm = [], [], []
    halo = jnp.where(c > 0, halo_ref[...], jnp.zeros_like(halo_ref))
    for ck in chunks:
        xcur = xbc_ref[rows[ck], :]
        acc = cb_ref[...] + cw_ref[SSM_CONV_K - 1:SSM_CONV_K, :] * xcur.astype(F32)
        for j in range(1, SSM_CONV_K):
            k = SSM_CONV_K - 1 - j
            if ck == 0:
                shifted = (_dot(shift_ref[j - 1, :, 0:CHUNK], halo)
                           + _dot(shift_ref[j - 1, :, CHUNK:2 * CHUNK], xcur))
            else:
                shifted = _dot(shift_ref[j - 1], xbc_ref[(ck - 1) * CHUNK:(ck + 1) * CHUNK, :])
            acc = acc + cw_ref[k:k + 1, :] * shifted
        xc = acc * _sigmoid(acc)
        xs.append(xc[:, :GROUP_W])
        bm.append(xc[:, GROUP_W:2 * GROUP_W])
        cm.append(xc[:, 2 * GROUP_W:].astype(BF16))

    acs, acs_t = [], []
    for ck in chunks:
        a1, a2, a3 = _split3(dt_ref[rows[ck], :] * neg_a)
        acs.append(_dot(ltri, a1) + _dot(ltri, a2) + _dot(ltri, a3))
        acs_t.append(acs[ck].T)

    if between_phases is not None:
        between_phases()

    xdt, eacs_e, mcat, snew = [], [], [], []
    for ck in chunks:
        xdt.append(xs[ck] * expand(dt_ref[rows[ck], :]))
        eacs_e.append(expand(jnp.exp(acs[ck])))
        dte_e = expand(jnp.exp(acs[ck][CHUNK - 1:CHUNK, :] - acs[ck]))
        ms = []
        for g in range(SSM_GROUPS):
            cb = _dot_nt(cm[ck][:, groups[g]], bm[ck][:, groups[g]].astype(BF16))
            for hh in range(N_HEADS // SSM_GROUPS):
                h = g * (N_HEADS // SSM_GROUPS) + hh
                seg = acs[ck][:, h:h + 1] - acs_t[ck][h:h + 1, :]
                ms.append((cb * jnp.exp(jnp.where(causal, seg, NEG_BIG))).astype(BF16))
        mcat.append(jnp.concatenate(ms, axis=1))
        w = (xdt[ck] * dte_e).astype(BF16)
        snew.append([_dot(bm[ck][:, groups[g]].T.astype(BF16), w[:, groups[g]])
                     for g in range(SSM_GROUPS)])

    state = [st_ref[g] for g in range(SSM_GROUPS)]
    entering = []
    for ck in chunks:
        entering.append([st.astype(BF16) for st in state])
        cdl = eacs_e[ck][CHUNK - 1:CHUNK, :]
        state = [state[g] * cdl[:, groups[g]] + snew[ck][g] for g in range(SSM_GROUPS)]
    for g in range(SSM_GROUPS):
        st_ref[g] = state[g]

    ys = []
    for ck in chunks:
        xdt_b = xdt[ck].astype(BF16)
        xstack = jnp.concatenate(
            [jnp.where(hid == h, xdt_b, jnp.zeros_like(xdt_b)) for h in range(N_HEADS)], axis=0)
        yoff = jnp.concatenate([_dot(cm[ck][:, groups[g]], entering[ck][g]) for g in range(SSM_GROUPS)], axis=1)
        y = _dot(mcat[ck], xstack) + yoff * eacs_e[ck] + dexp_ref[...] * xs[ck]
        ys.append(y * z_ref[rows[ck], :].astype(F32))

    for ck in chunks:
        msq = _dot((ys[ck] * ys[ck]).astype(BF16), g_ref[...])
        out_ref[rows[ck], :] = (ys[ck] * lax.rsqrt(msq + RMS_EPS) * nw_ref[...]).astype(BF16)


def _conv_shift_matrices():
    r = jnp.arange(CHUNK)[:, None]
    col = jnp.arange(2 * CHUNK)[None, :]
    return jnp.stack([(col == CHUNK + r - j) for j in range(1, SSM_CONV_K)]).astype(BF16)


def _pool_bands():
    r = jnp.arange(CHUNK)[:, None]
    col = jnp.arange(2 * CHUNK)[None, :]
    back = CHUNK + r - col
    bands = [jnp.where((back >= 0) & (back < w), 1.0 / w, 0.0) for w in POOL_WINDOWS]
    bandcat = jnp.concatenate(bands, axis=1).astype(BF16)
    t = jnp.arange(CHUNK, dtype=F32)[:, None] + 1.0
    win = jnp.repeat(jnp.asarray(POOL_WINDOWS, F32), HEAD_DIM)[None, :]
    fix = win / jnp.minimum(win, t)
    return bandcat, fix


def _pool_chunks(pc_ref, pch_ref, band_ref, fix_ref, wbd_ref, scale_ref, out_ref, first_step):
    hid = _head_id((CHUNK, GROUP_W), HEAD_DIM)
    prev = jnp.where(first_step, jnp.zeros_like(pch_ref), pch_ref[...])
    for ck in range(pc_ref.shape[0] // CHUNK):
        cur = pc_ref[ck * CHUNK:(ck + 1) * CHUNK, :]
        parts = []
        for g in range(len(POOL_WINDOWS)):
            parts += [jnp.where(hid == g, prev, jnp.zeros_like(prev)),
                      jnp.where(hid == g, cur, jnp.zeros_like(cur))]
        mean = _dot(band_ref[...], jnp.concatenate(parts, axis=0))
        if ck == 0:
            mean = jnp.where(first_step, mean * fix_ref[...], mean)
        p = mean - cur.astype(F32)
        y = _dot(p.astype(BF16), wbd_ref[...]) * scale_ref[...]
        out_ref[ck * CHUNK:(ck + 1) * CHUNK, :] = y.astype(BF16)
        prev = cur


N_GMLP_IN, N_POOL_IN, N_SSD_IN = 5, 6, 11


def _mixers_kernel(*refs):
    gm = refs[:N_GMLP_IN]
    po = refs[N_GMLP_IN:N_GMLP_IN + N_POOL_IN]
    sd = refs[N_GMLP_IN + N_POOL_IN:N_GMLP_IN + N_POOL_IN + N_SSD_IN]
    ya_ref, yc_ref, yb_ref, st_ref = refs[N_GMLP_IN + N_POOL_IN + N_SSD_IN:]
    first_step = pl.program_id(1) == 0

    def others():
        _gmlp_kernel(*gm, ya_ref)
        _pool_chunks(*po, yc_ref, first_step)

    _ssd_kernel(*sd, yb_ref, st_ref, between_phases=others)


def _mixers(proj, dt, gws, bsm, gmn, g64, band, fix, wbd, psc, cw, cb, shift, alog, dexp, nw, g128,
            layer, b, seq):
    m = proj.shape[0]
    ts = SSD_CPS * CHUNK
    assert seq % ts == 0
    nc = seq // ts
    const = lambda i, c: (0, 0)
    blk = lambda width, col: pl.BlockSpec((ts, width), lambda i, c: (i * nc + c, col // width))
    halo = lambda width, col: pl.BlockSpec(
        (CHUNK, width), lambda i, c: (jnp.maximum((i * nc + c) * SSD_CPS - 1, 0), col // width))
    out = pl.BlockSpec((ts, GROUP_W), lambda i, c: (i * nc + c, 0))
    gmlp_in = [blk(2 * GROUP_W, COL_PA), _per_layer((N_HEADS, CHUNK, CHUNK), layer),
               _per_layer((CHUNK, GROUP_W), layer), _per_layer((1, GROUP_W), layer),
               pl.BlockSpec((GROUP_W, GROUP_W), const)]
    pool_in = [blk(GROUP_W, COL_PC), halo(GROUP_W, COL_PC),
               pl.BlockSpec((CHUNK, len(POOL_WINDOWS) * 2 * CHUNK), const), pl.BlockSpec((CHUNK, GROUP_W), const),
               _per_layer((GROUP_W, GROUP_W), layer), _per_layer((1, GROUP_W), layer)]
    ssd_in = [blk(GROUP_W, COL_Z), blk(SSM_CONV_DIM, COL_XBC), halo(SSM_CONV_DIM, COL_XBC),
              pl.BlockSpec((ts, DT_W), lambda i, c: (i * nc + c, 0)),
              _per_layer((SSM_CONV_K, SSM_CONV_DIM), layer), _per_layer((1, SSM_CONV_DIM), layer),
              pl.BlockSpec((SSM_CONV_K - 1, CHUNK, 2 * CHUNK), lambda i, c: (0, 0, 0)),
              _per_layer((1, DT_W), layer), _per_layer((1, GROUP_W), layer), _per_layer((1, GROUP_W), layer),
              pl.BlockSpec((GROUP_W, GROUP_W), const)]
    assert (len(gmlp_in), len(pool_in), len(ssd_in)) == (N_GMLP_IN, N_POOL_IN, N_SSD_IN)
    return pl.pallas_call(
        _mixers_kernel,
        grid=(b, nc),
        in_specs=gmlp_in + pool_in + ssd_in,
        out_specs=[out, out, out],
        out_shape=[jax.ShapeDtypeStruct((m, GROUP_W), BF16)] * 3,
        scratch_shapes=[pltpu.VMEM((SSM_GROUPS, SSM_D_STATE, LANES), F32)],
        compiler_params=_cparams(("parallel", "arbitrary")),
        name="mixers",
    )(proj, gws, bsm, gmn, g64,
      proj, proj, band, fix, wbd, psc,
      proj, proj, proj, dt, cw, cb, shift, alog, dexp, nw, g128)


def _ssd(proj, dt, cw, cb, shift, alog, dexp, nw, g128, layer, b, seq):
    m = proj.shape[0]
    ts = SSD_CPS * CHUNK
    assert seq % ts == 0
    nc = seq // ts
    const = lambda i, c: (0, 0)
    return pl.pallas_call(
        _ssd_kernel,
        grid=(b, nc),
        in_specs=[
            pl.BlockSpec((ts, GROUP_W), lambda i, c: (i * nc + c, COL_Z // GROUP_W)),
            pl.BlockSpec((ts, SSM_CONV_DIM), lambda i, c: (i * nc + c, COL_XBC // SSM_CONV_DIM)),
            pl.BlockSpec((CHUNK, SSM_CONV_DIM),
                         lambda i, c: (jnp.maximum((i * nc + c) * SSD_CPS - 1, 0), COL_XBC // SSM_CONV_DIM)),
            pl.BlockSpec((ts, DT_W), lambda i, c: (i * nc + c, 0)),
            _per_layer((SSM_CONV_K, SSM_CONV_DIM), layer),
            _per_layer((1, SSM_CONV_DIM), layer),
            pl.BlockSpec((SSM_CONV_K - 1, CHUNK, 2 * CHUNK), lambda i, c: (0, 0, 0)),
            _per_layer((1, DT_W), layer),
            _per_layer((1, GROUP_W), layer),
            _per_layer((1, GROUP_W), layer),
            pl.BlockSpec((GROUP_W, GROUP_W), const),
        ],
        out_specs=pl.BlockSpec((ts, GROUP_W), lambda i, c: (i * nc + c, 0)),
        out_shape=jax.ShapeDtypeStruct((m, GROUP_W), BF16),
        scratch_shapes=[pltpu.VMEM((SSM_GROUPS, SSM_D_STATE, LANES), F32)],
        compiler_params=_cparams(("parallel", "arbitrary")),
        name="ssd",
    )(proj, proj, proj, dt, cw, cb, shift, alog, dexp, nw, g128)


def _alibi_slope(h):
    return 2.0 ** (-8.0 * (h + 1) / N_HEADS)


def _head_slot(x, h):
    base = x[:, (h // 2) * LANES:(h // 2 + 1) * LANES]
    return pltpu.roll(base, HEAD_DIM, 1) if h % 2 else base


def _attn_kernel(q_ref, k_ref, v_ref, qnw_ref, knw_ref, lq1_ref, lk1_ref, lq2_ref, lk2_ref, subw_ref,
                 g_ref, out_ref, ka_ref, vt_ref, qs_ref, acc_ref, s0_ref, kmax_ref, qa_ref, *, lam_init):
    qi = pl.program_id(1)
    tq, tk = ATT_TQ, ATT_TK
    seq = k_ref.shape[0]

    def aug_lanes(pos, slope, key_side):
        lane = lax.broadcasted_iota(jnp.int32, pos.shape, 1)
        hi = (slope * LANES) * lax.shift_right_logical(pos, 7).astype(F32)
        lo = slope * (pos & (LANES - 1)).astype(F32)
        one = jnp.ones_like(hi)
        c = (one, one, hi, lo) if key_side else (-hi, -lo, one, one)
        last = jnp.where(lane == 68, 1.0, 0.0) if key_side else 0.0
        return jnp.where(lane == 64, c[0], jnp.where(lane == 65, c[1],
                         jnp.where(lane == 66, c[2], jnp.where(lane == 67, c[3], last))))

    @pl.when(qi == 0)
    def _():
        k = k_ref[...].astype(F32)
        ms = _dot((k * k).astype(BF16), g_ref[...])
        kn = k * lax.rsqrt(ms + RMS_EPS) * knw_ref[...]
        ksq = _dot((kn * kn).astype(BF16), g_ref[...]) * DA_QK_DIM
        kmax_ref[...] = jnp.broadcast_to(jnp.max(ksq, axis=0, keepdims=True), kmax_ref.shape)
        pos = lax.broadcasted_iota(jnp.int32, (seq, LANES), 0)
        lane = lax.broadcasted_iota(jnp.int32, (seq, LANES), 1)
        for h in range(N_HEADS):
            aug = aug_lanes(pos, _alibi_slope(h), True)
            ka_ref[h] = jnp.where(lane < HEAD_DIM, _head_slot(kn, h), aug).astype(BF16)
            qa_ref[h] = aug_lanes(pos, _alibi_slope(h), False).astype(BF16)
        tail = (lax.broadcasted_iota(jnp.int32, (VT_ROWS - HEAD_DIM, tk), 0) == 0).astype(BF16)
        for j in range(seq // tk):
            vt = v_ref[j * tk:(j + 1) * tk, :].astype(F32).T.astype(BF16)
            for h in range(N_HEADS):
                vt_ref[j, h, 0:HEAD_DIM, :] = vt[h * HEAD_DIM:(h + 1) * HEAD_DIM, :]
                vt_ref[j, h, HEAD_DIM:VT_ROWS, :] = tail

    q = q_ref[...].astype(F32)
    ms = _dot((q * q).astype(BF16), g_ref[...])
    qn = q * lax.rsqrt(ms + RMS_EPS) * (qnw_ref[...] * (DA_QK_DIM ** -0.5))
    qsq = _dot((qn * qn).astype(BF16), g_ref[...]) * DA_QK_DIM
    b2 = qsq * kmax_ref[0:1, :]
    bound = b2 * lax.rsqrt(b2 + 1e-30) * ATT_BOUND_MARGIN
    one_pass = jnp.max(bound) <= ATT_ONE_PASS_BOUND
    lane = lax.broadcasted_iota(jnp.int32, (1, LANES), 1)
    q0 = pl.multiple_of(qi * tq, tq)
    for h in range(N_HEADS):
        base = _head_slot(qn, h)
        bnd = _head_slot(bound, h)
        qa = qa_ref[h, pl.ds(q0, tq), :]
        for comp in range(2):
            own = (lane >= comp * DA_QK_DIM) & (lane < (comp + 1) * DA_QK_DIM)
            y = jnp.where(own, base, 0.0)
            y = jnp.where(lane == 68, -bnd[:, comp * DA_QK_DIM:comp * DA_QK_DIM + 1], y)
            qs_ref[h, comp * tq:(comp + 1) * tq, :] = y.astype(BF16) + qa
    acc_ref[...] = jnp.zeros_like(acc_ref)

    qb = ATT_QB
    items = [(h, n) for h in range(N_HEADS) for n in range(2 * tq // qb)]

    def scores(item, j):
        h, n = item
        k0 = pl.multiple_of(j * tk, tk)
        return _dot_nt(ka_ref[h, pl.ds(k0, tk), :], qs_ref[h, n * qb:(n + 1) * qb, :])

    def tile(j, ms_, masked):
        new_m = []
        s_next = s0_ref[...]
        for i, (h, n) in enumerate(items):
            cols = slice(n * qb, (n + 1) * qb)
            s = s_next
            if i + 1 < len(items):
                s_next = scores(items[i + 1], j)
            elif not masked:
                s0_ref[...] = scores(items[0], j + 1)
            if masked:
                kk = lax.broadcasted_iota(jnp.int32, (tk, qb), 0)
                qq = (lax.broadcasted_iota(jnp.int32, (tk, qb), 1) + n * qb) & (tq - 1)
                s = jnp.where(kk <= qq, s, NEG_BIG)
            m_old = ms_[i]
            m_new = jnp.maximum(m_old, jnp.max(s, axis=0, keepdims=True))
            alpha = jnp.exp(m_old - m_new)
            p = jnp.exp(s - m_new)
            new_m.append(m_new)
            acc_ref[h, :, cols] = acc_ref[h, :, cols] * alpha + _dot(vt_ref[j, h], p.astype(BF16))
        return tuple(new_m)

    def tile_one_pass(j, masked):
        s_next = s0_ref[...]
        for i, (h, n) in enumerate(items):
            cols = slice(n * qb, (n + 1) * qb)
            s = s_next
            if i + 1 < len(items):
                s_next = scores(items[i + 1], j)
            elif not masked:
                s0_ref[...] = scores(items[0], j + 1)
            if masked:
                kk = lax.broadcasted_iota(jnp.int32, (tk, qb), 0)
                qq = (lax.broadcasted_iota(jnp.int32, (tk, qb), 1) + n * qb) & (tq - 1)
                s = jnp.where(kk <= qq, s, NEG_BIG)
            acc_ref[h, :, cols] += _dot(vt_ref[j, h], jnp.exp(s).astype(BF16))

    s0_ref[...] = scores(items[0], 0)

    @pl.when(one_pass)
    def _():
        def body(j, carry):
            tile_one_pass(j, False)
            return carry
        lax.fori_loop(0, qi, body, 0)
        tile_one_pass(qi, True)

    @pl.when(jnp.logical_not(one_pass))
    def _():
        m0 = tuple(jnp.full((1, qb), NEG_BIG, F32) for _ in items)
        m1 = lax.fori_loop(0, qi, lambda j, c: tile(j, c, False), m0)
        tile(qi, m1, True)

    def lane_sum(x):
        return jnp.broadcast_to(jnp.sum(x, axis=1, keepdims=True), x.shape)

    lam = (jnp.exp(lane_sum(lq1_ref[...] * lk1_ref[...]))
           - jnp.exp(lane_sum(lq2_ref[...] * lk2_ref[...])) + lam_init)
    lam = jnp.concatenate([lam] * (tq // LANES), axis=1)
    outs = []
    for h in range(N_HEADS):
        o = acc_ref[h, 0:HEAD_DIM, :] / acc_ref[h, HEAD_DIM:HEAD_DIM + 1, :]
        oh = o[:, 0:tq] - lam * o[:, tq:2 * tq]
        ms = jnp.mean(oh * oh, axis=0, keepdims=True)
        outs.append(oh * lax.rsqrt(ms + RMS_EPS) * (subw_ref[...] * (1.0 - lam_init)))
    out_ref[...] = jnp.concatenate(outs, axis=0).T.astype(BF16)


def _attn(proj, qnw, knw, lq1, lk1, lq2, lk2, subw, g32, layer, b, seq, lam_init):
    m = proj.shape[0]
    assert ATT_TQ == ATT_TK
    nq = seq // ATT_TQ
    const = lambda i, j: (0, 0)
    return pl.pallas_call(
        functools.partial(_attn_kernel, lam_init=lam_init),
        grid=(b, nq),
        in_specs=[
            pl.BlockSpec((ATT_TQ, GROUP_W), lambda i, j: (i * nq + j, COL_Q // GROUP_W)),
            pl.BlockSpec((seq, GROUP_W), lambda i, j: (i, COL_K // GROUP_W)),
            pl.BlockSpec((seq, GROUP_W), lambda i, j: (i, COL_V // GROUP_W)),
            _per_layer((1, GROUP_W), layer),
            _per_layer((1, GROUP_W), layer),
            _per_layer((1, LANES), layer),
            _per_layer((1, LANES), layer),
            _per_layer((1, LANES), layer),
            _per_layer((1, LANES), layer),
            _per_layer((HEAD_DIM, ATT_TQ), layer),
            pl.BlockSpec((GROUP_W, GROUP_W), const),
        ],
        out_specs=pl.BlockSpec((ATT_TQ, GROUP_W), lambda i, j: (i * nq + j, 0)),
        out_shape=jax.ShapeDtypeStruct((m, GROUP_W), BF16),
        scratch_shapes=[
            pltpu.VMEM((N_HEADS, seq, LANES), BF16),
            pltpu.VMEM((seq // ATT_TK, N_HEADS, VT_ROWS, ATT_TK), BF16),
            pltpu.VMEM((N_HEADS, 2 * ATT_TQ, LANES), BF16),
            pltpu.VMEM((N_HEADS, VT_ROWS, 2 * ATT_TQ), F32),
            pltpu.VMEM((ATT_TK, ATT_QB), F32),
            pltpu.VMEM((8, GROUP_W), F32),
            pltpu.VMEM((N_HEADS, seq, LANES), BF16),
        ],
        compiler_params=_cparams(("parallel", "arbitrary")),
        name="diffattn",
    )(proj, proj, proj, qnw, knw, lq1, lk1, lq2, lk2, subw, g32)


def _ffn_kernel(x_ref, ya_ref, yb_ref, yc_ref, yd_ref, wo_ref, nw_ref, wg_ref, wu_ref, wd_ref,
                out_ref, act_ref):
    x1 = x_ref[...]
    for i, y_ref in enumerate((ya_ref, yb_ref, yc_ref, yd_ref)):
        x1 = x1 + _dot(y_ref[...], wo_ref[i * GROUP_W:(i + 1) * GROUP_W, :])
    ms = jnp.mean(x1 * x1, axis=-1, keepdims=True)
    h = (x1 * lax.rsqrt(ms + RMS_EPS) * nw_ref[...]).astype(BF16)
    for c0 in range(0, D_FF, FFN_CHUNK):
        c1 = min(c0 + FFN_CHUNK, D_FF)
        g = _dot(h, wg_ref[:, c0:c1])
        u = _dot(h, wu_ref[:, c0:c1])
        act_ref[:, c0:c1] = (g * _sigmoid(g) * u).astype(BF16)
    out_ref[...] = x1 + _dot(act_ref[...], wd_ref[...])


def _ffn(x2, ya, yb, yc, yd, wo, nw, wg, wu, wd, layer, tm):
    m = x2.shape[0]
    row = lambda i: (i, 0)
    resident = lambda shape: pl.BlockSpec((None,) + shape, lambda i: (layer, 0, 0),
                                          pipeline_mode=pl.Buffered(1))
    return pl.pallas_call(
        _ffn_kernel,
        grid=(m // tm,),
        in_specs=[
            pl.BlockSpec((tm, D_MODEL), row),
            pl.BlockSpec((tm, GROUP_W), row),
            pl.BlockSpec((tm, GROUP_W), row),
            pl.BlockSpec((tm, GROUP_W), row),
            pl.BlockSpec((tm, GROUP_W), row),
            resident((D_MODEL, D_MODEL)),
            _per_layer((1, D_MODEL), layer),
            resident((D_MODEL, D_FF)),
            resident((D_MODEL, D_FF)),
            resident((D_FF, D_MODEL)),
        ],
        out_specs=pl.BlockSpec((tm, D_MODEL), row),
        out_shape=jax.ShapeDtypeStruct((m, D_MODEL), F32),
        scratch_shapes=[pltpu.VMEM((tm, D_FF), BF16)],
        compiler_params=_cparams(("parallel",)),
        name="outproj_ffn",
    )(x2, ya, yb, yc, yd, wo, nw, wg, wu, wd)


def _block_diag_mean(width, group):
    idx = jnp.arange(width) // group
    return jnp.where(idx[:, None] == idx[None, :], 1.0 / group, 0.0).astype(BF16)


def _rows(v, width=None):
    v = v.reshape(v.shape[0], 1, -1).astype(F32)
    return v if width is None else jnp.pad(v, ((0, 0), (0, 0), (0, width - v.shape[2])))


def kernel(x, norm1_w, w_in, gm_norm_w, gm_ws, gm_bs, ssm_conv_w, ssm_conv_b, ssm_dt_bias, ssm_a_log, ssm_d, ssm_norm_w, pool_w, pool_scale, da_q_norm_w, da_k_norm_w, da_lambda_q1, da_lambda_k1, da_lambda_q2, da_lambda_k2, da_subln_w, w_out, norm2_w, ffn_w_gate, ffn_w_up, ffn_w_down):
    b, seq, d = x.shape
    depth = w_in.shape[0]
    assert d == D_MODEL and seq % ATT_TQ == 0 and seq % CHUNK == 0
    m = b * seq
    tm = 512 if m % 512 == 0 else ATT_TQ

    g64 = _block_diag_mean(GROUP_W, HEAD_DIM)
    g128 = _block_diag_mean(GROUP_W, SSM_D_STATE)
    g32 = _block_diag_mean(GROUP_W, DA_QK_DIM)

    shift = _conv_shift_matrices()
    band, fix = _pool_bands()
    wo_b, wg_b, wu_b, wd_b = (w.astype(BF16) for w in (w_out, ffn_w_gate, ffn_w_up, ffn_w_down))

    assert w_in.shape[1:] == (D_MODEL, D_IN)

    n1, n2 = _rows(norm1_w), _rows(norm2_w)
    dtb, alog = _rows(ssm_dt_bias, DT_W), _rows(ssm_a_log, DT_W)
    bsm = jnp.repeat(jnp.swapaxes(gm_bs, 1, 2), HEAD_DIM, axis=2)
    gmn = _rows(gm_norm_w)
    cw, cb = jnp.swapaxes(ssm_conv_w, 1, 2), _rows(ssm_conv_b)
    dexp, ssn = _rows(jnp.repeat(ssm_d, HEAD_DIM, axis=1)), _rows(ssm_norm_w)
    eye = jnp.eye(len(POOL_WINDOWS), dtype=pool_w.dtype)
    wbd = jnp.einsum('lgab,gh->lgahb', pool_w, eye).reshape(depth, GROUP_W, GROUP_W).astype(BF16)
    psc = _rows(pool_scale)
    qnw = _rows(jnp.tile(da_q_norm_w, (1, GROUP_W // DA_QK_DIM)))
    knw = _rows(jnp.tile(da_k_norm_w, (1, GROUP_W // DA_QK_DIM)))
    lams = [_rows(v, LANES) for v in (da_lambda_q1, da_lambda_k1, da_lambda_q2, da_lambda_k2)]
    subw = jnp.broadcast_to(da_subln_w[:, :, None], (depth, HEAD_DIM, ATT_TQ))

    x2 = x.reshape(m, d)
    for i in range(depth):
        proj, dt = _inproj(x2, n1, w_in, i, dtb, tm)
        ya, yc, yb = _mixers(proj, dt, gm_ws, bsm, gmn, g64, band, fix, wbd, psc,
                             cw, cb, shift, alog, dexp, ssn, g128, i, b, seq)
        lam_init = 0.8 - 0.6 * math.exp(-0.3 * i)
        yd = _attn(proj, qnw, knw, *lams, subw, g32, i, b, seq, lam_init)
        x2 = _ffn(x2, ya, yb, yc, yd, wo_b, n2, wg_b, wu_b, wd_b, i, tm)
    return x2.reshape(b, seq, d)
```

```python
import functools
import math

import jax
import jax.numpy as jnp
from jax import lax
from jax.experimental import pallas as pl
from jax.experimental.pallas import tpu as pltpu

F32 = jnp.float32
BF16 = jnp.bfloat16

D_MODEL = 1024
GROUP_W = 256
CHUNK = 128
HEAD_DIM = 64
N_HEADS = 4
SSM_GROUPS = 2
SSM_D_STATE = 128
SSM_CONV_K = 4
SSM_CONV_DIM = GROUP_W + 2 * SSM_GROUPS * SSM_D_STATE
POOL_WINDOWS = (2, 4, 8, 16)
DA_QK_DIM = 32
D_FF = 2816
RMS_EPS = 1e-6
NEG_BIG = -1e30

LANES = 128
VMEM_LIMIT = 56 * 1024 * 1024

COL_PA = 0
COL_Z = 512
COL_XBC = 768
COL_PC = 1536
COL_Q = 1792
COL_K = 2048
COL_V = 2304
PROJ_W = 2560
DT_W = LANES
D_IN = PROJ_W + N_HEADS
DT_COL = COL_PC
IN_CHUNK = 256
FFN_CHUNK = 512
SSD_CPS = 4

ATT_TQ = 512
ATT_TK = 512
ATT_QB = 512
ATT_BOUND_MARGIN = 1.01
ATT_ONE_PASS_BOUND = 40.0
VT_ROWS = 80


def _cparams(sem):
    return pltpu.CompilerParams(dimension_semantics=sem, vmem_limit_bytes=VMEM_LIMIT)


def _sigmoid(x):
    return 1.0 / (1.0 + jnp.exp(-x))


def _dot(a, b):
    return jnp.dot(a, b, preferred_element_type=F32)


def _dot_nt(a, b):
    return lax.dot_general(a, b, (((1,), (1,)), ((), ())), preferred_element_type=F32)


def _softplus(t):
    return jnp.maximum(t, 0.0) + jnp.log(1.0 + jnp.exp(-jnp.abs(t)))


def _inproj_kernel(x_ref, nw_ref, w_ref, dtb_ref, proj_ref, dt_ref, wb_ref):
    @pl.when(pl.program_id(0) == 0)
    def _():
        for n0 in range(0, DT_COL, 2 * IN_CHUNK):
            wb_ref[:, n0:n0 + 2 * IN_CHUNK] = w_ref[:, n0:n0 + 2 * IN_CHUNK].astype(BF16)
        tail = w_ref[:, DT_COL:D_IN]
        wb_ref[:, DT_COL:PROJ_W] = tail[:, N_HEADS:].astype(BF16)
        wb_ref[:, PROJ_W:PROJ_W + DT_W] = tail[:, 0:DT_W].astype(BF16)

    x = x_ref[...]
    ms = jnp.mean(x * x, axis=-1, keepdims=True)
    h = (x * lax.rsqrt(ms + RMS_EPS) * nw_ref[...]).astype(BF16)
    for n0 in range(0, PROJ_W, IN_CHUNK):
        y = _dot(h, wb_ref[:, n0:n0 + IN_CHUNK])
        if n0 < COL_Z:
            y = jax.nn.gelu(y, approximate=True)
        elif n0 < COL_XBC:
            y = y * _sigmoid(y)
        proj_ref[:, n0:n0 + IN_CHUNK] = y.astype(BF16)
    dt_ref[...] = _softplus(_dot(h, wb_ref[:, PROJ_W:PROJ_W + DT_W]) + dtb_ref[...])


def _per_layer(shape, layer):
    return pl.BlockSpec((None,) + tuple(shape), lambda *_: (layer,) + (0,) * len(shape))


def _inproj(x2, nw, w_in, layer, dtb, tm):
    m = x2.shape[0]
    return pl.pallas_call(
        _inproj_kernel,
        grid=(m // tm,),
        in_specs=[
            pl.BlockSpec((tm, D_MODEL), lambda i: (i, 0)),
            _per_layer((1, D_MODEL), layer),
            pl.BlockSpec((None, D_MODEL, D_IN), lambda i: (layer, 0, 0), pipeline_mode=pl.Buffered(1)),
            _per_layer((1, DT_W), layer),
        ],
        out_specs=[
            pl.BlockSpec((tm, PROJ_W), lambda i: (i, 0)),
            pl.BlockSpec((tm, DT_W), lambda i: (i, 0)),
        ],
        out_shape=[
            jax.ShapeDtypeStruct((m, PROJ_W), BF16),
            jax.ShapeDtypeStruct((m, DT_W), F32),
        ],
        scratch_shapes=[pltpu.VMEM((D_MODEL, PROJ_W + DT_W), BF16)],
        compiler_params=_cparams(("arbitrary",)),
        name="inproj",
    )(x2, nw, w_in, dtb)


def _head_id(shape, width):
    lane = lax.broadcasted_iota(jnp.int32, shape, 1)
    return lax.shift_right_logical(lane, int(math.log2(width)))


def _gmlp_kernel(pa_ref, ws_ref, bsm_ref, nw_ref, g_ref, out_ref):
    t = pa_ref.shape[0]
    hact = pa_ref[...].astype(F32)
    u = hact[:, :GROUP_W]
    v = hact[:, GROUP_W:]
    ms = _dot((v * v).astype(BF16), g_ref[...])
    vn = (v * lax.rsqrt(ms + RMS_EPS) * nw_ref[...]).astype(BF16)
    row = lax.broadcasted_iota(jnp.int32, (CHUNK, CHUNK), 0)
    col = lax.broadcasted_iota(jnp.int32, (CHUNK, CHUNK), 1)
    wcat = jnp.concatenate(
        [jnp.where(row >= col, ws_ref[h], 0.0) for h in range(N_HEADS)], axis=1).astype(BF16)
    hid = _head_id((CHUNK, GROUP_W), HEAD_DIM)
    for c in range(t // CHUNK):
        vc = vn[c * CHUNK:(c + 1) * CHUNK]
        vstack = jnp.concatenate(
            [jnp.where(hid == h, vc, jnp.zeros_like(vc)) for h in range(N_HEADS)], axis=0)
        s = _dot(wcat, vstack) + bsm_ref[...]
        out_ref[c * CHUNK:(c + 1) * CHUNK, :] = (u[c * CHUNK:(c + 1) * CHUNK] * s).astype(BF16)


def _split3(a):
    a1 = a.astype(BF16)
    r1 = a - a1.astype(F32)
    a2 = r1.astype(BF16)
    r2 = r1 - a2.astype(F32)
    return a1, a2, r2.astype(BF16)


def _ssd_kernel(z_ref, xbc_ref, halo_ref, dt_ref, cw_ref, cb_ref, shift_ref, alog_ref, dexp_ref, nw_ref,
                g_ref, out_ref, st_ref, between_phases=None):
    c = pl.program_id(1)

    @pl.when(c == 0)
    def _():
        st_ref[...] = jnp.zeros_like(st_ref)

    row = lax.broadcasted_iota(jnp.int32, (CHUNK, CHUNK), 0)
    col = lax.broadcasted_iota(jnp.int32, (CHUNK, CHUNK), 1)
    causal = row >= col
    ltri = jnp.where(causal, 1.0, 0.0).astype(BF16)
    hid = _head_id((CHUNK, GROUP_W), HEAD_DIM)
    neg_a = -jnp.exp(alog_ref[...])

    def expand(c4):
        return jnp.where(hid == 0, c4[:, 0:1],
                         jnp.where(hid == 1, c4[:, 1:2],
                                   jnp.where(hid == 2, c4[:, 2:3], c4[:, 3:4])))

    chunks = range(SSD_CPS)
    rows = [slice(ck * CHUNK, (ck + 1) * CHUNK) for ck in chunks]
    groups = [slice(g * SSM_D_STATE, (g + 1) * SSM_D_STATE) for g in range(SSM_GROUPS)]

    xs, bm, cm = [], [], []
    halo = jnp.where(c > 0, halo_ref[...], jnp.zeros_like(halo_ref))
    for ck in chunks:
        xcur = xbc_ref[rows[ck], :]
        acc = cb_ref[...] + cw_ref[SSM_CONV_K - 1:SSM_CONV_K, :] * xcur.astype(F32)
        for j in range(1, SSM_CONV_K):
            k = SSM_CONV_K - 1 - j
            if ck == 0:
                shifted = (_dot(shift_ref[j - 1, :, 0:CHUNK], halo)
                           + _dot(shift_ref[j - 1, :, CHUNK:2 * CHUNK], xcur))
            else:
                shifted = _dot(shift_ref[j - 1], xbc_ref[(ck - 1) * CHUNK:(ck + 1) * CHUNK, :])
            acc = acc + cw_ref[k:k + 1, :] * shifted
        xc = acc * _sigmoid(acc)
        xs.append(xc[:, :GROUP_W])
        bm.append(xc[:, GROUP_W:2 * GROUP_W])
        cm.append(xc[:, 2 * GROUP_W:].astype(BF16))

    acs, acs_t = [], []
    for ck in chunks:
        a1, a2, a3 = _split3(dt_ref[rows[ck], :] * neg_a)
        acs.append(_dot(ltri, a1) + _dot(ltri, a2) + _dot(ltri, a3))
        acs_t.append(acs[ck].T)

    if between_phases is not None:
        between_phases()

    xdt, eacs_e, mcat, snew = [], [], [], []
    for ck in chunks:
        xdt.append(xs[ck] * expand(dt_ref[rows[ck], :]))
        eacs_e.append(expand(jnp.exp(acs[ck])))
        dte_e = expand(jnp.exp(acs[ck][CHUNK - 1:CHUNK, :] - acs[ck]))
        ms = []
        for g in range(SSM_GROUPS):
            cb = _dot_nt(cm[ck][:, groups[g]], bm[ck][:, groups[g]].astype(BF16))
            for hh in range(N_HEADS // SSM_GROUPS):
                h = g * (N_HEADS // SSM_GROUPS) + hh
                seg = acs[ck][:, h:h + 1] - acs_t[ck][h:h + 1, :]
                ms.append((cb * jnp.exp(jnp.where(causal, seg, NEG_BIG))).astype(BF16))
        mcat.append(jnp.concatenate(ms, axis=1))
        w = (xdt[ck] * dte_e).astype(BF16)
        snew.append([_dot(bm[ck][:, groups[g]].T.astype(BF16), w[:, groups[g]])
                     for g in range(SSM_GROUPS)])

    state = [st_ref[g] for g in range(SSM_GROUPS)]
    entering = []
    for ck in chunks:
        entering.append([st.astype(BF16) for st in state])
        cdl = eacs_e[ck][CHUNK - 1:CHUNK, :]
        state = [state[g] * cdl[:, groups[g]] + snew[ck][g] for g in range(SSM_GROUPS)]
    for g in range(SSM_GROUPS):
        st_ref[g] = state[g]

    ys = []
    for ck in chunks:
        xdt_b = xdt[ck].astype(BF16)
        xstack = jnp.concatenate(
            [jnp.where(hid == h, xdt_b, jnp.zeros_like(xdt_b)) for h in range(N_HEADS)], axis=0)
        yoff = jnp.concatenate([_dot(cm[ck][:, groups[g]], entering[ck][g]) for g in range(SSM_GROUPS)], axis=1)
        y = _dot(mcat[ck], xstack) + yoff * eacs_e[ck] + dexp_ref[...] * xs[ck]
        ys.append(y * z_ref[rows[ck], :].astype(F32))

    for ck in chunks:
        msq = _dot((ys[ck] * ys[ck]).astype(BF16), g_ref[...])
        out_ref[rows[ck], :] = (ys[ck] * lax.rsqrt(msq + RMS_EPS) * nw_ref[...]).astype(BF16)


def _conv_shift_matrices():
    r = jnp.arange(CHUNK)[:, None]
    col = jnp.arange(2 * CHUNK)[None, :]
    return jnp.stack([(col == CHUNK + r - j) for j in range(1, SSM_CONV_K)]).astype(BF16)


def _pool_bands():
    r = jnp.arange(CHUNK)[:, None]
    col = jnp.arange(2 * CHUNK)[None, :]
    back = CHUNK + r - col
    bands = [jnp.where((back >= 0) & (back < w), 1.0 / w, 0.0) for w in POOL_WINDOWS]
    bandcat = jnp.concatenate(bands, axis=1).astype(BF16)
    t = jnp.arange(CHUNK, dtype=F32)[:, None] + 1.0
    win = jnp.repeat(jnp.asarray(POOL_WINDOWS, F32), HEAD_DIM)[None, :]
    fix = win / jnp.minimum(win, t)
    return bandcat, fix


def _pool_chunks(pc_ref, pch_ref, band_ref, fix_ref, wbd_ref, scale_ref, out_ref, first_step):
    hid = _head_id((CHUNK, GROUP_W), HEAD_DIM)
    prev = jnp.where(first_step, jnp.zeros_like(pch_ref), pch_ref[...])
    for ck in range(pc_ref.shape[0] // CHUNK):
        cur = pc_ref[ck * CHUNK:(ck + 1) * CHUNK, :]
        parts = []
        for g in range(len(POOL_WINDOWS)):
            parts += [jnp.where(hid == g, prev, jnp.zeros_like(prev)),
                      jnp.where(hid == g, cur, jnp.zeros_like(cur))]
        mean = _dot(band_ref[...], jnp.concatenate(parts, axis=0))
        if ck == 0:
            mean = jnp.where(first_step, mean * fix_ref[...], mean)
        p = mean - cur.astype(F32)
        y = _dot(p.astype(BF16), wbd_ref[...]) * scale_ref[...]
        out_ref[ck * CHUNK:(ck + 1) * CHUNK, :] = y.astype(BF16)
        prev = cur


N_GMLP_IN, N_POOL_IN, N_SSD_IN = 5, 6, 11


def _mixers_kernel(*refs):
    gm = refs[:N_GMLP_IN]
    po = refs[N_GMLP_IN:N_GMLP_IN + N_POOL_IN]
    sd = refs[N_GMLP_IN + N_POOL_IN:N_GMLP_IN + N_POOL_IN + N_SSD_IN]
    ya_ref, yc_ref, yb_ref, st_ref = refs[N_GMLP_IN + N_POOL_IN + N_SSD_IN:]
    first_step = pl.program_id(1) == 0

    def others():
        _gmlp_kernel(*gm, ya_ref)
        _pool_chunks(*po, yc_ref, first_step)

    _ssd_kernel(*sd, yb_ref, st_ref, between_phases=others)


def _mixers(proj, dt, gws, bsm, gmn, g64, band, fix, wbd, psc, cw, cb, shift, alog, dexp, nw, g128,
            layer, b, seq):
    m = proj.shape[0]
    ts = SSD_CPS * CHUNK
    assert seq % ts == 0
    nc = seq // ts
    const = lambda i, c: (0, 0)
    blk = lambda width, col: pl.BlockSpec((ts, width), lambda i, c: (i * nc + c, col // width))
    halo = lambda width, col: pl.BlockSpec(
        (CHUNK, width), lambda i, c: (jnp.maximum((i * nc + c) * SSD_CPS - 1, 0), col // width))
    out = pl.BlockSpec((ts, GROUP_W), lambda i, c: (i * nc + c, 0))
    gmlp_in = [blk(2 * GROUP_W, COL_PA), _per_layer((N_HEADS, CHUNK, CHUNK), layer),
               _per_layer((CHUNK, GROUP_W), layer), _per_layer((1, GROUP_W), layer),
               pl.BlockSpec((GROUP_W, GROUP_W), const)]
    pool_in = [blk(GROUP_W, COL_PC), halo(GROUP_W, COL_PC),
               pl.BlockSpec((CHUNK, len(POOL_WINDOWS) * 2 * CHUNK), const), pl.BlockSpec((CHUNK, GROUP_W), const),
               _per_layer((GROUP_W, GROUP_W), layer), _per_layer((1, GROUP_W), layer)]
    ssd_in = [blk(GROUP_W, COL_Z), blk(SSM_CONV_DIM, COL_XBC), halo(SSM_CONV_DIM, COL_XBC),
              pl.BlockSpec((ts, DT_W), lambda i, c: (i * nc + c, 0)),
              _per_layer((SSM_CONV_K, SSM_CONV_DIM), layer), _per_layer((1, SSM_CONV_DIM), layer),
              pl.BlockSpec((SSM_CONV_K - 1, CHUNK, 2 * CHUNK), lambda i, c: (0, 0, 0)),
              _per_layer((1, DT_W), layer), _per_layer((1, GROUP_W), layer), _per_layer((1, GROUP_W), layer),
              pl.BlockSpec((GROUP_W, GROUP_W), const)]
    assert (len(gmlp_in), len(pool_in), len(ssd_in)) == (N_GMLP_IN, N_POOL_IN, N_SSD_IN)
    return pl.pallas_call(
        _mixers_kernel,
        grid=(b, nc),
        in_specs=gmlp_in + pool_in + ssd_in,
        out_specs=[out, out, out],
        out_shape=[jax.ShapeDtypeStruct((m, GROUP_W), BF16)] * 3,
        scratch_shapes=[pltpu.VMEM((SSM_GROUPS, SSM_D_STATE, LANES), F32)],
        compiler_params=_cparams(("parallel", "arbitrary")),
        name="mixers",
    )(proj, gws, bsm, gmn, g64,
      proj, proj, band, fix, wbd, psc,
      proj, proj, proj, dt, cw, cb, shift, alog, dexp, nw, g128)


def _alibi_slope(h):
    return 2.0 ** (-8.0 * (h + 1) / N_HEADS)


def _head_slot(x, h):
    base = x[:, (h // 2) * LANES:(h // 2 + 1) * LANES]
    return pltpu.roll(base, HEAD_DIM, 1) if h % 2 else base


def _attn_kernel(q_ref, k_ref, v_ref, qnw_ref, knw_ref, lq1_ref, lk1_ref, lq2_ref, lk2_ref, subw_ref,
                 g_ref, out_ref, ka_ref, vt_ref, qs_ref, acc_ref, s0_ref, kmax_ref, qa_ref, *, lam_init):
    qi = pl.program_id(1)
    tq, tk = ATT_TQ, ATT_TK
    seq = k_ref.shape[0]

    def aug_lanes(pos, slope, key_side):
        lane = lax.broadcasted_iota(jnp.int32, pos.shape, 1)
        hi = (slope * LANES) * lax.shift_right_logical(pos, 7).astype(F32)
        lo = slope * (pos & (LANES - 1)).astype(F32)
        one = jnp.ones_like(hi)
        c = (one, one, hi, lo) if key_side else (-hi, -lo, one, one)
        last = jnp.where(lane == 68, 1.0, 0.0) if key_side else 0.0
        return jnp.where(lane == 64, c[0], jnp.where(lane == 65, c[1],
                         jnp.where(lane == 66, c[2], jnp.where(lane == 67, c[3], last))))

    @pl.when(qi == 0)
    def _():
        k = k_ref[...].astype(F32)
        ms = _dot((k * k).astype(BF16), g_ref[...])
        kn = k * lax.rsqrt(ms + RMS_EPS) * knw_ref[...]
        ksq = _dot((kn * kn).astype(BF16), g_ref[...]) * DA_QK_DIM
        kmax_ref[...] = jnp.broadcast_to(jnp.max(ksq, axis=0, keepdims=True), kmax_ref.shape)
        pos = lax.broadcasted_iota(jnp.int32, (seq, LANES), 0)
        lane = lax.broadcasted_iota(jnp.int32, (seq, LANES), 1)
        for h in range(N_HEADS):
            aug = aug_lanes(pos, _alibi_slope(h), True)
            ka_ref[h] = jnp.where(lane < HEAD_DIM, _head_slot(kn, h), aug).astype(BF16)
            qa_ref[h] = aug_lanes(pos, _alibi_slope(h), False).astype(BF16)
        tail = (lax.broadcasted_iota(jnp.int32, (VT_ROWS - HEAD_DIM, tk), 0) == 0).astype(BF16)
        for j in range(seq // tk):
            vt = v_ref[j * tk:(j + 1) * tk, :].astype(F32).T.astype(BF16)
            for h in range(N_HEADS):
                vt_ref[j, h, 0:HEAD_DIM, :] = vt[h * HEAD_DIM:(h + 1) * HEAD_DIM, :]
                vt_ref[j, h, HEAD_DIM:VT_ROWS, :] = tail

    q = q_ref[...].astype(F32)
    ms = _dot((q * q).astype(BF16), g_ref[...])
    qn = q * lax.rsqrt(ms + RMS_EPS) * (qnw_ref[...] * (DA_QK_DIM ** -0.5))
    qsq = _dot((qn * qn).astype(BF16), g_ref[...]) * DA_QK_DIM
    b2 = qsq * kmax_ref[0:1, :]
    bound = b2 * lax.rsqrt(b2 + 1e-30) * ATT_BOUND_MARGIN
    one_pass = jnp.max(bound) <= ATT_ONE_PASS_BOUND
    lane = lax.broadcasted_iota(jnp.int32, (1, LANES), 1)
    q0 = pl.multiple_of(qi * tq, tq)
    for h in range(N_HEADS):
        base = _head_slot(qn, h)
        bnd = _head_slot(bound, h)
        qa = qa_ref[h, pl.ds(q0, tq), :]
        for comp in range(2):
            own = (lane >= comp * DA_QK_DIM) & (lane < (comp + 1) * DA_QK_DIM)
            y = jnp.where(own, base, 0.0)
            y = jnp.where(lane == 68, -bnd[:, comp * DA_QK_DIM:comp * DA_QK_DIM + 1], y)
            qs_ref[h, comp * tq:(comp + 1) * tq, :] = y.astype(BF16) + qa
    acc_ref[...] = jnp.zeros_like(acc_ref)

    qb = ATT_QB
    items = [(h, n) for h in range(N_HEADS) for n in range(2 * tq // qb)]

    def scores(item, j):
        h, n = item
        k0 = pl.multiple_of(j * tk, tk)
        return _dot_nt(ka_ref[h, pl.ds(k0, tk), :], qs_ref[h, n * qb:(n + 1) * qb, :])

    def tile(j, ms_, masked):
        new_m = []
        s_next = s0_ref[...]
        for i, (h, n) in enumerate(items):
            cols = slice(n * qb, (n + 1) * qb)
            s = s_next
            if i + 1 < len(items):
                s_next = scores(items[i + 1], j)
            elif not masked:
                s0_ref[...] = scores(items[0], j + 1)
            if masked:
                kk = lax.broadcasted_iota(jnp.int32, (tk, qb), 0)
                qq = (lax.broadcasted_iota(jnp.int32, (tk, qb), 1) + n * qb) & (tq - 1)
                s = jnp.where(kk <= qq, s, NEG_BIG)
            m_old = ms_[i]
            m_new = jnp.maximum(m_old, jnp.max(s, axis=0, keepdims=True))
            alpha = jnp.exp(m_old - m_new)
            p = jnp.exp(s - m_new)
            new_m.append(m_new)
            acc_ref[h, :, cols] = acc_ref[h, :, cols] * alpha + _dot(vt_ref[j, h], p.astype(BF16))
        return tuple(new_m)

    def tile_one_pass(j, masked):
        s_next = s0_ref[...]
        for i, (h, n) in enumerate(items):
            cols = slice(n * qb, (n + 1) * qb)
            s = s_next
            if i + 1 < len(items):
                s_next = scores(items[i + 1], j)
            elif not masked:
                s0_ref[...] = scores(items[0], j + 1)
            if masked:
                kk = lax.broadcasted_iota(jnp.int32, (tk, qb), 0)
                qq = (lax.broadcasted_iota(jnp.int32, (tk, qb), 1) + n * qb) & (tq - 1)
                s = jnp.where(kk <= qq, s, NEG_BIG)
            acc_ref[h, :, cols] += _dot(vt_ref[j, h], jnp.exp(s).astype(BF16))

    s0_ref[...] = scores(items[0], 0)

    @pl.when(one_pass)
    def _():
        def body(j, carry):
            tile_one_pass(j, False)
            return carry
        lax.fori_loop(0, qi, body, 0)
        tile_one_pass(qi, True)

    @pl.when(jnp.logical_not(one_pass))
    def _():
        m0 = tuple(jnp.full((1, qb), NEG_BIG, F32) for _ in items)
        m1 = lax.fori_loop(0, qi, lambda j, c: tile(j, c, False), m0)
        tile(qi, m1, True)

    def lane_sum(x):
        return jnp.broadcast_to(jnp.sum(x, axis=1, keepdims=True), x.shape)

    lam = (jnp.exp(lane_sum(lq1_ref[...] * lk1_ref[...]))
           - jnp.exp(lane_sum(lq2_ref[...] * lk2_ref[...])) + lam_init)
    lam = jnp.concatenate([lam] * (tq // LANES), axis=1)
    outs = []
    for h in range(N_HEADS):
        o = acc_ref[h, 0:HEAD_DIM, :] / acc_ref[h, HEAD_DIM:HEAD_DIM + 1, :]
        oh = o[:, 0:tq] - lam * o[:, tq:2 * tq]
        ms = jnp.mean(oh * oh, axis=0, keepdims=True)
        outs.append(oh * lax.rsqrt(ms + RMS_EPS) * (subw_ref[...] * (1.0 - lam_init)))
    out_ref[...] = jnp.concatenate(outs, axis=0).T.astype(BF16)


def _attn(proj, qnw, knw, lq1, lk1, lq2, lk2, subw, g32, layer, b, seq, lam_init):
    m = proj.shape[0]
    assert ATT_TQ == ATT_TK
    nq = seq // ATT_TQ
    const = lambda i, j: (0, 0)
    return pl.pallas_call(
        functools.partial(_attn_kernel, lam_init=lam_init),
        grid=(b, nq),
        in_specs=[
            pl.BlockSpec((ATT_TQ, GROUP_W), lambda i, j: (i * nq + j, COL_Q // GROUP_W)),
            pl.BlockSpec((seq, GROUP_W), lambda i, j: (i, COL_K // GROUP_W)),
            pl.BlockSpec((seq, GROUP_W), lambda i, j: (i, COL_V // GROUP_W)),
            _per_layer((1, GROUP_W), layer),
            _per_layer((1, GROUP_W), layer),
            _per_layer((1, LANES), layer),
            _per_layer((1, LANES), layer),
            _per_layer((1, LANES), layer),
            _per_layer((1, LANES), layer),
            _per_layer((HEAD_DIM, ATT_TQ), layer),
            pl.BlockSpec((GROUP_W, GROUP_W), const),
        ],
        out_specs=pl.BlockSpec((ATT_TQ, GROUP_W), lambda i, j: (i * nq + j, 0)),
        out_shape=jax.ShapeDtypeStruct((m, GROUP_W), BF16),
        scratch_shapes=[
            pltpu.VMEM((N_HEADS, seq, LANES), BF16),
            pltpu.VMEM((seq // ATT_TK, N_HEADS, VT_ROWS, ATT_TK), BF16),
            pltpu.VMEM((N_HEADS, 2 * ATT_TQ, LANES), BF16),
            pltpu.VMEM((N_HEADS, VT_ROWS, 2 * ATT_TQ), F32),
            pltpu.VMEM((ATT_TK, ATT_QB), F32),
            pltpu.VMEM((8, GROUP_W), F32),
            pltpu.VMEM((N_HEADS, seq, LANES), BF16),
        ],
        compiler_params=_cparams(("parallel", "arbitrary")),
        name="diffattn",
    )(proj, proj, proj, qnw, knw, lq1, lk1, lq2, lk2, subw, g32)


FFN_STAGE_WIDE = 256
FFN_STAGE_TALL = 704


def _ffn_weight_jobs(layer, wo_hbm, wg_hbm, wu_hbm, wd_hbm, wo_b, wg_b, wu_b, wd_b, wide, tall, sem_w, sem_t):
    jobs = []

    def add(src, dst, stage, sem, rows, n_rows):
        for k, r0 in enumerate(range(0, n_rows, rows)):
            r = min(rows, n_rows - r0)
            slot = k % 2
            view = stage.at[slot, 0:r, :]
            copy = pltpu.make_async_copy(src.at[layer, r0:r0 + r, :], view, sem.at[slot])
            jobs.append((copy, view, dst.at[r0:r0 + r, :]))

    add(wg_hbm, wg_b, wide, sem_w, FFN_STAGE_WIDE, D_MODEL)
    add(wu_hbm, wu_b, wide, sem_w, FFN_STAGE_WIDE, D_MODEL)
    add(wd_hbm, wd_b, tall, sem_t, FFN_STAGE_TALL, D_FF)
    add(wo_hbm, wo_b, tall, sem_t, FFN_STAGE_TALL, D_MODEL)
    return jobs


def _ffn_kernel(x_ref, ya_ref, yb_ref, yc_ref, yd_ref, nw_ref, wo_hbm, wg_hbm, wu_hbm, wd_hbm,
                out_ref, act_ref, wo_ref, wg_ref, wu_ref, wd_ref, wide_ref, tall_ref, sem_w, sem_t, *, layer):
    @pl.when(pl.program_id(0) == 0)
    def _():
        jobs = _ffn_weight_jobs(layer, wo_hbm, wg_hbm, wu_hbm, wd_hbm, wo_ref, wg_ref, wu_ref, wd_ref,
                                wide_ref, tall_ref, sem_w, sem_t)
        jobs[0][0].start()
        for k, (copy, view, dst) in enumerate(jobs):
            if k + 1 < len(jobs):
                jobs[k + 1][0].start()
            copy.wait()
            dst[...] = view[...].astype(BF16)

    x1 = x_ref[...]
    for i, y_ref in enumerate((ya_ref, yb_ref, yc_ref, yd_ref)):
        x1 = x1 + _dot(y_ref[...], wo_ref[i * GROUP_W:(i + 1) * GROUP_W, :])
    ms = jnp.mean(x1 * x1, axis=-1, keepdims=True)
    h = (x1 * lax.rsqrt(ms + RMS_EPS) * nw_ref[...]).astype(BF16)
    for c0 in range(0, D_FF, FFN_CHUNK):
        c1 = min(c0 + FFN_CHUNK, D_FF)
        g = _dot(h, wg_ref[:, c0:c1])
        u = _dot(h, wu_ref[:, c0:c1])
        act_ref[:, c0:c1] = (g * _sigmoid(g) * u).astype(BF16)
    out_ref[...] = x1 + _dot(act_ref[...], wd_ref[...])


def _ffn(x2, ya, yb, yc, yd, nw, wo, wg, wu, wd, layer, tm):
    m = x2.shape[0]
    row = lambda i: (i, 0)
    hbm = pl.BlockSpec(memory_space=pl.ANY)
    return pl.pallas_call(
        functools.partial(_ffn_kernel, layer=layer),
        grid=(m // tm,),
        in_specs=[
            pl.BlockSpec((tm, D_MODEL), row),
            pl.BlockSpec((tm, GROUP_W), row),
            pl.BlockSpec((tm, GROUP_W), row),
            pl.BlockSpec((tm, GROUP_W), row),
            pl.BlockSpec((tm, GROUP_W), row),
            _per_layer((1, D_MODEL), layer),
            hbm, hbm, hbm, hbm,
        ],
        out_specs=pl.BlockSpec((tm, D_MODEL), row),
        out_shape=jax.ShapeDtypeStruct((m, D_MODEL), F32),
        scratch_shapes=[
            pltpu.VMEM((tm, D_FF), BF16),
            pltpu.VMEM((D_MODEL, D_MODEL), BF16),
            pltpu.VMEM((D_MODEL, D_FF), BF16),
            pltpu.VMEM((D_MODEL, D_FF), BF16),
            pltpu.VMEM((D_FF, D_MODEL), BF16),
            pltpu.VMEM((2, FFN_STAGE_WIDE, D_FF), F32),
            pltpu.VMEM((2, FFN_STAGE_TALL, D_MODEL), F32),
            pltpu.SemaphoreType.DMA((2,)),
            pltpu.SemaphoreType.DMA((2,)),
        ],
        compiler_params=_cparams(("arbitrary",)),
        name="outproj_ffn",
    )(x2, ya, yb, yc, yd, nw, wo, wg, wu, wd)


def _block_diag_mean(width, group):
    idx = jnp.arange(width) // group
    return jnp.where(idx[:, None] == idx[None, :], 1.0 / group, 0.0).astype(BF16)


def _rows(v, width=None):
    v = v.reshape(v.shape[0], 1, -1).astype(F32)
    return v if width is None else jnp.pad(v, ((0, 0), (0, 0), (0, width - v.shape[2])))


def kernel(x, norm1_w, w_in, gm_norm_w, gm_ws, gm_bs, ssm_conv_w, ssm_conv_b, ssm_dt_bias, ssm_a_log, ssm_d, ssm_norm_w, pool_w, pool_scale, da_q_norm_w, da_k_norm_w, da_lambda_q1, da_lambda_k1, da_lambda_q2, da_lambda_k2, da_subln_w, w_out, norm2_w, ffn_w_gate, ffn_w_up, ffn_w_down):
    b, seq, d = x.shape
    depth = w_in.shape[0]
    assert d == D_MODEL and seq % ATT_TQ == 0 and seq % CHUNK == 0
    m = b * seq
    tm = 512 if m % 512 == 0 else ATT_TQ

    g64 = _block_diag_mean(GROUP_W, HEAD_DIM)
    g128 = _block_diag_mean(GROUP_W, SSM_D_STATE)
    g32 = _block_diag_mean(GROUP_W, DA_QK_DIM)

    shift = _conv_shift_matrices()
    band, fix = _pool_bands()

    assert w_in.shape[1:] == (D_MODEL, D_IN)

    n1, n2 = _rows(norm1_w), _rows(norm2_w)
    dtb, alog = _rows(ssm_dt_bias, DT_W), _rows(ssm_a_log, DT_W)
    bsm = jnp.repeat(jnp.swapaxes(gm_bs, 1, 2), HEAD_DIM, axis=2)
    gmn = _rows(gm_norm_w)
    cw, cb = jnp.swapaxes(ssm_conv_w, 1, 2), _rows(ssm_conv_b)
    dexp, ssn = _rows(jnp.repeat(ssm_d, HEAD_DIM, axis=1)), _rows(ssm_norm_w)
    eye = jnp.eye(len(POOL_WINDOWS), dtype=pool_w.dtype)
    wbd = jnp.einsum('lgab,gh->lgahb', pool_w, eye).reshape(depth, GROUP_W, GROUP_W).astype(BF16)
    psc = _rows(pool_scale)
    qnw = _rows(jnp.tile(da_q_norm_w, (1, GROUP_W // DA_QK_DIM)))
    knw = _rows(jnp.tile(da_k_norm_w, (1, GROUP_W // DA_QK_DIM)))
    lams = [_rows(v, LANES) for v in (da_lambda_q1, da_lambda_k1, da_lambda_q2, da_lambda_k2)]
    subw = jnp.broadcast_to(da_subln_w[:, :, None], (depth, HEAD_DIM, ATT_TQ))

    x2 = x.reshape(m, d)
    for i in range(depth):
        proj, dt = _inproj(x2, n1, w_in, i, dtb, tm)
        ya, yc, yb = _mixers(proj, dt, gm_ws, bsm, gmn, g64, band, fix, wbd, psc,
                             cw, cb, shift, alog, dexp, ssn, g128, i, b, seq)
        lam_init = 0.8 - 0.6 * math.exp(-0.3 * i)
        yd = _attn(proj, qnw, knw, *lams, subw, g32, i, b, seq, lam_init)
        x2 = _ffn(x2, ya, yb, yc, yd, n2, w_out, ffn_w_gate, ffn_w_up, ffn_w_down, i, tm)
    return x2.reshape(b, seq, d)
```

```python
import functools
import math

import jax
import jax.numpy as jnp
from jax import lax
from jax.experimental import pallas as pl
from jax.experimental.pallas import tpu as pltpu

F32 = jnp.float32
BF16 = jnp.bfloat16

D_MODEL = 1024
GROUP_W = 256
CHUNK = 128
HEAD_DIM = 64
N_HEADS = 4
SSM_GROUPS = 2
SSM_D_STATE = 128
SSM_CONV_K = 4
SSM_CONV_DIM = GROUP_W + 2 * SSM_GROUPS * SSM_D_STATE
POOL_WINDOWS = (2, 4, 8, 16)
DA_QK_DIM = 32
D_FF = 2816
RMS_EPS = 1e-6
NEG_BIG = -1e30

LANES = 128
VMEM_LIMIT = 56 * 1024 * 1024

COL_PA = 0
COL_Z = 512
COL_XBC = 768
COL_PC = 1536
COL_Q = 1792
COL_K = 2048
COL_V = 2304
PROJ_W = 2560
DT_W = LANES
D_IN = PROJ_W + N_HEADS
DT_COL = COL_PC
IN_CHUNK = 256
FFN_CHUNK = 512
SSD_CPS = 4

ATT_TQ = 512
ATT_TK = 512
ATT_QB = 512
ATT_BOUND_MARGIN = 1.01
ATT_ONE_PASS_BOUND = 40.0
VT_ROWS = 80


def _cparams(sem):
    return pltpu.CompilerParams(dimension_semantics=sem, vmem_limit_bytes=VMEM_LIMIT)


def _sigmoid(x):
    return 1.0 / (1.0 + jnp.exp(-x))


def _dot(a, b):
    return jnp.dot(a, b, preferred_element_type=F32)


def _dot_nt(a, b):
    return lax.dot_general(a, b, (((1,), (1,)), ((), ())), preferred_element_type=F32)


def _softplus(t):
    return jnp.maximum(t, 0.0) + jnp.log(1.0 + jnp.exp(-jnp.abs(t)))


def _inproj_kernel(x_ref, nw_ref, w_ref, dtb_ref, proj_ref, dt_ref, wb_ref):
    @pl.when(pl.program_id(0) == 0)
    def _():
        for n0 in range(0, DT_COL, 2 * IN_CHUNK):
            wb_ref[n0:n0 + 2 * IN_CHUNK, :] = w_ref[n0:n0 + 2 * IN_CHUNK, :].astype(BF16)
        tail = w_ref[DT_COL:D_IN, :]
        wb_ref[DT_COL:PROJ_W, :] = tail[N_HEADS:, :].astype(BF16)
        wb_ref[PROJ_W:PROJ_W + DT_W, :] = tail[0:DT_W, :].astype(BF16)

    x = x_ref[...]
    ms = jnp.mean(x * x, axis=-1, keepdims=True)
    h = (x * lax.rsqrt(ms + RMS_EPS) * nw_ref[...]).astype(BF16)
    for n0 in range(0, PROJ_W, IN_CHUNK):
        y = _dot_nt(h, wb_ref[n0:n0 + IN_CHUNK, :])
        if n0 < COL_Z:
            y = jax.nn.gelu(y, approximate=True)
        elif n0 < COL_XBC:
            y = y * _sigmoid(y)
        proj_ref[:, n0:n0 + IN_CHUNK] = y.astype(BF16)
    dt_ref[...] = _softplus(_dot_nt(h, wb_ref[PROJ_W:PROJ_W + DT_W, :]) + dtb_ref[...])


def _per_layer(shape, layer):
    return pl.BlockSpec((None,) + tuple(shape), lambda *_: (layer,) + (0,) * len(shape))


def _inproj(x2, nw, w_in, layer, dtb, tm):
    m = x2.shape[0]
    return pl.pallas_call(
        _inproj_kernel,
        grid=(m // tm,),
        in_specs=[
            pl.BlockSpec((tm, D_MODEL), lambda i: (i, 0)),
            _per_layer((1, D_MODEL), layer),
            pl.BlockSpec((None, D_IN, D_MODEL), lambda i: (layer, 0, 0), pipeline_mode=pl.Buffered(1)),
            _per_layer((1, DT_W), layer),
        ],
        out_specs=[
            pl.BlockSpec((tm, PROJ_W), lambda i: (i, 0)),
            pl.BlockSpec((tm, DT_W), lambda i: (i, 0)),
        ],
        out_shape=[
            jax.ShapeDtypeStruct((m, PROJ_W), BF16),
            jax.ShapeDtypeStruct((m, DT_W), F32),
        ],
        scratch_shapes=[pltpu.VMEM((PROJ_W + DT_W, D_MODEL), BF16)],
        compiler_params=_cparams(("arbitrary",)),
        name="inproj",
    )(x2, nw, w_in, dtb)


def _head_id(shape, width):
    lane = lax.broadcasted_iota(jnp.int32, shape, 1)
    return lax.shift_right_logical(lane, int(math.log2(width)))


def _gmlp_kernel(pa_ref, ws_ref, bsm_ref, nw_ref, g_ref, out_ref):
    t = pa_ref.shape[0]
    hact = pa_ref[...].astype(F32)
    u = hact[:, :GROUP_W]
    v = hact[:, GROUP_W:]
    ms = _dot((v * v).astype(BF16), g_ref[...])
    vn = (v * lax.rsqrt(ms + RMS_EPS) * nw_ref[...]).astype(BF16)
    row = lax.broadcasted_iota(jnp.int32, (CHUNK, CHUNK), 0)
    col = lax.broadcasted_iota(jnp.int32, (CHUNK, CHUNK), 1)
    wcat = jnp.concatenate(
        [jnp.where(row >= col, ws_ref[h], 0.0) for h in range(N_HEADS)], axis=1).astype(BF16)
    hid = _head_id((CHUNK, GROUP_W), HEAD_DIM)
    for c in range(t // CHUNK):
        vc = vn[c * CHUNK:(c + 1) * CHUNK]
        vstack = jnp.concatenate(
            [jnp.where(hid == h, vc, jnp.zeros_like(vc)) for h in range(N_HEADS)], axis=0)
        s = _dot(wcat, vstack) + bsm_ref[...]
        out_ref[c * CHUNK:(c + 1) * CHUNK, :] = (u[c * CHUNK:(c + 1) * CHUNK] * s).astype(BF16)


def _split3(a):
    a1 = a.astype(BF16)
    r1 = a - a1.astype(F32)
    a2 = r1.astype(BF16)
    r2 = r1 - a2.astype(F32)
    return a1, a2, r2.astype(BF16)


def _ssd_kernel(z_ref, xbc_ref, halo_ref, dt_ref, cw_ref, cb_ref, shift_ref, alog_ref, dexp_ref, nw_ref,
                g_ref, out_ref, st_ref, between_phases=None):
    c = pl.program_id(1)

    @pl.when(c == 0)
    def _():
        st_ref[...] = jnp.zeros_like(st_ref)

    row = lax.broadcasted_iota(jnp.int32, (CHUNK, CHUNK), 0)
    col = lax.broadcasted_iota(jnp.int32, (CHUNK, CHUNK), 1)
    causal = row >= col
    ltri = jnp.where(causal, 1.0, 0.0).astype(BF16)
    hid = _head_id((CHUNK, GROUP_W), HEAD_DIM)
    neg_a = -jnp.exp(alog_ref[...])

    def expand(c4):
        return jnp.where(hid == 0, c4[:, 0:1],
                         jnp.where(hid == 1, c4[:, 1:2],
                                   jnp.where(hid == 2, c4[:, 2:3], c4[:, 3:4])))

    chunks = range(SSD_CPS)
    rows = [slice(ck * CHUNK, (ck + 1) * CHUNK) for ck in chunks]
    groups = [slice(g * SSM_D_STATE, (g + 1) * SSM_D_STATE) for g in range(SSM_GROUPS)]

    xs, bm, cm = [], [], []
    halo = jnp.where(c > 0, halo_ref[...], jnp.zeros_like(halo_ref))
    for ck in chunks:
        xcur = xbc_ref[rows[ck], :]
        acc = cb_ref[...] + cw_ref[SSM_CONV_K - 1:SSM_CONV_K, :] * xcur.astype(F32)
        for j in range(1, SSM_CONV_K):
            k = SSM_CONV_K - 1 - j
            if ck == 0:
                shifted = (_dot(shift_ref[j - 1, :, 0:CHUNK], halo)
                           + _dot(shift_ref[j - 1, :, CHUNK:2 * CHUNK], xcur))
            else:
                shifted = _dot(shift_ref[j - 1], xbc_ref[(ck - 1) * CHUNK:(ck + 1) * CHUNK, :])
            acc = acc + cw_ref[k:k + 1, :] * shifted
        xc = acc * _sigmoid(acc)
        xs.append(xc[:, :GROUP_W])
        bm.append(xc[:, GROUP_W:2 * GROUP_W])
        cm.append(xc[:, 2 * GROUP_W:].astype(BF16))

    acs, acs_t = [], []
    for ck in chunks:
        a1, a2, a3 = _split3(dt_ref[rows[ck], :] * neg_a)
        acs.append(_dot(ltri, a1) + _dot(ltri, a2) + _dot(ltri, a3))
        acs_t.append(acs[ck].T)

    if between_phases is not None:
        between_phases()

    xdt, eacs_e, mcat, snew = [], [], [], []
    for ck in chunks:
        xdt.append(xs[ck] * expand(dt_ref[rows[ck], :]))
        eacs_e.append(expand(jnp.exp(acs[ck])))
        dte_e = expand(jnp.exp(acs[ck][CHUNK - 1:CHUNK, :] - acs[ck]))
        ms = []
        for g in range(SSM_GROUPS):
            cb = _dot_nt(cm[ck][:, groups[g]], bm[ck][:, groups[g]].astype(BF16))
            for hh in range(N_HEADS // SSM_GROUPS):
                h = g * (N_HEADS // SSM_GROUPS) + hh
                seg = acs[ck][:, h:h + 1] - acs_t[ck][h:h + 1, :]
                ms.append((cb * jnp.exp(jnp.where(causal, seg, NEG_BIG))).astype(BF16))
        mcat.append(jnp.concatenate(ms, axis=1))
        w = (xdt[ck] * dte_e).astype(BF16)
        snew.append([_dot(bm[ck][:, groups[g]].T.astype(BF16), w[:, groups[g]])
                     for g in range(SSM_GROUPS)])

    state = [st_ref[g] for g in range(SSM_GROUPS)]
    entering = []
    for ck in chunks:
        entering.append([st.astype(BF16) for st in state])
        cdl = eacs_e[ck][CHUNK - 1:CHUNK, :]
        state = [state[g] * cdl[:, groups[g]] + snew[ck][g] for g in range(SSM_GROUPS)]
    for g in range(SSM_GROUPS):
        st_ref[g] = state[g]

    ys = []
    for ck in chunks:
        xdt_b = xdt[ck].astype(BF16)
        xstack = jnp.concatenate(
            [jnp.where(hid == h, xdt_b, jnp.zeros_like(xdt_b)) for h in range(N_HEADS)], axis=0)
        yoff = jnp.concatenate([_dot(cm[ck][:, groups[g]], entering[ck][g]) for g in range(SSM_GROUPS)], axis=1)
        y = _dot(mcat[ck], xstack) + yoff * eacs_e[ck] + dexp_ref[...] * xs[ck]
        ys.append(y * z_ref[rows[ck], :].astype(F32))

    for ck in chunks:
        msq = _dot((ys[ck] * ys[ck]).astype(BF16), g_ref[...])
        out_ref[rows[ck], :] = (ys[ck] * lax.rsqrt(msq + RMS_EPS) * nw_ref[...]).astype(BF16)


def _conv_shift_matrices():
    r = jnp.arange(CHUNK)[:, None]
    col = jnp.arange(2 * CHUNK)[None, :]
    return jnp.stack([(col == CHUNK + r - j) for j in range(1, SSM_CONV_K)]).astype(BF16)


def _pool_bands():
    r = jnp.arange(CHUNK)[:, None]
    col = jnp.arange(2 * CHUNK)[None, :]
    back = CHUNK + r - col
    bands = [jnp.where((back >= 0) & (back < w), 1.0 / w, 0.0) for w in POOL_WINDOWS]
    bandcat = jnp.concatenate(bands, axis=1).astype(BF16)
    t = jnp.arange(CHUNK, dtype=F32)[:, None] + 1.0
    win = jnp.repeat(jnp.asarray(POOL_WINDOWS, F32), HEAD_DIM)[None, :]
    fix = win / jnp.minimum(win, t)
    return bandcat, fix


def _pool_chunks(pc_ref, pch_ref, band_ref, fix_ref, wbd_ref, scale_ref, out_ref, first_step):
    hid = _head_id((CHUNK, GROUP_W), HEAD_DIM)
    prev = jnp.where(first_step, jnp.zeros_like(pch_ref), pch_ref[...])
    for ck in range(pc_ref.shape[0] // CHUNK):
        cur = pc_ref[ck * CHUNK:(ck + 1) * CHUNK, :]
        parts = []
        for g in range(len(POOL_WINDOWS)):
            parts += [jnp.where(hid == g, prev, jnp.zeros_like(prev)),
                      jnp.where(hid == g, cur, jnp.zeros_like(cur))]
        mean = _dot(band_ref[...], jnp.concatenate(parts, axis=0))
        if ck == 0:
            mean = jnp.where(first_step, mean * fix_ref[...], mean)
        p = mean - cur.astype(F32)
        y = _dot(p.astype(BF16), wbd_ref[...]) * scale_ref[...]
        out_ref[ck * CHUNK:(ck + 1) * CHUNK, :] = y.astype(BF16)
        prev = cur


N_GMLP_IN, N_POOL_IN, N_SSD_IN = 5, 6, 11


def _mixers_kernel(*refs):
    gm = refs[:N_GMLP_IN]
    po = refs[N_GMLP_IN:N_GMLP_IN + N_POOL_IN]
    sd = refs[N_GMLP_IN + N_POOL_IN:N_GMLP_IN + N_POOL_IN + N_SSD_IN]
    ya_ref, yc_ref, yb_ref, st_ref = refs[N_GMLP_IN + N_POOL_IN + N_SSD_IN:]
    first_step = pl.program_id(1) == 0

    def others():
        _gmlp_kernel(*gm, ya_ref)
        _pool_chunks(*po, yc_ref, first_step)

    _ssd_kernel(*sd, yb_ref, st_ref, between_phases=others)


def _mixers(proj, dt, gws, bsm, gmn, g64, band, fix, wbd, psc, cw, cb, shift, alog, dexp, nw, g128,
            layer, b, seq):
    m = proj.shape[0]
    ts = SSD_CPS * CHUNK
    assert seq % ts == 0
    nc = seq // ts
    const = lambda i, c: (0, 0)
    blk = lambda width, col: pl.BlockSpec((ts, width), lambda i, c: (i * nc + c, col // width))
    halo = lambda width, col: pl.BlockSpec(
        (CHUNK, width), lambda i, c: (jnp.maximum((i * nc + c) * SSD_CPS - 1, 0), col // width))
    out = pl.BlockSpec((ts, GROUP_W), lambda i, c: (i * nc + c, 0))
    gmlp_in = [blk(2 * GROUP_W, COL_PA), _per_layer((N_HEADS, CHUNK, CHUNK), layer),
               _per_layer((CHUNK, GROUP_W), layer), _per_layer((1, GROUP_W), layer),
               pl.BlockSpec((GROUP_W, GROUP_W), const)]
    pool_in = [blk(GROUP_W, COL_PC), halo(GROUP_W, COL_PC),
               pl.BlockSpec((CHUNK, len(POOL_WINDOWS) * 2 * CHUNK), const), pl.BlockSpec((CHUNK, GROUP_W), const),
               _per_layer((GROUP_W, GROUP_W), layer), _per_layer((1, GROUP_W), layer)]
    ssd_in = [blk(GROUP_W, COL_Z), blk(SSM_CONV_DIM, COL_XBC), halo(SSM_CONV_DIM, COL_XBC),
              pl.BlockSpec((ts, DT_W), lambda i, c: (i * nc + c, 0)),
              _per_layer((SSM_CONV_K, SSM_CONV_DIM), layer), _per_layer((1, SSM_CONV_DIM), layer),
              pl.BlockSpec((SSM_CONV_K - 1, CHUNK, 2 * CHUNK), lambda i, c: (0, 0, 0)),
              _per_layer((1, DT_W), layer), _per_layer((1, GROUP_W), layer), _per_layer((1, GROUP_W), layer),
              pl.BlockSpec((GROUP_W, GROUP_W), const)]
    assert (len(gmlp_in), len(pool_in), len(ssd_in)) == (N_GMLP_IN, N_POOL_IN, N_SSD_IN)
    return pl.pallas_call(
        _mixers_kernel,
        grid=(b, nc),
        in_specs=gmlp_in + pool_in + ssd_in,
        out_specs=[out, out, out],
        out_shape=[jax.ShapeDtypeStruct((m, GROUP_W), BF16)] * 3,
        scratch_shapes=[pltpu.VMEM((SSM_GROUPS, SSM_D_STATE, LANES), F32)],
        compiler_params=_cparams(("parallel", "arbitrary")),
        name="mixers",
    )(proj, gws, bsm, gmn, g64,
      proj, proj, band, fix, wbd, psc,
      proj, proj, proj, dt, cw, cb, shift, alog, dexp, nw, g128)


def _alibi_slope(h):
    return 2.0 ** (-8.0 * (h + 1) / N_HEADS)


def _head_slot(x, h):
    base = x[:, (h // 2) * LANES:(h // 2 + 1) * LANES]
    return pltpu.roll(base, HEAD_DIM, 1) if h % 2 else base


def _attn_kernel(q_ref, k_ref, v_ref, qnw_ref, knw_ref, lq1_ref, lk1_ref, lq2_ref, lk2_ref, subw_ref,
                 g_ref, out_ref, ka_ref, vt_ref, qs_ref, acc_ref, s0_ref, kmax_ref, qa_ref, *, lam_init):
    qi = pl.program_id(1)
    tq, tk = ATT_TQ, ATT_TK
    seq = k_ref.shape[0]

    def aug_lanes(pos, slope, key_side):
        lane = lax.broadcasted_iota(jnp.int32, pos.shape, 1)
        hi = (slope * LANES) * lax.shift_right_logical(pos, 7).astype(F32)
        lo = slope * (pos & (LANES - 1)).astype(F32)
        one = jnp.ones_like(hi)
        c = (one, one, hi, lo) if key_side else (-hi, -lo, one, one)
        last = jnp.where(lane == 68, 1.0, 0.0) if key_side else 0.0
        return jnp.where(lane == 64, c[0], jnp.where(lane == 65, c[1],
                         jnp.where(lane == 66, c[2], jnp.where(lane == 67, c[3], last))))

    @pl.when(qi == 0)
    def _():
        k = k_ref[...].astype(F32)
        ms = _dot((k * k).astype(BF16), g_ref[...])
        kn = k * lax.rsqrt(ms + RMS_EPS) * knw_ref[...]
        ksq = _dot((kn * kn).astype(BF16), g_ref[...]) * DA_QK_DIM
        kmax_ref[...] = jnp.broadcast_to(jnp.max(ksq, axis=0, keepdims=True), kmax_ref.shape)
        pos = lax.broadcasted_iota(jnp.int32, (seq, LANES), 0)
        lane = lax.broadcasted_iota(jnp.int32, (seq, LANES), 1)
        for h in range(N_HEADS):
            aug = aug_lanes(pos, _alibi_slope(h), True)
            ka_ref[h] = jnp.where(lane < HEAD_DIM, _head_slot(kn, h), aug).astype(BF16)
            qa_ref[h] = aug_lanes(pos, _alibi_slope(h), False).astype(BF16)
        tail = (lax.broadcasted_iota(jnp.int32, (VT_ROWS - HEAD_DIM, tk), 0) == 0).astype(BF16)
        for j in range(seq // tk):
            vt = v_ref[j * tk:(j + 1) * tk, :].astype(F32).T.astype(BF16)
            for h in range(N_HEADS):
                vt_ref[j, h, 0:HEAD_DIM, :] = vt[h * HEAD_DIM:(h + 1) * HEAD_DIM, :]
                vt_ref[j, h, HEAD_DIM:VT_ROWS, :] = tail

    q = q_ref[...].astype(F32)
    ms = _dot((q * q).astype(BF16), g_ref[...])
    qn = q * lax.rsqrt(ms + RMS_EPS) * (qnw_ref[...] * (DA_QK_DIM ** -0.5))
    qsq = _dot((qn * qn).astype(BF16), g_ref[...]) * DA_QK_DIM
    b2 = qsq * kmax_ref[0:1, :]
    bound = b2 * lax.rsqrt(b2 + 1e-30) * ATT_BOUND_MARGIN
    one_pass = jnp.max(bound) <= ATT_ONE_PASS_BOUND
    lane = lax.broadcasted_iota(jnp.int32, (1, LANES), 1)
    q0 = pl.multiple_of(qi * tq, tq)
    for h in range(N_HEADS):
        base = _head_slot(qn, h)
        bnd = _head_slot(bound, h)
        qa = qa_ref[h, pl.ds(q0, tq), :]
        for comp in range(2):
            own = (lane >= comp * DA_QK_DIM) & (lane < (comp + 1) * DA_QK_DIM)
            y = jnp.where(own, base, 0.0)
            y = jnp.where(lane == 68, -bnd[:, comp * DA_QK_DIM:comp * DA_QK_DIM + 1], y)
            qs_ref[h, comp * tq:(comp + 1) * tq, :] = y.astype(BF16) + qa
    acc_ref[...] = jnp.zeros_like(acc_ref)

    qb = ATT_QB
    items = [(h, n) for h in range(N_HEADS) for n in range(2 * tq // qb)]

    def scores(item, j):
        h, n = item
        k0 = pl.multiple_of(j * tk, tk)
        return _dot_nt(ka_ref[h, pl.ds(k0, tk), :], qs_ref[h, n * qb:(n + 1) * qb, :])

    def tile(j, ms_, masked):
        new_m = []
        s_next = s0_ref[...]
        for i, (h, n) in enumerate(items):
            cols = slice(n * qb, (n + 1) * qb)
            s = s_next
            if i + 1 < len(items):
                s_next = scores(items[i + 1], j)
            elif not masked:
                s0_ref[...] = scores(items[0], j + 1)
            if masked:
                kk = lax.broadcasted_iota(jnp.int32, (tk, qb), 0)
                qq = (lax.broadcasted_iota(jnp.int32, (tk, qb), 1) + n * qb) & (tq - 1)
                s = jnp.where(kk <= qq, s, NEG_BIG)
            m_old = ms_[i]
            m_new = jnp.maximum(m_old, jnp.max(s, axis=0, keepdims=True))
            alpha = jnp.exp(m_old - m_new)
            p = jnp.exp(s - m_new)
            new_m.append(m_new)
            acc_ref[h, :, cols] = acc_ref[h, :, cols] * alpha + _dot(vt_ref[j, h], p.astype(BF16))
        return tuple(new_m)

    def tile_one_pass(j, masked):
        s_next = s0_ref[...]
        for i, (h, n) in enumerate(items):
            cols = slice(n * qb, (n + 1) * qb)
            s = s_next
            if i + 1 < len(items):
                s_next = scores(items[i + 1], j)
            elif not masked:
                s0_ref[...] = scores(items[0], j + 1)
            if masked:
                kk = lax.broadcasted_iota(jnp.int32, (tk, qb), 0)
                qq = (lax.broadcasted_iota(jnp.int32, (tk, qb), 1) + n * qb) & (tq - 1)
                s = jnp.where(kk <= qq, s, NEG_BIG)
            acc_ref[h, :, cols] += _dot(vt_ref[j, h], jnp.exp(s).astype(BF16))

    s0_ref[...] = scores(items[0], 0)

    @pl.when(one_pass)
    def _():
        def body(j, carry):
            tile_one_pass(j, False)
            return carry
        lax.fori_loop(0, qi, body, 0)
        tile_one_pass(qi, True)

    @pl.when(jnp.logical_not(one_pass))
    def _():
        m0 = tuple(jnp.full((1, qb), NEG_BIG, F32) for _ in items)
        m1 = lax.fori_loop(0, qi, lambda j, c: tile(j, c, False), m0)
        tile(qi, m1, True)

    def lane_sum(x):
        return jnp.broadcast_to(jnp.sum(x, axis=1, keepdims=True), x.shape)

    lam = (jnp.exp(lane_sum(lq1_ref[...] * lk1_ref[...]))
           - jnp.exp(lane_sum(lq2_ref[...] * lk2_ref[...])) + lam_init)
    lam = jnp.concatenate([lam] * (tq // LANES), axis=1)
    outs = []
    for h in range(N_HEADS):
        o = acc_ref[h, 0:HEAD_DIM, :] / acc_ref[h, HEAD_DIM:HEAD_DIM + 1, :]
        oh = o[:, 0:tq] - lam * o[:, tq:2 * tq]
        ms = jnp.mean(oh * oh, axis=0, keepdims=True)
        outs.append(oh * lax.rsqrt(ms + RMS_EPS) * (subw_ref[...] * (1.0 - lam_init)))
    out_ref[...] = jnp.concatenate(outs, axis=0).T.astype(BF16)


def _attn(proj, qnw, knw, lq1, lk1, lq2, lk2, subw, g32, layer, b, seq, lam_init):
    m = proj.shape[0]
    assert ATT_TQ == ATT_TK
    nq = seq // ATT_TQ
    const = lambda i, j: (0, 0)
    return pl.pallas_call(
        functools.partial(_attn_kernel, lam_init=lam_init),
        grid=(b, nq),
        in_specs=[
            pl.BlockSpec((ATT_TQ, GROUP_W), lambda i, j: (i * nq + j, COL_Q // GROUP_W)),
            pl.BlockSpec((seq, GROUP_W), lambda i, j: (i, COL_K // GROUP_W)),
            pl.BlockSpec((seq, GROUP_W), lambda i, j: (i, COL_V // GROUP_W)),
            _per_layer((1, GROUP_W), layer),
            _per_layer((1, GROUP_W), layer),
            _per_layer((1, LANES), layer),
            _per_layer((1, LANES), layer),
            _per_layer((1, LANES), layer),
            _per_layer((1, LANES), layer),
            _per_layer((HEAD_DIM, ATT_TQ), layer),
            pl.BlockSpec((GROUP_W, GROUP_W), const),
        ],
        out_specs=pl.BlockSpec((ATT_TQ, GROUP_W), lambda i, j: (i * nq + j, 0)),
        out_shape=jax.ShapeDtypeStruct((m, GROUP_W), BF16),
        scratch_shapes=[
            pltpu.VMEM((N_HEADS, seq, LANES), BF16),
            pltpu.VMEM((seq // ATT_TK, N_HEADS, VT_ROWS, ATT_TK), BF16),
            pltpu.VMEM((N_HEADS, 2 * ATT_TQ, LANES), BF16),
            pltpu.VMEM((N_HEADS, VT_ROWS, 2 * ATT_TQ), F32),
            pltpu.VMEM((ATT_TK, ATT_QB), F32),
            pltpu.VMEM((8, GROUP_W), F32),
            pltpu.VMEM((N_HEADS, seq, LANES), BF16),
        ],
        compiler_params=_cparams(("parallel", "arbitrary")),
        name="diffattn",
    )(proj, proj, proj, qnw, knw, lq1, lk1, lq2, lk2, subw, g32)


FFN_STAGE_WIDE = 256
FFN_STAGE_TALL = 704


def _ffn_weight_jobs(layer, wo_hbm, wg_hbm, wu_hbm, wd_hbm, wo_b, wg_b, wu_b, wd_b, wide, tall, sem_w, sem_t):
    jobs = []

    def add(src, dst, stage, sem, rows, n_rows):
        for k, r0 in enumerate(range(0, n_rows, rows)):
            r = min(rows, n_rows - r0)
            slot = k % 2
            view = stage.at[slot, 0:r, :]
            copy = pltpu.make_async_copy(src.at[layer, r0:r0 + r, :], view, sem.at[slot])
            jobs.append((copy, view, dst.at[r0:r0 + r, :]))

    add(wg_hbm, wg_b, wide, sem_w, FFN_STAGE_WIDE, D_MODEL)
    add(wu_hbm, wu_b, wide, sem_w, FFN_STAGE_WIDE, D_MODEL)
    add(wd_hbm, wd_b, tall, sem_t, FFN_STAGE_TALL, D_FF)
    add(wo_hbm, wo_b, tall, sem_t, FFN_STAGE_TALL, D_MODEL)
    return jobs


def _ffn_kernel(x_ref, ya_ref, yb_ref, yc_ref, yd_ref, nw_ref, wo_hbm, wg_hbm, wu_hbm, wd_hbm,
                out_ref, act_ref, wo_ref, wg_ref, wu_ref, wd_ref, wide_ref, tall_ref, sem_w, sem_t, *, layer):
    @pl.when(pl.program_id(0) == 0)
    def _():
        jobs = _ffn_weight_jobs(layer, wo_hbm, wg_hbm, wu_hbm, wd_hbm, wo_ref, wg_ref, wu_ref, wd_ref,
                                wide_ref, tall_ref, sem_w, sem_t)
        jobs[0][0].start()
        for k, (copy, view, dst) in enumerate(jobs):
            if k + 1 < len(jobs):
                jobs[k + 1][0].start()
            copy.wait()
            dst[...] = view[...].astype(BF16)

    x1 = x_ref[...]
    for i, y_ref in enumerate((ya_ref, yb_ref, yc_ref, yd_ref)):
        x1 = x1 + _dot(y_ref[...], wo_ref[i * GROUP_W:(i + 1) * GROUP_W, :])
    ms = jnp.mean(x1 * x1, axis=-1, keepdims=True)
    h = (x1 * lax.rsqrt(ms + RMS_EPS) * nw_ref[...]).astype(BF16)
    for c0 in range(0, D_FF, FFN_CHUNK):
        c1 = min(c0 + FFN_CHUNK, D_FF)
        g = _dot(h, wg_ref[:, c0:c1])
        u = _dot(h, wu_ref[:, c0:c1])
        act_ref[:, c0:c1] = (g * _sigmoid(g) * u).astype(BF16)
    out_ref[...] = x1 + _dot(act_ref[...], wd_ref[...])


def _ffn(x2, ya, yb, yc, yd, nw, wo, wg, wu, wd, layer, tm):
    m = x2.shape[0]
    row = lambda i: (i, 0)
    hbm = pl.BlockSpec(memory_space=pl.ANY)
    return pl.pallas_call(
        functools.partial(_ffn_kernel, layer=layer),
        grid=(m // tm,),
        in_specs=[
            pl.BlockSpec((tm, D_MODEL), row),
            pl.BlockSpec((tm, GROUP_W), row),
            pl.BlockSpec((tm, GROUP_W), row),
            pl.BlockSpec((tm, GROUP_W), row),
            pl.BlockSpec((tm, GROUP_W), row),
            _per_layer((1, D_MODEL), layer),
            hbm, hbm, hbm, hbm,
        ],
        out_specs=pl.BlockSpec((tm, D_MODEL), row),
        out_shape=jax.ShapeDtypeStruct((m, D_MODEL), F32),
        scratch_shapes=[
            pltpu.VMEM((tm, D_FF), BF16),
            pltpu.VMEM((D_MODEL, D_MODEL), BF16),
            pltpu.VMEM((D_MODEL, D_FF), BF16),
            pltpu.VMEM((D_MODEL, D_FF), BF16),
            pltpu.VMEM((D_FF, D_MODEL), BF16),
            pltpu.VMEM((2, FFN_STAGE_WIDE, D_FF), F32),
            pltpu.VMEM((2, FFN_STAGE_TALL, D_MODEL), F32),
            pltpu.SemaphoreType.DMA((2,)),
            pltpu.SemaphoreType.DMA((2,)),
        ],
        compiler_params=_cparams(("arbitrary",)),
        name="outproj_ffn",
    )(x2, ya, yb, yc, yd, nw, wo, wg, wu, wd)


def _block_diag_mean(width, group):
    idx = jnp.arange(width) // group
    return jnp.where(idx[:, None] == idx[None, :], 1.0 / group, 0.0).astype(BF16)


def _rows(v, width=None):
    v = v.reshape(v.shape[0], 1, -1).astype(F32)
    return v if width is None else jnp.pad(v, ((0, 0), (0, 0), (0, width - v.shape[2])))


def kernel(x, norm1_w, w_in, gm_norm_w, gm_ws, gm_bs, ssm_conv_w, ssm_conv_b, ssm_dt_bias, ssm_a_log, ssm_d, ssm_norm_w, pool_w, pool_scale, da_q_norm_w, da_k_norm_w, da_lambda_q1, da_lambda_k1, da_lambda_q2, da_lambda_k2, da_subln_w, w_out, norm2_w, ffn_w_gate, ffn_w_up, ffn_w_down):
    b, seq, d = x.shape
    depth = w_in.shape[0]
    assert d == D_MODEL and seq % ATT_TQ == 0 and seq % CHUNK == 0
    m = b * seq
    tm = 512 if m % 512 == 0 else ATT_TQ

    g64 = _block_diag_mean(GROUP_W, HEAD_DIM)
    g128 = _block_diag_mean(GROUP_W, SSM_D_STATE)
    g32 = _block_diag_mean(GROUP_W, DA_QK_DIM)

    shift = _conv_shift_matrices()
    band, fix = _pool_bands()

    assert w_in.shape[1:] == (D_MODEL, D_IN)
    w_in_t = jnp.swapaxes(w_in, 1, 2)

    n1, n2 = _rows(norm1_w), _rows(norm2_w)
    dtb, alog = _rows(ssm_dt_bias, DT_W), _rows(ssm_a_log, DT_W)
    bsm = jnp.repeat(jnp.swapaxes(gm_bs, 1, 2), HEAD_DIM, axis=2)
    gmn = _rows(gm_norm_w)
    cw, cb = jnp.swapaxes(ssm_conv_w, 1, 2), _rows(ssm_conv_b)
    dexp, ssn = _rows(jnp.repeat(ssm_d, HEAD_DIM, axis=1)), _rows(ssm_norm_w)
    eye = jnp.eye(len(POOL_WINDOWS), dtype=pool_w.dtype)
    wbd = jnp.einsum('lgab,gh->lgahb', pool_w, eye).reshape(depth, GROUP_W, GROUP_W).astype(BF16)
    psc = _rows(pool_scale)
    qnw = _rows(jnp.tile(da_q_norm_w, (1, GROUP_W // DA_QK_DIM)))
    knw = _rows(jnp.tile(da_k_norm_w, (1, GROUP_W // DA_QK_DIM)))
    lams = [_rows(v, LANES) for v in (da_lambda_q1, da_lambda_k1, da_lambda_q2, da_lambda_k2)]
    subw = jnp.broadcast_to(da_subln_w[:, :, None], (depth, HEAD_DIM, ATT_TQ))

    x2 = x.reshape(m, d)
    for i in range(depth):
        proj, dt = _inproj(x2, n1, w_in_t, i, dtb, tm)
        ya, yc, yb = _mixers(proj, dt, gm_ws, bsm, gmn, g64, band, fix, wbd, psc,
                             cw, cb, shift, alog, dexp, ssn, g128, i, b, seq)
        lam_init = 0.8 - 0.6 * math.exp(-0.3 * i)
        yd = _attn(proj, qnw, knw, *lams, subw, g32, i, b, seq, lam_init)
        x2 = _ffn(x2, ya, yb, yc, yd, n2, w_out, ffn_w_gate, ffn_w_up, ffn_w_down, i, tm)
    return x2.reshape(b, seq, d)
```

```python
import functools
import math

import jax
import jax.numpy as jnp
from jax import lax
from jax.experimental import pallas as pl
from jax.experimental.pallas import tpu as pltpu

F32 = jnp.float32
BF16 = jnp.bfloat16

D_MODEL = 1024
GROUP_W = 256
CHUNK = 128
HEAD_DIM = 64
N_HEADS = 4
SSM_GROUPS = 2
SSM_D_STATE = 128
SSM_CONV_K = 4
SSM_CONV_DIM = GROUP_W + 2 * SSM_GROUPS * SSM_D_STATE
POOL_WINDOWS = (2, 4, 8, 16)
DA_QK_DIM = 32
D_FF = 2816
RMS_EPS = 1e-6
NEG_BIG = -1e30

LANES = 128
VMEM_LIMIT = 56 * 1024 * 1024

COL_PA = 0
COL_Z = 512
COL_XBC = 768
COL_PC = 1536
COL_Q = 1792
COL_K = 2048
COL_V = 2304
PROJ_W = 2560
DT_W = LANES
D_IN = PROJ_W + N_HEADS
DT_COL = COL_PC
IN_CHUNK = 256
FFN_CHUNK = 512
SSD_CPS = 4

ATT_TQ = 512
ATT_TK = 512
ATT_QB = 512
ATT_BOUND_MARGIN = 1.01
ATT_ONE_PASS_BOUND = 40.0
VT_ROWS = 80


def _cparams(sem):
    return pltpu.CompilerParams(dimension_semantics=sem, vmem_limit_bytes=VMEM_LIMIT)


def _sigmoid(x):
    return 1.0 / (1.0 + jnp.exp(-x))


def _dot(a, b):
    return jnp.dot(a, b, preferred_element_type=F32)


def _dot_nt(a, b):
    return lax.dot_general(a, b, (((1,), (1,)), ((), ())), preferred_element_type=F32)


def _softplus(t):
    return jnp.maximum(t, 0.0) + jnp.log(1.0 + jnp.exp(-jnp.abs(t)))


def _inproj_kernel(x_ref, nw_ref, w_ref, dtb_ref, proj_ref, dt_ref, wb_ref):
    @pl.when(pl.program_id(0) == 0)
    def _():
        for n0 in range(0, DT_COL, 2 * IN_CHUNK):
            wb_ref[:, n0:n0 + 2 * IN_CHUNK] = w_ref[:, n0:n0 + 2 * IN_CHUNK].astype(BF16)
        tail = w_ref[:, DT_COL:D_IN]
        wb_ref[:, DT_COL:PROJ_W] = tail[:, N_HEADS:].astype(BF16)
        wb_ref[:, PROJ_W:PROJ_W + DT_W] = tail[:, 0:DT_W].astype(BF16)

    x = x_ref[...]
    ms = jnp.mean(x * x, axis=-1, keepdims=True)
    h = (x * lax.rsqrt(ms + RMS_EPS) * nw_ref[...]).astype(BF16)
    for n0 in range(0, PROJ_W, IN_CHUNK):
        y = _dot(h, wb_ref[:, n0:n0 + IN_CHUNK])
        if n0 < COL_Z:
            y = jax.nn.gelu(y, approximate=True)
        elif n0 < COL_XBC:
            y = y * _sigmoid(y)
        proj_ref[:, n0:n0 + IN_CHUNK] = y.astype(BF16)
    dt_ref[...] = _softplus(_dot(h, wb_ref[:, PROJ_W:PROJ_W + DT_W]) + dtb_ref[...])


def _per_layer(shape, layer):
    return pl.BlockSpec((None,) + tuple(shape), lambda *_: (layer,) + (0,) * len(shape))


def _inproj(x2, nw, w_in, layer, dtb, tm):
    m = x2.shape[0]
    return pl.pallas_call(
        _inproj_kernel,
        grid=(m // tm,),
        in_specs=[
            pl.BlockSpec((tm, D_MODEL), lambda i: (i, 0)),
            _per_layer((1, D_MODEL), layer),
            pl.BlockSpec((None, D_MODEL, D_IN), lambda i: (layer, 0, 0), pipeline_mode=pl.Buffered(1)),
            _per_layer((1, DT_W), layer),
        ],
        out_specs=[
            pl.BlockSpec((tm, PROJ_W), lambda i: (i, 0)),
            pl.BlockSpec((tm, DT_W), lambda i: (i, 0)),
        ],
        out_shape=[
            jax.ShapeDtypeStruct((m, PROJ_W), BF16),
            jax.ShapeDtypeStruct((m, DT_W), F32),
        ],
        scratch_shapes=[pltpu.VMEM((D_MODEL, PROJ_W + DT_W), BF16)],
        compiler_params=_cparams(("arbitrary",)),
        name="inproj",
    )(x2, nw, w_in, dtb)


def _head_id(shape, width):
    lane = lax.broadcasted_iota(jnp.int32, shape, 1)
    return lax.shift_right_logical(lane, int(math.log2(width)))


def _gmlp_kernel(pa_ref, ws_ref, bsm_ref, nw_ref, g_ref, out_ref):
    t = pa_ref.shape[0]
    hact = pa_ref[...].astype(F32)
    u = hact[:, :GROUP_W]
    v = hact[:, GROUP_W:]
    ms = _dot((v * v).astype(BF16), g_ref[...])
    vn = (v * lax.rsqrt(ms + RMS_EPS) * nw_ref[...]).astype(BF16)
    row = lax.broadcasted_iota(jnp.int32, (CHUNK, CHUNK), 0)
    col = lax.broadcasted_iota(jnp.int32, (CHUNK, CHUNK), 1)
    wcat = jnp.concatenate(
        [jnp.where(row >= col, ws_ref[h], 0.0) for h in range(N_HEADS)], axis=1).astype(BF16)
    hid = _head_id((CHUNK, GROUP_W), HEAD_DIM)
    for c in range(t // CHUNK):
        vc = vn[c * CHUNK:(c + 1) * CHUNK]
        vstack = jnp.concatenate(
            [jnp.where(hid == h, vc, jnp.zeros_like(vc)) for h in range(N_HEADS)], axis=0)
        s = _dot(wcat, vstack) + bsm_ref[...]
        out_ref[c * CHUNK:(c + 1) * CHUNK, :] = (u[c * CHUNK:(c + 1) * CHUNK] * s).astype(BF16)


def _split3(a):
    a1 = a.astype(BF16)
    r1 = a - a1.astype(F32)
    a2 = r1.astype(BF16)
    r2 = r1 - a2.astype(F32)
    return a1, a2, r2.astype(BF16)


def _ssd_kernel(z_ref, xbc_ref, halo_ref, dt_ref, cw_ref, cb_ref, shift_ref, alog_ref, dexp_ref, nw_ref,
                g_ref, out_ref, st_ref, between_phases=None):
    c = pl.program_id(1)

    @pl.when(c == 0)
    def _():
        st_ref[...] = jnp.zeros_like(st_ref)

    row = lax.broadcasted_iota(jnp.int32, (CHUNK, CHUNK), 0)
    col = lax.broadcasted_iota(jnp.int32, (CHUNK, CHUNK), 1)
    causal = row >= col
    ltri = jnp.where(causal, 1.0, 0.0).astype(BF16)
    hid = _head_id((CHUNK, GROUP_W), HEAD_DIM)
    neg_a = -jnp.exp(alog_ref[...])

    def expand(c4):
        return jnp.where(hid == 0, c4[:, 0:1],
                         jnp.where(hid == 1, c4[:, 1:2],
                                   jnp.where(hid == 2, c4[:, 2:3], c4[:, 3:4])))

    chunks = range(SSD_CPS)
    rows = [slice(ck * CHUNK, (ck + 1) * CHUNK) for ck in chunks]
    groups = [slice(g * SSM_D_STATE, (g + 1) * SSM_D_STATE) for g in range(SSM_GROUPS)]

    xs, bm, cm = [], [], []
    halo = jnp.where(c > 0, halo_ref[...], jnp.zeros_like(halo_ref))
    for ck in chunks:
        xcur = xbc_ref[rows[ck], :]
        acc = cb_ref[...] + cw_ref[SSM_CONV_K - 1:SSM_CONV_K, :] * xcur.astype(F32)
        for j in range(1, SSM_CONV_K):
            k = SSM_CONV_K - 1 - j
            if ck == 0:
                shifted = (_dot(shift_ref[j - 1, :, 0:CHUNK], halo)
                           + _dot(shift_ref[j - 1, :, CHUNK:2 * CHUNK], xcur))
            else:
                shifted = _dot(shift_ref[j - 1], xbc_ref[(ck - 1) * CHUNK:(ck + 1) * CHUNK, :])
            acc = acc + cw_ref[k:k + 1, :] * shifted
        xc = acc * _sigmoid(acc)
        xs.append(xc[:, :GROUP_W])
        bm.append(xc[:, GROUP_W:2 * GROUP_W])
        cm.append(xc[:, 2 * GROUP_W:].astype(BF16))

    acs, acs_t = [], []
    for ck in chunks:
        a1, a2, a3 = _split3(dt_ref[rows[ck], :] * neg_a)
        acs.append(_dot(ltri, a1) + _dot(ltri, a2) + _dot(ltri, a3))
        acs_t.append(acs[ck].T)

    if between_phases is not None:
        between_phases()

    xdt, eacs_e, mcat, snew = [], [], [], []
    for ck in chunks:
        xdt.append(xs[ck] * expand(dt_ref[rows[ck], :]))
        eacs_e.append(expand(jnp.exp(acs[ck])))
        dte_e = expand(jnp.exp(acs[ck][CHUNK - 1:CHUNK, :] - acs[ck]))
        ms = []
        for g in range(SSM_GROUPS):
            cb = _dot_nt(cm[ck][:, groups[g]], bm[ck][:, groups[g]].astype(BF16))
            for hh in range(N_HEADS // SSM_GROUPS):
                h = g * (N_HEADS // SSM_GROUPS) + hh
                seg = acs[ck][:, h:h + 1] - acs_t[ck][h:h + 1, :]
                ms.append((cb * jnp.exp(jnp.where(causal, seg, NEG_BIG))).astype(BF16))
        mcat.append(jnp.concatenate(ms, axis=1))
        w = (xdt[ck] * dte_e).astype(BF16)
        snew.append([_dot(bm[ck][:, groups[g]].T.astype(BF16), w[:, groups[g]])
                     for g in range(SSM_GROUPS)])

    state = [st_ref[g] for g in range(SSM_GROUPS)]
    entering = []
    for ck in chunks:
        entering.append([st.astype(BF16) for st in state])
        cdl = eacs_e[ck][CHUNK - 1:CHUNK, :]
        state = [state[g] * cdl[:, groups[g]] + snew[ck][g] for g in range(SSM_GROUPS)]
    for g in range(SSM_GROUPS):
        st_ref[g] = state[g]

    ys = []
    for ck in chunks:
        xdt_b = xdt[ck].astype(BF16)
        xstack = jnp.concatenate(
            [jnp.where(hid == h, xdt_b, jnp.zeros_like(xdt_b)) for h in range(N_HEADS)], axis=0)
        yoff = jnp.concatenate([_dot(cm[ck][:, groups[g]], entering[ck][g]) for g in range(SSM_GROUPS)], axis=1)
        y = _dot(mcat[ck], xstack) + yoff * eacs_e[ck] + dexp_ref[...] * xs[ck]
        ys.append(y * z_ref[rows[ck], :].astype(F32))

    for ck in chunks:
        msq = _dot((ys[ck] * ys[ck]).astype(BF16), g_ref[...])
        out_ref[rows[ck], :] = (ys[ck] * lax.rsqrt(msq + RMS_EPS) * nw_ref[...]).astype(BF16)


def _conv_shift_matrices():
    r = jnp.arange(CHUNK)[:, None]
    col = jnp.arange(2 * CHUNK)[None, :]
    return jnp.stack([(col == CHUNK + r - j) for j in range(1, SSM_CONV_K)]).astype(BF16)


def _pool_bands():
    r = jnp.arange(CHUNK)[:, None]
    col = jnp.arange(2 * CHUNK)[None, :]
    back = CHUNK + r - col
    bands = [jnp.where((back >= 0) & (back < w), 1.0 / w, 0.0) for w in POOL_WINDOWS]
    bandcat = jnp.concatenate(bands, axis=1).astype(BF16)
    t = jnp.arange(CHUNK, dtype=F32)[:, None] + 1.0
    win = jnp.repeat(jnp.asarray(POOL_WINDOWS, F32), HEAD_DIM)[None, :]
    fix = win / jnp.minimum(win, t)
    return bandcat, fix


def _pool_chunks(pc_ref, pch_ref, band_ref, fix_ref, wbd_ref, scale_ref, out_ref, first_step):
    hid = _head_id((CHUNK, GROUP_W), HEAD_DIM)
    prev = jnp.where(first_step, jnp.zeros_like(pch_ref), pch_ref[...])
    for ck in range(pc_ref.shape[0] // CHUNK):
        cur = pc_ref[ck * CHUNK:(ck + 1) * CHUNK, :]
        parts = []
        for g in range(len(POOL_WINDOWS)):
            parts += [jnp.where(hid == g, prev, jnp.zeros_like(prev)),
                      jnp.where(hid == g, cur, jnp.zeros_like(cur))]
        mean = _dot(band_ref[...], jnp.concatenate(parts, axis=0))
        if ck == 0:
            mean = jnp.where(first_step, mean * fix_ref[...], mean)
        p = mean - cur.astype(F32)
        y = _dot(p.astype(BF16), wbd_ref[...]) * scale_ref[...]
        out_ref[ck * CHUNK:(ck + 1) * CHUNK, :] = y.astype(BF16)
        prev = cur


N_GMLP_IN, N_POOL_IN, N_SSD_IN = 5, 6, 11


def _mixers_kernel(*refs):
    gm = refs[:N_GMLP_IN]
    po = refs[N_GMLP_IN:N_GMLP_IN + N_POOL_IN]
    sd = refs[N_GMLP_IN + N_POOL_IN:N_GMLP_IN + N_POOL_IN + N_SSD_IN]
    ya_ref, yc_ref, yb_ref, st_ref = refs[N_GMLP_IN + N_POOL_IN + N_SSD_IN:]
    first_step = pl.program_id(1) == 0

    def others():
        _gmlp_kernel(*gm, ya_ref)
        _pool_chunks(*po, yc_ref, first_step)

    _ssd_kernel(*sd, yb_ref, st_ref, between_phases=others)


def _mixers(proj, dt, gws, bsm, gmn, g64, band, fix, wbd, psc, cw, cb, shift, alog, dexp, nw, g128,
            layer, b, seq):
    m = proj.shape[0]
    ts = SSD_CPS * CHUNK
    assert seq % ts == 0
    nc = seq // ts
    const = lambda i, c: (0, 0)
    blk = lambda width, col: pl.BlockSpec((ts, width), lambda i, c: (i * nc + c, col // width))
    halo = lambda width, col: pl.BlockSpec(
        (CHUNK, width), lambda i, c: (jnp.maximum((i * nc + c) * SSD_CPS - 1, 0), col // width))
    out = pl.BlockSpec((ts, GROUP_W), lambda i, c: (i * nc + c, 0))
    gmlp_in = [blk(2 * GROUP_W, COL_PA), _per_layer((N_HEADS, CHUNK, CHUNK), layer),
               _per_layer((CHUNK, GROUP_W), layer), _per_layer((1, GROUP_W), layer),
               pl.BlockSpec((GROUP_W, GROUP_W), const)]
    pool_in = [blk(GROUP_W, COL_PC), halo(GROUP_W, COL_PC),
               pl.BlockSpec((CHUNK, len(POOL_WINDOWS) * 2 * CHUNK), const), pl.BlockSpec((CHUNK, GROUP_W), const),
               _per_layer((GROUP_W, GROUP_W), layer), _per_layer((1, GROUP_W), layer)]
    ssd_in = [blk(GROUP_W, COL_Z), blk(SSM_CONV_DIM, COL_XBC), halo(SSM_CONV_DIM, COL_XBC),
              pl.BlockSpec((ts, DT_W), lambda i, c: (i * nc + c, 0)),
              _per_layer((SSM_CONV_K, SSM_CONV_DIM), layer), _per_layer((1, SSM_CONV_DIM), layer),
              pl.BlockSpec((SSM_CONV_K - 1, CHUNK, 2 * CHUNK), lambda i, c: (0, 0, 0)),
              _per_layer((1, DT_W), layer), _per_layer((1, GROUP_W), layer), _per_layer((1, GROUP_W), layer),
              pl.BlockSpec((GROUP_W, GROUP_W), const)]
    assert (len(gmlp_in), len(pool_in), len(ssd_in)) == (N_GMLP_IN, N_POOL_IN, N_SSD_IN)
    return pl.pallas_call(
        _mixers_kernel,
        grid=(b, nc),
        in_specs=gmlp_in + pool_in + ssd_in,
        out_specs=[out, out, out],
        out_shape=[jax.ShapeDtypeStruct((m, GROUP_W), BF16)] * 3,
        scratch_shapes=[pltpu.VMEM((SSM_GROUPS, SSM_D_STATE, LANES), F32)],
        compiler_params=_cparams(("parallel", "arbitrary")),
        name="mixers",
    )(proj, gws, bsm, gmn, g64,
      proj, proj, band, fix, wbd, psc,
      proj, proj, proj, dt, cw, cb, shift, alog, dexp, nw, g128)


def _alibi_slope(h):
    return 2.0 ** (-8.0 * (h + 1) / N_HEADS)


def _head_slot(x, h):
    base = x[:, (h // 2) * LANES:(h // 2 + 1) * LANES]
    return pltpu.roll(base, HEAD_DIM, 1) if h % 2 else base


def _attn_kernel(q_ref, k_ref, v_ref, qnw_ref, knw_ref, lq1_ref, lk1_ref, lq2_ref, lk2_ref, subw_ref,
                 g_ref, out_ref, ka_ref, vt_ref, qs_ref, acc_ref, s0_ref, kmax_ref, qa_ref, *, lam_init):
    qi = pl.program_id(1)
    tq, tk = ATT_TQ, ATT_TK
    seq = k_ref.shape[0]

    def aug_lanes(pos, slope, key_side):
        lane = lax.broadcasted_iota(jnp.int32, pos.shape, 1)
        hi = (slope * LANES) * lax.shift_right_logical(pos, 7).astype(F32)
        lo = slope * (pos & (LANES - 1)).astype(F32)
        one = jnp.ones_like(hi)
        c = (one, one, hi, lo) if key_side else (-hi, -lo, one, one)
        last = jnp.where(lane == 68, 1.0, 0.0) if key_side else 0.0
        return jnp.where(lane == 64, c[0], jnp.where(lane == 65, c[1],
                         jnp.where(lane == 66, c[2], jnp.where(lane == 67, c[3], last))))

    @pl.when(qi == 0)
    def _():
        k = k_ref[...].astype(F32)
        ms = _dot((k * k).astype(BF16), g_ref[...])
        kn = k * lax.rsqrt(ms + RMS_EPS) * knw_ref[...]
        ksq = _dot((kn * kn).astype(BF16), g_ref[...]) * DA_QK_DIM
        kmax_ref[...] = jnp.broadcast_to(jnp.max(ksq, axis=0, keepdims=True), kmax_ref.shape)
        pos = lax.broadcasted_iota(jnp.int32, (seq, LANES), 0)
        lane = lax.broadcasted_iota(jnp.int32, (seq, LANES), 1)
        for h in range(N_HEADS):
            aug = aug_lanes(pos, _alibi_slope(h), True)
            ka_ref[h] = jnp.where(lane < HEAD_DIM, _head_slot(kn, h), aug).astype(BF16)
            qa_ref[h] = aug_lanes(pos, _alibi_slope(h), False).astype(BF16)
        tail = (lax.broadcasted_iota(jnp.int32, (VT_ROWS - HEAD_DIM, tk), 0) == 0).astype(BF16)
        for j in range(seq // tk):
            vt = v_ref[j * tk:(j + 1) * tk, :].astype(F32).T.astype(BF16)
            for h in range(N_HEADS):
                vt_ref[j, h, 0:HEAD_DIM, :] = vt[h * HEAD_DIM:(h + 1) * HEAD_DIM, :]
                vt_ref[j, h, HEAD_DIM:VT_ROWS, :] = tail

    q = q_ref[...].astype(F32)
    ms = _dot((q * q).astype(BF16), g_ref[...])
    qn = q * lax.rsqrt(ms + RMS_EPS) * (qnw_ref[...] * (DA_QK_DIM ** -0.5))
    qsq = _dot((qn * qn).astype(BF16), g_ref[...]) * DA_QK_DIM
    b2 = qsq * kmax_ref[0:1, :]
    bound = b2 * lax.rsqrt(b2 + 1e-30) * ATT_BOUND_MARGIN
    one_pass = jnp.max(bound) <= ATT_ONE_PASS_BOUND
    lane = lax.broadcasted_iota(jnp.int32, (1, LANES), 1)
    q0 = pl.multiple_of(qi * tq, tq)
    for h in range(N_HEADS):
        base = _head_slot(qn, h)
        bnd = _head_slot(bound, h)
        qa = qa_ref[h, pl.ds(q0, tq), :]
        for comp in range(2):
            own = (lane >= comp * DA_QK_DIM) & (lane < (comp + 1) * DA_QK_DIM)
            y = jnp.where(own, base, 0.0)
            y = jnp.where(lane == 68, -bnd[:, comp * DA_QK_DIM:comp * DA_QK_DIM + 1], y)
            qs_ref[h, comp * tq:(comp + 1) * tq, :] = y.astype(BF16) + qa
    acc_ref[...] = jnp.zeros_like(acc_ref)

    qb = ATT_QB
    items = [(h, n) for h in range(N_HEADS) for n in range(2 * tq // qb)]

    def scores(item, j):
        h, n = item
        k0 = pl.multiple_of(j * tk, tk)
        return _dot_nt(ka_ref[h, pl.ds(k0, tk), :], qs_ref[h, n * qb:(n + 1) * qb, :])

    def tile(j, ms_, masked):
        new_m = []
        s_next = s0_ref[...]
        for i, (h, n) in enumerate(items):
            cols = slice(n * qb, (n + 1) * qb)
            s = s_next
            if i + 1 < len(items):
                s_next = scores(items[i + 1], j)
            elif not masked:
                s0_ref[...] = scores(items[0], j + 1)
            if masked:
                kk = lax.broadcasted_iota(jnp.int32, (tk, qb), 0)
                qq = (lax.broadcasted_iota(jnp.int32, (tk, qb), 1) + n * qb) & (tq - 1)
                s = jnp.where(kk <= qq, s, NEG_BIG)
            m_old = ms_[i]
            m_new = jnp.maximum(m_old, jnp.max(s, axis=0, keepdims=True))
            alpha = jnp.exp(m_old - m_new)
            p = jnp.exp(s - m_new)
            new_m.append(m_new)
            acc_ref[h, :, cols] = acc_ref[h, :, cols] * alpha + _dot(vt_ref[j, h], p.astype(BF16))
        return tuple(new_m)

    def tile_one_pass(j, masked):
        s_next = s0_ref[...]
        for i, (h, n) in enumerate(items):
            cols = slice(n * qb, (n + 1) * qb)
            s = s_next
            if i + 1 < len(items):
                s_next = scores(items[i + 1], j)
            elif not masked:
                s0_ref[...] = scores(items[0], j + 1)
            if masked:
                kk = lax.broadcasted_iota(jnp.int32, (tk, qb), 0)
                qq = (lax.broadcasted_iota(jnp.int32, (tk, qb), 1) + n * qb) & (tq - 1)
                s = jnp.where(kk <= qq, s, NEG_BIG)
            acc_ref[h, :, cols] += _dot(vt_ref[j, h], jnp.exp(s).astype(BF16))

    s0_ref[...] = scores(items[0], 0)

    @pl.when(one_pass)
    def _():
        def body(j, carry):
            tile_one_pass(j, False)
            return carry
        lax.fori_loop(0, qi, body, 0)
        tile_one_pass(qi, True)

    @pl.when(jnp.logical_not(one_pass))
    def _():
        m0 = tuple(jnp.full((1, qb), NEG_BIG, F32) for _ in items)
        m1 = lax.fori_loop(0, qi, lambda j, c: tile(j, c, False), m0)
        tile(qi, m1, True)

    def lane_sum(x):
        return jnp.broadcast_to(jnp.sum(x, axis=1, keepdims=True), x.shape)

    lam = (jnp.exp(lane_sum(lq1_ref[...] * lk1_ref[...]))
           - jnp.exp(lane_sum(lq2_ref[...] * lk2_ref[...])) + lam_init)
    lam = jnp.concatenate([lam] * (tq // LANES), axis=1)
    outs = []
    for h in range(N_HEADS):
        o = acc_ref[h, 0:HEAD_DIM, :] / acc_ref[h, HEAD_DIM:HEAD_DIM + 1, :]
        oh = o[:, 0:tq] - lam * o[:, tq:2 * tq]
        ms = jnp.mean(oh * oh, axis=0, keepdims=True)
        outs.append(oh * lax.rsqrt(ms + RMS_EPS) * (subw_ref[...] * (1.0 - lam_init)))
    out_ref[...] = jnp.concatenate(outs, axis=0).T.astype(BF16)


def _attn(proj, qnw, knw, lq1, lk1, lq2, lk2, subw, g32, layer, b, seq, lam_init):
    m = proj.shape[0]
    assert ATT_TQ == ATT_TK
    nq = seq // ATT_TQ
    const = lambda i, j: (0, 0)
    return pl.pallas_call(
        functools.partial(_attn_kernel, lam_init=lam_init),
        grid=(b, nq),
        in_specs=[
            pl.BlockSpec((ATT_TQ, GROUP_W), lambda i, j: (i * nq + j, COL_Q // GROUP_W)),
            pl.BlockSpec((seq, GROUP_W), lambda i, j: (i, COL_K // GROUP_W)),
            pl.BlockSpec((seq, GROUP_W), lambda i, j: (i, COL_V // GROUP_W)),
            _per_layer((1, GROUP_W), layer),
            _per_layer((1, GROUP_W), layer),
            _per_layer((1, LANES), layer),
            _per_layer((1, LANES), layer),
            _per_layer((1, LANES), layer),
            _per_layer((1, LANES), layer),
            _per_layer((HEAD_DIM, ATT_TQ), layer),
            pl.BlockSpec((GROUP_W, GROUP_W), const),
        ],
        out_specs=pl.BlockSpec((ATT_TQ, GROUP_W), lambda i, j: (i * nq + j, 0)),
        out_shape=jax.ShapeDtypeStruct((m, GROUP_W), BF16),
        scratch_shapes=[
            pltpu.VMEM((N_HEADS, seq, LANES), BF16),
            pltpu.VMEM((seq // ATT_TK, N_HEADS, VT_ROWS, ATT_TK), BF16),
            pltpu.VMEM((N_HEADS, 2 * ATT_TQ, LANES), BF16),
            pltpu.VMEM((N_HEADS, VT_ROWS, 2 * ATT_TQ), F32),
            pltpu.VMEM((ATT_TK, ATT_QB), F32),
            pltpu.VMEM((8, GROUP_W), F32),
            pltpu.VMEM((N_HEADS, seq, LANES), BF16),
        ],
        compiler_params=_cparams(("parallel", "arbitrary")),
        name="diffattn",
    )(proj, proj, proj, qnw, knw, lq1, lk1, lq2, lk2, subw, g32)


FFN_STAGE_TALL = 704


def _ffn_weight_jobs(layer, wo_hbm, wg_hbm, wu_hbm, wd_hbm, wo_b, wg_b, wu_b, wd_b, cols, tall, sem_c, sem_t):
    jobs = []
    used = {id(cols): 0, id(tall): 0}

    def add(src_view, dst_view, stage, sem, stage_view):
        slot = used[id(stage)] % 2
        used[id(stage)] += 1
        view = stage_view(stage.at[slot])
        jobs.append((pltpu.make_async_copy(src_view, view, sem.at[slot]), view, dst_view))

    def rows(src, dst, n_rows):
        for r0 in range(0, n_rows, FFN_STAGE_TALL):
            r = min(FFN_STAGE_TALL, n_rows - r0)
            add(src.at[layer, r0:r0 + r, :], dst.at[r0:r0 + r, :], tall, sem_t, lambda st: st.at[0:r, :])

    rows(wo_hbm, wo_b, D_MODEL)
    n_out = len(jobs)
    for c0 in range(0, D_FF, FFN_CHUNK):
        w = min(FFN_CHUNK, D_FF - c0)
        for src, dst in ((wg_hbm, wg_b), (wu_hbm, wu_b)):
            add(src.at[layer, :, c0:c0 + w], dst.at[:, c0:c0 + w], cols, sem_c, lambda st: st.at[:, 0:w])
    n_before_down = len(jobs)
    rows(wd_hbm, wd_b, D_FF)
    return jobs, {"out": n_out, "gate_up": 2, "down": len(jobs) - n_before_down}


def _ffn_kernel(x_ref, ya_ref, yb_ref, yc_ref, yd_ref, nw_ref, wo_hbm, wg_hbm, wu_hbm, wd_hbm,
                out_ref, act_ref, wo_ref, wg_ref, wu_ref, wd_ref, cols_ref, tall_ref, sem_c, sem_t, *, layer):
    def compute(fetch):
        fetch("out")
        x1 = x_ref[...]
        for i, y_ref in enumerate((ya_ref, yb_ref, yc_ref, yd_ref)):
            x1 = x1 + _dot(y_ref[...], wo_ref[i * GROUP_W:(i + 1) * GROUP_W, :])
        ms = jnp.mean(x1 * x1, axis=-1, keepdims=True)
        h = (x1 * lax.rsqrt(ms + RMS_EPS) * nw_ref[...]).astype(BF16)
        for c0 in range(0, D_FF, FFN_CHUNK):
            c1 = min(c0 + FFN_CHUNK, D_FF)
            fetch("gate_up")
            g = _dot(h, wg_ref[:, c0:c1])
            u = _dot(h, wu_ref[:, c0:c1])
            act_ref[:, c0:c1] = (g * _sigmoid(g) * u).astype(BF16)
        fetch("down")
        out_ref[...] = x1 + _dot(act_ref[...], wd_ref[...])

    @pl.when(pl.program_id(0) == 0)
    def _():
        jobs, count = _ffn_weight_jobs(layer, wo_hbm, wg_hbm, wu_hbm, wd_hbm, wo_ref, wg_ref, wu_ref, wd_ref,
                                       cols_ref, tall_ref, sem_c, sem_t)
        done = [0]
        jobs[0][0].start()

        def fetch(kind):
            for _ in range(count[kind]):
                k = done[0]
                if k + 1 < len(jobs):
                    jobs[k + 1][0].start()
                copy, view, dst = jobs[k]
                copy.wait()
                dst[...] = view[...].astype(BF16)
                done[0] = k + 1

        compute(fetch)
        assert done[0] == len(jobs)

    @pl.when(pl.program_id(0) != 0)
    def _():
        compute(lambda kind: None)


def _ffn(x2, ya, yb, yc, yd, nw, wo, wg, wu, wd, layer, tm):
    m = x2.shape[0]
    row = lambda i: (i, 0)
    hbm = pl.BlockSpec(memory_space=pl.ANY)
    return pl.pallas_call(
        functools.partial(_ffn_kernel, layer=layer),
        grid=(m // tm,),
        in_specs=[
            pl.BlockSpec((tm, D_MODEL), row),
            pl.BlockSpec((tm, GROUP_W), row),
            pl.BlockSpec((tm, GROUP_W), row),
            pl.BlockSpec((tm, GROUP_W), row),
            pl.BlockSpec((tm, GROUP_W), row),
            _per_layer((1, D_MODEL), layer),
            hbm, hbm, hbm, hbm,
        ],
        out_specs=pl.BlockSpec((tm, D_MODEL), row),
        out_shape=jax.ShapeDtypeStruct((m, D_MODEL), F32),
        scratch_shapes=[
            pltpu.VMEM((tm, D_FF), BF16),
            pltpu.VMEM((D_MODEL, D_MODEL), BF16),
            pltpu.VMEM((D_MODEL, D_FF), BF16),
            pltpu.VMEM((D_MODEL, D_FF), BF16),
            pltpu.VMEM((D_FF, D_MODEL), BF16),
            pltpu.VMEM((2, D_MODEL, FFN_CHUNK), F32),
            pltpu.VMEM((2, FFN_STAGE_TALL, D_MODEL), F32),
            pltpu.SemaphoreType.DMA((2,)),
            pltpu.SemaphoreType.DMA((2,)),
        ],
        compiler_params=_cparams(("arbitrary",)),
        name="outproj_ffn",
    )(x2, ya, yb, yc, yd, nw, wo, wg, wu, wd)


def _block_diag_mean(width, group):
    idx = jnp.arange(width) // group
    return jnp.where(idx[:, None] == idx[None, :], 1.0 / group, 0.0).astype(BF16)


def _rows(v, width=None):
    v = v.reshape(v.shape[0], 1, -1).astype(F32)
    return v if width is None else jnp.pad(v, ((0, 0), (0, 0), (0, width - v.shape[2])))


def kernel(x, norm1_w, w_in, gm_norm_w, gm_ws, gm_bs, ssm_conv_w, ssm_conv_b, ssm_dt_bias, ssm_a_log, ssm_d, ssm_norm_w, pool_w, pool_scale, da_q_norm_w, da_k_norm_w, da_lambda_q1, da_lambda_k1, da_lambda_q2, da_lambda_k2, da_subln_w, w_out, norm2_w, ffn_w_gate, ffn_w_up, ffn_w_down):
    b, seq, d = x.shape
    depth = w_in.shape[0]
    assert d == D_MODEL and seq % ATT_TQ == 0 and seq % CHUNK == 0
    m = b * seq
    tm = 512 if m % 512 == 0 else ATT_TQ

    g64 = _block_diag_mean(GROUP_W, HEAD_DIM)
    g128 = _block_diag_mean(GROUP_W, SSM_D_STATE)
    g32 = _block_diag_mean(GROUP_W, DA_QK_DIM)

    shift = _conv_shift_matrices()
    band, fix = _pool_bands()

    assert w_in.shape[1:] == (D_MODEL, D_IN)

    n1, n2 = _rows(norm1_w), _rows(norm2_w)
    dtb, alog = _rows(ssm_dt_bias, DT_W), _rows(ssm_a_log, DT_W)
    bsm = jnp.repeat(jnp.swapaxes(gm_bs, 1, 2), HEAD_DIM, axis=2)
    gmn = _rows(gm_norm_w)
    cw, cb = jnp.swapaxes(ssm_conv_w, 1, 2), _rows(ssm_conv_b)
    dexp, ssn = _rows(jnp.repeat(ssm_d, HEAD_DIM, axis=1)), _rows(ssm_norm_w)
    eye = jnp.eye(len(POOL_WINDOWS), dtype=pool_w.dtype)
    wbd = jnp.einsum('lgab,gh->lgahb', pool_w, eye).reshape(depth, GROUP_W, GROUP_W).astype(BF16)
    psc = _rows(pool_scale)
    qnw = _rows(jnp.tile(da_q_norm_w, (1, GROUP_W // DA_QK_DIM)))
    knw = _rows(jnp.tile(da_k_norm_w, (1, GROUP_W // DA_QK_DIM)))
    lams = [_rows(v, LANES) for v in (da_lambda_q1, da_lambda_k1, da_lambda_q2, da_lambda_k2)]
    subw = jnp.broadcast_to(da_subln_w[:, :, None], (depth, HEAD_DIM, ATT_TQ))

    x2 = x.reshape(m, d)
    for i in range(depth):
        proj, dt = _inproj(x2, n1, w_in, i, dtb, 2 * tm if m % (2 * tm) == 0 else tm)
        ya, yc, yb = _mixers(proj, dt, gm_ws, bsm, gmn, g64, band, fix, wbd, psc,
                             cw, cb, shift, alog, dexp, ssn, g128, i, b, seq)
        lam_init = 0.8 - 0.6 * math.exp(-0.3 * i)
        yd = _attn(proj, qnw, knw, *lams, subw, g32, i, b, seq, lam_init)
        x2 = _ffn(x2, ya, yb, yc, yd, n2, w_out, ffn_w_gate, ffn_w_up, ffn_w_down, i, tm)
    return x2.reshape(b, seq, d)
```

```python
import functools
import math

import jax
import jax.numpy as jnp
from jax import lax
from jax.experimental import pallas as pl
from jax.experimental.pallas import tpu as pltpu

F32 = jnp.float32
BF16 = jnp.bfloat16

D_MODEL = 1024
GROUP_W = 256
CHUNK = 128
HEAD_DIM = 64
N_HEADS = 4
SSM_GROUPS = 2
SSM_D_STATE = 128
SSM_CONV_K = 4
SSM_CONV_DIM = GROUP_W + 2 * SSM_GROUPS * SSM_D_STATE
POOL_WINDOWS = (2, 4, 8, 16)
DA_QK_DIM = 32
D_FF = 2816
RMS_EPS = 1e-6
NEG_BIG = -1e30

LANES = 128
VMEM_LIMIT = 56 * 1024 * 1024

COL_PA = 0
COL_Z = 512
COL_XBC = 768
COL_PC = 1536
COL_Q = 1792
COL_K = 2048
COL_V = 2304
PROJ_W = 2560
DT_W = LANES
D_IN = PROJ_W + N_HEADS
DT_COL = COL_PC
IN_CHUNK = 256
FFN_CHUNK = 512
SSD_CPS = 4

ATT_TQ = 512
ATT_TK = 512
ATT_QB = 512
ATT_BOUND_MARGIN = 1.01
ATT_ONE_PASS_BOUND = 40.0
VT_ROWS = 80


def _cparams(sem):
    return pltpu.CompilerParams(dimension_semantics=sem, vmem_limit_bytes=VMEM_LIMIT)


def _sigmoid(x):
    return 1.0 / (1.0 + jnp.exp(-x))


def _dot(a, b):
    return jnp.dot(a, b, preferred_element_type=F32)


def _dot_nt(a, b):
    return lax.dot_general(a, b, (((1,), (1,)), ((), ())), preferred_element_type=F32)


def _softplus(t):
    return jnp.maximum(t, 0.0) + jnp.log(1.0 + jnp.exp(-jnp.abs(t)))


def _inproj_kernel(x_ref, nw_ref, w_ref, dtb_ref, proj_ref, dt_ref, wb_ref):
    @pl.when(pl.program_id(0) == 0)
    def _():
        for n0 in range(0, DT_COL, 2 * IN_CHUNK):
            wb_ref[:, n0:n0 + 2 * IN_CHUNK] = w_ref[:, n0:n0 + 2 * IN_CHUNK].astype(BF16)
        tail = w_ref[:, DT_COL:D_IN]
        wb_ref[:, DT_COL:PROJ_W] = tail[:, N_HEADS:].astype(BF16)
        wb_ref[:, PROJ_W:PROJ_W + DT_W] = tail[:, 0:DT_W].astype(BF16)

    x = x_ref[...]
    ms = jnp.mean(x * x, axis=-1, keepdims=True)
    h = (x * lax.rsqrt(ms + RMS_EPS) * nw_ref[...]).astype(BF16)
    for n0 in range(0, PROJ_W, IN_CHUNK):
        y = _dot(h, wb_ref[:, n0:n0 + IN_CHUNK])
        if n0 < COL_Z:
            y = jax.nn.gelu(y, approximate=True)
        elif n0 < COL_XBC:
            y = y * _sigmoid(y)
        proj_ref[:, n0:n0 + IN_CHUNK] = y.astype(BF16)
    dt_ref[...] = _softplus(_dot(h, wb_ref[:, PROJ_W:PROJ_W + DT_W]) + dtb_ref[...])


def _per_layer(shape, layer):
    return pl.BlockSpec((None,) + tuple(shape), lambda *_: (layer,) + (0,) * len(shape))


def _inproj(x2, nw, w_in, layer, dtb, tm):
    m = x2.shape[0]
    return pl.pallas_call(
        _inproj_kernel,
        grid=(m // tm,),
        in_specs=[
            pl.BlockSpec((tm, D_MODEL), lambda i: (i, 0)),
            _per_layer((1, D_MODEL), layer),
            pl.BlockSpec((None, D_MODEL, D_IN), lambda i: (layer, 0, 0), pipeline_mode=pl.Buffered(1)),
            _per_layer((1, DT_W), layer),
        ],
        out_specs=[
            pl.BlockSpec((tm, PROJ_W), lambda i: (i, 0)),
            pl.BlockSpec((tm, DT_W), lambda i: (i, 0)),
        ],
        out_shape=[
            jax.ShapeDtypeStruct((m, PROJ_W), BF16),
            jax.ShapeDtypeStruct((m, DT_W), F32),
        ],
        scratch_shapes=[pltpu.VMEM((D_MODEL, PROJ_W + DT_W), BF16)],
        compiler_params=_cparams(("arbitrary",)),
        name="inproj",
    )(x2, nw, w_in, dtb)


def _head_id(shape, width):
    lane = lax.broadcasted_iota(jnp.int32, shape, 1)
    return lax.shift_right_logical(lane, int(math.log2(width)))


def _gmlp_kernel(pa_ref, ws_ref, bsm_ref, nw_ref, g_ref, out_ref):
    t = pa_ref.shape[0]
    hact = pa_ref[...].astype(F32)
    u = hact[:, :GROUP_W]
    v = hact[:, GROUP_W:]
    ms = _dot((v * v).astype(BF16), g_ref[...])
    vn = (v * lax.rsqrt(ms + RMS_EPS) * nw_ref[...]).astype(BF16)
    row = lax.broadcasted_iota(jnp.int32, (CHUNK, CHUNK), 0)
    col = lax.broadcasted_iota(jnp.int32, (CHUNK, CHUNK), 1)
    wcat = jnp.concatenate(
        [jnp.where(row >= col, ws_ref[h], 0.0) for h in range(N_HEADS)], axis=1).astype(BF16)
    hid = _head_id((CHUNK, GROUP_W), HEAD_DIM)
    for c in range(t // CHUNK):
        vc = vn[c * CHUNK:(c + 1) * CHUNK]
        vstack = jnp.concatenate(
            [jnp.where(hid == h, vc, jnp.zeros_like(vc)) for h in range(N_HEADS)], axis=0)
        s = _dot(wcat, vstack) + bsm_ref[...]
        out_ref[c * CHUNK:(c + 1) * CHUNK, :] = (u[c * CHUNK:(c + 1) * CHUNK] * s).astype(BF16)


def _split3(a):
    a1 = a.astype(BF16)
    r1 = a - a1.astype(F32)
    a2 = r1.astype(BF16)
    r2 = r1 - a2.astype(F32)
    return a1, a2, r2.astype(BF16)


def _ssd_phases(z_ref, xbc_ref, halo_ref, dt_ref, cw_ref, cb_ref, shift_ref, alog_ref, dexp_ref, nw_ref,
                g_ref, out_ref, st_ref, between_phases=None):
    c = pl.program_id(1)

    @pl.when(c == 0)
    def _():
        st_ref[...] = jnp.zeros_like(st_ref)

    row = lax.broadcasted_iota(jnp.int32, (CHUNK, CHUNK), 0)
    col = lax.broadcasted_iota(jnp.int32, (CHUNK, CHUNK), 1)
    causal = row >= col
    ltri = jnp.where(causal, 1.0, 0.0).astype(BF16)
    hid = _head_id((CHUNK, GROUP_W), HEAD_DIM)
    neg_a = -jnp.exp(alog_ref[...])

    def expand(c4):
        return jnp.where(hid == 0, c4[:, 0:1],
                         jnp.where(hid == 1, c4[:, 1:2],
                                   jnp.where(hid == 2, c4[:, 2:3], c4[:, 3:4])))

    chunks = range(SSD_CPS)
    rows = [slice(ck * CHUNK, (ck + 1) * CHUNK) for ck in chunks]
    groups = [slice(g * SSM_D_STATE, (g + 1) * SSM_D_STATE) for g in range(SSM_GROUPS)]

    xs, bm, cm = [], [], []
    halo = jnp.where(c > 0, halo_ref[...], jnp.zeros_like(halo_ref))
    for ck in chunks:
        xcur = xbc_ref[rows[ck], :]
        acc = cb_ref[...] + cw_ref[SSM_CONV_K - 1:SSM_CONV_K, :] * xcur.astype(F32)
        for j in range(1, SSM_CONV_K):
            k = SSM_CONV_K - 1 - j
            if ck == 0:
                shifted = (_dot(shift_ref[j - 1, :, 0:CHUNK], halo)
                           + _dot(shift_ref[j - 1, :, CHUNK:2 * CHUNK], xcur))
            else:
                shifted = _dot(shift_ref[j - 1], xbc_ref[(ck - 1) * CHUNK:(ck + 1) * CHUNK, :])
            acc = acc + cw_ref[k:k + 1, :] * shifted
        xc = acc * _sigmoid(acc)
        xs.append(xc[:, :GROUP_W])
        bm.append(xc[:, GROUP_W:2 * GROUP_W])
        cm.append(xc[:, 2 * GROUP_W:].astype(BF16))
    yield

    acs, acs_t = [], []
    for ck in chunks:
        a1, a2, a3 = _split3(dt_ref[rows[ck], :] * neg_a)
        acs.append(_dot(ltri, a1) + _dot(ltri, a2) + _dot(ltri, a3))
        acs_t.append(acs[ck].T)
    yield

    if between_phases is not None:
        yield from between_phases()

    xdt, eacs_e, mcat, snew = [], [], [], []
    for ck in chunks:
        xdt.append(xs[ck] * expand(dt_ref[rows[ck], :]))
        eacs_e.append(expand(jnp.exp(acs[ck])))
        dte_e = expand(jnp.exp(acs[ck][CHUNK - 1:CHUNK, :] - acs[ck]))
        ms = []
        for g in range(SSM_GROUPS):
            cb = _dot_nt(cm[ck][:, groups[g]], bm[ck][:, groups[g]].astype(BF16))
            for hh in range(N_HEADS // SSM_GROUPS):
                h = g * (N_HEADS // SSM_GROUPS) + hh
                seg = acs[ck][:, h:h + 1] - acs_t[ck][h:h + 1, :]
                ms.append((cb * jnp.exp(jnp.where(causal, seg, NEG_BIG))).astype(BF16))
        mcat.append(jnp.concatenate(ms, axis=1))
        w = (xdt[ck] * dte_e).astype(BF16)
        snew.append([_dot(bm[ck][:, groups[g]].T.astype(BF16), w[:, groups[g]])
                     for g in range(SSM_GROUPS)])
    yield

    state = [st_ref[g] for g in range(SSM_GROUPS)]
    entering = []
    for ck in chunks:
        entering.append([st.astype(BF16) for st in state])
        cdl = eacs_e[ck][CHUNK - 1:CHUNK, :]
        state = [state[g] * cdl[:, groups[g]] + snew[ck][g] for g in range(SSM_GROUPS)]
    for g in range(SSM_GROUPS):
        st_ref[g] = state[g]

    ys = []
    for ck in chunks:
        xdt_b = xdt[ck].astype(BF16)
        xstack = jnp.concatenate(
            [jnp.where(hid == h, xdt_b, jnp.zeros_like(xdt_b)) for h in range(N_HEADS)], axis=0)
        yoff = jnp.concatenate([_dot(cm[ck][:, groups[g]], entering[ck][g]) for g in range(SSM_GROUPS)], axis=1)
        y = _dot(mcat[ck], xstack) + yoff * eacs_e[ck] + dexp_ref[...] * xs[ck]
        ys.append(y * z_ref[rows[ck], :].astype(F32))
    yield

    for ck in chunks:
        msq = _dot((ys[ck] * ys[ck]).astype(BF16), g_ref[...])
        out_ref[rows[ck], :] = (ys[ck] * lax.rsqrt(msq + RMS_EPS) * nw_ref[...]).astype(BF16)


def _conv_shift_matrices():
    r = jnp.arange(CHUNK)[:, None]
    col = jnp.arange(2 * CHUNK)[None, :]
    return jnp.stack([(col == CHUNK + r - j) for j in range(1, SSM_CONV_K)]).astype(BF16)


def _pool_bands():
    r = jnp.arange(CHUNK)[:, None]
    col = jnp.arange(2 * CHUNK)[None, :]
    back = CHUNK + r - col
    bands = [jnp.where((back >= 0) & (back < w), 1.0 / w, 0.0) for w in POOL_WINDOWS]
    bandcat = jnp.concatenate(bands, axis=1).astype(BF16)
    t = jnp.arange(CHUNK, dtype=F32)[:, None] + 1.0
    win = jnp.repeat(jnp.asarray(POOL_WINDOWS, F32), HEAD_DIM)[None, :]
    fix = win / jnp.minimum(win, t)
    return bandcat, fix


def _pool_chunks(pc_ref, pch_ref, band_ref, fix_ref, wbd_ref, scale_ref, out_ref, first_step):
    hid = _head_id((CHUNK, GROUP_W), HEAD_DIM)
    prev = jnp.where(first_step, jnp.zeros_like(pch_ref), pch_ref[...])
    for ck in range(pc_ref.shape[0] // CHUNK):
        cur = pc_ref[ck * CHUNK:(ck + 1) * CHUNK, :]
        parts = []
        for g in range(len(POOL_WINDOWS)):
            parts += [jnp.where(hid == g, prev, jnp.zeros_like(prev)),
                      jnp.where(hid == g, cur, jnp.zeros_like(cur))]
        mean = _dot(band_ref[...], jnp.concatenate(parts, axis=0))
        if ck == 0:
            mean = jnp.where(first_step, mean * fix_ref[...], mean)
        p = mean - cur.astype(F32)
        y = _dot(p.astype(BF16), wbd_ref[...]) * scale_ref[...]
        out_ref[ck * CHUNK:(ck + 1) * CHUNK, :] = y.astype(BF16)
        prev = cur


N_GMLP_IN, N_POOL_IN, N_SSD_IN = 5, 6, 11
N_MIXER_IN = N_GMLP_IN + N_POOL_IN + N_SSD_IN


def _mixers_phases(refs, ya_ref, yc_ref, yb_ref, st_ref):
    gm = refs[:N_GMLP_IN]
    po = refs[N_GMLP_IN:N_GMLP_IN + N_POOL_IN]
    sd = refs[N_GMLP_IN + N_POOL_IN:N_MIXER_IN]
    first_step = pl.program_id(1) == 0

    def others():
        _gmlp_kernel(*gm, ya_ref)
        yield
        _pool_chunks(*po, yc_ref, first_step)
        yield

    yield from _ssd_phases(*sd, yb_ref, st_ref, between_phases=others)


def _mixer_specs(layer, nc):
    ts = SSD_CPS * CHUNK
    const = lambda i, c: (0, 0)
    blk = lambda width, col: pl.BlockSpec((ts, width), lambda i, c: (i * nc + c, col // width))
    halo = lambda width, col: pl.BlockSpec(
        (CHUNK, width), lambda i, c: (jnp.maximum((i * nc + c) * SSD_CPS - 1, 0), col // width))
    gmlp_in = [blk(2 * GROUP_W, COL_PA), _per_layer((N_HEADS, CHUNK, CHUNK), layer),
               _per_layer((CHUNK, GROUP_W), layer), _per_layer((1, GROUP_W), layer),
               pl.BlockSpec((GROUP_W, GROUP_W), const)]
    pool_in = [blk(GROUP_W, COL_PC), halo(GROUP_W, COL_PC),
               pl.BlockSpec((CHUNK, len(POOL_WINDOWS) * 2 * CHUNK), const), pl.BlockSpec((CHUNK, GROUP_W), const),
               _per_layer((GROUP_W, GROUP_W), layer), _per_layer((1, GROUP_W), layer)]
    ssd_in = [blk(GROUP_W, COL_Z), blk(SSM_CONV_DIM, COL_XBC), halo(SSM_CONV_DIM, COL_XBC),
              pl.BlockSpec((ts, DT_W), lambda i, c: (i * nc + c, 0)),
              _per_layer((SSM_CONV_K, SSM_CONV_DIM), layer), _per_layer((1, SSM_CONV_DIM), layer),
              pl.BlockSpec((SSM_CONV_K - 1, CHUNK, 2 * CHUNK), lambda i, c: (0, 0, 0)),
              _per_layer((1, DT_W), layer), _per_layer((1, GROUP_W), layer), _per_layer((1, GROUP_W), layer),
              pl.BlockSpec((GROUP_W, GROUP_W), const)]
    assert (len(gmlp_in), len(pool_in), len(ssd_in)) == (N_GMLP_IN, N_POOL_IN, N_SSD_IN)
    return gmlp_in + pool_in + ssd_in


def _alibi_slope(h):
    return 2.0 ** (-8.0 * (h + 1) / N_HEADS)


def _head_slot(x, h):
    base = x[:, (h // 2) * LANES:(h // 2 + 1) * LANES]
    return pltpu.roll(base, HEAD_DIM, 1) if h % 2 else base


N_ATTN_IN = 11


def _attn_mixers_kernel(*refs, lam_init):
    attn_in = refs[:N_ATTN_IN]
    mixer_in = refs[N_ATTN_IN:N_ATTN_IN + N_MIXER_IN]
    out_ref, ya_ref, yc_ref, yb_ref = refs[N_ATTN_IN + N_MIXER_IN:N_ATTN_IN + N_MIXER_IN + 4]
    scratch = refs[N_ATTN_IN + N_MIXER_IN + 4:]
    st_ref = scratch[-1]
    _attn_kernel(*attn_in, out_ref, *scratch[:-1], lam_init=lam_init,
                 mixers=lambda: _mixers_phases(mixer_in, ya_ref, yc_ref, yb_ref, st_ref))


def _attn_kernel(q_ref, k_ref, v_ref, qnw_ref, knw_ref, lq1_ref, lk1_ref, lq2_ref, lk2_ref, subw_ref,
                 g_ref, out_ref, ka_ref, vt_ref, qs_ref, acc_ref, s0_ref, kmax_ref, qa_ref, *, lam_init,
                 mixers):
    qi = pl.program_id(1)
    tq, tk = ATT_TQ, ATT_TK
    seq = k_ref.shape[0]

    def aug_lanes(pos, slope, key_side):
        lane = lax.broadcasted_iota(jnp.int32, pos.shape, 1)
        hi = (slope * LANES) * lax.shift_right_logical(pos, 7).astype(F32)
        lo = slope * (pos & (LANES - 1)).astype(F32)
        one = jnp.ones_like(hi)
        c = (one, one, hi, lo) if key_side else (-hi, -lo, one, one)
        last = jnp.where(lane == 68, 1.0, 0.0) if key_side else 0.0
        return jnp.where(lane == 64, c[0], jnp.where(lane == 65, c[1],
                         jnp.where(lane == 66, c[2], jnp.where(lane == 67, c[3], last))))

    @pl.when(qi == 0)
    def _():
        k = k_ref[...].astype(F32)
        ms = _dot((k * k).astype(BF16), g_ref[...])
        kn = k * lax.rsqrt(ms + RMS_EPS) * knw_ref[...]
        ksq = _dot((kn * kn).astype(BF16), g_ref[...]) * DA_QK_DIM
        kmax_ref[...] = jnp.broadcast_to(jnp.max(ksq, axis=0, keepdims=True), kmax_ref.shape)
        pos = lax.broadcasted_iota(jnp.int32, (seq, LANES), 0)
        lane = lax.broadcasted_iota(jnp.int32, (seq, LANES), 1)
        for h in range(N_HEADS):
            aug = aug_lanes(pos, _alibi_slope(h), True)
            ka_ref[h] = jnp.where(lane < HEAD_DIM, _head_slot(kn, h), aug).astype(BF16)
            qa_ref[h] = aug_lanes(pos, _alibi_slope(h), False).astype(BF16)
        tail = (lax.broadcasted_iota(jnp.int32, (VT_ROWS - HEAD_DIM, tk), 0) == 0).astype(BF16)
        for j in range(seq // tk):
            vt = v_ref[j * tk:(j + 1) * tk, :].astype(F32).T.astype(BF16)
            for h in range(N_HEADS):
                vt_ref[j, h, 0:HEAD_DIM, :] = vt[h * HEAD_DIM:(h + 1) * HEAD_DIM, :]
                vt_ref[j, h, HEAD_DIM:VT_ROWS, :] = tail

    q = q_ref[...].astype(F32)
    ms = _dot((q * q).astype(BF16), g_ref[...])
    qn = q * lax.rsqrt(ms + RMS_EPS) * (qnw_ref[...] * (DA_QK_DIM ** -0.5))
    qsq = _dot((qn * qn).astype(BF16), g_ref[...]) * DA_QK_DIM
    b2 = qsq * kmax_ref[0:1, :]
    bound = b2 * lax.rsqrt(b2 + 1e-30) * ATT_BOUND_MARGIN
    one_pass = jnp.max(bound) <= ATT_ONE_PASS_BOUND
    lane = lax.broadcasted_iota(jnp.int32, (1, LANES), 1)
    q0 = pl.multiple_of(qi * tq, tq)
    for h in range(N_HEADS):
        base = _head_slot(qn, h)
        bnd = _head_slot(bound, h)
        qa = qa_ref[h, pl.ds(q0, tq), :]
        for comp in range(2):
            own = (lane >= comp * DA_QK_DIM) & (lane < (comp + 1) * DA_QK_DIM)
            y = jnp.where(own, base, 0.0)
            y = jnp.where(lane == 68, -bnd[:, comp * DA_QK_DIM:comp * DA_QK_DIM + 1], y)
            qs_ref[h, comp * tq:(comp + 1) * tq, :] = y.astype(BF16) + qa
    acc_ref[...] = jnp.zeros_like(acc_ref)

    qb = ATT_QB
    items = [(h, n) for h in range(N_HEADS) for n in range(2 * tq // qb)]

    def scores(item, j):
        h, n = item
        k0 = pl.multiple_of(j * tk, tk)
        return _dot_nt(ka_ref[h, pl.ds(k0, tk), :], qs_ref[h, n * qb:(n + 1) * qb, :])

    def tile(j, ms_, masked):
        new_m = []
        s_next = s0_ref[...]
        for i, (h, n) in enumerate(items):
            cols = slice(n * qb, (n + 1) * qb)
            s = s_next
            if i + 1 < len(items):
                s_next = scores(items[i + 1], j)
            elif not masked:
                s0_ref[...] = scores(items[0], j + 1)
            if masked:
                kk = lax.broadcasted_iota(jnp.int32, (tk, qb), 0)
                qq = (lax.broadcasted_iota(jnp.int32, (tk, qb), 1) + n * qb) & (tq - 1)
                s = jnp.where(kk <= qq, s, NEG_BIG)
            m_old = ms_[i]
            m_new = jnp.maximum(m_old, jnp.max(s, axis=0, keepdims=True))
            alpha = jnp.exp(m_old - m_new)
            p = jnp.exp(s - m_new)
            new_m.append(m_new)
            acc_ref[h, :, cols] = acc_ref[h, :, cols] * alpha + _dot(vt_ref[j, h], p.astype(BF16))
        return tuple(new_m)

    def tile_one_pass(j, masked, filler=None):
        s_next = s0_ref[...]
        for i, (h, n) in enumerate(items):
            cols = slice(n * qb, (n + 1) * qb)
            s = s_next
            if i + 1 < len(items):
                s_next = scores(items[i + 1], j)
            elif not masked:
                s0_ref[...] = scores(items[0], j + 1)
            if masked:
                kk = lax.broadcasted_iota(jnp.int32, (tk, qb), 0)
                qq = (lax.broadcasted_iota(jnp.int32, (tk, qb), 1) + n * qb) & (tq - 1)
                s = jnp.where(kk <= qq, s, NEG_BIG)
            acc_ref[h, :, cols] += _dot(vt_ref[j, h], jnp.exp(s).astype(BF16))
            if filler is not None:
                next(filler, None)

    s0_ref[...] = scores(items[0], 0)

    @pl.when(one_pass)
    def _():
        def body(j, carry):
            tile_one_pass(j, False)
            return carry
        lax.fori_loop(0, qi, body, 0)
        filler = mixers()
        tile_one_pass(qi, True, filler)
        for _ in filler:
            pass

    @pl.when(jnp.logical_not(one_pass))
    def _():
        m0 = tuple(jnp.full((1, qb), NEG_BIG, F32) for _ in items)
        m1 = lax.fori_loop(0, qi, lambda j, c: tile(j, c, False), m0)
        tile(qi, m1, True)
        for _ in mixers():
            pass

    def lane_sum(x):
        return jnp.broadcast_to(jnp.sum(x, axis=1, keepdims=True), x.shape)

    lam = (jnp.exp(lane_sum(lq1_ref[...] * lk1_ref[...]))
           - jnp.exp(lane_sum(lq2_ref[...] * lk2_ref[...])) + lam_init)
    lam = jnp.concatenate([lam] * (tq // LANES), axis=1)
    outs = []
    for h in range(N_HEADS):
        o = acc_ref[h, 0:HEAD_DIM, :] / acc_ref[h, HEAD_DIM:HEAD_DIM + 1, :]
        oh = o[:, 0:tq] - lam * o[:, tq:2 * tq]
        ms = jnp.mean(oh * oh, axis=0, keepdims=True)
        outs.append(oh * lax.rsqrt(ms + RMS_EPS) * (subw_ref[...] * (1.0 - lam_init)))
    out_ref[...] = jnp.concatenate(outs, axis=0).T.astype(BF16)


def _attn_mixers(proj, attn_args, mixer_args, layer, b, seq, lam_init):
    m = proj.shape[0]
    assert ATT_TQ == ATT_TK == SSD_CPS * CHUNK and seq % ATT_TQ == 0
    nq = seq // ATT_TQ
    const = lambda i, j: (0, 0)
    out = pl.BlockSpec((ATT_TQ, GROUP_W), lambda i, j: (i * nq + j, 0))
    return pl.pallas_call(
        functools.partial(_attn_mixers_kernel, lam_init=lam_init),
        grid=(b, nq),
        in_specs=[
            pl.BlockSpec((ATT_TQ, GROUP_W), lambda i, j: (i * nq + j, COL_Q // GROUP_W)),
            pl.BlockSpec((seq, GROUP_W), lambda i, j: (i, COL_K // GROUP_W)),
            pl.BlockSpec((seq, GROUP_W), lambda i, j: (i, COL_V // GROUP_W)),
            _per_layer((1, GROUP_W), layer),
            _per_layer((1, GROUP_W), layer),
            _per_layer((1, LANES), layer),
            _per_layer((1, LANES), layer),
            _per_layer((1, LANES), layer),
            _per_layer((1, LANES), layer),
            _per_layer((HEAD_DIM, ATT_TQ), layer),
            pl.BlockSpec((GROUP_W, GROUP_W), const),
        ] + _mixer_specs(layer, nq),
        out_specs=[out] * 4,
        out_shape=[jax.ShapeDtypeStruct((m, GROUP_W), BF16)] * 4,
        scratch_shapes=[
            pltpu.VMEM((N_HEADS, seq, LANES), BF16),
            pltpu.VMEM((seq // ATT_TK, N_HEADS, VT_ROWS, ATT_TK), BF16),
            pltpu.VMEM((N_HEADS, 2 * ATT_TQ, LANES), BF16),
            pltpu.VMEM((N_HEADS, VT_ROWS, 2 * ATT_TQ), F32),
            pltpu.VMEM((ATT_TK, ATT_QB), F32),
            pltpu.VMEM((8, GROUP_W), F32),
            pltpu.VMEM((N_HEADS, seq, LANES), BF16),
            pltpu.VMEM((SSM_GROUPS, SSM_D_STATE, LANES), F32),
        ],
        compiler_params=_cparams(("parallel", "arbitrary")),
        name="attn_mixers",
    )(proj, proj, proj, *attn_args, *mixer_args)


FFN_STAGE_WIDE = 256
FFN_STAGE_TALL = 704


def _ffn_weight_jobs(layer, wo_hbm, wg_hbm, wu_hbm, wd_hbm, wo_b, wg_b, wu_b, wd_b, wide, tall, sem_w, sem_t):
    jobs = []

    def add(src, dst, stage, sem, rows, n_rows):
        for k, r0 in enumerate(range(0, n_rows, rows)):
            r = min(rows, n_rows - r0)
            slot = k % 2
            view = stage.at[slot, 0:r, :]
            copy = pltpu.make_async_copy(src.at[layer, r0:r0 + r, :], view, sem.at[slot])
            jobs.append((copy, view, dst.at[r0:r0 + r, :]))

    add(wg_hbm, wg_b, wide, sem_w, FFN_STAGE_WIDE, D_MODEL)
    add(wu_hbm, wu_b, wide, sem_w, FFN_STAGE_WIDE, D_MODEL)
    add(wd_hbm, wd_b, tall, sem_t, FFN_STAGE_TALL, D_FF)
    add(wo_hbm, wo_b, tall, sem_t, FFN_STAGE_TALL, D_MODEL)
    return jobs


def _ffn_kernel(x_ref, ya_ref, yb_ref, yc_ref, yd_ref, nw_ref, wo_hbm, wg_hbm, wu_hbm, wd_hbm,
                out_ref, act_ref, wo_ref, wg_ref, wu_ref, wd_ref, wide_ref, tall_ref, sem_w, sem_t, *, layer):
    @pl.when(pl.program_id(0) == 0)
    def _():
        jobs = _ffn_weight_jobs(layer, wo_hbm, wg_hbm, wu_hbm, wd_hbm, wo_ref, wg_ref, wu_ref, wd_ref,
                                wide_ref, tall_ref, sem_w, sem_t)
        jobs[0][0].start()
        for k, (copy, view, dst) in enumerate(jobs):
            if k + 1 < len(jobs):
                jobs[k + 1][0].start()
            copy.wait()
            dst[...] = view[...].astype(BF16)

    x1 = x_ref[...]
    for i, y_ref in enumerate((ya_ref, yb_ref, yc_ref, yd_ref)):
        x1 = x1 + _dot(y_ref[...], wo_ref[i * GROUP_W:(i + 1) * GROUP_W, :])
    ms = jnp.mean(x1 * x1, axis=-1, keepdims=True)
    h = (x1 * lax.rsqrt(ms + RMS_EPS) * nw_ref[...]).astype(BF16)
    for c0 in range(0, D_FF, FFN_CHUNK):
        c1 = min(c0 + FFN_CHUNK, D_FF)
        g = _dot(h, wg_ref[:, c0:c1])
        u = _dot(h, wu_ref[:, c0:c1])
        act_ref[:, c0:c1] = (g * _sigmoid(g) * u).astype(BF16)
    out_ref[...] = x1 + _dot(act_ref[...], wd_ref[...])


def _ffn(x2, ya, yb, yc, yd, nw, wo, wg, wu, wd, layer, tm):
    m = x2.shape[0]
    row = lambda i: (i, 0)
    hbm = pl.BlockSpec(memory_space=pl.ANY)
    return pl.pallas_call(
        functools.partial(_ffn_kernel, layer=layer),
        grid=(m // tm,),
        in_specs=[
            pl.BlockSpec((tm, D_MODEL), row),
            pl.BlockSpec((tm, GROUP_W), row),
            pl.BlockSpec((tm, GROUP_W), row),
            pl.BlockSpec((tm, GROUP_W), row),
            pl.BlockSpec((tm, GROUP_W), row),
            _per_layer((1, D_MODEL), layer),
            hbm, hbm, hbm, hbm,
        ],
        out_specs=pl.BlockSpec((tm, D_MODEL), row),
        out_shape=jax.ShapeDtypeStruct((m, D_MODEL), F32),
        scratch_shapes=[
            pltpu.VMEM((tm, D_FF), BF16),
            pltpu.VMEM((D_MODEL, D_MODEL), BF16),
            pltpu.VMEM((D_MODEL, D_FF), BF16),
            pltpu.VMEM((D_MODEL, D_FF), BF16),
            pltpu.VMEM((D_FF, D_MODEL), BF16),
            pltpu.VMEM((2, FFN_STAGE_WIDE, D_FF), F32),
            pltpu.VMEM((2, FFN_STAGE_TALL, D_MODEL), F32),
            pltpu.SemaphoreType.DMA((2,)),
            pltpu.SemaphoreType.DMA((2,)),
        ],
        compiler_params=_cparams(("arbitrary",)),
        name="outproj_ffn",
    )(x2, ya, yb, yc, yd, nw, wo, wg, wu, wd)


def _block_diag_mean(width, group):
    idx = jnp.arange(width) // group
    return jnp.where(idx[:, None] == idx[None, :], 1.0 / group, 0.0).astype(BF16)


def _rows(v, width=None):
    v = v.reshape(v.shape[0], 1, -1).astype(F32)
    return v if width is None else jnp.pad(v, ((0, 0), (0, 0), (0, width - v.shape[2])))


def kernel(x, norm1_w, w_in, gm_norm_w, gm_ws, gm_bs, ssm_conv_w, ssm_conv_b, ssm_dt_bias, ssm_a_log, ssm_d, ssm_norm_w, pool_w, pool_scale, da_q_norm_w, da_k_norm_w, da_lambda_q1, da_lambda_k1, da_lambda_q2, da_lambda_k2, da_subln_w, w_out, norm2_w, ffn_w_gate, ffn_w_up, ffn_w_down):
    b, seq, d = x.shape
    depth = w_in.shape[0]
    assert d == D_MODEL and seq % ATT_TQ == 0 and seq % CHUNK == 0
    m = b * seq
    tm = 512 if m % 512 == 0 else ATT_TQ

    g64 = _block_diag_mean(GROUP_W, HEAD_DIM)
    g128 = _block_diag_mean(GROUP_W, SSM_D_STATE)
    g32 = _block_diag_mean(GROUP_W, DA_QK_DIM)

    shift = _conv_shift_matrices()
    band, fix = _pool_bands()

    assert w_in.shape[1:] == (D_MODEL, D_IN)

    n1, n2 = _rows(norm1_w), _rows(norm2_w)
    dtb, alog = _rows(ssm_dt_bias, DT_W), _rows(ssm_a_log, DT_W)
    bsm = jnp.repeat(jnp.swapaxes(gm_bs, 1, 2), HEAD_DIM, axis=2)
    gmn = _rows(gm_norm_w)
    cw, cb = jnp.swapaxes(ssm_conv_w, 1, 2), _rows(ssm_conv_b)
    dexp, ssn = _rows(jnp.repeat(ssm_d, HEAD_DIM, axis=1)), _rows(ssm_norm_w)
    eye = jnp.eye(len(POOL_WINDOWS), dtype=pool_w.dtype)
    wbd = jnp.einsum('lgab,gh->lgahb', pool_w, eye).reshape(depth, GROUP_W, GROUP_W).astype(BF16)
    psc = _rows(pool_scale)
    qnw = _rows(jnp.tile(da_q_norm_w, (1, GROUP_W // DA_QK_DIM)))
    knw = _rows(jnp.tile(da_k_norm_w, (1, GROUP_W // DA_QK_DIM)))
    lams = [_rows(v, LANES) for v in (da_lambda_q1, da_lambda_k1, da_lambda_q2, da_lambda_k2)]
    subw = jnp.broadcast_to(da_subln_w[:, :, None], (depth, HEAD_DIM, ATT_TQ))

    x2 = x.reshape(m, d)
    for i in range(depth):
        proj, dt = _inproj(x2, n1, w_in, i, dtb, tm)
        lam_init = 0.8 - 0.6 * math.exp(-0.3 * i)
        attn_args = (qnw, knw, *lams, subw, g32)
        mixer_args = (proj, gm_ws, bsm, gmn, g64,
                      proj, proj, band, fix, wbd, psc,
                      proj, proj, proj, dt, cw, cb, shift, alog, dexp, ssn, g128)
        yd, ya, yc, yb = _attn_mixers(proj, attn_args, mixer_args, i, b, seq, lam_init)
        x2 = _ffn(x2, ya, yb, yc, yd, n2, w_out, ffn_w_gate, ffn_w_up, ffn_w_down, i, tm)
    return x2.reshape(b, seq, d)
```

```python
import functools
import math

import jax
import jax.numpy as jnp
from jax import lax
from jax.experimental import pallas as pl
from jax.experimental.pallas import tpu as pltpu

F32 = jnp.float32
BF16 = jnp.bfloat16

D_MODEL = 1024
GROUP_W = 256
CHUNK = 128
HEAD_DIM = 64
N_HEADS = 4
SSM_GROUPS = 2
SSM_D_STATE = 128
SSM_CONV_K = 4
SSM_CONV_DIM = GROUP_W + 2 * SSM_GROUPS * SSM_D_STATE
POOL_WINDOWS = (2, 4, 8, 16)
DA_QK_DIM = 32
D_FF = 2816
RMS_EPS = 1e-6
NEG_BIG = -1e30

LANES = 128
VMEM_LIMIT = 56 * 1024 * 1024

COL_PA = 0
COL_Z = 512
COL_XBC = 768
COL_PC = 1536
COL_Q = 1792
COL_K = 2048
COL_V = 2304
PROJ_W = 2560
DT_W = LANES
D_IN = PROJ_W + N_HEADS
DT_COL = COL_PC
IN_CHUNK = 256
FFN_CHUNK = 512
SSD_CPS = 4

ATT_TQ = 512
ATT_TK = 512
ATT_QB = 512
ATT_BOUND_MARGIN = 1.01
ATT_ONE_PASS_BOUND = 40.0
VT_ROWS = 80
ATT_LANE_ALIBI = HEAD_DIM
ATT_LANE_BOUND = HEAD_DIM + 4
TINY = 1e-30


def _cparams(sem):
    return pltpu.CompilerParams(dimension_semantics=sem, vmem_limit_bytes=VMEM_LIMIT)


def _sigmoid(x):
    return 1.0 / (1.0 + jnp.exp(-x))


def _dot(a, b):
    return jnp.dot(a, b, preferred_element_type=F32)


def _dot_nt(a, b):
    return lax.dot_general(a, b, (((1,), (1,)), ((), ())), preferred_element_type=F32)


def _softplus(t):
    return jnp.maximum(t, 0.0) + jnp.log(1.0 + jnp.exp(-jnp.abs(t)))


def _inproj_kernel(x_ref, nw_ref, w_ref, dtb_ref, proj_ref, dt_ref, wb_ref):
    @pl.when(pl.program_id(0) == 0)
    def _():
        for n0 in range(0, DT_COL, 2 * IN_CHUNK):
            wb_ref[:, n0:n0 + 2 * IN_CHUNK] = w_ref[:, n0:n0 + 2 * IN_CHUNK].astype(BF16)
        tail = w_ref[:, DT_COL:D_IN]
        wb_ref[:, DT_COL:PROJ_W] = tail[:, N_HEADS:].astype(BF16)
        wb_ref[:, PROJ_W:PROJ_W + DT_W] = tail[:, 0:DT_W].astype(BF16)

    x = x_ref[...]
    ms = jnp.mean(x * x, axis=-1, keepdims=True)
    h = (x * lax.rsqrt(ms + RMS_EPS) * nw_ref[...]).astype(BF16)
    for n0 in range(0, PROJ_W, IN_CHUNK):
        y = _dot(h, wb_ref[:, n0:n0 + IN_CHUNK])
        if n0 < COL_Z:
            y = jax.nn.gelu(y, approximate=True)
        elif n0 < COL_XBC:
            y = y * _sigmoid(y)
        proj_ref[:, n0:n0 + IN_CHUNK] = y.astype(BF16)
    dt_ref[...] = _softplus(_dot(h, wb_ref[:, PROJ_W:PROJ_W + DT_W]) + dtb_ref[...])


def _per_layer(shape, layer):
    return pl.BlockSpec((None,) + tuple(shape), lambda *_: (layer,) + (0,) * len(shape))


def _inproj(x2, nw, w_in, layer, dtb, tm):
    m = x2.shape[0]
    return pl.pallas_call(
        _inproj_kernel,
        grid=(m // tm,),
        in_specs=[
            pl.BlockSpec((tm, D_MODEL), lambda i: (i, 0)),
            _per_layer((1, D_MODEL), layer),
            pl.BlockSpec((None, D_MODEL, D_IN), lambda i: (layer, 0, 0), pipeline_mode=pl.Buffered(1)),
            _per_layer((1, DT_W), layer),
        ],
        out_specs=[
            pl.BlockSpec((tm, PROJ_W), lambda i: (i, 0)),
            pl.BlockSpec((tm, DT_W), lambda i: (i, 0)),
        ],
        out_shape=[
            jax.ShapeDtypeStruct((m, PROJ_W), BF16),
            jax.ShapeDtypeStruct((m, DT_W), F32),
        ],
        scratch_shapes=[pltpu.VMEM((D_MODEL, PROJ_W + DT_W), BF16)],
        compiler_params=_cparams(("arbitrary",)),
        name="inproj",
    )(x2, nw, w_in, dtb)


def _head_id(shape, width):
    lane = lax.broadcasted_iota(jnp.int32, shape, 1)
    return lax.shift_right_logical(lane, int(math.log2(width)))


def _gmlp_kernel(pa_ref, ws_ref, bsm_ref, nw_ref, g_ref, out_ref):
    t = pa_ref.shape[0]
    hact = pa_ref[...].astype(F32)
    u = hact[:, :GROUP_W]
    v = hact[:, GROUP_W:]
    ms = _dot((v * v).astype(BF16), g_ref[...])
    vn = (v * lax.rsqrt(ms + RMS_EPS) * nw_ref[...]).astype(BF16)
    row = lax.broadcasted_iota(jnp.int32, (CHUNK, CHUNK), 0)
    col = lax.broadcasted_iota(jnp.int32, (CHUNK, CHUNK), 1)
    wcat = jnp.concatenate(
        [jnp.where(row >= col, ws_ref[h], 0.0) for h in range(N_HEADS)], axis=1).astype(BF16)
    hid = _head_id((CHUNK, GROUP_W), HEAD_DIM)
    for c in range(t // CHUNK):
        vc = vn[c * CHUNK:(c + 1) * CHUNK]
        vstack = jnp.concatenate(
            [jnp.where(hid == h, vc, jnp.zeros_like(vc)) for h in range(N_HEADS)], axis=0)
        s = _dot(wcat, vstack) + bsm_ref[...]
        out_ref[c * CHUNK:(c + 1) * CHUNK, :] = (u[c * CHUNK:(c + 1) * CHUNK] * s).astype(BF16)


def _split3(a):
    a1 = a.astype(BF16)
    r1 = a - a1.astype(F32)
    a2 = r1.astype(BF16)
    r2 = r1 - a2.astype(F32)
    return a1, a2, r2.astype(BF16)


def _ssd_kernel(z_ref, xbc_ref, halo_ref, dt_ref, cw_ref, cb_ref, shift_ref, alog_ref, dexp_ref, nw_ref,
                g_ref, out_ref, st_ref, between_phases=None):
    c = pl.program_id(1)

    @pl.when(c == 0)
    def _():
        st_ref[...] = jnp.zeros_like(st_ref)

    row = lax.broadcasted_iota(jnp.int32, (CHUNK, CHUNK), 0)
    col = lax.broadcasted_iota(jnp.int32, (CHUNK, CHUNK), 1)
    causal = row >= col
    ltri = jnp.where(causal, 1.0, 0.0).astype(BF16)
    hid = _head_id((CHUNK, GROUP_W), HEAD_DIM)
    neg_a = -jnp.exp(alog_ref[...])

    def expand(c4):
        return jnp.where(hid == 0, c4[:, 0:1],
                         jnp.where(hid == 1, c4[:, 1:2],
                                   jnp.where(hid == 2, c4[:, 2:3], c4[:, 3:4])))

    chunks = range(SSD_CPS)
    rows = [slice(ck * CHUNK, (ck + 1) * CHUNK) for ck in chunks]
    groups = [slice(g * SSM_D_STATE, (g + 1) * SSM_D_STATE) for g in range(SSM_GROUPS)]

    xs, bm, cm = [], [], []
    halo = jnp.where(c > 0, halo_ref[...], jnp.zeros_like(halo_ref))
    for ck in chunks:
        xcur = xbc_ref[rows[ck], :]
        acc = cb_ref[...] + cw_ref[SSM_CONV_K - 1:SSM_CONV_K, :] * xcur.astype(F32)
        for j in range(1, SSM_CONV_K):
            k = SSM_CONV_K - 1 - j
            if ck == 0:
                shifted = (_dot(shift_ref[j - 1, :, 0:CHUNK], halo)
                           + _dot(shift_ref[j - 1, :, CHUNK:2 * CHUNK], xcur))
            else:
                shifted = _dot(shift_ref[j - 1], xbc_ref[(ck - 1) * CHUNK:(ck + 1) * CHUNK, :])
            acc = acc + cw_ref[k:k + 1, :] * shifted
        xc = acc * _sigmoid(acc)
        xs.append(xc[:, :GROUP_W])
        bm.append(xc[:, GROUP_W:2 * GROUP_W])
        cm.append(xc[:, 2 * GROUP_W:].astype(BF16))

    acs, acs_t = [], []
    for ck in chunks:
        a1, a2, a3 = _split3(dt_ref[rows[ck], :] * neg_a)
        acs.append(_dot(ltri, a1) + _dot(ltri, a2) + _dot(ltri, a3))
        acs_t.append(acs[ck].T)

    if between_phases is not None:
        between_phases()

    xdt, eacs_e, mcat, snew = [], [], [], []
    for ck in chunks:
        xdt.append(xs[ck] * expand(dt_ref[rows[ck], :]))
        eacs_e.append(expand(jnp.exp(acs[ck])))
        dte_e = expand(jnp.exp(acs[ck][CHUNK - 1:CHUNK, :] - acs[ck]))
        ms = []
        for g in range(SSM_GROUPS):
            cb = _dot_nt(cm[ck][:, groups[g]], bm[ck][:, groups[g]].astype(BF16))
            for hh in range(N_HEADS // SSM_GROUPS):
                h = g * (N_HEADS // SSM_GROUPS) + hh
                seg = acs[ck][:, h:h + 1] - acs_t[ck][h:h + 1, :]
                ms.append((cb * jnp.exp(jnp.where(causal, seg, NEG_BIG))).astype(BF16))
        mcat.append(jnp.concatenate(ms, axis=1))
        w = (xdt[ck] * dte_e).astype(BF16)
        snew.append([_dot(bm[ck][:, groups[g]].T.astype(BF16), w[:, groups[g]])
                     for g in range(SSM_GROUPS)])

    state = [st_ref[g] for g in range(SSM_GROUPS)]
    entering = []
    for ck in chunks:
        entering.append([st.astype(BF16) for st in state])
        cdl = eacs_e[ck][CHUNK - 1:CHUNK, :]
        state = [state[g] * cdl[:, groups[g]] + snew[ck][g] for g in range(SSM_GROUPS)]
    for g in range(SSM_GROUPS):
        st_ref[g] = state[g]

    ys = []
    for ck in chunks:
        xdt_b = xdt[ck].astype(BF16)
        xstack = jnp.concatenate(
            [jnp.where(hid == h, xdt_b, jnp.zeros_like(xdt_b)) for h in range(N_HEADS)], axis=0)
        yoff = jnp.concatenate([_dot(cm[ck][:, groups[g]], entering[ck][g]) for g in range(SSM_GROUPS)], axis=1)
        y = _dot(mcat[ck], xstack) + yoff * eacs_e[ck] + dexp_ref[...] * xs[ck]
        ys.append(y * z_ref[rows[ck], :].astype(F32))

    for ck in chunks:
        msq = _dot((ys[ck] * ys[ck]).astype(BF16), g_ref[...])
        out_ref[rows[ck], :] = (ys[ck] * lax.rsqrt(msq + RMS_EPS) * nw_ref[...]).astype(BF16)


def _conv_shift_matrices():
    r = jnp.arange(CHUNK)[:, None]
    col = jnp.arange(2 * CHUNK)[None, :]
    return jnp.stack([(col == CHUNK + r - j) for j in range(1, SSM_CONV_K)]).astype(BF16)


def _pool_bands():
    r = jnp.arange(CHUNK)[:, None]
    col = jnp.arange(2 * CHUNK)[None, :]
    back = CHUNK + r - col
    bands = [jnp.where((back >= 0) & (back < w), 1.0 / w, 0.0) for w in POOL_WINDOWS]
    bandcat = jnp.concatenate(bands, axis=1).astype(BF16)
    t = jnp.arange(CHUNK, dtype=F32)[:, None] + 1.0
    win = jnp.repeat(jnp.asarray(POOL_WINDOWS, F32), HEAD_DIM)[None, :]
    fix = win / jnp.minimum(win, t)
    return bandcat, fix


def _pool_chunks(pc_ref, pch_ref, band_ref, fix_ref, wbd_ref, scale_ref, out_ref, first_step):
    hid = _head_id((CHUNK, GROUP_W), HEAD_DIM)
    prev = jnp.where(first_step, jnp.zeros_like(pch_ref), pch_ref[...])
    for ck in range(pc_ref.shape[0] // CHUNK):
        cur = pc_ref[ck * CHUNK:(ck + 1) * CHUNK, :]
        parts = []
        for g in range(len(POOL_WINDOWS)):
            parts += [jnp.where(hid == g, prev, jnp.zeros_like(prev)),
                      jnp.where(hid == g, cur, jnp.zeros_like(cur))]
        mean = _dot(band_ref[...], jnp.concatenate(parts, axis=0))
        if ck == 0:
            mean = jnp.where(first_step, mean * fix_ref[...], mean)
        p = mean - cur.astype(F32)
        y = _dot(p.astype(BF16), wbd_ref[...]) * scale_ref[...]
        out_ref[ck * CHUNK:(ck + 1) * CHUNK, :] = y.astype(BF16)
        prev = cur


N_GMLP_IN, N_POOL_IN, N_SSD_IN = 5, 6, 11


def _mixers_kernel(*refs):
    gm = refs[:N_GMLP_IN]
    po = refs[N_GMLP_IN:N_GMLP_IN + N_POOL_IN]
    sd = refs[N_GMLP_IN + N_POOL_IN:N_GMLP_IN + N_POOL_IN + N_SSD_IN]
    ya_ref, yc_ref, yb_ref, st_ref = refs[N_GMLP_IN + N_POOL_IN + N_SSD_IN:]
    first_step = pl.program_id(1) == 0

    def others():
        _gmlp_kernel(*gm, ya_ref)
        _pool_chunks(*po, yc_ref, first_step)

    _ssd_kernel(*sd, yb_ref, st_ref, between_phases=others)


def _mixers(proj, dt, gws, bsm, gmn, g64, band, fix, wbd, psc, cw, cb, shift, alog, dexp, nw, g128,
            layer, b, seq):
    m = proj.shape[0]
    ts = SSD_CPS * CHUNK
    assert seq % ts == 0
    nc = seq // ts
    const = lambda i, c: (0, 0)
    blk = lambda width, col: pl.BlockSpec((ts, width), lambda i, c: (i * nc + c, col // width))
    halo = lambda width, col: pl.BlockSpec(
        (CHUNK, width), lambda i, c: (jnp.maximum((i * nc + c) * SSD_CPS - 1, 0), col // width))
    out = pl.BlockSpec((ts, GROUP_W), lambda i, c: (i * nc + c, 0))
    gmlp_in = [blk(2 * GROUP_W, COL_PA), _per_layer((N_HEADS, CHUNK, CHUNK), layer),
               _per_layer((CHUNK, GROUP_W), layer), _per_layer((1, GROUP_W), layer),
               pl.BlockSpec((GROUP_W, GROUP_W), const)]
    pool_in = [blk(GROUP_W, COL_PC), halo(GROUP_W, COL_PC),
               pl.BlockSpec((CHUNK, len(POOL_WINDOWS) * 2 * CHUNK), const), pl.BlockSpec((CHUNK, GROUP_W), const),
               _per_layer((GROUP_W, GROUP_W), layer), _per_layer((1, GROUP_W), layer)]
    ssd_in = [blk(GROUP_W, COL_Z), blk(SSM_CONV_DIM, COL_XBC), halo(SSM_CONV_DIM, COL_XBC),
              pl.BlockSpec((ts, DT_W), lambda i, c: (i * nc + c, 0)),
              _per_layer((SSM_CONV_K, SSM_CONV_DIM), layer), _per_layer((1, SSM_CONV_DIM), layer),
              pl.BlockSpec((SSM_CONV_K - 1, CHUNK, 2 * CHUNK), lambda i, c: (0, 0, 0)),
              _per_layer((1, DT_W), layer), _per_layer((1, GROUP_W), layer), _per_layer((1, GROUP_W), layer),
              pl.BlockSpec((GROUP_W, GROUP_W), const)]
    assert (len(gmlp_in), len(pool_in), len(ssd_in)) == (N_GMLP_IN, N_POOL_IN, N_SSD_IN)
    return pl.pallas_call(
        _mixers_kernel,
        grid=(b, nc),
        in_specs=gmlp_in + pool_in + ssd_in,
        out_specs=[out, out, out],
        out_shape=[jax.ShapeDtypeStruct((m, GROUP_W), BF16)] * 3,
        scratch_shapes=[pltpu.VMEM((SSM_GROUPS, SSM_D_STATE, LANES), F32)],
        compiler_params=_cparams(("parallel", "arbitrary")),
        name="mixers",
    )(proj, gws, bsm, gmn, g64,
      proj, proj, band, fix, wbd, psc,
      proj, proj, proj, dt, cw, cb, shift, alog, dexp, nw, g128)


def _alibi_slope(h):
    return 2.0 ** (-8.0 * (h + 1) / N_HEADS)


def _head_slot(x, h):
    base = x[:, (h // 2) * LANES:(h // 2 + 1) * LANES]
    return pltpu.roll(base, HEAD_DIM, 1) if h % 2 else base


def _attn_kernel(q_ref, k_ref, v_ref, qnw_ref, knw_ref, lq1_ref, lk1_ref, lq2_ref, lk2_ref, subw_ref,
                 g_ref, out_ref, ka_ref, vt_ref, qs_ref, acc_ref, s0_ref, kmax_ref, qa_ref, *, lam_init):
    qi = pl.program_id(1)
    tq, tk = ATT_TQ, ATT_TK
    seq = k_ref.shape[0]

    def aug_lanes(pos, slope, key_side):
        lane = lax.broadcasted_iota(jnp.int32, pos.shape, 1) - ATT_LANE_ALIBI
        hi = (slope * LANES) * lax.shift_right_logical(pos, int(math.log2(LANES))).astype(F32)
        lo = slope * (pos & (LANES - 1)).astype(F32)
        one = jnp.ones_like(hi)
        c = (one, one, hi, lo) if key_side else (-hi, -lo, one, one)
        last = jnp.where(lane == ATT_LANE_BOUND - ATT_LANE_ALIBI, 1.0, 0.0) if key_side else 0.0
        return jnp.where(lane == 0, c[0], jnp.where(lane == 1, c[1],
                         jnp.where(lane == 2, c[2], jnp.where(lane == 3, c[3], last))))

    @pl.when(qi == 0)
    def _():
        k = k_ref[...].astype(F32)
        ms = _dot((k * k).astype(BF16), g_ref[...])
        kn = k * lax.rsqrt(ms + RMS_EPS) * knw_ref[...]
        ksq = _dot((kn * kn).astype(BF16), g_ref[...]) * DA_QK_DIM
        kmax_ref[...] = jnp.broadcast_to(jnp.max(ksq, axis=0, keepdims=True), kmax_ref.shape)
        pos = lax.broadcasted_iota(jnp.int32, (seq, LANES), 0)
        lane = lax.broadcasted_iota(jnp.int32, (seq, LANES), 1)
        for h in range(N_HEADS):
            aug = aug_lanes(pos, _alibi_slope(h), True)
            ka_ref[h] = jnp.where(lane < HEAD_DIM, _head_slot(kn, h), aug).astype(BF16)
            qa_ref[h] = aug_lanes(pos, _alibi_slope(h), False).astype(BF16)
        tail = (lax.broadcasted_iota(jnp.int32, (VT_ROWS - HEAD_DIM, tk), 0) == 0).astype(BF16)
        for j in range(seq // tk):
            vt = v_ref[j * tk:(j + 1) * tk, :].astype(F32).T.astype(BF16)
            for h in range(N_HEADS):
                vt_ref[j, h, 0:HEAD_DIM, :] = vt[h * HEAD_DIM:(h + 1) * HEAD_DIM, :]
                vt_ref[j, h, HEAD_DIM:VT_ROWS, :] = tail

    q = q_ref[...].astype(F32)
    ms = _dot((q * q).astype(BF16), g_ref[...])
    qn = q * lax.rsqrt(ms + RMS_EPS) * (qnw_ref[...] * (DA_QK_DIM ** -0.5))
    qsq = _dot((qn * qn).astype(BF16), g_ref[...]) * DA_QK_DIM
    b2 = qsq * kmax_ref[0:1, :]
    bound = b2 * lax.rsqrt(b2 + TINY) * ATT_BOUND_MARGIN
    one_pass = jnp.max(bound) <= ATT_ONE_PASS_BOUND
    lane = lax.broadcasted_iota(jnp.int32, (1, LANES), 1)
    q0 = pl.multiple_of(qi * tq, tq)
    for h in range(N_HEADS):
        base = _head_slot(qn, h)
        bnd = _head_slot(bound, h)
        qa = qa_ref[h, pl.ds(q0, tq), :]
        for comp in range(2):
            own = (lane >= comp * DA_QK_DIM) & (lane < (comp + 1) * DA_QK_DIM)
            y = jnp.where(own, base, 0.0)
            y = jnp.where(lane == ATT_LANE_BOUND, -bnd[:, comp * DA_QK_DIM:comp * DA_QK_DIM + 1], y)
            qs_ref[h, comp * tq:(comp + 1) * tq, :] = y.astype(BF16) + qa
    acc_ref[...] = jnp.zeros_like(acc_ref)

    qb = ATT_QB
    items = [(h, n) for h in range(N_HEADS) for n in range(2 * tq // qb)]

    def scores(item, j):
        h, n = item
        k0 = pl.multiple_of(j * tk, tk)
        return _dot_nt(ka_ref[h, pl.ds(k0, tk), :], qs_ref[h, n * qb:(n + 1) * qb, :])

    def tile(j, ms_, masked):
        new_m = []
        s_next = s0_ref[...]
        for i, (h, n) in enumerate(items):
            cols = slice(n * qb, (n + 1) * qb)
            s = s_next
            if i + 1 < len(items):
                s_next = scores(items[i + 1], j)
            elif not masked:
                s0_ref[...] = scores(items[0], j + 1)
            if masked:
                kk = lax.broadcasted_iota(jnp.int32, (tk, qb), 0)
                qq = (lax.broadcasted_iota(jnp.int32, (tk, qb), 1) + n * qb) & (tq - 1)
                s = jnp.where(kk <= qq, s, NEG_BIG)
            m_old = ms_[i]
            m_new = jnp.maximum(m_old, jnp.max(s, axis=0, keepdims=True))
            alpha = jnp.exp(m_old - m_new)
            p = jnp.exp(s - m_new)
            new_m.append(m_new)
            acc_ref[h, :, cols] = acc_ref[h, :, cols] * alpha + _dot(vt_ref[j, h], p.astype(BF16))
        return tuple(new_m)

    def tile_one_pass(j, masked):
        s_next = s0_ref[...]
        for i, (h, n) in enumerate(items):
            cols = slice(n * qb, (n + 1) * qb)
            s = s_next
            if i + 1 < len(items):
                s_next = scores(items[i + 1], j)
            elif not masked:
                s0_ref[...] = scores(items[0], j + 1)
            if masked:
                kk = lax.broadcasted_iota(jnp.int32, (tk, qb), 0)
                qq = (lax.broadcasted_iota(jnp.int32, (tk, qb), 1) + n * qb) & (tq - 1)
                s = jnp.where(kk <= qq, s, NEG_BIG)
            acc_ref[h, :, cols] += _dot(vt_ref[j, h], jnp.exp(s).astype(BF16))

    s0_ref[...] = scores(items[0], 0)

    @pl.when(one_pass)
    def _():
        def body(j, carry):
            tile_one_pass(j, False)
            return carry
        lax.fori_loop(0, qi, body, 0)
        tile_one_pass(qi, True)

    @pl.when(jnp.logical_not(one_pass))
    def _():
        m0 = tuple(jnp.full((1, qb), NEG_BIG, F32) for _ in items)
        m1 = lax.fori_loop(0, qi, lambda j, c: tile(j, c, False), m0)
        tile(qi, m1, True)

    def lane_sum(x):
        return jnp.broadcast_to(jnp.sum(x, axis=1, keepdims=True), x.shape)

    lam = (jnp.exp(lane_sum(lq1_ref[...] * lk1_ref[...]))
           - jnp.exp(lane_sum(lq2_ref[...] * lk2_ref[...])) + lam_init)
    lam = jnp.concatenate([lam] * (tq // LANES), axis=1)
    outs = []
    for h in range(N_HEADS):
        o = acc_ref[h, 0:HEAD_DIM, :] / acc_ref[h, HEAD_DIM:HEAD_DIM + 1, :]
        oh = o[:, 0:tq] - lam * o[:, tq:2 * tq]
        ms = jnp.mean(oh * oh, axis=0, keepdims=True)
        outs.append(oh * lax.rsqrt(ms + RMS_EPS) * (subw_ref[...] * (1.0 - lam_init)))
    out_ref[...] = jnp.concatenate(outs, axis=0).T.astype(BF16)


def _attn(proj, qnw, knw, lq1, lk1, lq2, lk2, subw, g32, layer, b, seq, lam_init):
    m = proj.shape[0]
    assert ATT_TQ == ATT_TK
    nq = seq // ATT_TQ
    const = lambda i, j: (0, 0)
    return pl.pallas_call(
        functools.partial(_attn_kernel, lam_init=lam_init),
        grid=(b, nq),
        in_specs=[
            pl.BlockSpec((ATT_TQ, GROUP_W), lambda i, j: (i * nq + j, COL_Q // GROUP_W)),
            pl.BlockSpec((seq, GROUP_W), lambda i, j: (i, COL_K // GROUP_W)),
            pl.BlockSpec((seq, GROUP_W), lambda i, j: (i, COL_V // GROUP_W)),
            _per_layer((1, GROUP_W), layer),
            _per_layer((1, GROUP_W), layer),
            _per_layer((1, LANES), layer),
            _per_layer((1, LANES), layer),
            _per_layer((1, LANES), layer),
            _per_layer((1, LANES), layer),
            _per_layer((HEAD_DIM, ATT_TQ), layer),
            pl.BlockSpec((GROUP_W, GROUP_W), const),
        ],
        out_specs=pl.BlockSpec((ATT_TQ, GROUP_W), lambda i, j: (i * nq + j, 0)),
        out_shape=jax.ShapeDtypeStruct((m, GROUP_W), BF16),
        scratch_shapes=[
            pltpu.VMEM((N_HEADS, seq, LANES), BF16),
            pltpu.VMEM((seq // ATT_TK, N_HEADS, VT_ROWS, ATT_TK), BF16),
            pltpu.VMEM((N_HEADS, 2 * ATT_TQ, LANES), BF16),
            pltpu.VMEM((N_HEADS, VT_ROWS, 2 * ATT_TQ), F32),
            pltpu.VMEM((ATT_TK, ATT_QB), F32),
            pltpu.VMEM((8, GROUP_W), F32),
            pltpu.VMEM((N_HEADS, seq, LANES), BF16),
        ],
        compiler_params=_cparams(("parallel", "arbitrary")),
        name="diffattn",
    )(proj, proj, proj, qnw, knw, lq1, lk1, lq2, lk2, subw, g32)


FFN_STAGE_WIDE = 256
FFN_STAGE_TALL = 704


def _ffn_weight_jobs(layer, wo_hbm, wg_hbm, wu_hbm, wd_hbm, wo_b, wg_b, wu_b, wd_b, wide, tall, sem_w, sem_t):
    jobs = []

    def add(src, dst, stage, sem, rows, n_rows):
        for k, r0 in enumerate(range(0, n_rows, rows)):
            r = min(rows, n_rows - r0)
            slot = k % 2
            view = stage.at[slot, 0:r, :]
            copy = pltpu.make_async_copy(src.at[layer, r0:r0 + r, :], view, sem.at[slot])
            jobs.append((copy, view, dst.at[r0:r0 + r, :]))

    add(wg_hbm, wg_b, wide, sem_w, FFN_STAGE_WIDE, D_MODEL)
    add(wu_hbm, wu_b, wide, sem_w, FFN_STAGE_WIDE, D_MODEL)
    add(wd_hbm, wd_b, tall, sem_t, FFN_STAGE_TALL, D_FF)
    add(wo_hbm, wo_b, tall, sem_t, FFN_STAGE_TALL, D_MODEL)
    return jobs


def _ffn_kernel(x_ref, ya_ref, yb_ref, yc_ref, yd_ref, nw_ref, wo_hbm, wg_hbm, wu_hbm, wd_hbm,
                out_ref, act_ref, wo_ref, wg_ref, wu_ref, wd_ref, wide_ref, tall_ref, sem_w, sem_t, *, layer):
    @pl.when(pl.program_id(0) == 0)
    def _():
        jobs = _ffn_weight_jobs(layer, wo_hbm, wg_hbm, wu_hbm, wd_hbm, wo_ref, wg_ref, wu_ref, wd_ref,
                                wide_ref, tall_ref, sem_w, sem_t)
        jobs[0][0].start()
        for k, (copy, view, dst) in enumerate(jobs):
            if k + 1 < len(jobs):
                jobs[k + 1][0].start()
            copy.wait()
            dst[...] = view[...].astype(BF16)

    x1 = x_ref[...]
    for i, y_ref in enumerate((ya_ref, yb_ref, yc_ref, yd_ref)):
        x1 = x1 + _dot(y_ref[...], wo_ref[i * GROUP_W:(i + 1) * GROUP_W, :])
    ms = jnp.mean(x1 * x1, axis=-1, keepdims=True)
    h = (x1 * lax.rsqrt(ms + RMS_EPS) * nw_ref[...]).astype(BF16)
    for c0 in range(0, D_FF, FFN_CHUNK):
        c1 = min(c0 + FFN_CHUNK, D_FF)
        g = _dot(h, wg_ref[:, c0:c1])
        u = _dot(h, wu_ref[:, c0:c1])
        act_ref[:, c0:c1] = (g * _sigmoid(g) * u).astype(BF16)
    out_ref[...] = x1 + _dot(act_ref[...], wd_ref[...])


def _ffn(x2, ya, yb, yc, yd, nw, wo, wg, wu, wd, layer, tm):
    m = x2.shape[0]
    row = lambda i: (i, 0)
    hbm = pl.BlockSpec(memory_space=pl.ANY)
    return pl.pallas_call(
        functools.partial(_ffn_kernel, layer=layer),
        grid=(m // tm,),
        in_specs=[
            pl.BlockSpec((tm, D_MODEL), row),
            pl.BlockSpec((tm, GROUP_W), row),
            pl.BlockSpec((tm, GROUP_W), row),
            pl.BlockSpec((tm, GROUP_W), row),
            pl.BlockSpec((tm, GROUP_W), row),
            _per_layer((1, D_MODEL), layer),
            hbm, hbm, hbm, hbm,
        ],
        out_specs=pl.BlockSpec((tm, D_MODEL), row),
        out_shape=jax.ShapeDtypeStruct((m, D_MODEL), F32),
        scratch_shapes=[
            pltpu.VMEM((tm, D_FF), BF16),
            pltpu.VMEM((D_MODEL, D_MODEL), BF16),
            pltpu.VMEM((D_MODEL, D_FF), BF16),
            pltpu.VMEM((D_MODEL, D_FF), BF16),
            pltpu.VMEM((D_FF, D_MODEL), BF16),
            pltpu.VMEM((2, FFN_STAGE_WIDE, D_FF), F32),
            pltpu.VMEM((2, FFN_STAGE_TALL, D_MODEL), F32),
            pltpu.SemaphoreType.DMA((2,)),
            pltpu.SemaphoreType.DMA((2,)),
        ],
        compiler_params=_cparams(("arbitrary",)),
        name="outproj_ffn",
    )(x2, ya, yb, yc, yd, nw, wo, wg, wu, wd)


def _block_diag_mean(width, group):
    idx = jnp.arange(width) // group
    return jnp.where(idx[:, None] == idx[None, :], 1.0 / group, 0.0).astype(BF16)


def _rows(v, width=None):
    v = v.reshape(v.shape[0], 1, -1).astype(F32)
    return v if width is None else jnp.pad(v, ((0, 0), (0, 0), (0, width - v.shape[2])))


def kernel(x, norm1_w, w_in, gm_norm_w, gm_ws, gm_bs, ssm_conv_w, ssm_conv_b, ssm_dt_bias, ssm_a_log, ssm_d, ssm_norm_w, pool_w, pool_scale, da_q_norm_w, da_k_norm_w, da_lambda_q1, da_lambda_k1, da_lambda_q2, da_lambda_k2, da_subln_w, w_out, norm2_w, ffn_w_gate, ffn_w_up, ffn_w_down):
    b, seq, d = x.shape
    depth = w_in.shape[0]
    assert d == D_MODEL and seq % ATT_TQ == 0 and seq % CHUNK == 0
    m = b * seq
    tm = 512 if m % 512 == 0 else ATT_TQ

    g64 = _block_diag_mean(GROUP_W, HEAD_DIM)
    g128 = _block_diag_mean(GROUP_W, SSM_D_STATE)
    g32 = _block_diag_mean(GROUP_W, DA_QK_DIM)

    shift = _conv_shift_matrices()
    band, fix = _pool_bands()

    assert w_in.shape[1:] == (D_MODEL, D_IN)

    n1, n2 = _rows(norm1_w), _rows(norm2_w)
    dtb, alog = _rows(ssm_dt_bias, DT_W), _rows(ssm_a_log, DT_W)
    bsm = jnp.repeat(jnp.swapaxes(gm_bs, 1, 2), HEAD_DIM, axis=2)
    gmn = _rows(gm_norm_w)
    cw, cb = jnp.swapaxes(ssm_conv_w, 1, 2), _rows(ssm_conv_b)
    dexp, ssn = _rows(jnp.repeat(ssm_d, HEAD_DIM, axis=1)), _rows(ssm_norm_w)
    eye = jnp.eye(len(POOL_WINDOWS), dtype=pool_w.dtype)
    wbd = jnp.einsum('lgab,gh->lgahb', pool_w, eye).reshape(depth, GROUP_W, GROUP_W).astype(BF16)
    psc = _rows(pool_scale)
    qnw = _rows(jnp.tile(da_q_norm_w, (1, GROUP_W // DA_QK_DIM)))
    knw = _rows(jnp.tile(da_k_norm_w, (1, GROUP_W // DA_QK_DIM)))
    lams = [_rows(v, LANES) for v in (da_lambda_q1, da_lambda_k1, da_lambda_q2, da_lambda_k2)]
    subw = jnp.broadcast_to(da_subln_w[:, :, None], (depth, HEAD_DIM, ATT_TQ))

    x2 = x.reshape(m, d)
    for i in range(depth):
        proj, dt = _inproj(x2, n1, w_in, i, dtb, tm)
        ya, yc, yb = _mixers(proj, dt, gm_ws, bsm, gmn, g64, band, fix, wbd, psc,
                             cw, cb, shift, alog, dexp, ssn, g128, i, b, seq)
        lam_init = 0.8 - 0.6 * math.exp(-0.3 * i)
        yd = _attn(proj, qnw, knw, *lams, subw, g32, i, b, seq, lam_init)
        x2 = _ffn(x2, ya, yb, yc, yd, n2, w_out, ffn_w_gate, ffn_w_up, ffn_w_down, i, tm)
    return x2.reshape(b, seq, d)
```

```python
import functools
import math

import jax
import jax.numpy as jnp
from jax import lax
from jax.experimental import pallas as pl
from jax.experimental.pallas import tpu as pltpu

F32 = jnp.float32
BF16 = jnp.bfloat16

D_MODEL = 1024
GROUP_W = 256
CHUNK = 128
HEAD_DIM = 64
N_HEADS = 4
SSM_GROUPS = 2
SSM_D_STATE = 128
SSM_CONV_K = 4
SSM_CONV_DIM = GROUP_W + 2 * SSM_GROUPS * SSM_D_STATE
POOL_WINDOWS = (2, 4, 8, 16)
DA_QK_DIM = 32
D_FF = 2816
RMS_EPS = 1e-6
NEG_BIG = -1e30

LANES = 128
VMEM_LIMIT = 56 * 1024 * 1024

COL_PA = 0
COL_Z = 512
COL_XBC = 768
COL_PC = 1536
COL_Q = 1792
COL_K = 2048
COL_V = 2304
PROJ_W = 2560
DT_W = LANES
D_IN = PROJ_W + N_HEADS
DT_COL = COL_PC
IN_CHUNK = 256
FFN_CHUNK = 512
SSD_CPS = 4

ATT_TQ = 512
ATT_TK = 512
ATT_QB = 512
ATT_BOUND_MARGIN = 1.01
ATT_ONE_PASS_BOUND = 40.0
VT_ROWS = 80
ATT_LANE_ALIBI = HEAD_DIM
ATT_LANE_BOUND = HEAD_DIM + 4
TINY = 1e-30


def _cparams(sem):
    return pltpu.CompilerParams(dimension_semantics=sem, vmem_limit_bytes=VMEM_LIMIT)


def _sigmoid(x):
    return 1.0 / (1.0 + jnp.exp(-x))


def _dot(a, b):
    return jnp.dot(a, b, preferred_element_type=F32)


def _dot_nt(a, b):
    return lax.dot_general(a, b, (((1,), (1,)), ((), ())), preferred_element_type=F32)


def _softplus(t):
    return jnp.maximum(t, 0.0) + jnp.log(1.0 + jnp.exp(-jnp.abs(t)))


def _inproj_kernel(x_ref, nw_ref, w_ref, dtb_ref, proj_ref, dt_ref, wb_ref):
    @pl.when(pl.program_id(0) == 0)
    def _():
        for n0 in range(0, DT_COL, 2 * IN_CHUNK):
            wb_ref[:, n0:n0 + 2 * IN_CHUNK] = w_ref[:, n0:n0 + 2 * IN_CHUNK].astype(BF16)
        tail = w_ref[:, DT_COL:D_IN]
        wb_ref[:, DT_COL:PROJ_W] = tail[:, N_HEADS:].astype(BF16)
        wb_ref[:, PROJ_W:PROJ_W + DT_W] = tail[:, 0:DT_W].astype(BF16)

    x = x_ref[...]
    ms = jnp.mean(x * x, axis=-1, keepdims=True)
    h = (x * lax.rsqrt(ms + RMS_EPS) * nw_ref[...]).astype(BF16)
    for n0 in range(0, PROJ_W, IN_CHUNK):
        y = _dot(h, wb_ref[:, n0:n0 + IN_CHUNK])
        if n0 < COL_Z:
            y = jax.nn.gelu(y, approximate=True)
        elif n0 < COL_XBC:
            y = y * _sigmoid(y)
        proj_ref[:, n0:n0 + IN_CHUNK] = y.astype(BF16)
    dt_ref[...] = _softplus(_dot(h, wb_ref[:, PROJ_W:PROJ_W + DT_W]) + dtb_ref[...])


def _per_layer(shape, layer):
    return pl.BlockSpec((None,) + tuple(shape), lambda *_: (layer,) + (0,) * len(shape))


def _inproj(x2, nw, w_in, layer, dtb, tm):
    m = x2.shape[0]
    return pl.pallas_call(
        _inproj_kernel,
        grid=(m // tm,),
        in_specs=[
            pl.BlockSpec((tm, D_MODEL), lambda i: (i, 0)),
            _per_layer((1, D_MODEL), layer),
            pl.BlockSpec((None, D_MODEL, D_IN), lambda i: (layer, 0, 0), pipeline_mode=pl.Buffered(1)),
            _per_layer((1, DT_W), layer),
        ],
        out_specs=[
            pl.BlockSpec((tm, PROJ_W), lambda i: (i, 0)),
            pl.BlockSpec((tm, DT_W), lambda i: (i, 0)),
        ],
        out_shape=[
            jax.ShapeDtypeStruct((m, PROJ_W), BF16),
            jax.ShapeDtypeStruct((m, DT_W), F32),
        ],
        scratch_shapes=[pltpu.VMEM((D_MODEL, PROJ_W + DT_W), BF16)],
        compiler_params=_cparams(("arbitrary",)),
        name="inproj",
    )(x2, nw, w_in, dtb)


def _head_id(shape, width):
    lane = lax.broadcasted_iota(jnp.int32, shape, 1)
    return lax.shift_right_logical(lane, int(math.log2(width)))


def _gmlp_kernel(pa_ref, ws_ref, bsm_ref, nw_ref, g_ref, out_ref):
    t = pa_ref.shape[0]
    hact = pa_ref[...].astype(F32)
    u = hact[:, :GROUP_W]
    v = hact[:, GROUP_W:]
    ms = _dot((v * v).astype(BF16), g_ref[...])
    vn = (v * lax.rsqrt(ms + RMS_EPS) * nw_ref[...]).astype(BF16)
    row = lax.broadcasted_iota(jnp.int32, (CHUNK, CHUNK), 0)
    col = lax.broadcasted_iota(jnp.int32, (CHUNK, CHUNK), 1)
    wcat = jnp.concatenate(
        [jnp.where(row >= col, ws_ref[h], 0.0) for h in range(N_HEADS)], axis=1).astype(BF16)
    hid = _head_id((CHUNK, GROUP_W), HEAD_DIM)
    for c in range(t // CHUNK):
        vc = vn[c * CHUNK:(c + 1) * CHUNK]
        vstack = jnp.concatenate(
            [jnp.where(hid == h, vc, jnp.zeros_like(vc)) for h in range(N_HEADS)], axis=0)
        s = _dot(wcat, vstack) + bsm_ref[...]
        out_ref[c * CHUNK:(c + 1) * CHUNK, :] = (u[c * CHUNK:(c + 1) * CHUNK] * s).astype(BF16)


def _split3(a):
    a1 = a.astype(BF16)
    r1 = a - a1.astype(F32)
    a2 = r1.astype(BF16)
    r2 = r1 - a2.astype(F32)
    return a1, a2, r2.astype(BF16)


def _ssd_kernel(z_ref, xbc_ref, halo_ref, dt_ref, cw_ref, cb_ref, shift_ref, alog_ref, dexp_ref, nw_ref,
                g_ref, out_ref, st_ref, between_phases=None):
    c = pl.program_id(1)

    @pl.when(c == 0)
    def _():
        st_ref[...] = jnp.zeros_like(st_ref)

    row = lax.broadcasted_iota(jnp.int32, (CHUNK, CHUNK), 0)
    col = lax.broadcasted_iota(jnp.int32, (CHUNK, CHUNK), 1)
    causal = row >= col
    ltri = jnp.where(causal, 1.0, 0.0).astype(BF16)
    hid = _head_id((CHUNK, GROUP_W), HEAD_DIM)
    neg_a = -jnp.exp(alog_ref[...])

    def expand(c4):
        return jnp.where(hid == 0, c4[:, 0:1],
                         jnp.where(hid == 1, c4[:, 1:2],
                                   jnp.where(hid == 2, c4[:, 2:3], c4[:, 3:4])))

    chunks = range(SSD_CPS)
    rows = [slice(ck * CHUNK, (ck + 1) * CHUNK) for ck in chunks]
    groups = [slice(g * SSM_D_STATE, (g + 1) * SSM_D_STATE) for g in range(SSM_GROUPS)]

    xs, bm, cm = [], [], []
    halo = jnp.where(c > 0, halo_ref[...], jnp.zeros_like(halo_ref))
    for ck in chunks:
        xcur = xbc_ref[rows[ck], :]
        acc = cb_ref[...] + cw_ref[SSM_CONV_K - 1:SSM_CONV_K, :] * xcur.astype(F32)
        for j in range(1, SSM_CONV_K):
            k = SSM_CONV_K - 1 - j
            if ck == 0:
                shifted = (_dot(shift_ref[j - 1, :, 0:CHUNK], halo)
                           + _dot(shift_ref[j - 1, :, CHUNK:2 * CHUNK], xcur))
            else:
                shifted = _dot(shift_ref[j - 1], xbc_ref[(ck - 1) * CHUNK:(ck + 1) * CHUNK, :])
            acc = acc + cw_ref[k:k + 1, :] * shifted
        xc = acc * _sigmoid(acc)
        xs.append(xc[:, :GROUP_W])
        bm.append(xc[:, GROUP_W:2 * GROUP_W])
        cm.append(xc[:, 2 * GROUP_W:].astype(BF16))

    acs, acs_t = [], []
    for ck in chunks:
        a1, a2, a3 = _split3(dt_ref[rows[ck], :] * neg_a)
        acs.append(_dot(ltri, a1) + _dot(ltri, a2) + _dot(ltri, a3))
        acs_t.append(acs[ck].T)

    if between_phases is not None:
        between_phases()

    xdt, eacs_e, mcat, snew = [], [], [], []
    for ck in chunks:
        xdt.append(xs[ck] * expand(dt_ref[rows[ck], :]))
        eacs_e.append(expand(jnp.exp(acs[ck])))
        dte_e = expand(jnp.exp(acs[ck][CHUNK - 1:CHUNK, :] - acs[ck]))
        ms = []
        for g in range(SSM_GROUPS):
            cb = _dot_nt(cm[ck][:, groups[g]], bm[ck][:, groups[g]].astype(BF16))
            for hh in range(N_HEADS // SSM_GROUPS):
                h = g * (N_HEADS // SSM_GROUPS) + hh
                seg = acs[ck][:, h:h + 1] - acs_t[ck][h:h + 1, :]
                ms.append((cb * jnp.exp(jnp.where(causal, seg, NEG_BIG))).astype(BF16))
        mcat.append(jnp.concatenate(ms, axis=1))
        w = (xdt[ck] * dte_e).astype(BF16)
        snew.append([_dot(bm[ck][:, groups[g]].T.astype(BF16), w[:, groups[g]])
                     for g in range(SSM_GROUPS)])

    state = [st_ref[g] for g in range(SSM_GROUPS)]
    entering = []
    for ck in chunks:
        entering.append([st.astype(BF16) for st in state])
        cdl = eacs_e[ck][CHUNK - 1:CHUNK, :]
        state = [state[g] * cdl[:, groups[g]] + snew[ck][g] for g in range(SSM_GROUPS)]
    for g in range(SSM_GROUPS):
        st_ref[g] = state[g]

    ys = []
    for ck in chunks:
        xdt_b = xdt[ck].astype(BF16)
        xstack = jnp.concatenate(
            [jnp.where(hid == h, xdt_b, jnp.zeros_like(xdt_b)) for h in range(N_HEADS)], axis=0)
        yoff = jnp.concatenate([_dot(cm[ck][:, groups[g]], entering[ck][g]) for g in range(SSM_GROUPS)], axis=1)
        y = _dot(mcat[ck], xstack) + yoff * eacs_e[ck] + dexp_ref[...] * xs[ck]
        ys.append(y * z_ref[rows[ck], :].astype(F32))

    for ck in chunks:
        msq = _dot((ys[ck] * ys[ck]).astype(BF16), g_ref[...])
        out_ref[rows[ck], :] = (ys[ck] * lax.rsqrt(msq + RMS_EPS) * nw_ref[...]).astype(BF16)


def _conv_shift_matrices():
    r = jnp.arange(CHUNK)[:, None]
    col = jnp.arange(2 * CHUNK)[None, :]
    return jnp.stack([(col == CHUNK + r - j) for j in range(1, SSM_CONV_K)]).astype(BF16)


def _pool_bands():
    r = jnp.arange(CHUNK)[:, None]
    col = jnp.arange(2 * CHUNK)[None, :]
    back = CHUNK + r - col
    bands = [jnp.where((back >= 0) & (back < w), 1.0 / w, 0.0) for w in POOL_WINDOWS]
    bandcat = jnp.concatenate(bands, axis=1).astype(BF16)
    t = jnp.arange(CHUNK, dtype=F32)[:, None] + 1.0
    win = jnp.repeat(jnp.asarray(POOL_WINDOWS, F32), HEAD_DIM)[None, :]
    fix = win / jnp.minimum(win, t)
    return bandcat, fix


def _pool_chunks(pc_ref, pch_ref, band_ref, fix_ref, wbd_ref, scale_ref, out_ref, first_step):
    hid = _head_id((CHUNK, GROUP_W), HEAD_DIM)
    prev = jnp.where(first_step, jnp.zeros_like(pch_ref), pch_ref[...])
    for ck in range(pc_ref.shape[0] // CHUNK):
        cur = pc_ref[ck * CHUNK:(ck + 1) * CHUNK, :]
        parts = []
        for g in range(len(POOL_WINDOWS)):
            parts += [jnp.where(hid == g, prev, jnp.zeros_like(prev)),
                      jnp.where(hid == g, cur, jnp.zeros_like(cur))]
        mean = _dot(band_ref[...], jnp.concatenate(parts, axis=0))
        if ck == 0:
            mean = jnp.where(first_step, mean * fix_ref[...], mean)
        p = mean - cur.astype(F32)
        y = _dot(p.astype(BF16), wbd_ref[...]) * scale_ref[...]
        out_ref[ck * CHUNK:(ck + 1) * CHUNK, :] = y.astype(BF16)
        prev = cur


N_GMLP_IN, N_POOL_IN, N_SSD_IN = 5, 6, 11


def _mixers_kernel(*refs):
    gm = refs[:N_GMLP_IN]
    po = refs[N_GMLP_IN:N_GMLP_IN + N_POOL_IN]
    sd = refs[N_GMLP_IN + N_POOL_IN:N_GMLP_IN + N_POOL_IN + N_SSD_IN]
    ya_ref, yc_ref, yb_ref, st_ref = refs[N_GMLP_IN + N_POOL_IN + N_SSD_IN:]
    first_step = pl.program_id(1) == 0

    def others():
        _gmlp_kernel(*gm, ya_ref)
        _pool_chunks(*po, yc_ref, first_step)

    _ssd_kernel(*sd, yb_ref, st_ref, between_phases=others)


def _mixers(proj, dt, gws, bsm, gmn, g64, band, fix, wbd, psc, cw, cb, shift, alog, dexp, nw, g128,
            layer, b, seq):
    m = proj.shape[0]
    ts = SSD_CPS * CHUNK
    assert seq % ts == 0
    nc = seq // ts
    const = lambda i, c: (0, 0)
    blk = lambda width, col: pl.BlockSpec((ts, width), lambda i, c: (i * nc + c, col // width))
    halo = lambda width, col: pl.BlockSpec(
        (CHUNK, width), lambda i, c: (jnp.maximum((i * nc + c) * SSD_CPS - 1, 0), col // width))
    out = pl.BlockSpec((ts, GROUP_W), lambda i, c: (i * nc + c, 0))
    gmlp_in = [blk(2 * GROUP_W, COL_PA), _per_layer((N_HEADS, CHUNK, CHUNK), layer),
               _per_layer((CHUNK, GROUP_W), layer), _per_layer((1, GROUP_W), layer),
               pl.BlockSpec((GROUP_W, GROUP_W), const)]
    pool_in = [blk(GROUP_W, COL_PC), halo(GROUP_W, COL_PC),
               pl.BlockSpec((CHUNK, len(POOL_WINDOWS) * 2 * CHUNK), const), pl.BlockSpec((CHUNK, GROUP_W), const),
               _per_layer((GROUP_W, GROUP_W), layer), _per_layer((1, GROUP_W), layer)]
    ssd_in = [blk(GROUP_W, COL_Z), blk(SSM_CONV_DIM, COL_XBC), halo(SSM_CONV_DIM, COL_XBC),
              pl.BlockSpec((ts, DT_W), lambda i, c: (i * nc + c, 0)),
              _per_layer((SSM_CONV_K, SSM_CONV_DIM), layer), _per_layer((1, SSM_CONV_DIM), layer),
              pl.BlockSpec((SSM_CONV_K - 1, CHUNK, 2 * CHUNK), lambda i, c: (0, 0, 0)),
              _per_layer((1, DT_W), layer), _per_layer((1, GROUP_W), layer), _per_layer((1, GROUP_W), layer),
              pl.BlockSpec((GROUP_W, GROUP_W), const)]
    assert (len(gmlp_in), len(pool_in), len(ssd_in)) == (N_GMLP_IN, N_POOL_IN, N_SSD_IN)
    return pl.pallas_call(
        _mixers_kernel,
        grid=(b, nc),
        in_specs=gmlp_in + pool_in + ssd_in,
        out_specs=[out, out, out],
        out_shape=[jax.ShapeDtypeStruct((m, GROUP_W), BF16)] * 3,
        scratch_shapes=[pltpu.VMEM((SSM_GROUPS, SSM_D_STATE, LANES), F32)],
        compiler_params=_cparams(("parallel", "arbitrary")),
        name="mixers",
    )(proj, gws, bsm, gmn, g64,
      proj, proj, band, fix, wbd, psc,
      proj, proj, proj, dt, cw, cb, shift, alog, dexp, nw, g128)


def _alibi_slope(h):
    return 2.0 ** (-8.0 * (h + 1) / N_HEADS)


def _head_slot(x, h):
    base = x[:, (h // 2) * LANES:(h // 2 + 1) * LANES]
    return pltpu.roll(base, HEAD_DIM, 1) if h % 2 else base


ATT_PAIR = 2


def _attn_kernel(q_ref, k_ref, v_ref, qnw_ref, knw_ref, lq1_ref, lk1_ref, lq2_ref, lk2_ref, subw_ref,
                 g_ref, out_ref, ka_ref, vt_ref, qs_ref, acc_ref, s0_ref, kmax_ref, qa_ref, *, lam_init):
    pj = pl.program_id(1)
    tq, tk = ATT_TQ, ATT_TK
    seq = k_ref.shape[0]

    def aug_lanes(pos, slope, key_side):
        lane = lax.broadcasted_iota(jnp.int32, pos.shape, 1) - ATT_LANE_ALIBI
        hi = (slope * LANES) * lax.shift_right_logical(pos, int(math.log2(LANES))).astype(F32)
        lo = slope * (pos & (LANES - 1)).astype(F32)
        one = jnp.ones_like(hi)
        c = (one, one, hi, lo) if key_side else (-hi, -lo, one, one)
        last = jnp.where(lane == ATT_LANE_BOUND - ATT_LANE_ALIBI, 1.0, 0.0) if key_side else 0.0
        return jnp.where(lane == 0, c[0], jnp.where(lane == 1, c[1],
                         jnp.where(lane == 2, c[2], jnp.where(lane == 3, c[3], last))))

    @pl.when(pj == 0)
    def _():
        k = k_ref[...].astype(F32)
        ms = _dot((k * k).astype(BF16), g_ref[...])
        kn = k * lax.rsqrt(ms + RMS_EPS) * knw_ref[...]
        ksq = _dot((kn * kn).astype(BF16), g_ref[...]) * DA_QK_DIM
        kmax_ref[...] = jnp.broadcast_to(jnp.max(ksq, axis=0, keepdims=True), kmax_ref.shape)
        pos = lax.broadcasted_iota(jnp.int32, (seq, LANES), 0)
        lane = lax.broadcasted_iota(jnp.int32, (seq, LANES), 1)
        for h in range(N_HEADS):
            aug = aug_lanes(pos, _alibi_slope(h), True)
            ka_ref[h] = jnp.where(lane < HEAD_DIM, _head_slot(kn, h), aug).astype(BF16)
            qa_ref[h] = aug_lanes(pos, _alibi_slope(h), False).astype(BF16)
        tail = (lax.broadcasted_iota(jnp.int32, (VT_ROWS - HEAD_DIM, tk), 0) == 0).astype(BF16)
        for j in range(seq // tk):
            vt = v_ref[j * tk:(j + 1) * tk, :].astype(F32).T.astype(BF16)
            for h in range(N_HEADS):
                vt_ref[j, h, 0:HEAD_DIM, :] = vt[h * HEAD_DIM:(h + 1) * HEAD_DIM, :]
                vt_ref[j, h, HEAD_DIM:VT_ROWS, :] = tail

    lane = lax.broadcasted_iota(jnp.int32, (1, LANES), 1)

    def normalised_q(t):
        q = q_ref[t * tq:(t + 1) * tq, :].astype(F32)
        ms = _dot((q * q).astype(BF16), g_ref[...])
        qn = q * lax.rsqrt(ms + RMS_EPS) * (qnw_ref[...] * (DA_QK_DIM ** -0.5))
        qsq = _dot((qn * qn).astype(BF16), g_ref[...]) * DA_QK_DIM
        b2 = qsq * kmax_ref[0:1, :]
        bound = b2 * lax.rsqrt(b2 + TINY) * ATT_BOUND_MARGIN
        return qn, bound, jnp.max(bound) <= ATT_ONE_PASS_BOUND

    def build_queries(t, qi, qn, bound):
        q0 = pl.multiple_of(qi * tq, tq)
        for h in range(N_HEADS):
            base = _head_slot(qn, h)
            bnd = _head_slot(bound, h)
            qa = qa_ref[h, pl.ds(q0, tq), :]
            for comp in range(2):
                own = (lane >= comp * DA_QK_DIM) & (lane < (comp + 1) * DA_QK_DIM)
                y = jnp.where(own, base, 0.0)
                y = jnp.where(lane == ATT_LANE_BOUND, -bnd[:, comp * DA_QK_DIM:comp * DA_QK_DIM + 1], y)
                qs_ref[t, h, comp * tq:(comp + 1) * tq, :] = y.astype(BF16) + qa
                yield

    qb = ATT_QB
    items = [(h, n) for h in range(N_HEADS) for n in range(2 * tq // qb)]

    def scores(t, item, j):
        h, n = item
        k0 = pl.multiple_of(j * tk, tk)
        return _dot_nt(ka_ref[h, pl.ds(k0, tk), :], qs_ref[t, h, n * qb:(n + 1) * qb, :])

    def causal_mask(s, n):
        kk = lax.broadcasted_iota(jnp.int32, (tk, qb), 0)
        qq = (lax.broadcasted_iota(jnp.int32, (tk, qb), 1) + n * qb) & (tq - 1)
        return jnp.where(kk <= qq, s, NEG_BIG)

    def tile(t, j, ms_, masked):
        new_m = []
        s_next = s0_ref[...]
        for i, (h, n) in enumerate(items):
            cols = slice(n * qb, (n + 1) * qb)
            s = s_next
            if i + 1 < len(items):
                s_next = scores(t, items[i + 1], j)
            elif not masked:
                s0_ref[...] = scores(t, items[0], j + 1)
            if masked:
                s = causal_mask(s, n)
            m_old = ms_[i]
            m_new = jnp.maximum(m_old, jnp.max(s, axis=0, keepdims=True))
            alpha = jnp.exp(m_old - m_new)
            p = jnp.exp(s - m_new)
            new_m.append(m_new)
            acc_ref[t, h, :, cols] = acc_ref[t, h, :, cols] * alpha + _dot(vt_ref[j, h], p.astype(BF16))
        return tuple(new_m)

    def tile_one_pass(t, j, masked, filler=None):
        s_next = s0_ref[...]
        for i, (h, n) in enumerate(items):
            cols = slice(n * qb, (n + 1) * qb)
            s = s_next
            if i + 1 < len(items):
                s_next = scores(t, items[i + 1], j)
            elif not masked:
                s0_ref[...] = scores(t, items[0], j + 1)
            if masked:
                s = causal_mask(s, n)
            acc_ref[t, h, :, cols] += _dot(vt_ref[j, h], jnp.exp(s).astype(BF16))
            if filler is not None:
                next(filler, None)

    def attend(t, qi, one_pass, other_work):
        acc_ref[t] = jnp.zeros(acc_ref.shape[1:], F32)
        s0_ref[...] = scores(t, items[0], 0)

        @pl.when(one_pass)
        def _():
            def body(j, carry):
                tile_one_pass(t, j, False)
                return carry
            lax.fori_loop(0, qi, body, 0)
            filler = other_work()
            tile_one_pass(t, qi, True, filler)
            for _ in filler:
                pass

        @pl.when(jnp.logical_not(one_pass))
        def _():
            m0 = tuple(jnp.full((1, qb), NEG_BIG, F32) for _ in items)
            m1 = lax.fori_loop(0, qi, lambda j, c: tile(t, j, c, False), m0)
            tile(t, qi, m1, True)
            for _ in other_work():
                pass

    def lane_sum(x):
        return jnp.broadcast_to(jnp.sum(x, axis=1, keepdims=True), x.shape)

    lam = (jnp.exp(lane_sum(lq1_ref[...] * lk1_ref[...]))
           - jnp.exp(lane_sum(lq2_ref[...] * lk2_ref[...])) + lam_init)
    lam = jnp.concatenate([lam] * (tq // LANES), axis=1)

    def finish(t):
        outs = []
        for h in range(N_HEADS):
            o = acc_ref[t, h, 0:HEAD_DIM, :] / acc_ref[t, h, HEAD_DIM:HEAD_DIM + 1, :]
            oh = o[:, 0:tq] - lam * o[:, tq:2 * tq]
            ms = jnp.mean(oh * oh, axis=0, keepdims=True)
            outs.append(oh * lax.rsqrt(ms + RMS_EPS) * (subw_ref[...] * (1.0 - lam_init)))
            yield
        out_ref[t * tq:(t + 1) * tq, :] = jnp.concatenate(outs, axis=0).T.astype(BF16)
        yield

    def chained(makers):
        def run():
            for make in makers:
                yield from make()
        return run

    prepared = [normalised_q(t) for t in range(ATT_PAIR)]
    for _ in build_queries(0, pj * ATT_PAIR, *prepared[0][:2]):
        pass
    for t in range(ATT_PAIR):
        qi = pj * ATT_PAIR + t
        work = []
        if t + 1 < ATT_PAIR:
            work.append(functools.partial(build_queries, t + 1, qi + 1, *prepared[t + 1][:2]))
        if t > 0:
            work.append(functools.partial(finish, t - 1))
        attend(t, qi, prepared[t][2], chained(work))
    for _ in finish(ATT_PAIR - 1):
        pass


def _attn(proj, qnw, knw, lq1, lk1, lq2, lk2, subw, g32, layer, b, seq, lam_init):
    m = proj.shape[0]
    assert ATT_TQ == ATT_TK and seq % (ATT_PAIR * ATT_TQ) == 0
    tp = ATT_PAIR * ATT_TQ
    nq = seq // tp
    const = lambda i, j: (0, 0)
    return pl.pallas_call(
        functools.partial(_attn_kernel, lam_init=lam_init),
        grid=(b, nq),
        in_specs=[
            pl.BlockSpec((tp, GROUP_W), lambda i, j: (i * nq + j, COL_Q // GROUP_W)),
            pl.BlockSpec((seq, GROUP_W), lambda i, j: (i, COL_K // GROUP_W)),
            pl.BlockSpec((seq, GROUP_W), lambda i, j: (i, COL_V // GROUP_W)),
            _per_layer((1, GROUP_W), layer),
            _per_layer((1, GROUP_W), layer),
            _per_layer((1, LANES), layer),
            _per_layer((1, LANES), layer),
            _per_layer((1, LANES), layer),
            _per_layer((1, LANES), layer),
            _per_layer((HEAD_DIM, ATT_TQ), layer),
            pl.BlockSpec((GROUP_W, GROUP_W), const),
        ],
        out_specs=pl.BlockSpec((tp, GROUP_W), lambda i, j: (i * nq + j, 0)),
        out_shape=jax.ShapeDtypeStruct((m, GROUP_W), BF16),
        scratch_shapes=[
            pltpu.VMEM((N_HEADS, seq, LANES), BF16),
            pltpu.VMEM((seq // ATT_TK, N_HEADS, VT_ROWS, ATT_TK), BF16),
            pltpu.VMEM((ATT_PAIR, N_HEADS, 2 * ATT_TQ, LANES), BF16),
            pltpu.VMEM((ATT_PAIR, N_HEADS, VT_ROWS, 2 * ATT_TQ), F32),
            pltpu.VMEM((ATT_TK, ATT_QB), F32),
            pltpu.VMEM((8, GROUP_W), F32),
            pltpu.VMEM((N_HEADS, seq, LANES), BF16),
        ],
        compiler_params=_cparams(("parallel", "arbitrary")),
        name="diffattn",
    )(proj, proj, proj, qnw, knw, lq1, lk1, lq2, lk2, subw, g32)


FFN_STAGE_WIDE = 256
FFN_STAGE_TALL = 704


def _ffn_weight_jobs(layer, wo_hbm, wg_hbm, wu_hbm, wd_hbm, wo_b, wg_b, wu_b, wd_b, wide, tall, sem_w, sem_t):
    jobs = []

    def add(src, dst, stage, sem, rows, n_rows):
        for k, r0 in enumerate(range(0, n_rows, rows)):
            r = min(rows, n_rows - r0)
            slot = k % 2
            view = stage.at[slot, 0:r, :]
            copy = pltpu.make_async_copy(src.at[layer, r0:r0 + r, :], view, sem.at[slot])
            jobs.append((copy, view, dst.at[r0:r0 + r, :]))

    add(wg_hbm, wg_b, wide, sem_w, FFN_STAGE_WIDE, D_MODEL)
    add(wu_hbm, wu_b, wide, sem_w, FFN_STAGE_WIDE, D_MODEL)
    add(wd_hbm, wd_b, tall, sem_t, FFN_STAGE_TALL, D_FF)
    add(wo_hbm, wo_b, tall, sem_t, FFN_STAGE_TALL, D_MODEL)
    return jobs


def _ffn_kernel(x_ref, ya_ref, yb_ref, yc_ref, yd_ref, nw_ref, wo_hbm, wg_hbm, wu_hbm, wd_hbm,
                out_ref, act_ref, wo_ref, wg_ref, wu_ref, wd_ref, wide_ref, tall_ref, sem_w, sem_t, *, layer):
    @pl.when(pl.program_id(0) == 0)
    def _():
        jobs = _ffn_weight_jobs(layer, wo_hbm, wg_hbm, wu_hbm, wd_hbm, wo_ref, wg_ref, wu_ref, wd_ref,
                                wide_ref, tall_ref, sem_w, sem_t)
        jobs[0][0].start()
        for k, (copy, view, dst) in enumerate(jobs):
            if k + 1 < len(jobs):
                jobs[k + 1][0].start()
            copy.wait()
            dst[...] = view[...].astype(BF16)

    x1 = x_ref[...]
    for i, y_ref in enumerate((ya_ref, yb_ref, yc_ref, yd_ref)):
        x1 = x1 + _dot(y_ref[...], wo_ref[i * GROUP_W:(i + 1) * GROUP_W, :])
    ms = jnp.mean(x1 * x1, axis=-1, keepdims=True)
    h = (x1 * lax.rsqrt(ms + RMS_EPS) * nw_ref[...]).astype(BF16)
    for c0 in range(0, D_FF, FFN_CHUNK):
        c1 = min(c0 + FFN_CHUNK, D_FF)
        g = _dot(h, wg_ref[:, c0:c1])
        u = _dot(h, wu_ref[:, c0:c1])
        act_ref[:, c0:c1] = (g * _sigmoid(g) * u).astype(BF16)
    out_ref[...] = x1 + _dot(act_ref[...], wd_ref[...])


def _ffn(x2, ya, yb, yc, yd, nw, wo, wg, wu, wd, layer, tm):
    m = x2.shape[0]
    row = lambda i: (i, 0)
    hbm = pl.BlockSpec(memory_space=pl.ANY)
    return pl.pallas_call(
        functools.partial(_ffn_kernel, layer=layer),
        grid=(m // tm,),
        in_specs=[
            pl.BlockSpec((tm, D_MODEL), row),
            pl.BlockSpec((tm, GROUP_W), row),
            pl.BlockSpec((tm, GROUP_W), row),
            pl.BlockSpec((tm, GROUP_W), row),
            pl.BlockSpec((tm, GROUP_W), row),
            _per_layer((1, D_MODEL), layer),
            hbm, hbm, hbm, hbm,
        ],
        out_specs=pl.BlockSpec((tm, D_MODEL), row),
        out_shape=jax.ShapeDtypeStruct((m, D_MODEL), F32),
        scratch_shapes=[
            pltpu.VMEM((tm, D_FF), BF16),
            pltpu.VMEM((D_MODEL, D_MODEL), BF16),
            pltpu.VMEM((D_MODEL, D_FF), BF16),
            pltpu.VMEM((D_MODEL, D_FF), BF16),
            pltpu.VMEM((D_FF, D_MODEL), BF16),
            pltpu.VMEM((2, FFN_STAGE_WIDE, D_FF), F32),
            pltpu.VMEM((2, FFN_STAGE_TALL, D_MODEL), F32),
            pltpu.SemaphoreType.DMA((2,)),
            pltpu.SemaphoreType.DMA((2,)),
        ],
        compiler_params=_cparams(("arbitrary",)),
        name="outproj_ffn",
    )(x2, ya, yb, yc, yd, nw, wo, wg, wu, wd)


def _block_diag_mean(width, group):
    idx = jnp.arange(width) // group
    return jnp.where(idx[:, None] == idx[None, :], 1.0 / group, 0.0).astype(BF16)


def _rows(v, width=None):
    v = v.reshape(v.shape[0], 1, -1).astype(F32)
    return v if width is None else jnp.pad(v, ((0, 0), (0, 0), (0, width - v.shape[2])))


def kernel(x, norm1_w, w_in, gm_norm_w, gm_ws, gm_bs, ssm_conv_w, ssm_conv_b, ssm_dt_bias, ssm_a_log, ssm_d, ssm_norm_w, pool_w, pool_scale, da_q_norm_w, da_k_norm_w, da_lambda_q1, da_lambda_k1, da_lambda_q2, da_lambda_k2, da_subln_w, w_out, norm2_w, ffn_w_gate, ffn_w_up, ffn_w_down):
    b, seq, d = x.shape
    depth = w_in.shape[0]
    assert d == D_MODEL and seq % ATT_TQ == 0 and seq % CHUNK == 0
    m = b * seq
    tm = 512 if m % 512 == 0 else ATT_TQ

    g64 = _block_diag_mean(GROUP_W, HEAD_DIM)
    g128 = _block_diag_mean(GROUP_W, SSM_D_STATE)
    g32 = _block_diag_mean(GROUP_W, DA_QK_DIM)

    shift = _conv_shift_matrices()
    band, fix = _pool_bands()

    assert w_in.shape[1:] == (D_MODEL, D_IN)

    n1, n2 = _rows(norm1_w), _rows(norm2_w)
    dtb, alog = _rows(ssm_dt_bias, DT_W), _rows(ssm_a_log, DT_W)
    bsm = jnp.repeat(jnp.swapaxes(gm_bs, 1, 2), HEAD_DIM, axis=2)
    gmn = _rows(gm_norm_w)
    cw, cb = jnp.swapaxes(ssm_conv_w, 1, 2), _rows(ssm_conv_b)
    dexp, ssn = _rows(jnp.repeat(ssm_d, HEAD_DIM, axis=1)), _rows(ssm_norm_w)
    eye = jnp.eye(len(POOL_WINDOWS), dtype=pool_w.dtype)
    wbd = jnp.einsum('lgab,gh->lgahb', pool_w, eye).reshape(depth, GROUP_W, GROUP_W).astype(BF16)
    psc = _rows(pool_scale)
    qnw = _rows(jnp.tile(da_q_norm_w, (1, GROUP_W // DA_QK_DIM)))
    knw = _rows(jnp.tile(da_k_norm_w, (1, GROUP_W // DA_QK_DIM)))
    lams = [_rows(v, LANES) for v in (da_lambda_q1, da_lambda_k1, da_lambda_q2, da_lambda_k2)]
    subw = jnp.broadcast_to(da_subln_w[:, :, None], (depth, HEAD_DIM, ATT_TQ))

    x2 = x.reshape(m, d)
    for i in range(depth):
        proj, dt = _inproj(x2, n1, w_in, i, dtb, tm)
        ya, yc, yb = _mixers(proj, dt, gm_ws, bsm, gmn, g64, band, fix, wbd, psc,
                             cw, cb, shift, alog, dexp, ssn, g128, i, b, seq)
        lam_init = 0.8 - 0.6 * math.exp(-0.3 * i)
        yd = _attn(proj, qnw, knw, *lams, subw, g32, i, b, seq, lam_init)
        x2 = _ffn(x2, ya, yb, yc, yd, n2, w_out, ffn_w_gate, ffn_w_up, ffn_w_down, i, tm)
    return x2.reshape(b, seq, d)
```

```python
import functools
import math

import jax
import jax.numpy as jnp
from jax import lax
from jax.experimental import pallas as pl
from jax.experimental.pallas import tpu as pltpu

F32 = jnp.float32
BF16 = jnp.bfloat16

D_MODEL = 1024
GROUP_W = 256
CHUNK = 128
HEAD_DIM = 64
N_HEADS = 4
SSM_GROUPS = 2
SSM_D_STATE = 128
SSM_CONV_K = 4
SSM_CONV_DIM = GROUP_W + 2 * SSM_GROUPS * SSM_D_STATE
POOL_WINDOWS = (2, 4, 8, 16)
DA_QK_DIM = 32
D_FF = 2816
RMS_EPS = 1e-6
NEG_BIG = -1e30

LANES = 128
VMEM_LIMIT = 56 * 1024 * 1024

COL_PA = 0
COL_Z = 512
COL_XBC = 768
COL_PC = 1536
COL_Q = 1792
COL_K = 2048
COL_V = 2304
PROJ_W = 2560
DT_W = LANES
D_IN = PROJ_W + N_HEADS
DT_COL = COL_PC
IN_CHUNK = 256
IN_ROWS = 512
FFN_CHUNK = 256
SSD_CPS = 4

ATT_TQ = 512
ATT_TK = 512
ATT_QB = 512
ATT_BOUND_MARGIN = 1.01
ATT_ONE_PASS_BOUND = 40.0
VT_ROWS = 80
ATT_LANE_ALIBI = HEAD_DIM
ATT_LANE_BOUND = HEAD_DIM + 4
TINY = 1e-30


def _cparams(sem):
    return pltpu.CompilerParams(dimension_semantics=sem, vmem_limit_bytes=VMEM_LIMIT)


def _sigmoid(x):
    return 1.0 / (1.0 + jnp.exp(-x))


def _dot(a, b):
    return jnp.dot(a, b, preferred_element_type=F32)


def _dot_nt(a, b):
    return lax.dot_general(a, b, (((1,), (1,)), ((), ())), preferred_element_type=F32)


def _softplus(t):
    return jnp.maximum(t, 0.0) + jnp.log(1.0 + jnp.exp(-jnp.abs(t)))


def _inproj_kernel(x_ref, nw_ref, w_ref, dtb_ref, proj_ref, dt_ref, wb_ref):
    @pl.when(pl.program_id(0) == 0)
    def _():
        for n0 in range(0, DT_COL, 2 * IN_CHUNK):
            wb_ref[:, n0:n0 + 2 * IN_CHUNK] = w_ref[:, n0:n0 + 2 * IN_CHUNK].astype(BF16)
        tail = w_ref[:, DT_COL:D_IN]
        wb_ref[:, DT_COL:PROJ_W] = tail[:, N_HEADS:].astype(BF16)
        wb_ref[:, PROJ_W:PROJ_W + DT_W] = tail[:, 0:DT_W].astype(BF16)

    for r0 in range(0, x_ref.shape[0], IN_ROWS):
        rows = slice(r0, r0 + IN_ROWS)
        x = x_ref[rows, :]
        ms = jnp.mean(x * x, axis=-1, keepdims=True)
        h = (x * lax.rsqrt(ms + RMS_EPS) * nw_ref[...]).astype(BF16)
        for n0 in range(0, PROJ_W, IN_CHUNK):
            y = _dot(h, wb_ref[:, n0:n0 + IN_CHUNK])
            if n0 < COL_Z:
                y = jax.nn.gelu(y, approximate=True)
            elif n0 < COL_XBC:
                y = y * _sigmoid(y)
            proj_ref[rows, n0:n0 + IN_CHUNK] = y.astype(BF16)
        dt_ref[rows, :] = _softplus(_dot(h, wb_ref[:, PROJ_W:PROJ_W + DT_W]) + dtb_ref[...])


def _per_layer(shape, layer):
    return pl.BlockSpec((None,) + tuple(shape), lambda *_: (layer,) + (0,) * len(shape))


def _inproj(x2, nw, w_in, layer, dtb, tm):
    m = x2.shape[0]
    return pl.pallas_call(
        _inproj_kernel,
        grid=(m // tm,),
        in_specs=[
            pl.BlockSpec((tm, D_MODEL), lambda i: (i, 0)),
            _per_layer((1, D_MODEL), layer),
            pl.BlockSpec((None, D_MODEL, D_IN), lambda i: (layer, 0, 0), pipeline_mode=pl.Buffered(1)),
            _per_layer((1, DT_W), layer),
        ],
        out_specs=[
            pl.BlockSpec((tm, PROJ_W), lambda i: (i, 0)),
            pl.BlockSpec((tm, DT_W), lambda i: (i, 0)),
        ],
        out_shape=[
            jax.ShapeDtypeStruct((m, PROJ_W), BF16),
            jax.ShapeDtypeStruct((m, DT_W), F32),
        ],
        scratch_shapes=[pltpu.VMEM((D_MODEL, PROJ_W + DT_W), BF16)],
        compiler_params=_cparams(("arbitrary",)),
        name="inproj",
    )(x2, nw, w_in, dtb)


def _head_id(shape, width):
    lane = lax.broadcasted_iota(jnp.int32, shape, 1)
    return lax.shift_right_logical(lane, int(math.log2(width)))


def _gmlp_kernel(pa_ref, ws_ref, bsm_ref, nw_ref, g_ref, out_ref):
    t = pa_ref.shape[0]
    hact = pa_ref[...].astype(F32)
    u = hact[:, :GROUP_W]
    v = hact[:, GROUP_W:]
    ms = _dot((v * v).astype(BF16), g_ref[...])
    vn = (v * lax.rsqrt(ms + RMS_EPS) * nw_ref[...]).astype(BF16)
    row = lax.broadcasted_iota(jnp.int32, (CHUNK, CHUNK), 0)
    col = lax.broadcasted_iota(jnp.int32, (CHUNK, CHUNK), 1)
    wcat = jnp.concatenate(
        [jnp.where(row >= col, ws_ref[h], 0.0) for h in range(N_HEADS)], axis=1).astype(BF16)
    hid = _head_id((CHUNK, GROUP_W), HEAD_DIM)
    for c in range(t // CHUNK):
        vc = vn[c * CHUNK:(c + 1) * CHUNK]
        vstack = jnp.concatenate(
            [jnp.where(hid == h, vc, jnp.zeros_like(vc)) for h in range(N_HEADS)], axis=0)
        s = _dot(wcat, vstack) + bsm_ref[...]
        out_ref[c * CHUNK:(c + 1) * CHUNK, :] = (u[c * CHUNK:(c + 1) * CHUNK] * s).astype(BF16)


def _split3(a):
    a1 = a.astype(BF16)
    r1 = a - a1.astype(F32)
    a2 = r1.astype(BF16)
    r2 = r1 - a2.astype(F32)
    return a1, a2, r2.astype(BF16)


def _ssd_kernel(z_ref, xbc_ref, halo_ref, dt_ref, cw_ref, cb_ref, shift_ref, alog_ref, dexp_ref, nw_ref,
                g_ref, out_ref, st_ref, between_phases=None):
    c = pl.program_id(1)

    @pl.when(c == 0)
    def _():
        st_ref[...] = jnp.zeros_like(st_ref)

    row = lax.broadcasted_iota(jnp.int32, (CHUNK, CHUNK), 0)
    col = lax.broadcasted_iota(jnp.int32, (CHUNK, CHUNK), 1)
    causal = row >= col
    ltri = jnp.where(causal, 1.0, 0.0).astype(BF16)
    hid = _head_id((CHUNK, GROUP_W), HEAD_DIM)
    neg_a = -jnp.exp(alog_ref[...])

    def expand(c4):
        return jnp.where(hid == 0, c4[:, 0:1],
                         jnp.where(hid == 1, c4[:, 1:2],
                                   jnp.where(hid == 2, c4[:, 2:3], c4[:, 3:4])))

    chunks = range(SSD_CPS)
    rows = [slice(ck * CHUNK, (ck + 1) * CHUNK) for ck in chunks]
    groups = [slice(g * SSM_D_STATE, (g + 1) * SSM_D_STATE) for g in range(SSM_GROUPS)]

    xs, bm, cm = [], [], []
    halo = jnp.where(c > 0, halo_ref[...], jnp.zeros_like(halo_ref))
    for ck in chunks:
        xcur = xbc_ref[rows[ck], :]
        acc = cb_ref[...] + cw_ref[SSM_CONV_K - 1:SSM_CONV_K, :] * xcur.astype(F32)
        for j in range(1, SSM_CONV_K):
            k = SSM_CONV_K - 1 - j
            if ck == 0:
                shifted = (_dot(shift_ref[j - 1, :, 0:CHUNK], halo)
                           + _dot(shift_ref[j - 1, :, CHUNK:2 * CHUNK], xcur))
            else:
                shifted = _dot(shift_ref[j - 1], xbc_ref[(ck - 1) * CHUNK:(ck + 1) * CHUNK, :])
            acc = acc + cw_ref[k:k + 1, :] * shifted
        xc = acc * _sigmoid(acc)
        xs.append(xc[:, :GROUP_W])
        bm.append(xc[:, GROUP_W:2 * GROUP_W])
        cm.append(xc[:, 2 * GROUP_W:].astype(BF16))

    acs, acs_t = [], []
    for ck in chunks:
        a1, a2, a3 = _split3(dt_ref[rows[ck], :] * neg_a)
        acs.append(_dot(ltri, a1) + _dot(ltri, a2) + _dot(ltri, a3))
        acs_t.append(acs[ck].T)

    if between_phases is not None:
        between_phases()

    xdt, eacs_e, mcat, snew = [], [], [], []
    for ck in chunks:
        xdt.append(xs[ck] * expand(dt_ref[rows[ck], :]))
        eacs_e.append(expand(jnp.exp(acs[ck])))
        dte_e = expand(jnp.exp(acs[ck][CHUNK - 1:CHUNK, :] - acs[ck]))
        ms = []
        for g in range(SSM_GROUPS):
            cb = _dot_nt(cm[ck][:, groups[g]], bm[ck][:, groups[g]].astype(BF16))
            for hh in range(N_HEADS // SSM_GROUPS):
                h = g * (N_HEADS // SSM_GROUPS) + hh
                seg = acs[ck][:, h:h + 1] - acs_t[ck][h:h + 1, :]
                ms.append((cb * jnp.exp(jnp.where(causal, seg, NEG_BIG))).astype(BF16))
        mcat.append(jnp.concatenate(ms, axis=1))
        w = (xdt[ck] * dte_e).astype(BF16)
        snew.append([_dot(bm[ck][:, groups[g]].T.astype(BF16), w[:, groups[g]])
                     for g in range(SSM_GROUPS)])

    state = [st_ref[g] for g in range(SSM_GROUPS)]
    entering = []
    for ck in chunks:
        entering.append([st.astype(BF16) for st in state])
        cdl = eacs_e[ck][CHUNK - 1:CHUNK, :]
        state = [state[g] * cdl[:, groups[g]] + snew[ck][g] for g in range(SSM_GROUPS)]
    for g in range(SSM_GROUPS):
        st_ref[g] = state[g]

    ys = []
    for ck in chunks:
        xdt_b = xdt[ck].astype(BF16)
        xstack = jnp.concatenate(
            [jnp.where(hid == h, xdt_b, jnp.zeros_like(xdt_b)) for h in range(N_HEADS)], axis=0)
        yoff = jnp.concatenate([_dot(cm[ck][:, groups[g]], entering[ck][g]) for g in range(SSM_GROUPS)], axis=1)
        y = _dot(mcat[ck], xstack) + yoff * eacs_e[ck] + dexp_ref[...] * xs[ck]
        ys.append(y * z_ref[rows[ck], :].astype(F32))

    for ck in chunks:
        msq = _dot((ys[ck] * ys[ck]).astype(BF16), g_ref[...])
        out_ref[rows[ck], :] = (ys[ck] * lax.rsqrt(msq + RMS_EPS) * nw_ref[...]).astype(BF16)


def _conv_shift_matrices():
    r = jnp.arange(CHUNK)[:, None]
    col = jnp.arange(2 * CHUNK)[None, :]
    return jnp.stack([(col == CHUNK + r - j) for j in range(1, SSM_CONV_K)]).astype(BF16)


def _pool_bands():
    r = jnp.arange(CHUNK)[:, None]
    col = jnp.arange(2 * CHUNK)[None, :]
    back = CHUNK + r - col
    bands = [jnp.where((back >= 0) & (back < w), 1.0 / w, 0.0) for w in POOL_WINDOWS]
    bandcat = jnp.concatenate(bands, axis=1).astype(BF16)
    t = jnp.arange(CHUNK, dtype=F32)[:, None] + 1.0
    win = jnp.repeat(jnp.asarray(POOL_WINDOWS, F32), HEAD_DIM)[None, :]
    fix = win / jnp.minimum(win, t)
    return bandcat, fix


def _pool_chunks(pc_ref, pch_ref, band_ref, fix_ref, wbd_ref, scale_ref, out_ref, first_step):
    hid = _head_id((CHUNK, GROUP_W), HEAD_DIM)
    prev = jnp.where(first_step, jnp.zeros_like(pch_ref), pch_ref[...])
    for ck in range(pc_ref.shape[0] // CHUNK):
        cur = pc_ref[ck * CHUNK:(ck + 1) * CHUNK, :]
        parts = []
        for g in range(len(POOL_WINDOWS)):
            parts += [jnp.where(hid == g, prev, jnp.zeros_like(prev)),
                      jnp.where(hid == g, cur, jnp.zeros_like(cur))]
        mean = _dot(band_ref[...], jnp.concatenate(parts, axis=0))
        if ck == 0:
            mean = jnp.where(first_step, mean * fix_ref[...], mean)
        p = mean - cur.astype(F32)
        y = _dot(p.astype(BF16), wbd_ref[...]) * scale_ref[...]
        out_ref[ck * CHUNK:(ck + 1) * CHUNK, :] = y.astype(BF16)
        prev = cur


N_GMLP_IN, N_POOL_IN, N_SSD_IN = 5, 6, 11


def _mixers_kernel(*refs):
    gm = refs[:N_GMLP_IN]
    po = refs[N_GMLP_IN:N_GMLP_IN + N_POOL_IN]
    sd = refs[N_GMLP_IN + N_POOL_IN:N_GMLP_IN + N_POOL_IN + N_SSD_IN]
    ya_ref, yc_ref, yb_ref, st_ref = refs[N_GMLP_IN + N_POOL_IN + N_SSD_IN:]
    first_step = pl.program_id(1) == 0

    def others():
        _gmlp_kernel(*gm, ya_ref)
        _pool_chunks(*po, yc_ref, first_step)

    _ssd_kernel(*sd, yb_ref, st_ref, between_phases=others)


def _mixers(proj, dt, gws, bsm, gmn, g64, band, fix, wbd, psc, cw, cb, shift, alog, dexp, nw, g128,
            layer, b, seq):
    m = proj.shape[0]
    ts = SSD_CPS * CHUNK
    assert seq % ts == 0
    nc = seq // ts
    const = lambda i, c: (0, 0)
    blk = lambda width, col: pl.BlockSpec((ts, width), lambda i, c: (i * nc + c, col // width))
    halo = lambda width, col: pl.BlockSpec(
        (CHUNK, width), lambda i, c: (jnp.maximum((i * nc + c) * SSD_CPS - 1, 0), col // width))
    out = pl.BlockSpec((ts, GROUP_W), lambda i, c: (i * nc + c, 0))
    gmlp_in = [blk(2 * GROUP_W, COL_PA), _per_layer((N_HEADS, CHUNK, CHUNK), layer),
               _per_layer((CHUNK, GROUP_W), layer), _per_layer((1, GROUP_W), layer),
               pl.BlockSpec((GROUP_W, GROUP_W), const)]
    pool_in = [blk(GROUP_W, COL_PC), halo(GROUP_W, COL_PC),
               pl.BlockSpec((CHUNK, len(POOL_WINDOWS) * 2 * CHUNK), const), pl.BlockSpec((CHUNK, GROUP_W), const),
               _per_layer((GROUP_W, GROUP_W), layer), _per_layer((1, GROUP_W), layer)]
    ssd_in = [blk(GROUP_W, COL_Z), blk(SSM_CONV_DIM, COL_XBC), halo(SSM_CONV_DIM, COL_XBC),
              pl.BlockSpec((ts, DT_W), lambda i, c: (i * nc + c, 0)),
              _per_layer((SSM_CONV_K, SSM_CONV_DIM), layer), _per_layer((1, SSM_CONV_DIM), layer),
              pl.BlockSpec((SSM_CONV_K - 1, CHUNK, 2 * CHUNK), lambda i, c: (0, 0, 0)),
              _per_layer((1, DT_W), layer), _per_layer((1, GROUP_W), layer), _per_layer((1, GROUP_W), layer),
              pl.BlockSpec((GROUP_W, GROUP_W), const)]
    assert (len(gmlp_in), len(pool_in), len(ssd_in)) == (N_GMLP_IN, N_POOL_IN, N_SSD_IN)
    return pl.pallas_call(
        _mixers_kernel,
        grid=(b, nc),
        in_specs=gmlp_in + pool_in + ssd_in,
        out_specs=[out, out, out],
        out_shape=[jax.ShapeDtypeStruct((m, GROUP_W), BF16)] * 3,
        scratch_shapes=[pltpu.VMEM((SSM_GROUPS, SSM_D_STATE, LANES), F32)],
        compiler_params=_cparams(("parallel", "arbitrary")),
        name="mixers",
    )(proj, gws, bsm, gmn, g64,
      proj, proj, band, fix, wbd, psc,
      proj, proj, proj, dt, cw, cb, shift, alog, dexp, nw, g128)


def _alibi_slope(h):
    return 2.0 ** (-8.0 * (h + 1) / N_HEADS)


def _head_slot(x, h):
    base = x[:, (h // 2) * LANES:(h // 2 + 1) * LANES]
    return pltpu.roll(base, HEAD_DIM, 1) if h % 2 else base


ATT_PAIR = 4


def _attn_kernel(q_ref, k_ref, v_ref, qnw_ref, knw_ref, lq1_ref, lk1_ref, lq2_ref, lk2_ref, subw_ref,
                 g_ref, out_ref, ka_ref, vt_ref, qs_ref, acc_ref, s0_ref, kmax_ref, qa_ref, *, lam_init):
    pj = pl.program_id(1)
    tq, tk = ATT_TQ, ATT_TK
    seq = k_ref.shape[0]

    def aug_lanes(pos, slope, key_side):
        lane = lax.broadcasted_iota(jnp.int32, pos.shape, 1) - ATT_LANE_ALIBI
        hi = (slope * LANES) * lax.shift_right_logical(pos, int(math.log2(LANES))).astype(F32)
        lo = slope * (pos & (LANES - 1)).astype(F32)
        one = jnp.ones_like(hi)
        c = (one, one, hi, lo) if key_side else (-hi, -lo, one, one)
        last = jnp.where(lane == ATT_LANE_BOUND - ATT_LANE_ALIBI, 1.0, 0.0) if key_side else 0.0
        return jnp.where(lane == 0, c[0], jnp.where(lane == 1, c[1],
                         jnp.where(lane == 2, c[2], jnp.where(lane == 3, c[3], last))))

    @pl.when(pj == 0)
    def _():
        k = k_ref[...].astype(F32)
        ms = _dot((k * k).astype(BF16), g_ref[...])
        kn = k * lax.rsqrt(ms + RMS_EPS) * knw_ref[...]
        ksq = _dot((kn * kn).astype(BF16), g_ref[...]) * DA_QK_DIM
        kmax_ref[...] = jnp.broadcast_to(jnp.max(ksq, axis=0, keepdims=True), kmax_ref.shape)
        pos = lax.broadcasted_iota(jnp.int32, (seq, LANES), 0)
        lane = lax.broadcasted_iota(jnp.int32, (seq, LANES), 1)
        for h in range(N_HEADS):
            aug = aug_lanes(pos, _alibi_slope(h), True)
            ka_ref[h] = jnp.where(lane < HEAD_DIM, _head_slot(kn, h), aug).astype(BF16)
            qa_ref[h] = aug_lanes(pos, _alibi_slope(h), False).astype(BF16)
        tail = (lax.broadcasted_iota(jnp.int32, (VT_ROWS - HEAD_DIM, tk), 0) == 0).astype(BF16)
        for j in range(seq // tk):
            vt = v_ref[j * tk:(j + 1) * tk, :].astype(F32).T.astype(BF16)
            for h in range(N_HEADS):
                vt_ref[j, h, 0:HEAD_DIM, :] = vt[h * HEAD_DIM:(h + 1) * HEAD_DIM, :]
                vt_ref[j, h, HEAD_DIM:VT_ROWS, :] = tail

    lane = lax.broadcasted_iota(jnp.int32, (1, LANES), 1)

    def normalised_q(t):
        q = q_ref[t * tq:(t + 1) * tq, :].astype(F32)
        ms = _dot((q * q).astype(BF16), g_ref[...])
        qn = q * lax.rsqrt(ms + RMS_EPS) * (qnw_ref[...] * (DA_QK_DIM ** -0.5))
        qsq = _dot((qn * qn).astype(BF16), g_ref[...]) * DA_QK_DIM
        b2 = qsq * kmax_ref[0:1, :]
        bound = b2 * lax.rsqrt(b2 + TINY) * ATT_BOUND_MARGIN
        return qn, bound, jnp.max(bound) <= ATT_ONE_PASS_BOUND

    def build_queries(t, qi, qn, bound):
        q0 = pl.multiple_of(qi * tq, tq)
        for h in range(N_HEADS):
            base = _head_slot(qn, h)
            bnd = _head_slot(bound, h)
            qa = qa_ref[h, pl.ds(q0, tq), :]
            for comp in range(2):
                own = (lane >= comp * DA_QK_DIM) & (lane < (comp + 1) * DA_QK_DIM)
                y = jnp.where(own, base, 0.0)
                y = jnp.where(lane == ATT_LANE_BOUND, -bnd[:, comp * DA_QK_DIM:comp * DA_QK_DIM + 1], y)
                qs_ref[t, h, comp * tq:(comp + 1) * tq, :] = y.astype(BF16) + qa
                yield

    qb = ATT_QB
    items = [(h, n) for h in range(N_HEADS) for n in range(2 * tq // qb)]

    def scores(t, item, j):
        h, n = item
        k0 = pl.multiple_of(j * tk, tk)
        return _dot_nt(ka_ref[h, pl.ds(k0, tk), :], qs_ref[t, h, n * qb:(n + 1) * qb, :])

    def causal_mask(s, n):
        kk = lax.broadcasted_iota(jnp.int32, (tk, qb), 0)
        qq = (lax.broadcasted_iota(jnp.int32, (tk, qb), 1) + n * qb) & (tq - 1)
        return jnp.where(kk <= qq, s, NEG_BIG)

    def tile(t, j, ms_, masked):
        new_m = []
        s_next = s0_ref[...]
        for i, (h, n) in enumerate(items):
            cols = slice(n * qb, (n + 1) * qb)
            s = s_next
            if i + 1 < len(items):
                s_next = scores(t, items[i + 1], j)
            elif not masked:
                s0_ref[...] = scores(t, items[0], j + 1)
            if masked:
                s = causal_mask(s, n)
            m_old = ms_[i]
            m_new = jnp.maximum(m_old, jnp.max(s, axis=0, keepdims=True))
            alpha = jnp.exp(m_old - m_new)
            p = jnp.exp(s - m_new)
            new_m.append(m_new)
            acc_ref[t, h, :, cols] = acc_ref[t, h, :, cols] * alpha + _dot(vt_ref[j, h], p.astype(BF16))
        return tuple(new_m)

    def tile_one_pass(t, j, masked, filler=None):
        s_next = s0_ref[...]
        for i, (h, n) in enumerate(items):
            cols = slice(n * qb, (n + 1) * qb)
            s = s_next
            if i + 1 < len(items):
                s_next = scores(t, items[i + 1], j)
            elif not masked:
                s0_ref[...] = scores(t, items[0], j + 1)
            if masked:
                s = causal_mask(s, n)
            acc_ref[t, h, :, cols] += _dot(vt_ref[j, h], jnp.exp(s).astype(BF16))
            if filler is not None:
                next(filler, None)

    def attend(t, qi, one_pass, other_work):
        acc_ref[t] = jnp.zeros(acc_ref.shape[1:], F32)
        s0_ref[...] = scores(t, items[0], 0)

        @pl.when(one_pass)
        def _():
            def body(j, carry):
                tile_one_pass(t, j, False)
                return carry
            lax.fori_loop(0, qi, body, 0)
            filler = other_work()
            tile_one_pass(t, qi, True, filler)
            for _ in filler:
                pass

        @pl.when(jnp.logical_not(one_pass))
        def _():
            m0 = tuple(jnp.full((1, qb), NEG_BIG, F32) for _ in items)
            m1 = lax.fori_loop(0, qi, lambda j, c: tile(t, j, c, False), m0)
            tile(t, qi, m1, True)
            for _ in other_work():
                pass

    def lane_sum(x):
        return jnp.broadcast_to(jnp.sum(x, axis=1, keepdims=True), x.shape)

    lam = (jnp.exp(lane_sum(lq1_ref[...] * lk1_ref[...]))
           - jnp.exp(lane_sum(lq2_ref[...] * lk2_ref[...])) + lam_init)
    lam = jnp.concatenate([lam] * (tq // LANES), axis=1)

    def finish(t):
        outs = []
        for h in range(N_HEADS):
            o = acc_ref[t, h, 0:HEAD_DIM, :] / acc_ref[t, h, HEAD_DIM:HEAD_DIM + 1, :]
            oh = o[:, 0:tq] - lam * o[:, tq:2 * tq]
            ms = jnp.mean(oh * oh, axis=0, keepdims=True)
            outs.append(oh * lax.rsqrt(ms + RMS_EPS) * (subw_ref[...] * (1.0 - lam_init)))
            yield
        out_ref[t * tq:(t + 1) * tq, :] = jnp.concatenate(outs, axis=0).T.astype(BF16)
        yield

    def chained(makers):
        def run():
            for make in makers:
                yield from make()
        return run

    prepared = [normalised_q(t) for t in range(ATT_PAIR)]
    for _ in build_queries(0, pj * ATT_PAIR, *prepared[0][:2]):
        pass
    for t in range(ATT_PAIR):
        qi = pj * ATT_PAIR + t
        work = []
        if t + 1 < ATT_PAIR:
            work.append(functools.partial(build_queries, t + 1, qi + 1, *prepared[t + 1][:2]))
        if t > 0:
            work.append(functools.partial(finish, t - 1))
        attend(t, qi, prepared[t][2], chained(work))
    for _ in finish(ATT_PAIR - 1):
        pass


def _attn(proj, qnw, knw, lq1, lk1, lq2, lk2, subw, g32, layer, b, seq, lam_init):
    m = proj.shape[0]
    assert ATT_TQ == ATT_TK and seq % (ATT_PAIR * ATT_TQ) == 0
    tp = ATT_PAIR * ATT_TQ
    nq = seq // tp
    const = lambda i, j: (0, 0)
    return pl.pallas_call(
        functools.partial(_attn_kernel, lam_init=lam_init),
        grid=(b, nq),
        in_specs=[
            pl.BlockSpec((tp, GROUP_W), lambda i, j: (i * nq + j, COL_Q // GROUP_W)),
            pl.BlockSpec((seq, GROUP_W), lambda i, j: (i, COL_K // GROUP_W)),
            pl.BlockSpec((seq, GROUP_W), lambda i, j: (i, COL_V // GROUP_W)),
            _per_layer((1, GROUP_W), layer),
            _per_layer((1, GROUP_W), layer),
            _per_layer((1, LANES), layer),
            _per_layer((1, LANES), layer),
            _per_layer((1, LANES), layer),
            _per_layer((1, LANES), layer),
            _per_layer((HEAD_DIM, ATT_TQ), layer),
            pl.BlockSpec((GROUP_W, GROUP_W), const),
        ],
        out_specs=pl.BlockSpec((tp, GROUP_W), lambda i, j: (i * nq + j, 0)),
        out_shape=jax.ShapeDtypeStruct((m, GROUP_W), BF16),
        scratch_shapes=[
            pltpu.VMEM((N_HEADS, seq, LANES), BF16),
            pltpu.VMEM((seq // ATT_TK, N_HEADS, VT_ROWS, ATT_TK), BF16),
            pltpu.VMEM((ATT_PAIR, N_HEADS, 2 * ATT_TQ, LANES), BF16),
            pltpu.VMEM((ATT_PAIR, N_HEADS, VT_ROWS, 2 * ATT_TQ), F32),
            pltpu.VMEM((ATT_TK, ATT_QB), F32),
            pltpu.VMEM((8, GROUP_W), F32),
            pltpu.VMEM((N_HEADS, seq, LANES), BF16),
        ],
        compiler_params=_cparams(("parallel", "arbitrary")),
        name="diffattn",
    )(proj, proj, proj, qnw, knw, lq1, lk1, lq2, lk2, subw, g32)


FFN_STAGE_WIDE = 256
FFN_STAGE_TALL = 704


def _ffn_weight_jobs(layer, wo_hbm, wg_hbm, wu_hbm, wd_hbm, wo_b, wg_b, wu_b, wd_b, wide, tall, sem_w, sem_t):
    jobs = []

    def add(src, dst, stage, sem, rows, n_rows):
        for k, r0 in enumerate(range(0, n_rows, rows)):
            r = min(rows, n_rows - r0)
            slot = k % 2
            view = stage.at[slot, 0:r, :]
            copy = pltpu.make_async_copy(src.at[layer, r0:r0 + r, :], view, sem.at[slot])
            jobs.append((copy, view, dst.at[r0:r0 + r, :]))

    add(wg_hbm, wg_b, wide, sem_w, FFN_STAGE_WIDE, D_MODEL)
    add(wu_hbm, wu_b, wide, sem_w, FFN_STAGE_WIDE, D_MODEL)
    add(wd_hbm, wd_b, tall, sem_t, FFN_STAGE_TALL, D_FF)
    add(wo_hbm, wo_b, tall, sem_t, FFN_STAGE_TALL, D_MODEL)
    return jobs


def _ffn_kernel(x_ref, ya_ref, yb_ref, yc_ref, yd_ref, nw_ref, wo_hbm, wg_hbm, wu_hbm, wd_hbm,
                out_ref, act_ref, wo_ref, wg_ref, wu_ref, wd_ref, wide_ref, tall_ref, sem_w, sem_t, *, layer):
    @pl.when(pl.program_id(0) == 0)
    def _():
        jobs = _ffn_weight_jobs(layer, wo_hbm, wg_hbm, wu_hbm, wd_hbm, wo_ref, wg_ref, wu_ref, wd_ref,
                                wide_ref, tall_ref, sem_w, sem_t)
        jobs[0][0].start()
        for k, (copy, view, dst) in enumerate(jobs):
            if k + 1 < len(jobs):
                jobs[k + 1][0].start()
            copy.wait()
            dst[...] = view[...].astype(BF16)

    x1 = x_ref[...]
    for i, y_ref in enumerate((ya_ref, yb_ref, yc_ref, yd_ref)):
        x1 = x1 + _dot(y_ref[...], wo_ref[i * GROUP_W:(i + 1) * GROUP_W, :])
    ms = jnp.mean(x1 * x1, axis=-1, keepdims=True)
    h = (x1 * lax.rsqrt(ms + RMS_EPS) * nw_ref[...]).astype(BF16)
    for c0 in range(0, D_FF, FFN_CHUNK):
        c1 = min(c0 + FFN_CHUNK, D_FF)
        g = _dot(h, wg_ref[:, c0:c1])
        u = _dot(h, wu_ref[:, c0:c1])
        act_ref[:, c0:c1] = (g * _sigmoid(g) * u).astype(BF16)
    out_ref[...] = x1 + _dot(act_ref[...], wd_ref[...])


def _ffn(x2, ya, yb, yc, yd, nw, wo, wg, wu, wd, layer, tm):
    m = x2.shape[0]
    row = lambda i: (i, 0)
    hbm = pl.BlockSpec(memory_space=pl.ANY)
    return pl.pallas_call(
        functools.partial(_ffn_kernel, layer=layer),
        grid=(m // tm,),
        in_specs=[
            pl.BlockSpec((tm, D_MODEL), row),
            pl.BlockSpec((tm, GROUP_W), row),
            pl.BlockSpec((tm, GROUP_W), row),
            pl.BlockSpec((tm, GROUP_W), row),
            pl.BlockSpec((tm, GROUP_W), row),
            _per_layer((1, D_MODEL), layer),
            hbm, hbm, hbm, hbm,
        ],
        out_specs=pl.BlockSpec((tm, D_MODEL), row),
        out_shape=jax.ShapeDtypeStruct((m, D_MODEL), F32),
        scratch_shapes=[
            pltpu.VMEM((tm, D_FF), BF16),
            pltpu.VMEM((D_MODEL, D_MODEL), BF16),
            pltpu.VMEM((D_MODEL, D_FF), BF16),
            pltpu.VMEM((D_MODEL, D_FF), BF16),
            pltpu.VMEM((D_FF, D_MODEL), BF16),
            pltpu.VMEM((2, FFN_STAGE_WIDE, D_FF), F32),
            pltpu.VMEM((2, FFN_STAGE_TALL, D_MODEL), F32),
            pltpu.SemaphoreType.DMA((2,)),
            pltpu.SemaphoreType.DMA((2,)),
        ],
        compiler_params=_cparams(("arbitrary",)),
        name="outproj_ffn",
    )(x2, ya, yb, yc, yd, nw, wo, wg, wu, wd)


def _block_diag_mean(width, group):
    idx = jnp.arange(width) // group
    return jnp.where(idx[:, None] == idx[None, :], 1.0 / group, 0.0).astype(BF16)


def _rows(v, width=None):
    v = v.reshape(v.shape[0], 1, -1).astype(F32)
    return v if width is None else jnp.pad(v, ((0, 0), (0, 0), (0, width - v.shape[2])))


def kernel(x, norm1_w, w_in, gm_norm_w, gm_ws, gm_bs, ssm_conv_w, ssm_conv_b, ssm_dt_bias, ssm_a_log, ssm_d, ssm_norm_w, pool_w, pool_scale, da_q_norm_w, da_k_norm_w, da_lambda_q1, da_lambda_k1, da_lambda_q2, da_lambda_k2, da_subln_w, w_out, norm2_w, ffn_w_gate, ffn_w_up, ffn_w_down):
    b, seq, d = x.shape
    depth = w_in.shape[0]
    assert d == D_MODEL and seq % ATT_TQ == 0 and seq % CHUNK == 0
    m = b * seq
    tm = 512 if m % 512 == 0 else ATT_TQ

    g64 = _block_diag_mean(GROUP_W, HEAD_DIM)
    g128 = _block_diag_mean(GROUP_W, SSM_D_STATE)
    g32 = _block_diag_mean(GROUP_W, DA_QK_DIM)

    shift = _conv_shift_matrices()
    band, fix = _pool_bands()

    assert w_in.shape[1:] == (D_MODEL, D_IN)

    n1, n2 = _rows(norm1_w), _rows(norm2_w)
    dtb, alog = _rows(ssm_dt_bias, DT_W), _rows(ssm_a_log, DT_W)
    bsm = jnp.repeat(jnp.swapaxes(gm_bs, 1, 2), HEAD_DIM, axis=2)
    gmn = _rows(gm_norm_w)
    cw, cb = jnp.swapaxes(ssm_conv_w, 1, 2), _rows(ssm_conv_b)
    dexp, ssn = _rows(jnp.repeat(ssm_d, HEAD_DIM, axis=1)), _rows(ssm_norm_w)
    eye = jnp.eye(len(POOL_WINDOWS), dtype=pool_w.dtype)
    wbd = jnp.einsum('lgab,gh->lgahb', pool_w, eye).reshape(depth, GROUP_W, GROUP_W).astype(BF16)
    psc = _rows(pool_scale)
    qnw = _rows(jnp.tile(da_q_norm_w, (1, GROUP_W // DA_QK_DIM)))
    knw = _rows(jnp.tile(da_k_norm_w, (1, GROUP_W // DA_QK_DIM)))
    lams = [_rows(v, LANES) for v in (da_lambda_q1, da_lambda_k1, da_lambda_q2, da_lambda_k2)]
    subw = jnp.broadcast_to(da_subln_w[:, :, None], (depth, HEAD_DIM, ATT_TQ))

    x2 = x.reshape(m, d)
    for i in range(depth):
        proj, dt = _inproj(x2, n1, w_in, i, dtb, 2 * IN_ROWS if m % (2 * IN_ROWS) == 0 else IN_ROWS)
        ya, yc, yb = _mixers(proj, dt, gm_ws, bsm, gmn, g64, band, fix, wbd, psc,
                             cw, cb, shift, alog, dexp, ssn, g128, i, b, seq)
        lam_init = 0.8 - 0.6 * math.exp(-0.3 * i)
        yd = _attn(proj, qnw, knw, *lams, subw, g32, i, b, seq, lam_init)
        x2 = _ffn(x2, ya, yb, yc, yd, n2, w_out, ffn_w_gate, ffn_w_up, ffn_w_down, i, tm)
    return x2.reshape(b, seq, d)
```

```python
import functools
import math

import jax
import jax.numpy as jnp
from jax import lax
from jax.experimental import pallas as pl
from jax.experimental.pallas import tpu as pltpu

F32 = jnp.float32
BF16 = jnp.bfloat16

D_MODEL = 1024
GROUP_W = 256
CHUNK = 128
HEAD_DIM = 64
N_HEADS = 4
SSM_GROUPS = 2
SSM_D_STATE = 128
SSM_CONV_K = 4
SSM_CONV_DIM = GROUP_W + 2 * SSM_GROUPS * SSM_D_STATE
POOL_WINDOWS = (2, 4, 8, 16)
DA_QK_DIM = 32
D_FF = 2816
RMS_EPS = 1e-6
NEG_BIG = -1e30

LANES = 128
VMEM_LIMIT = 56 * 1024 * 1024

COL_PA = 0
COL_Z = 512
COL_XBC = 768
COL_PC = 1536
COL_Q = 1792
COL_K = 2048
COL_V = 2304
PROJ_W = 2560
DT_W = LANES
D_IN = PROJ_W + N_HEADS
DT_COL = COL_PC
IN_CHUNK = 256
IN_ROWS = 512
FFN_CHUNK = 256
SSD_CPS = 4

ATT_TQ = 512
ATT_TK = 512
ATT_QB = 512
ATT_BOUND_MARGIN = 1.01
ATT_ONE_PASS_BOUND = 40.0
VT_ROWS = 80
ATT_LANE_ALIBI = HEAD_DIM
ATT_LANE_BOUND = HEAD_DIM + 4
TINY = 1e-30


def _cparams(sem):
    return pltpu.CompilerParams(dimension_semantics=sem, vmem_limit_bytes=VMEM_LIMIT)


def _sigmoid(x):
    return 1.0 / (1.0 + jnp.exp(-x))


def _dot(a, b):
    return jnp.dot(a, b, preferred_element_type=F32)


def _dot_nt(a, b):
    return lax.dot_general(a, b, (((1,), (1,)), ((), ())), preferred_element_type=F32)


def _softplus(t):
    return jnp.maximum(t, 0.0) + jnp.log(1.0 + jnp.exp(-jnp.abs(t)))


def _inproj_kernel(x_ref, nw_ref, w_ref, dtb_ref, proj_ref, dt_ref, wb_ref):
    @pl.when(pl.program_id(0) == 0)
    def _():
        for n0 in range(0, DT_COL, 2 * IN_CHUNK):
            wb_ref[:, n0:n0 + 2 * IN_CHUNK] = w_ref[:, n0:n0 + 2 * IN_CHUNK].astype(BF16)
        tail = w_ref[:, DT_COL:D_IN]
        wb_ref[:, DT_COL:PROJ_W] = tail[:, N_HEADS:].astype(BF16)
        wb_ref[:, PROJ_W:PROJ_W + DT_W] = tail[:, 0:DT_W].astype(BF16)

    for r0 in range(0, x_ref.shape[0], IN_ROWS):
        rows = slice(r0, r0 + IN_ROWS)
        x = x_ref[rows, :]
        ms = jnp.mean(x * x, axis=-1, keepdims=True)
        h = (x * lax.rsqrt(ms + RMS_EPS) * nw_ref[...]).astype(BF16)
        for n0 in range(0, PROJ_W, IN_CHUNK):
            y = _dot(h, wb_ref[:, n0:n0 + IN_CHUNK])
            if n0 < COL_Z:
                y = jax.nn.gelu(y, approximate=True)
            elif n0 < COL_XBC:
                y = y * _sigmoid(y)
            proj_ref[rows, n0:n0 + IN_CHUNK] = y.astype(BF16)
        dt_ref[rows, :] = _softplus(_dot(h, wb_ref[:, PROJ_W:PROJ_W + DT_W]) + dtb_ref[...])


def _per_layer(shape, layer):
    return pl.BlockSpec((None,) + tuple(shape), lambda *_: (layer,) + (0,) * len(shape))


def _inproj(x2, nw, w_in, layer, dtb, tm):
    m = x2.shape[0]
    return pl.pallas_call(
        _inproj_kernel,
        grid=(m // tm,),
        in_specs=[
            pl.BlockSpec((tm, D_MODEL), lambda i: (i, 0)),
            _per_layer((1, D_MODEL), layer),
            pl.BlockSpec((None, D_MODEL, D_IN), lambda i: (layer, 0, 0), pipeline_mode=pl.Buffered(1)),
            _per_layer((1, DT_W), layer),
        ],
        out_specs=[
            pl.BlockSpec((tm, PROJ_W), lambda i: (i, 0)),
            pl.BlockSpec((tm, DT_W), lambda i: (i, 0)),
        ],
        out_shape=[
            jax.ShapeDtypeStruct((m, PROJ_W), BF16),
            jax.ShapeDtypeStruct((m, DT_W), F32),
        ],
        scratch_shapes=[pltpu.VMEM((D_MODEL, PROJ_W + DT_W), BF16)],
        compiler_params=_cparams(("arbitrary",)),
        name="inproj",
    )(x2, nw, w_in, dtb)


def _head_id(shape, width):
    lane = lax.broadcasted_iota(jnp.int32, shape, 1)
    return lax.shift_right_logical(lane, int(math.log2(width)))


def _gmlp_kernel(pa_ref, ws_ref, bsm_ref, nw_ref, g_ref, out_ref):
    t = pa_ref.shape[0]
    hact = pa_ref[...].astype(F32)
    u = hact[:, :GROUP_W]
    v = hact[:, GROUP_W:]
    ms = _dot((v * v).astype(BF16), g_ref[...])
    vn = (v * lax.rsqrt(ms + RMS_EPS) * nw_ref[...]).astype(BF16)
    row = lax.broadcasted_iota(jnp.int32, (CHUNK, CHUNK), 0)
    col = lax.broadcasted_iota(jnp.int32, (CHUNK, CHUNK), 1)
    wcat = jnp.concatenate(
        [jnp.where(row >= col, ws_ref[h], 0.0) for h in range(N_HEADS)], axis=1).astype(BF16)
    hid = _head_id((CHUNK, GROUP_W), HEAD_DIM)
    for c in range(t // CHUNK):
        vc = vn[c * CHUNK:(c + 1) * CHUNK]
        vstack = jnp.concatenate(
            [jnp.where(hid == h, vc, jnp.zeros_like(vc)) for h in range(N_HEADS)], axis=0)
        s = _dot(wcat, vstack) + bsm_ref[...]
        out_ref[c * CHUNK:(c + 1) * CHUNK, :] = (u[c * CHUNK:(c + 1) * CHUNK] * s).astype(BF16)


def _split3(a):
    a1 = a.astype(BF16)
    r1 = a - a1.astype(F32)
    a2 = r1.astype(BF16)
    r2 = r1 - a2.astype(F32)
    return a1, a2, r2.astype(BF16)


def _ssd_kernel(z_ref, xbc_ref, halo_ref, dt_ref, cw_ref, cb_ref, shift_ref, alog_ref, dexp_ref, nw_ref,
                g_ref, out_ref, st_ref, between_phases=None):
    c = pl.program_id(1)

    @pl.when(c == 0)
    def _():
        st_ref[...] = jnp.zeros_like(st_ref)

    row = lax.broadcasted_iota(jnp.int32, (CHUNK, CHUNK), 0)
    col = lax.broadcasted_iota(jnp.int32, (CHUNK, CHUNK), 1)
    causal = row >= col
    ltri = jnp.where(causal, 1.0, 0.0).astype(BF16)
    hid = _head_id((CHUNK, GROUP_W), HEAD_DIM)
    neg_a = -jnp.exp(alog_ref[...])

    def expand(c4):
        return jnp.where(hid == 0, c4[:, 0:1],
                         jnp.where(hid == 1, c4[:, 1:2],
                                   jnp.where(hid == 2, c4[:, 2:3], c4[:, 3:4])))

    chunks = range(SSD_CPS)
    rows = [slice(ck * CHUNK, (ck + 1) * CHUNK) for ck in chunks]
    groups = [slice(g * SSM_D_STATE, (g + 1) * SSM_D_STATE) for g in range(SSM_GROUPS)]

    xs, bm, cm = [], [], []
    halo = jnp.where(c > 0, halo_ref[...], jnp.zeros_like(halo_ref))
    for ck in chunks:
        xcur = xbc_ref[rows[ck], :]
        acc = cb_ref[...] + cw_ref[SSM_CONV_K - 1:SSM_CONV_K, :] * xcur.astype(F32)
        for j in range(1, SSM_CONV_K):
            k = SSM_CONV_K - 1 - j
            if ck == 0:
                shifted = (_dot(shift_ref[j - 1, :, 0:CHUNK], halo)
                           + _dot(shift_ref[j - 1, :, CHUNK:2 * CHUNK], xcur))
            else:
                shifted = _dot(shift_ref[j - 1], xbc_ref[(ck - 1) * CHUNK:(ck + 1) * CHUNK, :])
            acc = acc + cw_ref[k:k + 1, :] * shifted
        xc = acc * _sigmoid(acc)
        xs.append(xc[:, :GROUP_W])
        bm.append(xc[:, GROUP_W:2 * GROUP_W])
        cm.append(xc[:, 2 * GROUP_W:].astype(BF16))

    acs, acs_t = [], []
    for ck in chunks:
        a1, a2, a3 = _split3(dt_ref[rows[ck], :] * neg_a)
        acs.append(_dot(ltri, a1) + _dot(ltri, a2) + _dot(ltri, a3))
        acs_t.append(acs[ck].T)

    if between_phases is not None:
        between_phases()

    xdt, eacs_e, mcat, snew = [], [], [], []
    for ck in chunks:
        xdt.append(xs[ck] * expand(dt_ref[rows[ck], :]))
        eacs_e.append(expand(jnp.exp(acs[ck])))
        dte_e = expand(jnp.exp(acs[ck][CHUNK - 1:CHUNK, :] - acs[ck]))
        ms = []
        for g in range(SSM_GROUPS):
            cb = _dot_nt(cm[ck][:, groups[g]], bm[ck][:, groups[g]].astype(BF16))
            for hh in range(N_HEADS // SSM_GROUPS):
                h = g * (N_HEADS // SSM_GROUPS) + hh
                seg = acs[ck][:, h:h + 1] - acs_t[ck][h:h + 1, :]
                ms.append((cb * jnp.exp(jnp.where(causal, seg, NEG_BIG))).astype(BF16))
        mcat.append(jnp.concatenate(ms, axis=1))
        w = (xdt[ck] * dte_e).astype(BF16)
        snew.append([_dot(bm[ck][:, groups[g]].T.astype(BF16), w[:, groups[g]])
                     for g in range(SSM_GROUPS)])

    state = [st_ref[g] for g in range(SSM_GROUPS)]
    entering = []
    for ck in chunks:
        entering.append([st.astype(BF16) for st in state])
        cdl = eacs_e[ck][CHUNK - 1:CHUNK, :]
        state = [state[g] * cdl[:, groups[g]] + snew[ck][g] for g in range(SSM_GROUPS)]
    for g in range(SSM_GROUPS):
        st_ref[g] = state[g]

    ys = []
    for ck in chunks:
        xdt_b = xdt[ck].astype(BF16)
        xstack = jnp.concatenate(
            [jnp.where(hid == h, xdt_b, jnp.zeros_like(xdt_b)) for h in range(N_HEADS)], axis=0)
        yoff = jnp.concatenate([_dot(cm[ck][:, groups[g]], entering[ck][g]) for g in range(SSM_GROUPS)], axis=1)
        y = _dot(mcat[ck], xstack) + yoff * eacs_e[ck] + dexp_ref[...] * xs[ck]
        ys.append(y * z_ref[rows[ck], :].astype(F32))

    for ck in chunks:
        msq = _dot((ys[ck] * ys[ck]).astype(BF16), g_ref[...])
        out_ref[rows[ck], :] = (ys[ck] * lax.rsqrt(msq + RMS_EPS) * nw_ref[...]).astype(BF16)


def _conv_shift_matrices():
    r = jnp.arange(CHUNK)[:, None]
    col = jnp.arange(2 * CHUNK)[None, :]
    return jnp.stack([(col == CHUNK + r - j) for j in range(1, SSM_CONV_K)]).astype(BF16)


def _pool_bands():
    r = jnp.arange(CHUNK)[:, None]
    col = jnp.arange(2 * CHUNK)[None, :]
    back = CHUNK + r - col
    bands = [jnp.where((back >= 0) & (back < w), 1.0 / w, 0.0) for w in POOL_WINDOWS]
    bandcat = jnp.concatenate(bands, axis=1).astype(BF16)
    t = jnp.arange(CHUNK, dtype=F32)[:, None] + 1.0
    win = jnp.repeat(jnp.asarray(POOL_WINDOWS, F32), HEAD_DIM)[None, :]
    fix = win / jnp.minimum(win, t)
    return bandcat, fix


def _pool_chunks(pc_ref, pch_ref, band_ref, fix_ref, wbd_ref, scale_ref, out_ref, first_step):
    hid = _head_id((CHUNK, GROUP_W), HEAD_DIM)
    prev = jnp.where(first_step, jnp.zeros_like(pch_ref), pch_ref[...])
    for ck in range(pc_ref.shape[0] // CHUNK):
        cur = pc_ref[ck * CHUNK:(ck + 1) * CHUNK, :]
        parts = []
        for g in range(len(POOL_WINDOWS)):
            parts += [jnp.where(hid == g, prev, jnp.zeros_like(prev)),
                      jnp.where(hid == g, cur, jnp.zeros_like(cur))]
        mean = _dot(band_ref[...], jnp.concatenate(parts, axis=0))
        if ck == 0:
            mean = jnp.where(first_step, mean * fix_ref[...], mean)
        p = mean - cur.astype(F32)
        y = _dot(p.astype(BF16), wbd_ref[...]) * scale_ref[...]
        out_ref[ck * CHUNK:(ck + 1) * CHUNK, :] = y.astype(BF16)
        prev = cur


N_GMLP_IN, N_POOL_IN, N_SSD_IN = 5, 6, 11


def _mixers_kernel(*refs):
    gm = refs[:N_GMLP_IN]
    po = refs[N_GMLP_IN:N_GMLP_IN + N_POOL_IN]
    sd = refs[N_GMLP_IN + N_POOL_IN:N_GMLP_IN + N_POOL_IN + N_SSD_IN]
    ya_ref, yc_ref, yb_ref, st_ref = refs[N_GMLP_IN + N_POOL_IN + N_SSD_IN:]
    first_step = pl.program_id(1) == 0

    def others():
        _gmlp_kernel(*gm, ya_ref)
        _pool_chunks(*po, yc_ref, first_step)

    _ssd_kernel(*sd, yb_ref, st_ref, between_phases=others)


def _mixers(proj, dt, gws, bsm, gmn, g64, band, fix, wbd, psc, cw, cb, shift, alog, dexp, nw, g128,
            layer, b, seq):
    m = proj.shape[0]
    ts = SSD_CPS * CHUNK
    assert seq % ts == 0
    nc = seq // ts
    const = lambda i, c: (0, 0)
    blk = lambda width, col: pl.BlockSpec((ts, width), lambda i, c: (i * nc + c, col // width))
    halo = lambda width, col: pl.BlockSpec(
        (CHUNK, width), lambda i, c: (jnp.maximum((i * nc + c) * SSD_CPS - 1, 0), col // width))
    out = pl.BlockSpec((ts, GROUP_W), lambda i, c: (i * nc + c, 0))
    gmlp_in = [blk(2 * GROUP_W, COL_PA), _per_layer((N_HEADS, CHUNK, CHUNK), layer),
               _per_layer((CHUNK, GROUP_W), layer), _per_layer((1, GROUP_W), layer),
               pl.BlockSpec((GROUP_W, GROUP_W), const)]
    pool_in = [blk(GROUP_W, COL_PC), halo(GROUP_W, COL_PC),
               pl.BlockSpec((CHUNK, len(POOL_WINDOWS) * 2 * CHUNK), const), pl.BlockSpec((CHUNK, GROUP_W), const),
               _per_layer((GROUP_W, GROUP_W), layer), _per_layer((1, GROUP_W), layer)]
    ssd_in = [blk(GROUP_W, COL_Z), blk(SSM_CONV_DIM, COL_XBC), halo(SSM_CONV_DIM, COL_XBC),
              pl.BlockSpec((ts, DT_W), lambda i, c: (i * nc + c, 0)),
              _per_layer((SSM_CONV_K, SSM_CONV_DIM), layer), _per_layer((1, SSM_CONV_DIM), layer),
              pl.BlockSpec((SSM_CONV_K - 1, CHUNK, 2 * CHUNK), lambda i, c: (0, 0, 0)),
              _per_layer((1, DT_W), layer), _per_layer((1, GROUP_W), layer), _per_layer((1, GROUP_W), layer),
              pl.BlockSpec((GROUP_W, GROUP_W), const)]
    assert (len(gmlp_in), len(pool_in), len(ssd_in)) == (N_GMLP_IN, N_POOL_IN, N_SSD_IN)
    return pl.pallas_call(
        _mixers_kernel,
        grid=(b, nc),
        in_specs=gmlp_in + pool_in + ssd_in,
        out_specs=[out, out, out],
        out_shape=[jax.ShapeDtypeStruct((m, GROUP_W), BF16)] * 3,
        scratch_shapes=[pltpu.VMEM((SSM_GROUPS, SSM_D_STATE, LANES), F32)],
        compiler_params=_cparams(("parallel", "arbitrary")),
        name="mixers",
    )(proj, gws, bsm, gmn, g64,
      proj, proj, band, fix, wbd, psc,
      proj, proj, proj, dt, cw, cb, shift, alog, dexp, nw, g128)


def _alibi_slope(h):
    return 2.0 ** (-8.0 * (h + 1) / N_HEADS)


def _head_slot(x, h):
    base = x[:, (h // 2) * LANES:(h // 2 + 1) * LANES]
    return pltpu.roll(base, HEAD_DIM, 1) if h % 2 else base


ATT_PAIR = 2


def _attn_kernel(q_ref, k_ref, v_ref, qnw_ref, knw_ref, lq1_ref, lk1_ref, lq2_ref, lk2_ref, subw_ref,
                 g_ref, out_ref, ka_ref, vt_ref, qs_ref, acc_ref, s0_ref, kmax_ref, qa_ref, *, lam_init):
    pj = pl.program_id(1)
    tq, tk = ATT_TQ, ATT_TK
    seq = k_ref.shape[0]

    def aug_lanes(pos, slope, key_side):
        lane = lax.broadcasted_iota(jnp.int32, pos.shape, 1) - ATT_LANE_ALIBI
        hi = (slope * LANES) * lax.shift_right_logical(pos, int(math.log2(LANES))).astype(F32)
        lo = slope * (pos & (LANES - 1)).astype(F32)
        one = jnp.ones_like(hi)
        c = (one, one, hi, lo) if key_side else (-hi, -lo, one, one)
        last = jnp.where(lane == ATT_LANE_BOUND - ATT_LANE_ALIBI, 1.0, 0.0) if key_side else 0.0
        return jnp.where(lane == 0, c[0], jnp.where(lane == 1, c[1],
                         jnp.where(lane == 2, c[2], jnp.where(lane == 3, c[3], last))))

    @pl.when(pj == 0)
    def _():
        k = k_ref[...].astype(F32)
        ms = _dot((k * k).astype(BF16), g_ref[...])
        kn = k * lax.rsqrt(ms + RMS_EPS) * knw_ref[...]
        ksq = _dot((kn * kn).astype(BF16), g_ref[...]) * DA_QK_DIM
        kmax_ref[...] = jnp.broadcast_to(jnp.max(ksq, axis=0, keepdims=True), kmax_ref.shape)
        pos = lax.broadcasted_iota(jnp.int32, (seq, LANES), 0)
        lane = lax.broadcasted_iota(jnp.int32, (seq, LANES), 1)
        for h in range(N_HEADS):
            aug = aug_lanes(pos, _alibi_slope(h), True)
            ka_ref[h] = jnp.where(lane < HEAD_DIM, _head_slot(kn, h), aug).astype(BF16)
            qa_ref[h] = aug_lanes(pos, _alibi_slope(h), False).astype(BF16)
        tail = (lax.broadcasted_iota(jnp.int32, (VT_ROWS - HEAD_DIM, tk), 0) == 0).astype(BF16)
        for j in range(seq // tk):
            vt = v_ref[j * tk:(j + 1) * tk, :].astype(F32).T.astype(BF16)
            for h in range(N_HEADS):
                vt_ref[j, h, 0:HEAD_DIM, :] = vt[h * HEAD_DIM:(h + 1) * HEAD_DIM, :]
                vt_ref[j, h, HEAD_DIM:VT_ROWS, :] = tail

    lane = lax.broadcasted_iota(jnp.int32, (1, LANES), 1)

    def normalised_q(t):
        q = q_ref[t * tq:(t + 1) * tq, :].astype(F32)
        ms = _dot((q * q).astype(BF16), g_ref[...])
        qn = q * lax.rsqrt(ms + RMS_EPS) * (qnw_ref[...] * (DA_QK_DIM ** -0.5))
        qsq = _dot((qn * qn).astype(BF16), g_ref[...]) * DA_QK_DIM
        b2 = qsq * kmax_ref[0:1, :]
        bound = b2 * lax.rsqrt(b2 + TINY) * ATT_BOUND_MARGIN
        return qn, bound, jnp.max(bound) <= ATT_ONE_PASS_BOUND

    def build_queries(t, qi, qn, bound):
        q0 = pl.multiple_of(qi * tq, tq)
        for h in range(N_HEADS):
            base = _head_slot(qn, h)
            bnd = _head_slot(bound, h)
            qa = qa_ref[h, pl.ds(q0, tq), :]
            for comp in range(2):
                own = (lane >= comp * DA_QK_DIM) & (lane < (comp + 1) * DA_QK_DIM)
                y = jnp.where(own, base, 0.0)
                y = jnp.where(lane == ATT_LANE_BOUND, -bnd[:, comp * DA_QK_DIM:comp * DA_QK_DIM + 1], y)
                qs_ref[t, h, comp * tq:(comp + 1) * tq, :] = y.astype(BF16) + qa
                yield

    qb = ATT_QB
    items = [(h, n) for h in range(N_HEADS) for n in range(2 * tq // qb)]

    def scores(t, item, j):
        h, n = item
        k0 = pl.multiple_of(j * tk, tk)
        return _dot_nt(ka_ref[h, pl.ds(k0, tk), :], qs_ref[t, h, n * qb:(n + 1) * qb, :])

    def causal_mask(s, n):
        kk = lax.broadcasted_iota(jnp.int32, (tk, qb), 0)
        qq = (lax.broadcasted_iota(jnp.int32, (tk, qb), 1) + n * qb) & (tq - 1)
        return jnp.where(kk <= qq, s, NEG_BIG)

    def tile(t, j, ms_, masked):
        new_m = []
        s_next = s0_ref[...]
        for i, (h, n) in enumerate(items):
            cols = slice(n * qb, (n + 1) * qb)
            s = s_next
            if i + 1 < len(items):
                s_next = scores(t, items[i + 1], j)
            elif not masked:
                s0_ref[...] = scores(t, items[0], j + 1)
            if masked:
                s = causal_mask(s, n)
            m_old = ms_[i]
            m_new = jnp.maximum(m_old, jnp.max(s, axis=0, keepdims=True))
            alpha = jnp.exp(m_old - m_new)
            p = jnp.exp(s - m_new)
            new_m.append(m_new)
            acc_ref[t, h, :, cols] = acc_ref[t, h, :, cols] * alpha + _dot(vt_ref[j, h], p.astype(BF16))
        return tuple(new_m)

    def tile_one_pass(t, j, masked, filler=None):
        s_next = s0_ref[...]
        for i, (h, n) in enumerate(items):
            cols = slice(n * qb, (n + 1) * qb)
            s = s_next
            if i + 1 < len(items):
                s_next = scores(t, items[i + 1], j)
            elif not masked:
                s0_ref[...] = scores(t, items[0], j + 1)
            if masked:
                s = causal_mask(s, n)
            acc_ref[t, h, :, cols] += _dot(vt_ref[j, h], jnp.exp(s).astype(BF16))
            if filler is not None:
                next(filler, None)

    def attend(t, qi, one_pass, other_work):
        acc_ref[t] = jnp.zeros(acc_ref.shape[1:], F32)
        s0_ref[...] = scores(t, items[0], 0)

        @pl.when(one_pass)
        def _():
            def body(j, carry):
                tile_one_pass(t, j, False)
                return carry
            lax.fori_loop(0, qi, body, 0)
            filler = other_work()
            tile_one_pass(t, qi, True, filler)
            for _ in filler:
                pass

        @pl.when(jnp.logical_not(one_pass))
        def _():
            m0 = tuple(jnp.full((1, qb), NEG_BIG, F32) for _ in items)
            m1 = lax.fori_loop(0, qi, lambda j, c: tile(t, j, c, False), m0)
            tile(t, qi, m1, True)
            for _ in other_work():
                pass

    def lane_sum(x):
        return jnp.broadcast_to(jnp.sum(x, axis=1, keepdims=True), x.shape)

    lam = (jnp.exp(lane_sum(lq1_ref[...] * lk1_ref[...]))
           - jnp.exp(lane_sum(lq2_ref[...] * lk2_ref[...])) + lam_init)
    lam = jnp.concatenate([lam] * (tq // LANES), axis=1)

    def finish(t):
        outs = []
        for h in range(N_HEADS):
            o = acc_ref[t, h, 0:HEAD_DIM, :] / acc_ref[t, h, HEAD_DIM:HEAD_DIM + 1, :]
            oh = o[:, 0:tq] - lam * o[:, tq:2 * tq]
            ms = jnp.mean(oh * oh, axis=0, keepdims=True)
            outs.append(oh * lax.rsqrt(ms + RMS_EPS) * (subw_ref[...] * (1.0 - lam_init)))
            yield
        out_ref[t * tq:(t + 1) * tq, :] = jnp.concatenate(outs, axis=0).T.astype(BF16)
        yield

    def chained(makers):
        def run():
            for make in makers:
                yield from make()
        return run

    prepared = [normalised_q(t) for t in range(ATT_PAIR)]
    for _ in build_queries(0, pj * ATT_PAIR, *prepared[0][:2]):
        pass
    for t in range(ATT_PAIR):
        qi = pj * ATT_PAIR + t
        work = []
        if t + 1 < ATT_PAIR:
            work.append(functools.partial(build_queries, t + 1, qi + 1, *prepared[t + 1][:2]))
        if t > 0:
            work.append(functools.partial(finish, t - 1))
        attend(t, qi, prepared[t][2], chained(work))
    for _ in finish(ATT_PAIR - 1):
        pass


def _attn(proj, qnw, knw, lq1, lk1, lq2, lk2, subw, g32, layer, b, seq, lam_init):
    m = proj.shape[0]
    assert ATT_TQ == ATT_TK and seq % (ATT_PAIR * ATT_TQ) == 0
    tp = ATT_PAIR * ATT_TQ
    nq = seq // tp
    const = lambda i, j: (0, 0)
    return pl.pallas_call(
        functools.partial(_attn_kernel, lam_init=lam_init),
        grid=(b, nq),
        in_specs=[
            pl.BlockSpec((tp, GROUP_W), lambda i, j: (i * nq + j, COL_Q // GROUP_W)),
            pl.BlockSpec((seq, GROUP_W), lambda i, j: (i, COL_K // GROUP_W)),
            pl.BlockSpec((seq, GROUP_W), lambda i, j: (i, COL_V // GROUP_W)),
            _per_layer((1, GROUP_W), layer),
            _per_layer((1, GROUP_W), layer),
            _per_layer((1, LANES), layer),
            _per_layer((1, LANES), layer),
            _per_layer((1, LANES), layer),
            _per_layer((1, LANES), layer),
            _per_layer((HEAD_DIM, ATT_TQ), layer),
            pl.BlockSpec((GROUP_W, GROUP_W), const),
        ],
        out_specs=pl.BlockSpec((tp, GROUP_W), lambda i, j: (i * nq + j, 0)),
        out_shape=jax.ShapeDtypeStruct((m, GROUP_W), BF16),
        scratch_shapes=[
            pltpu.VMEM((N_HEADS, seq, LANES), BF16),
            pltpu.VMEM((seq // ATT_TK, N_HEADS, VT_ROWS, ATT_TK), BF16),
            pltpu.VMEM((ATT_PAIR, N_HEADS, 2 * ATT_TQ, LANES), BF16),
            pltpu.VMEM((ATT_PAIR, N_HEADS, VT_ROWS, 2 * ATT_TQ), F32),
            pltpu.VMEM((ATT_TK, ATT_QB), F32),
            pltpu.VMEM((8, GROUP_W), F32),
            pltpu.VMEM((N_HEADS, seq, LANES), BF16),
        ],
        compiler_params=_cparams(("parallel", "arbitrary")),
        name="diffattn",
    )(proj, proj, proj, qnw, knw, lq1, lk1, lq2, lk2, subw, g32)


FFN_STAGE_WIDE = 256
FFN_STAGE_TALL = 704


def _ffn_weight_jobs(layer, wo_hbm, wg_hbm, wu_hbm, wd_hbm, wo_b, wg_b, wu_b, wd_b, wide, tall, sem_w, sem_t):
    jobs = []

    def add(src, dst, stage, sem, rows, n_rows):
        for k, r0 in enumerate(range(0, n_rows, rows)):
            r = min(rows, n_rows - r0)
            slot = k % 2
            view = stage.at[slot, 0:r, :]
            copy = pltpu.make_async_copy(src.at[layer, r0:r0 + r, :], view, sem.at[slot])
            jobs.append((copy, view, dst.at[r0:r0 + r, :]))

    add(wg_hbm, wg_b, wide, sem_w, FFN_STAGE_WIDE, D_MODEL)
    add(wu_hbm, wu_b, wide, sem_w, FFN_STAGE_WIDE, D_MODEL)
    add(wd_hbm, wd_b, tall, sem_t, FFN_STAGE_TALL, D_FF)
    add(wo_hbm, wo_b, tall, sem_t, FFN_STAGE_TALL, D_MODEL)
    return jobs


def _ffn_kernel(x_ref, ya_ref, yb_ref, yc_ref, yd_ref, nw_ref, wo_hbm, wg_hbm, wu_hbm, wd_hbm,
                out_ref, act_ref, wo_ref, wg_ref, wu_ref, wd_ref, wide_ref, tall_ref, sem_w, sem_t, *, layer):
    @pl.when(pl.program_id(0) == 0)
    def _():
        jobs = _ffn_weight_jobs(layer, wo_hbm, wg_hbm, wu_hbm, wd_hbm, wo_ref, wg_ref, wu_ref, wd_ref,
                                wide_ref, tall_ref, sem_w, sem_t)
        jobs[0][0].start()
        for k, (copy, view, dst) in enumerate(jobs):
            if k + 1 < len(jobs):
                jobs[k + 1][0].start()
            copy.wait()
            dst[...] = view[...].astype(BF16)

    x1 = x_ref[...]
    for i, y_ref in enumerate((ya_ref, yb_ref, yc_ref, yd_ref)):
        x1 = x1 + _dot(y_ref[...], wo_ref[i * GROUP_W:(i + 1) * GROUP_W, :])
    ms = jnp.mean(x1 * x1, axis=-1, keepdims=True)
    h = (x1 * lax.rsqrt(ms + RMS_EPS) * nw_ref[...]).astype(BF16)
    for c0 in range(0, D_FF, FFN_CHUNK):
        c1 = min(c0 + FFN_CHUNK, D_FF)
        g = _dot(h, wg_ref[:, c0:c1])
        u = _dot(h, wu_ref[:, c0:c1])
        act_ref[:, c0:c1] = (g * _sigmoid(g) * u).astype(BF16)
    out_ref[...] = x1 + _dot(act_ref[...], wd_ref[...])


def _ffn(x2, ya, yb, yc, yd, nw, wo, wg, wu, wd, layer, tm):
    m = x2.shape[0]
    row = lambda i: (i, 0)
    hbm = pl.BlockSpec(memory_space=pl.ANY)
    return pl.pallas_call(
        functools.partial(_ffn_kernel, layer=layer),
        grid=(m // tm,),
        in_specs=[
            pl.BlockSpec((tm, D_MODEL), row),
            pl.BlockSpec((tm, GROUP_W), row),
            pl.BlockSpec((tm, GROUP_W), row),
            pl.BlockSpec((tm, GROUP_W), row),
            pl.BlockSpec((tm, GROUP_W), row),
            _per_layer((1, D_MODEL), layer),
            hbm, hbm, hbm, hbm,
        ],
        out_specs=pl.BlockSpec((tm, D_MODEL), row),
        out_shape=jax.ShapeDtypeStruct((m, D_MODEL), F32),
        scratch_shapes=[
            pltpu.VMEM((tm, D_FF), BF16),
            pltpu.VMEM((D_MODEL, D_MODEL), BF16),
            pltpu.VMEM((D_MODEL, D_FF), BF16),
            pltpu.VMEM((D_MODEL, D_FF), BF16),
            pltpu.VMEM((D_FF, D_MODEL), BF16),
            pltpu.VMEM((2, FFN_STAGE_WIDE, D_FF), F32),
            pltpu.VMEM((2, FFN_STAGE_TALL, D_MODEL), F32),
            pltpu.SemaphoreType.DMA((2,)),
            pltpu.SemaphoreType.DMA((2,)),
        ],
        compiler_params=_cparams(("arbitrary",)),
        name="outproj_ffn",
    )(x2, ya, yb, yc, yd, nw, wo, wg, wu, wd)


def _block_diag_mean(width, group):
    idx = jnp.arange(width) // group
    return jnp.where(idx[:, None] == idx[None, :], 1.0 / group, 0.0).astype(BF16)


def _rows(v, width=None):
    v = v.reshape(v.shape[0], 1, -1).astype(F32)
    return v if width is None else jnp.pad(v, ((0, 0), (0, 0), (0, width - v.shape[2])))


def kernel(x, norm1_w, w_in, gm_norm_w, gm_ws, gm_bs, ssm_conv_w, ssm_conv_b, ssm_dt_bias, ssm_a_log, ssm_d, ssm_norm_w, pool_w, pool_scale, da_q_norm_w, da_k_norm_w, da_lambda_q1, da_lambda_k1, da_lambda_q2, da_lambda_k2, da_subln_w, w_out, norm2_w, ffn_w_gate, ffn_w_up, ffn_w_down):
    b, seq, d = x.shape
    depth = w_in.shape[0]
    assert d == D_MODEL and seq % ATT_TQ == 0 and seq % CHUNK == 0
    m = b * seq
    tm = 512 if m % 512 == 0 else ATT_TQ

    g64 = _block_diag_mean(GROUP_W, HEAD_DIM)
    g128 = _block_diag_mean(GROUP_W, SSM_D_STATE)
    g32 = _block_diag_mean(GROUP_W, DA_QK_DIM)

    shift = _conv_shift_matrices()
    band, fix = _pool_bands()

    assert w_in.shape[1:] == (D_MODEL, D_IN)

    n1, n2 = _rows(norm1_w), _rows(norm2_w)
    dtb, alog = _rows(ssm_dt_bias, DT_W), _rows(ssm_a_log, DT_W)
    bsm = jnp.repeat(jnp.swapaxes(gm_bs, 1, 2), HEAD_DIM, axis=2)
    gmn = _rows(gm_norm_w)
    cw, cb = jnp.swapaxes(ssm_conv_w, 1, 2), _rows(ssm_conv_b)
    dexp, ssn = _rows(jnp.repeat(ssm_d, HEAD_DIM, axis=1)), _rows(ssm_norm_w)
    eye = jnp.eye(len(POOL_WINDOWS), dtype=pool_w.dtype)
    wbd = jnp.einsum('lgab,gh->lgahb', pool_w, eye).reshape(depth, GROUP_W, GROUP_W).astype(BF16)
    psc = _rows(pool_scale)
    qnw = _rows(jnp.tile(da_q_norm_w, (1, GROUP_W // DA_QK_DIM)))
    knw = _rows(jnp.tile(da_k_norm_w, (1, GROUP_W // DA_QK_DIM)))
    lams = [_rows(v, LANES) for v in (da_lambda_q1, da_lambda_k1, da_lambda_q2, da_lambda_k2)]
    subw = jnp.broadcast_to(da_subln_w[:, :, None], (depth, HEAD_DIM, ATT_TQ))

    x2 = x.reshape(m, d)
    for i in range(depth):
        proj, dt = _inproj(x2, n1, w_in, i, dtb, 2 * IN_ROWS if m % (2 * IN_ROWS) == 0 else IN_ROWS)
        ya, yc, yb = _mixers(proj, dt, gm_ws, bsm, gmn, g64, band, fix, wbd, psc,
                             cw, cb, shift, alog, dexp, ssn, g128, i, b, seq)
        lam_init = 0.8 - 0.6 * math.exp(-0.3 * i)
        yd = _attn(proj, qnw, knw, *lams, subw, g32, i, b, seq, lam_init)
        x2 = _ffn(x2, ya, yb, yc, yd, n2, w_out, ffn_w_gate, ffn_w_up, ffn_w_down, i, tm)
    return x2.reshape(b, seq, d)
```

```python
import functools
import math

import jax
import jax.numpy as jnp
from jax import lax
from jax.experimental import pallas as pl
from jax.experimental.pallas import tpu as pltpu

F32 = jnp.float32
BF16 = jnp.bfloat16

D_MODEL = 1024
GROUP_W = 256
CHUNK = 128
HEAD_DIM = 64
N_HEADS = 4
SSM_GROUPS = 2
SSM_D_STATE = 128
SSM_CONV_K = 4
SSM_CONV_DIM = GROUP_W + 2 * SSM_GROUPS * SSM_D_STATE
POOL_WINDOWS = (2, 4, 8, 16)
DA_QK_DIM = 32
D_FF = 2816
RMS_EPS = 1e-6
NEG_BIG = -1e30

LANES = 128
VMEM_LIMIT = 56 * 1024 * 1024

COL_PA = 0
COL_Z = 512
COL_XBC = 768
COL_PC = 1536
COL_Q = 1792
COL_K = 2048
COL_V = 2304
PROJ_W = 2560
DT_W = LANES
D_IN = PROJ_W + N_HEADS
DT_COL = COL_PC
IN_CHUNK = 256
IN_ROWS = 512
FFN_CHUNK = 256
SSD_CPS = 8

ATT_TQ = 512
ATT_TK = 512
ATT_QB = 512
ATT_BOUND_MARGIN = 1.01
ATT_ONE_PASS_BOUND = 40.0
VT_ROWS = 80
ATT_LANE_ALIBI = HEAD_DIM
ATT_LANE_BOUND = HEAD_DIM + 4
TINY = 1e-30


def _cparams(sem):
    return pltpu.CompilerParams(dimension_semantics=sem, vmem_limit_bytes=VMEM_LIMIT)


def _sigmoid(x):
    return 1.0 / (1.0 + jnp.exp(-x))


def _dot(a, b):
    return jnp.dot(a, b, preferred_element_type=F32)


def _dot_nt(a, b):
    return lax.dot_general(a, b, (((1,), (1,)), ((), ())), preferred_element_type=F32)


def _softplus(t):
    return jnp.maximum(t, 0.0) + jnp.log(1.0 + jnp.exp(-jnp.abs(t)))


def _inproj_kernel(x_ref, nw_ref, w_ref, dtb_ref, proj_ref, dt_ref, wb_ref):
    @pl.when(pl.program_id(0) == 0)
    def _():
        for n0 in range(0, DT_COL, 2 * IN_CHUNK):
            wb_ref[:, n0:n0 + 2 * IN_CHUNK] = w_ref[:, n0:n0 + 2 * IN_CHUNK].astype(BF16)
        tail = w_ref[:, DT_COL:D_IN]
        wb_ref[:, DT_COL:PROJ_W] = tail[:, N_HEADS:].astype(BF16)
        wb_ref[:, PROJ_W:PROJ_W + DT_W] = tail[:, 0:DT_W].astype(BF16)

    for r0 in range(0, x_ref.shape[0], IN_ROWS):
        rows = slice(r0, r0 + IN_ROWS)
        x = x_ref[rows, :]
        ms = jnp.mean(x * x, axis=-1, keepdims=True)
        h = (x * lax.rsqrt(ms + RMS_EPS) * nw_ref[...]).astype(BF16)
        for n0 in range(0, PROJ_W, IN_CHUNK):
            y = _dot(h, wb_ref[:, n0:n0 + IN_CHUNK])
            if n0 < COL_Z:
                y = jax.nn.gelu(y, approximate=True)
            elif n0 < COL_XBC:
                y = y * _sigmoid(y)
            proj_ref[rows, n0:n0 + IN_CHUNK] = y.astype(BF16)
        dt_ref[rows, :] = _softplus(_dot(h, wb_ref[:, PROJ_W:PROJ_W + DT_W]) + dtb_ref[...])


def _per_layer(shape, layer):
    return pl.BlockSpec((None,) + tuple(shape), lambda *_: (layer,) + (0,) * len(shape))


def _inproj(x2, nw, w_in, layer, dtb, tm):
    m = x2.shape[0]
    return pl.pallas_call(
        _inproj_kernel,
        grid=(m // tm,),
        in_specs=[
            pl.BlockSpec((tm, D_MODEL), lambda i: (i, 0)),
            _per_layer((1, D_MODEL), layer),
            pl.BlockSpec((None, D_MODEL, D_IN), lambda i: (layer, 0, 0), pipeline_mode=pl.Buffered(1)),
            _per_layer((1, DT_W), layer),
        ],
        out_specs=[
            pl.BlockSpec((tm, PROJ_W), lambda i: (i, 0)),
            pl.BlockSpec((tm, DT_W), lambda i: (i, 0)),
        ],
        out_shape=[
            jax.ShapeDtypeStruct((m, PROJ_W), BF16),
            jax.ShapeDtypeStruct((m, DT_W), F32),
        ],
        scratch_shapes=[pltpu.VMEM((D_MODEL, PROJ_W + DT_W), BF16)],
        compiler_params=_cparams(("arbitrary",)),
        name="inproj",
    )(x2, nw, w_in, dtb)


def _head_id(shape, width):
    lane = lax.broadcasted_iota(jnp.int32, shape, 1)
    return lax.shift_right_logical(lane, int(math.log2(width)))


def _gmlp_kernel(pa_ref, ws_ref, bsm_ref, nw_ref, g_ref, out_ref):
    t = pa_ref.shape[0]
    hact = pa_ref[...].astype(F32)
    u = hact[:, :GROUP_W]
    v = hact[:, GROUP_W:]
    ms = _dot((v * v).astype(BF16), g_ref[...])
    vn = (v * lax.rsqrt(ms + RMS_EPS) * nw_ref[...]).astype(BF16)
    row = lax.broadcasted_iota(jnp.int32, (CHUNK, CHUNK), 0)
    col = lax.broadcasted_iota(jnp.int32, (CHUNK, CHUNK), 1)
    wcat = jnp.concatenate(
        [jnp.where(row >= col, ws_ref[h], 0.0) for h in range(N_HEADS)], axis=1).astype(BF16)
    hid = _head_id((CHUNK, GROUP_W), HEAD_DIM)
    for c in range(t // CHUNK):
        vc = vn[c * CHUNK:(c + 1) * CHUNK]
        vstack = jnp.concatenate(
            [jnp.where(hid == h, vc, jnp.zeros_like(vc)) for h in range(N_HEADS)], axis=0)
        s = _dot(wcat, vstack) + bsm_ref[...]
        out_ref[c * CHUNK:(c + 1) * CHUNK, :] = (u[c * CHUNK:(c + 1) * CHUNK] * s).astype(BF16)


def _split3(a):
    a1 = a.astype(BF16)
    r1 = a - a1.astype(F32)
    a2 = r1.astype(BF16)
    r2 = r1 - a2.astype(F32)
    return a1, a2, r2.astype(BF16)


def _ssd_kernel(z_ref, xbc_ref, halo_ref, dt_ref, cw_ref, cb_ref, shift_ref, alog_ref, dexp_ref, nw_ref,
                g_ref, out_ref, st_ref, between_phases=None):
    c = pl.program_id(1)

    @pl.when(c == 0)
    def _():
        st_ref[...] = jnp.zeros_like(st_ref)

    row = lax.broadcasted_iota(jnp.int32, (CHUNK, CHUNK), 0)
    col = lax.broadcasted_iota(jnp.int32, (CHUNK, CHUNK), 1)
    causal = row >= col
    ltri = jnp.where(causal, 1.0, 0.0).astype(BF16)
    hid = _head_id((CHUNK, GROUP_W), HEAD_DIM)
    neg_a = -jnp.exp(alog_ref[...])

    def expand(c4):
        return jnp.where(hid == 0, c4[:, 0:1],
                         jnp.where(hid == 1, c4[:, 1:2],
                                   jnp.where(hid == 2, c4[:, 2:3], c4[:, 3:4])))

    chunks = range(SSD_CPS)
    rows = [slice(ck * CHUNK, (ck + 1) * CHUNK) for ck in chunks]
    groups = [slice(g * SSM_D_STATE, (g + 1) * SSM_D_STATE) for g in range(SSM_GROUPS)]

    xs, bm, cm = [], [], []
    halo = jnp.where(c > 0, halo_ref[...], jnp.zeros_like(halo_ref))
    for ck in chunks:
        xcur = xbc_ref[rows[ck], :]
        acc = cb_ref[...] + cw_ref[SSM_CONV_K - 1:SSM_CONV_K, :] * xcur.astype(F32)
        for j in range(1, SSM_CONV_K):
            k = SSM_CONV_K - 1 - j
            if ck == 0:
                shifted = (_dot(shift_ref[j - 1, :, 0:CHUNK], halo)
                           + _dot(shift_ref[j - 1, :, CHUNK:2 * CHUNK], xcur))
            else:
                shifted = _dot(shift_ref[j - 1], xbc_ref[(ck - 1) * CHUNK:(ck + 1) * CHUNK, :])
            acc = acc + cw_ref[k:k + 1, :] * shifted
        xc = acc * _sigmoid(acc)
        xs.append(xc[:, :GROUP_W])
        bm.append(xc[:, GROUP_W:2 * GROUP_W])
        cm.append(xc[:, 2 * GROUP_W:].astype(BF16))

    acs, acs_t = [], []
    for ck in chunks:
        a1, a2, a3 = _split3(dt_ref[rows[ck], :] * neg_a)
        acs.append(_dot(ltri, a1) + _dot(ltri, a2) + _dot(ltri, a3))
        acs_t.append(acs[ck].T)

    if between_phases is not None:
        between_phases()

    xdt, eacs_e, mcat, snew = [], [], [], []
    for ck in chunks:
        xdt.append(xs[ck] * expand(dt_ref[rows[ck], :]))
        eacs_e.append(expand(jnp.exp(acs[ck])))
        dte_e = expand(jnp.exp(acs[ck][CHUNK - 1:CHUNK, :] - acs[ck]))
        ms = []
        for g in range(SSM_GROUPS):
            cb = _dot_nt(cm[ck][:, groups[g]], bm[ck][:, groups[g]].astype(BF16))
            for hh in range(N_HEADS // SSM_GROUPS):
                h = g * (N_HEADS // SSM_GROUPS) + hh
                seg = acs[ck][:, h:h + 1] - acs_t[ck][h:h + 1, :]
                ms.append((cb * jnp.exp(jnp.where(causal, seg, NEG_BIG))).astype(BF16))
        mcat.append(jnp.concatenate(ms, axis=1))
        w = (xdt[ck] * dte_e).astype(BF16)
        snew.append([_dot(bm[ck][:, groups[g]].T.astype(BF16), w[:, groups[g]])
                     for g in range(SSM_GROUPS)])

    state = [st_ref[g] for g in range(SSM_GROUPS)]
    entering = []
    for ck in chunks:
        entering.append([st.astype(BF16) for st in state])
        cdl = eacs_e[ck][CHUNK - 1:CHUNK, :]
        state = [state[g] * cdl[:, groups[g]] + snew[ck][g] for g in range(SSM_GROUPS)]
    for g in range(SSM_GROUPS):
        st_ref[g] = state[g]

    ys = []
    for ck in chunks:
        xdt_b = xdt[ck].astype(BF16)
        xstack = jnp.concatenate(
            [jnp.where(hid == h, xdt_b, jnp.zeros_like(xdt_b)) for h in range(N_HEADS)], axis=0)
        yoff = jnp.concatenate([_dot(cm[ck][:, groups[g]], entering[ck][g]) for g in range(SSM_GROUPS)], axis=1)
        y = _dot(mcat[ck], xstack) + yoff * eacs_e[ck] + dexp_ref[...] * xs[ck]
        ys.append(y * z_ref[rows[ck], :].astype(F32))

    for ck in chunks:
        msq = _dot((ys[ck] * ys[ck]).astype(BF16), g_ref[...])
        out_ref[rows[ck], :] = (ys[ck] * lax.rsqrt(msq + RMS_EPS) * nw_ref[...]).astype(BF16)


def _conv_shift_matrices():
    r = jnp.arange(CHUNK)[:, None]
    col = jnp.arange(2 * CHUNK)[None, :]
    return jnp.stack([(col == CHUNK + r - j) for j in range(1, SSM_CONV_K)]).astype(BF16)


def _pool_bands():
    r = jnp.arange(CHUNK)[:, None]
    col = jnp.arange(2 * CHUNK)[None, :]
    back = CHUNK + r - col
    bands = [jnp.where((back >= 0) & (back < w), 1.0 / w, 0.0) for w in POOL_WINDOWS]
    bandcat = jnp.concatenate(bands, axis=1).astype(BF16)
    t = jnp.arange(CHUNK, dtype=F32)[:, None] + 1.0
    win = jnp.repeat(jnp.asarray(POOL_WINDOWS, F32), HEAD_DIM)[None, :]
    fix = win / jnp.minimum(win, t)
    return bandcat, fix


def _pool_chunks(pc_ref, pch_ref, band_ref, fix_ref, wbd_ref, scale_ref, out_ref, first_step):
    hid = _head_id((CHUNK, GROUP_W), HEAD_DIM)
    prev = jnp.where(first_step, jnp.zeros_like(pch_ref), pch_ref[...])
    for ck in range(pc_ref.shape[0] // CHUNK):
        cur = pc_ref[ck * CHUNK:(ck + 1) * CHUNK, :]
        parts = []
        for g in range(len(POOL_WINDOWS)):
            parts += [jnp.where(hid == g, prev, jnp.zeros_like(prev)),
                      jnp.where(hid == g, cur, jnp.zeros_like(cur))]
        mean = _dot(band_ref[...], jnp.concatenate(parts, axis=0))
        if ck == 0:
            mean = jnp.where(first_step, mean * fix_ref[...], mean)
        p = mean - cur.astype(F32)
        y = _dot(p.astype(BF16), wbd_ref[...]) * scale_ref[...]
        out_ref[ck * CHUNK:(ck + 1) * CHUNK, :] = y.astype(BF16)
        prev = cur


N_GMLP_IN, N_POOL_IN, N_SSD_IN = 5, 6, 11


def _mixers_kernel(*refs):
    gm = refs[:N_GMLP_IN]
    po = refs[N_GMLP_IN:N_GMLP_IN + N_POOL_IN]
    sd = refs[N_GMLP_IN + N_POOL_IN:N_GMLP_IN + N_POOL_IN + N_SSD_IN]
    ya_ref, yc_ref, yb_ref, st_ref = refs[N_GMLP_IN + N_POOL_IN + N_SSD_IN:]
    first_step = pl.program_id(1) == 0

    def others():
        _gmlp_kernel(*gm, ya_ref)
        _pool_chunks(*po, yc_ref, first_step)

    _ssd_kernel(*sd, yb_ref, st_ref, between_phases=others)


def _mixers(proj, dt, gws, bsm, gmn, g64, band, fix, wbd, psc, cw, cb, shift, alog, dexp, nw, g128,
            layer, b, seq):
    m = proj.shape[0]
    ts = SSD_CPS * CHUNK
    assert seq % ts == 0
    nc = seq // ts
    const = lambda i, c: (0, 0)
    blk = lambda width, col: pl.BlockSpec((ts, width), lambda i, c: (i * nc + c, col // width))
    halo = lambda width, col: pl.BlockSpec(
        (CHUNK, width), lambda i, c: (jnp.maximum((i * nc + c) * SSD_CPS - 1, 0), col // width))
    out = pl.BlockSpec((ts, GROUP_W), lambda i, c: (i * nc + c, 0))
    gmlp_in = [blk(2 * GROUP_W, COL_PA), _per_layer((N_HEADS, CHUNK, CHUNK), layer),
               _per_layer((CHUNK, GROUP_W), layer), _per_layer((1, GROUP_W), layer),
               pl.BlockSpec((GROUP_W, GROUP_W), const)]
    pool_in = [blk(GROUP_W, COL_PC), halo(GROUP_W, COL_PC),
               pl.BlockSpec((CHUNK, len(POOL_WINDOWS) * 2 * CHUNK), const), pl.BlockSpec((CHUNK, GROUP_W), const),
               _per_layer((GROUP_W, GROUP_W), layer), _per_layer((1, GROUP_W), layer)]
    ssd_in = [blk(GROUP_W, COL_Z), blk(SSM_CONV_DIM, COL_XBC), halo(SSM_CONV_DIM, COL_XBC),
              pl.BlockSpec((ts, DT_W), lambda i, c: (i * nc + c, 0)),
              _per_layer((SSM_CONV_K, SSM_CONV_DIM), layer), _per_layer((1, SSM_CONV_DIM), layer),
              pl.BlockSpec((SSM_CONV_K - 1, CHUNK, 2 * CHUNK), lambda i, c: (0, 0, 0)),
              _per_layer((1, DT_W), layer), _per_layer((1, GROUP_W), layer), _per_layer((1, GROUP_W), layer),
              pl.BlockSpec((GROUP_W, GROUP_W), const)]
    assert (len(gmlp_in), len(pool_in), len(ssd_in)) == (N_GMLP_IN, N_POOL_IN, N_SSD_IN)
    return pl.pallas_call(
        _mixers_kernel,
        grid=(b, nc),
        in_specs=gmlp_in + pool_in + ssd_in,
        out_specs=[out, out, out],
        out_shape=[jax.ShapeDtypeStruct((m, GROUP_W), BF16)] * 3,
        scratch_shapes=[pltpu.VMEM((SSM_GROUPS, SSM_D_STATE, LANES), F32)],
        compiler_params=_cparams(("parallel", "arbitrary")),
        name="mixers",
    )(proj, gws, bsm, gmn, g64,
      proj, proj, band, fix, wbd, psc,
      proj, proj, proj, dt, cw, cb, shift, alog, dexp, nw, g128)


def _alibi_slope(h):
    return 2.0 ** (-8.0 * (h + 1) / N_HEADS)


def _head_slot(x, h):
    base = x[:, (h // 2) * LANES:(h // 2 + 1) * LANES]
    return pltpu.roll(base, HEAD_DIM, 1) if h % 2 else base


ATT_PAIR = 2


def _attn_kernel(q_ref, k_ref, v_ref, qnw_ref, knw_ref, lq1_ref, lk1_ref, lq2_ref, lk2_ref, subw_ref,
                 g_ref, out_ref, ka_ref, vt_ref, qs_ref, acc_ref, s0_ref, kmax_ref, qa_ref, *, lam_init):
    pj = pl.program_id(1)
    tq, tk = ATT_TQ, ATT_TK
    seq = k_ref.shape[0]

    def aug_lanes(pos, slope, key_side):
        lane = lax.broadcasted_iota(jnp.int32, pos.shape, 1) - ATT_LANE_ALIBI
        hi = (slope * LANES) * lax.shift_right_logical(pos, int(math.log2(LANES))).astype(F32)
        lo = slope * (pos & (LANES - 1)).astype(F32)
        one = jnp.ones_like(hi)
        c = (one, one, hi, lo) if key_side else (-hi, -lo, one, one)
        last = jnp.where(lane == ATT_LANE_BOUND - ATT_LANE_ALIBI, 1.0, 0.0) if key_side else 0.0
        return jnp.where(lane == 0, c[0], jnp.where(lane == 1, c[1],
                         jnp.where(lane == 2, c[2], jnp.where(lane == 3, c[3], last))))

    @pl.when(pj == 0)
    def _():
        k = k_ref[...].astype(F32)
        ms = _dot((k * k).astype(BF16), g_ref[...])
        kn = k * lax.rsqrt(ms + RMS_EPS) * knw_ref[...]
        ksq = _dot((kn * kn).astype(BF16), g_ref[...]) * DA_QK_DIM
        kmax_ref[...] = jnp.broadcast_to(jnp.max(ksq, axis=0, keepdims=True), kmax_ref.shape)
        pos = lax.broadcasted_iota(jnp.int32, (seq, LANES), 0)
        lane = lax.broadcasted_iota(jnp.int32, (seq, LANES), 1)
        for h in range(N_HEADS):
            aug = aug_lanes(pos, _alibi_slope(h), True)
            ka_ref[h] = jnp.where(lane < HEAD_DIM, _head_slot(kn, h), aug).astype(BF16)
            qa_ref[h] = aug_lanes(pos, _alibi_slope(h), False).astype(BF16)
        tail = (lax.broadcasted_iota(jnp.int32, (VT_ROWS - HEAD_DIM, tk), 0) == 0).astype(BF16)
        for j in range(seq // tk):
            vt = v_ref[j * tk:(j + 1) * tk, :].astype(F32).T.astype(BF16)
            for h in range(N_HEADS):
                vt_ref[j, h, 0:HEAD_DIM, :] = vt[h * HEAD_DIM:(h + 1) * HEAD_DIM, :]
                vt_ref[j, h, HEAD_DIM:VT_ROWS, :] = tail

    lane = lax.broadcasted_iota(jnp.int32, (1, LANES), 1)

    def normalised_q(t):
        q = q_ref[t * tq:(t + 1) * tq, :].astype(F32)
        ms = _dot((q * q).astype(BF16), g_ref[...])
        qn = q * lax.rsqrt(ms + RMS_EPS) * (qnw_ref[...] * (DA_QK_DIM ** -0.5))
        qsq = _dot((qn * qn).astype(BF16), g_ref[...]) * DA_QK_DIM
        b2 = qsq * kmax_ref[0:1, :]
        bound = b2 * lax.rsqrt(b2 + TINY) * ATT_BOUND_MARGIN
        return qn, bound, jnp.max(bound) <= ATT_ONE_PASS_BOUND

    def build_queries(t, qi, qn, bound):
        q0 = pl.multiple_of(qi * tq, tq)
        for h in range(N_HEADS):
            base = _head_slot(qn, h)
            bnd = _head_slot(bound, h)
            qa = qa_ref[h, pl.ds(q0, tq), :]
            for comp in range(2):
                own = (lane >= comp * DA_QK_DIM) & (lane < (comp + 1) * DA_QK_DIM)
                y = jnp.where(own, base, 0.0)
                y = jnp.where(lane == ATT_LANE_BOUND, -bnd[:, comp * DA_QK_DIM:comp * DA_QK_DIM + 1], y)
                qs_ref[t, h, comp * tq:(comp + 1) * tq, :] = y.astype(BF16) + qa
                yield

    qb = ATT_QB
    items = [(h, n) for h in range(N_HEADS) for n in range(2 * tq // qb)]

    def scores(t, item, j):
        h, n = item
        k0 = pl.multiple_of(j * tk, tk)
        return _dot_nt(ka_ref[h, pl.ds(k0, tk), :], qs_ref[t, h, n * qb:(n + 1) * qb, :])

    def causal_mask(s, n):
        kk = lax.broadcasted_iota(jnp.int32, (tk, qb), 0)
        qq = (lax.broadcasted_iota(jnp.int32, (tk, qb), 1) + n * qb) & (tq - 1)
        return jnp.where(kk <= qq, s, NEG_BIG)

    def tile(t, j, ms_, masked):
        new_m = []
        s_next = s0_ref[...]
        for i, (h, n) in enumerate(items):
            cols = slice(n * qb, (n + 1) * qb)
            s = s_next
            if i + 1 < len(items):
                s_next = scores(t, items[i + 1], j)
            elif not masked:
                s0_ref[...] = scores(t, items[0], j + 1)
            if masked:
                s = causal_mask(s, n)
            m_old = ms_[i]
            m_new = jnp.maximum(m_old, jnp.max(s, axis=0, keepdims=True))
            alpha = jnp.exp(m_old - m_new)
            p = jnp.exp(s - m_new)
            new_m.append(m_new)
            acc_ref[t, h, :, cols] = acc_ref[t, h, :, cols] * alpha + _dot(vt_ref[j, h], p.astype(BF16))
        return tuple(new_m)

    def tile_one_pass(t, j, masked, filler=None):
        s_next = s0_ref[...]
        for i, (h, n) in enumerate(items):
            cols = slice(n * qb, (n + 1) * qb)
            s = s_next
            if i + 1 < len(items):
                s_next = scores(t, items[i + 1], j)
            elif not masked:
                s0_ref[...] = scores(t, items[0], j + 1)
            if masked:
                s = causal_mask(s, n)
            acc_ref[t, h, :, cols] += _dot(vt_ref[j, h], jnp.exp(s).astype(BF16))
            if filler is not None:
                next(filler, None)

    def attend(t, qi, one_pass, other_work):
        acc_ref[t] = jnp.zeros(acc_ref.shape[1:], F32)
        s0_ref[...] = scores(t, items[0], 0)

        @pl.when(one_pass)
        def _():
            def body(j, carry):
                tile_one_pass(t, j, False)
                return carry
            lax.fori_loop(0, qi, body, 0)
            filler = other_work()
            tile_one_pass(t, qi, True, filler)
            for _ in filler:
                pass

        @pl.when(jnp.logical_not(one_pass))
        def _():
            m0 = tuple(jnp.full((1, qb), NEG_BIG, F32) for _ in items)
            m1 = lax.fori_loop(0, qi, lambda j, c: tile(t, j, c, False), m0)
            tile(t, qi, m1, True)
            for _ in other_work():
                pass

    def lane_sum(x):
        return jnp.broadcast_to(jnp.sum(x, axis=1, keepdims=True), x.shape)

    lam = (jnp.exp(lane_sum(lq1_ref[...] * lk1_ref[...]))
           - jnp.exp(lane_sum(lq2_ref[...] * lk2_ref[...])) + lam_init)
    lam = jnp.concatenate([lam] * (tq // LANES), axis=1)

    def finish(t):
        outs = []
        for h in range(N_HEADS):
            o = acc_ref[t, h, 0:HEAD_DIM, :] / acc_ref[t, h, HEAD_DIM:HEAD_DIM + 1, :]
            oh = o[:, 0:tq] - lam * o[:, tq:2 * tq]
            ms = jnp.mean(oh * oh, axis=0, keepdims=True)
            outs.append(oh * lax.rsqrt(ms + RMS_EPS) * (subw_ref[...] * (1.0 - lam_init)))
            yield
        out_ref[t * tq:(t + 1) * tq, :] = jnp.concatenate(outs, axis=0).T.astype(BF16)
        yield

    def chained(makers):
        def run():
            for make in makers:
                yield from make()
        return run

    prepared = [normalised_q(t) for t in range(ATT_PAIR)]
    for _ in build_queries(0, pj * ATT_PAIR, *prepared[0][:2]):
        pass
    for t in range(ATT_PAIR):
        qi = pj * ATT_PAIR + t
        work = []
        if t + 1 < ATT_PAIR:
            work.append(functools.partial(build_queries, t + 1, qi + 1, *prepared[t + 1][:2]))
        if t > 0:
            work.append(functools.partial(finish, t - 1))
        attend(t, qi, prepared[t][2], chained(work))
    for _ in finish(ATT_PAIR - 1):
        pass


def _attn(proj, qnw, knw, lq1, lk1, lq2, lk2, subw, g32, layer, b, seq, lam_init):
    m = proj.shape[0]
    assert ATT_TQ == ATT_TK and seq % (ATT_PAIR * ATT_TQ) == 0
    tp = ATT_PAIR * ATT_TQ
    nq = seq // tp
    const = lambda i, j: (0, 0)
    return pl.pallas_call(
        functools.partial(_attn_kernel, lam_init=lam_init),
        grid=(b, nq),
        in_specs=[
            pl.BlockSpec((tp, GROUP_W), lambda i, j: (i * nq + j, COL_Q // GROUP_W)),
            pl.BlockSpec((seq, GROUP_W), lambda i, j: (i, COL_K // GROUP_W)),
            pl.BlockSpec((seq, GROUP_W), lambda i, j: (i, COL_V // GROUP_W)),
            _per_layer((1, GROUP_W), layer),
            _per_layer((1, GROUP_W), layer),
            _per_layer((1, LANES), layer),
            _per_layer((1, LANES), layer),
            _per_layer((1, LANES), layer),
            _per_layer((1, LANES), layer),
            _per_layer((HEAD_DIM, ATT_TQ), layer),
            pl.BlockSpec((GROUP_W, GROUP_W), const),
        ],
        out_specs=pl.BlockSpec((tp, GROUP_W), lambda i, j: (i * nq + j, 0)),
        out_shape=jax.ShapeDtypeStruct((m, GROUP_W), BF16),
        scratch_shapes=[
            pltpu.VMEM((N_HEADS, seq, LANES), BF16),
            pltpu.VMEM((seq // ATT_TK, N_HEADS, VT_ROWS, ATT_TK), BF16),
            pltpu.VMEM((ATT_PAIR, N_HEADS, 2 * ATT_TQ, LANES), BF16),
            pltpu.VMEM((ATT_PAIR, N_HEADS, VT_ROWS, 2 * ATT_TQ), F32),
            pltpu.VMEM((ATT_TK, ATT_QB), F32),
            pltpu.VMEM((8, GROUP_W), F32),
            pltpu.VMEM((N_HEADS, seq, LANES), BF16),
        ],
        compiler_params=_cparams(("parallel", "arbitrary")),
        name="diffattn",
    )(proj, proj, proj, qnw, knw, lq1, lk1, lq2, lk2, subw, g32)


FFN_STAGE_WIDE = 256
FFN_STAGE_TALL = 704


def _ffn_weight_jobs(layer, wo_hbm, wg_hbm, wu_hbm, wd_hbm, wo_b, wg_b, wu_b, wd_b, wide, tall, sem_w, sem_t):
    jobs = []

    def add(src, dst, stage, sem, rows, n_rows):
        for k, r0 in enumerate(range(0, n_rows, rows)):
            r = min(rows, n_rows - r0)
            slot = k % 2
            view = stage.at[slot, 0:r, :]
            copy = pltpu.make_async_copy(src.at[layer, r0:r0 + r, :], view, sem.at[slot])
            jobs.append((copy, view, dst.at[r0:r0 + r, :]))

    add(wg_hbm, wg_b, wide, sem_w, FFN_STAGE_WIDE, D_MODEL)
    add(wu_hbm, wu_b, wide, sem_w, FFN_STAGE_WIDE, D_MODEL)
    add(wd_hbm, wd_b, tall, sem_t, FFN_STAGE_TALL, D_FF)
    add(wo_hbm, wo_b, tall, sem_t, FFN_STAGE_TALL, D_MODEL)
    return jobs


def _ffn_kernel(x_ref, ya_ref, yb_ref, yc_ref, yd_ref, nw_ref, wo_hbm, wg_hbm, wu_hbm, wd_hbm,
                out_ref, act_ref, wo_ref, wg_ref, wu_ref, wd_ref, wide_ref, tall_ref, sem_w, sem_t, *, layer):
    @pl.when(pl.program_id(0) == 0)
    def _():
        jobs = _ffn_weight_jobs(layer, wo_hbm, wg_hbm, wu_hbm, wd_hbm, wo_ref, wg_ref, wu_ref, wd_ref,
                                wide_ref, tall_ref, sem_w, sem_t)
        jobs[0][0].start()
        for k, (copy, view, dst) in enumerate(jobs):
            if k + 1 < len(jobs):
                jobs[k + 1][0].start()
            copy.wait()
            dst[...] = view[...].astype(BF16)

    x1 = x_ref[...]
    for i, y_ref in enumerate((ya_ref, yb_ref, yc_ref, yd_ref)):
        x1 = x1 + _dot(y_ref[...], wo_ref[i * GROUP_W:(i + 1) * GROUP_W, :])
    ms = jnp.mean(x1 * x1, axis=-1, keepdims=True)
    h = (x1 * lax.rsqrt(ms + RMS_EPS) * nw_ref[...]).astype(BF16)
    for c0 in range(0, D_FF, FFN_CHUNK):
        c1 = min(c0 + FFN_CHUNK, D_FF)
        g = _dot(h, wg_ref[:, c0:c1])
        u = _dot(h, wu_ref[:, c0:c1])
        act_ref[:, c0:c1] = (g * _sigmoid(g) * u).astype(BF16)
    out_ref[...] = x1 + _dot(act_ref[...], wd_ref[...])


def _ffn(x2, ya, yb, yc, yd, nw, wo, wg, wu, wd, layer, tm):
    m = x2.shape[0]
    row = lambda i: (i, 0)
    hbm = pl.BlockSpec(memory_space=pl.ANY)
    return pl.pallas_call(
        functools.partial(_ffn_kernel, layer=layer),
        grid=(m // tm,),
        in_specs=[
            pl.BlockSpec((tm, D_MODEL), row),
            pl.BlockSpec((tm, GROUP_W), row),
            pl.BlockSpec((tm, GROUP_W), row),
            pl.BlockSpec((tm, GROUP_W), row),
            pl.BlockSpec((tm, GROUP_W), row),
            _per_layer((1, D_MODEL), layer),
            hbm, hbm, hbm, hbm,
        ],
        out_specs=pl.BlockSpec((tm, D_MODEL), row),
        out_shape=jax.ShapeDtypeStruct((m, D_MODEL), F32),
        scratch_shapes=[
            pltpu.VMEM((tm, D_FF), BF16),
            pltpu.VMEM((D_MODEL, D_MODEL), BF16),
            pltpu.VMEM((D_MODEL, D_FF), BF16),
            pltpu.VMEM((D_MODEL, D_FF), BF16),
            pltpu.VMEM((D_FF, D_MODEL), BF16),
            pltpu.VMEM((2, FFN_STAGE_WIDE, D_FF), F32),
            pltpu.VMEM((2, FFN_STAGE_TALL, D_MODEL), F32),
            pltpu.SemaphoreType.DMA((2,)),
            pltpu.SemaphoreType.DMA((2,)),
        ],
        compiler_params=_cparams(("arbitrary",)),
        name="outproj_ffn",
    )(x2, ya, yb, yc, yd, nw, wo, wg, wu, wd)


def _block_diag_mean(width, group):
    idx = jnp.arange(width) // group
    return jnp.where(idx[:, None] == idx[None, :], 1.0 / group, 0.0).astype(BF16)


def _rows(v, width=None):
    v = v.reshape(v.shape[0], 1, -1).astype(F32)
    return v if width is None else jnp.pad(v, ((0, 0), (0, 0), (0, width - v.shape[2])))


def kernel(x, norm1_w, w_in, gm_norm_w, gm_ws, gm_bs, ssm_conv_w, ssm_conv_b, ssm_dt_bias, ssm_a_log, ssm_d, ssm_norm_w, pool_w, pool_scale, da_q_norm_w, da_k_norm_w, da_lambda_q1, da_lambda_k1, da_lambda_q2, da_lambda_k2, da_subln_w, w_out, norm2_w, ffn_w_gate, ffn_w_up, ffn_w_down):
    b, seq, d = x.shape
    depth = w_in.shape[0]
    assert d == D_MODEL and seq % ATT_TQ == 0 and seq % CHUNK == 0
    m = b * seq
    tm = 512 if m % 512 == 0 else ATT_TQ

    g64 = _block_diag_mean(GROUP_W, HEAD_DIM)
    g128 = _block_diag_mean(GROUP_W, SSM_D_STATE)
    g32 = _block_diag_mean(GROUP_W, DA_QK_DIM)

    shift = _conv_shift_matrices()
    band, fix = _pool_bands()

    assert w_in.shape[1:] == (D_MODEL, D_IN)

    n1, n2 = _rows(norm1_w), _rows(norm2_w)
    dtb, alog = _rows(ssm_dt_bias, DT_W), _rows(ssm_a_log, DT_W)
    bsm = jnp.repeat(jnp.swapaxes(gm_bs, 1, 2), HEAD_DIM, axis=2)
    gmn = _rows(gm_norm_w)
    cw, cb = jnp.swapaxes(ssm_conv_w, 1, 2), _rows(ssm_conv_b)
    dexp, ssn = _rows(jnp.repeat(ssm_d, HEAD_DIM, axis=1)), _rows(ssm_norm_w)
    eye = jnp.eye(len(POOL_WINDOWS), dtype=pool_w.dtype)
    wbd = jnp.einsum('lgab,gh->lgahb', pool_w, eye).reshape(depth, GROUP_W, GROUP_W).astype(BF16)
    psc = _rows(pool_scale)
    qnw = _rows(jnp.tile(da_q_norm_w, (1, GROUP_W // DA_QK_DIM)))
    knw = _rows(jnp.tile(da_k_norm_w, (1, GROUP_W // DA_QK_DIM)))
    lams = [_rows(v, LANES) for v in (da_lambda_q1, da_lambda_k1, da_lambda_q2, da_lambda_k2)]
    subw = jnp.broadcast_to(da_subln_w[:, :, None], (depth, HEAD_DIM, ATT_TQ))

    x2 = x.reshape(m, d)
    for i in range(depth):
        proj, dt = _inproj(x2, n1, w_in, i, dtb, 2 * IN_ROWS if m % (2 * IN_ROWS) == 0 else IN_ROWS)
        ya, yc, yb = _mixers(proj, dt, gm_ws, bsm, gmn, g64, band, fix, wbd, psc,
                             cw, cb, shift, alog, dexp, ssn, g128, i, b, seq)
        lam_init = 0.8 - 0.6 * math.exp(-0.3 * i)
        yd = _attn(proj, qnw, knw, *lams, subw, g32, i, b, seq, lam_init)
        x2 = _ffn(x2, ya, yb, yc, yd, n2, w_out, ffn_w_gate, ffn_w_up, ffn_w_down, i, tm)
    return x2.reshape(b, seq, d)
```

```python
import functools
import math

import jax
import jax.numpy as jnp
from jax import lax
from jax.experimental import pallas as pl
from jax.experimental.pallas import tpu as pltpu

F32 = jnp.float32
BF16 = jnp.bfloat16

D_MODEL = 1024
GROUP_W = 256
CHUNK = 128
HEAD_DIM = 64
N_HEADS = 4
SSM_GROUPS = 2
SSM_D_STATE = 128
SSM_CONV_K = 4
SSM_CONV_DIM = GROUP_W + 2 * SSM_GROUPS * SSM_D_STATE
POOL_WINDOWS = (2, 4, 8, 16)
DA_QK_DIM = 32
D_FF = 2816
RMS_EPS = 1e-6
NEG_BIG = -1e30

LANES = 128
VMEM_LIMIT = 56 * 1024 * 1024

COL_PA = 0
COL_Z = 512
COL_XBC = 768
COL_PC = 1536
COL_Q = 1792
COL_K = 2048
COL_V = 2304
PROJ_W = 2560
DT_W = LANES
D_IN = PROJ_W + N_HEADS
DT_COL = COL_PC
IN_CHUNK = 256
IN_ROWS = 512
FFN_CHUNK = 256
SSD_CPS = 8

ATT_TQ = 512
ATT_TK = 512
ATT_QB = 512
ATT_BOUND_MARGIN = 1.01
ATT_ONE_PASS_BOUND = 40.0
VT_ROWS = 80
ATT_LANE_ALIBI = HEAD_DIM
ATT_LANE_BOUND = HEAD_DIM + 4
TINY = 1e-30


def _cparams(sem):
    return pltpu.CompilerParams(dimension_semantics=sem, vmem_limit_bytes=VMEM_LIMIT)


def _sigmoid(x):
    return 1.0 / (1.0 + jnp.exp(-x))


def _dot(a, b):
    return jnp.dot(a, b, preferred_element_type=F32)


def _dot_nt(a, b):
    return lax.dot_general(a, b, (((1,), (1,)), ((), ())), preferred_element_type=F32)


def _softplus(t):
    return jnp.maximum(t, 0.0) + jnp.log(1.0 + jnp.exp(-jnp.abs(t)))


def _inproj_kernel(x_ref, nw_ref, w_ref, dtb_ref, proj_ref, dt_ref, wb_ref):
    @pl.when(pl.program_id(0) == 0)
    def _():
        for n0 in range(0, DT_COL, 2 * IN_CHUNK):
            wb_ref[:, n0:n0 + 2 * IN_CHUNK] = w_ref[n0:n0 + 2 * IN_CHUNK, :].T.astype(BF16)
        for n0 in range(DT_COL, PROJ_W, 2 * IN_CHUNK):
            wb_ref[:, n0:n0 + 2 * IN_CHUNK] = (
                w_ref[n0 + N_HEADS:n0 + N_HEADS + 2 * IN_CHUNK, :].T.astype(BF16))
        wb_ref[:, PROJ_W:PROJ_W + DT_W] = w_ref[DT_COL:DT_COL + DT_W, :].T.astype(BF16)

    for r0 in range(0, x_ref.shape[0], IN_ROWS):
        rows = slice(r0, r0 + IN_ROWS)
        x = x_ref[rows, :]
        ms = jnp.mean(x * x, axis=-1, keepdims=True)
        h = (x * lax.rsqrt(ms + RMS_EPS) * nw_ref[...]).astype(BF16)
        for n0 in range(0, PROJ_W, IN_CHUNK):
            y = _dot(h, wb_ref[:, n0:n0 + IN_CHUNK])
            if n0 < COL_Z:
                y = jax.nn.gelu(y, approximate=True)
            elif n0 < COL_XBC:
                y = y * _sigmoid(y)
            proj_ref[rows, n0:n0 + IN_CHUNK] = y.astype(BF16)
        dt_ref[rows, :] = _softplus(_dot(h, wb_ref[:, PROJ_W:PROJ_W + DT_W]) + dtb_ref[...])


def _per_layer(shape, layer):
    return pl.BlockSpec((None,) + tuple(shape), lambda *_: (layer,) + (0,) * len(shape))


def _inproj(x2, nw, w_in, layer, dtb, tm):
    m = x2.shape[0]
    return pl.pallas_call(
        _inproj_kernel,
        grid=(m // tm,),
        in_specs=[
            pl.BlockSpec((tm, D_MODEL), lambda i: (i, 0)),
            _per_layer((1, D_MODEL), layer),
            pl.BlockSpec((None, D_IN, D_MODEL), lambda i: (layer, 0, 0), pipeline_mode=pl.Buffered(1)),
            _per_layer((1, DT_W), layer),
        ],
        out_specs=[
            pl.BlockSpec((tm, PROJ_W), lambda i: (i, 0)),
            pl.BlockSpec((tm, DT_W), lambda i: (i, 0)),
        ],
        out_shape=[
            jax.ShapeDtypeStruct((m, PROJ_W), BF16),
            jax.ShapeDtypeStruct((m, DT_W), F32),
        ],
        scratch_shapes=[pltpu.VMEM((D_MODEL, PROJ_W + DT_W), BF16)],
        compiler_params=_cparams(("arbitrary",)),
        name="inproj",
    )(x2, nw, w_in, dtb)


def _head_id(shape, width):
    lane = lax.broadcasted_iota(jnp.int32, shape, 1)
    return lax.shift_right_logical(lane, int(math.log2(width)))


def _gmlp_kernel(pa_ref, ws_ref, bsm_ref, nw_ref, g_ref, out_ref):
    t = pa_ref.shape[0]
    hact = pa_ref[...].astype(F32)
    u = hact[:, :GROUP_W]
    v = hact[:, GROUP_W:]
    ms = _dot((v * v).astype(BF16), g_ref[...])
    vn = (v * lax.rsqrt(ms + RMS_EPS) * nw_ref[...]).astype(BF16)
    row = lax.broadcasted_iota(jnp.int32, (CHUNK, CHUNK), 0)
    col = lax.broadcasted_iota(jnp.int32, (CHUNK, CHUNK), 1)
    wcat = jnp.concatenate(
        [jnp.where(row >= col, ws_ref[h], 0.0) for h in range(N_HEADS)], axis=1).astype(BF16)
    hid = _head_id((CHUNK, GROUP_W), HEAD_DIM)
    for c in range(t // CHUNK):
        vc = vn[c * CHUNK:(c + 1) * CHUNK]
        vstack = jnp.concatenate(
            [jnp.where(hid == h, vc, jnp.zeros_like(vc)) for h in range(N_HEADS)], axis=0)
        s = _dot(wcat, vstack) + bsm_ref[...]
        out_ref[c * CHUNK:(c + 1) * CHUNK, :] = (u[c * CHUNK:(c + 1) * CHUNK] * s).astype(BF16)


def _split3(a):
    a1 = a.astype(BF16)
    r1 = a - a1.astype(F32)
    a2 = r1.astype(BF16)
    r2 = r1 - a2.astype(F32)
    return a1, a2, r2.astype(BF16)


def _ssd_kernel(z_ref, xbc_ref, halo_ref, dt_ref, cw_ref, cb_ref, shift_ref, alog_ref, dexp_ref, nw_ref,
                g_ref, out_ref, st_ref, between_phases=None):
    c = pl.program_id(1)

    @pl.when(c == 0)
    def _():
        st_ref[...] = jnp.zeros_like(st_ref)

    row = lax.broadcasted_iota(jnp.int32, (CHUNK, CHUNK), 0)
    col = lax.broadcasted_iota(jnp.int32, (CHUNK, CHUNK), 1)
    causal = row >= col
    ltri = jnp.where(causal, 1.0, 0.0).astype(BF16)
    hid = _head_id((CHUNK, GROUP_W), HEAD_DIM)
    neg_a = -jnp.exp(alog_ref[...])

    def expand(c4):
        return jnp.where(hid == 0, c4[:, 0:1],
                         jnp.where(hid == 1, c4[:, 1:2],
                                   jnp.where(hid == 2, c4[:, 2:3], c4[:, 3:4])))

    chunks = range(SSD_CPS)
    rows = [slice(ck * CHUNK, (ck + 1) * CHUNK) for ck in chunks]
    groups = [slice(g * SSM_D_STATE, (g + 1) * SSM_D_STATE) for g in range(SSM_GROUPS)]

    xs, bm, cm = [], [], []
    halo = jnp.where(c > 0, halo_ref[...], jnp.zeros_like(halo_ref))
    for ck in chunks:
        xcur = xbc_ref[rows[ck], :]
        acc = cb_ref[...] + cw_ref[SSM_CONV_K - 1:SSM_CONV_K, :] * xcur.astype(F32)
        for j in range(1, SSM_CONV_K):
            k = SSM_CONV_K - 1 - j
            if ck == 0:
                shifted = (_dot(shift_ref[j - 1, :, 0:CHUNK], halo)
                           + _dot(shift_ref[j - 1, :, CHUNK:2 * CHUNK], xcur))
            else:
                shifted = _dot(shift_ref[j - 1], xbc_ref[(ck - 1) * CHUNK:(ck + 1) * CHUNK, :])
            acc = acc + cw_ref[k:k + 1, :] * shifted
        xc = acc * _sigmoid(acc)
        xs.append(xc[:, :GROUP_W])
        bm.append(xc[:, GROUP_W:2 * GROUP_W])
        cm.append(xc[:, 2 * GROUP_W:].astype(BF16))

    acs, acs_t = [], []
    for ck in chunks:
        a1, a2, a3 = _split3(dt_ref[rows[ck], :] * neg_a)
        acs.append(_dot(ltri, a1) + _dot(ltri, a2) + _dot(ltri, a3))
        acs_t.append(acs[ck].T)

    if between_phases is not None:
        between_phases()

    xdt, eacs_e, mcat, snew = [], [], [], []
    for ck in chunks:
        xdt.append(xs[ck] * expand(dt_ref[rows[ck], :]))
        eacs_e.append(expand(jnp.exp(acs[ck])))
        dte_e = expand(jnp.exp(acs[ck][CHUNK - 1:CHUNK, :] - acs[ck]))
        ms = []
        for g in range(SSM_GROUPS):
            cb = _dot_nt(cm[ck][:, groups[g]], bm[ck][:, groups[g]].astype(BF16))
            for hh in range(N_HEADS // SSM_GROUPS):
                h = g * (N_HEADS // SSM_GROUPS) + hh
                seg = acs[ck][:, h:h + 1] - acs_t[ck][h:h + 1, :]
                ms.append((cb * jnp.exp(jnp.where(causal, seg, NEG_BIG))).astype(BF16))
        mcat.append(jnp.concatenate(ms, axis=1))
        w = (xdt[ck] * dte_e).astype(BF16)
        snew.append([_dot(bm[ck][:, groups[g]].T.astype(BF16), w[:, groups[g]])
                     for g in range(SSM_GROUPS)])

    state = [st_ref[g] for g in range(SSM_GROUPS)]
    entering = []
    for ck in chunks:
        entering.append([st.astype(BF16) for st in state])
        cdl = eacs_e[ck][CHUNK - 1:CHUNK, :]
        state = [state[g] * cdl[:, groups[g]] + snew[ck][g] for g in range(SSM_GROUPS)]
    for g in range(SSM_GROUPS):
        st_ref[g] = state[g]

    ys = []
    for ck in chunks:
        xdt_b = xdt[ck].astype(BF16)
        xstack = jnp.concatenate(
            [jnp.where(hid == h, xdt_b, jnp.zeros_like(xdt_b)) for h in range(N_HEADS)], axis=0)
        yoff = jnp.concatenate([_dot(cm[ck][:, groups[g]], entering[ck][g]) for g in range(SSM_GROUPS)], axis=1)
        y = _dot(mcat[ck], xstack) + yoff * eacs_e[ck] + dexp_ref[...] * xs[ck]
        ys.append(y * z_ref[rows[ck], :].astype(F32))

    for ck in chunks:
        msq = _dot((ys[ck] * ys[ck]).astype(BF16), g_ref[...])
        out_ref[rows[ck], :] = (ys[ck] * lax.rsqrt(msq + RMS_EPS) * nw_ref[...]).astype(BF16)


def _conv_shift_matrices():
    r = jnp.arange(CHUNK)[:, None]
    col = jnp.arange(2 * CHUNK)[None, :]
    return jnp.stack([(col == CHUNK + r - j) for j in range(1, SSM_CONV_K)]).astype(BF16)


def _pool_bands():
    r = jnp.arange(CHUNK)[:, None]
    col = jnp.arange(2 * CHUNK)[None, :]
    back = CHUNK + r - col
    bands = [jnp.where((back >= 0) & (back < w), 1.0 / w, 0.0) for w in POOL_WINDOWS]
    bandcat = jnp.concatenate(bands, axis=1).astype(BF16)
    t = jnp.arange(CHUNK, dtype=F32)[:, None] + 1.0
    win = jnp.repeat(jnp.asarray(POOL_WINDOWS, F32), HEAD_DIM)[None, :]
    fix = win / jnp.minimum(win, t)
    return bandcat, fix


def _pool_chunks(pc_ref, pch_ref, band_ref, fix_ref, wbd_ref, scale_ref, out_ref, first_step):
    hid = _head_id((CHUNK, GROUP_W), HEAD_DIM)
    prev = jnp.where(first_step, jnp.zeros_like(pch_ref), pch_ref[...])
    for ck in range(pc_ref.shape[0] // CHUNK):
        cur = pc_ref[ck * CHUNK:(ck + 1) * CHUNK, :]
        parts = []
        for g in range(len(POOL_WINDOWS)):
            parts += [jnp.where(hid == g, prev, jnp.zeros_like(prev)),
                      jnp.where(hid == g, cur, jnp.zeros_like(cur))]
        mean = _dot(band_ref[...], jnp.concatenate(parts, axis=0))
        if ck == 0:
            mean = jnp.where(first_step, mean * fix_ref[...], mean)
        p = mean - cur.astype(F32)
        y = _dot(p.astype(BF16), wbd_ref[...]) * scale_ref[...]
        out_ref[ck * CHUNK:(ck + 1) * CHUNK, :] = y.astype(BF16)
        prev = cur


N_GMLP_IN, N_POOL_IN, N_SSD_IN = 5, 6, 11


def _mixers_kernel(*refs):
    gm = refs[:N_GMLP_IN]
    po = refs[N_GMLP_IN:N_GMLP_IN + N_POOL_IN]
    sd = refs[N_GMLP_IN + N_POOL_IN:N_GMLP_IN + N_POOL_IN + N_SSD_IN]
    ya_ref, yc_ref, yb_ref, st_ref = refs[N_GMLP_IN + N_POOL_IN + N_SSD_IN:]
    first_step = pl.program_id(1) == 0

    def others():
        _gmlp_kernel(*gm, ya_ref)
        _pool_chunks(*po, yc_ref, first_step)

    _ssd_kernel(*sd, yb_ref, st_ref, between_phases=others)


def _mixers(proj, dt, gws, bsm, gmn, g64, band, fix, wbd, psc, cw, cb, shift, alog, dexp, nw, g128,
            layer, b, seq):
    m = proj.shape[0]
    ts = SSD_CPS * CHUNK
    assert seq % ts == 0
    nc = seq // ts
    const = lambda i, c: (0, 0)
    blk = lambda width, col: pl.BlockSpec((ts, width), lambda i, c: (i * nc + c, col // width))
    halo = lambda width, col: pl.BlockSpec(
        (CHUNK, width), lambda i, c: (jnp.maximum((i * nc + c) * SSD_CPS - 1, 0), col // width))
    out = pl.BlockSpec((ts, GROUP_W), lambda i, c: (i * nc + c, 0))
    gmlp_in = [blk(2 * GROUP_W, COL_PA), _per_layer((N_HEADS, CHUNK, CHUNK), layer),
               _per_layer((CHUNK, GROUP_W), layer), _per_layer((1, GROUP_W), layer),
               pl.BlockSpec((GROUP_W, GROUP_W), const)]
    pool_in = [blk(GROUP_W, COL_PC), halo(GROUP_W, COL_PC),
               pl.BlockSpec((CHUNK, len(POOL_WINDOWS) * 2 * CHUNK), const), pl.BlockSpec((CHUNK, GROUP_W), const),
               _per_layer((GROUP_W, GROUP_W), layer), _per_layer((1, GROUP_W), layer)]
    ssd_in = [blk(GROUP_W, COL_Z), blk(SSM_CONV_DIM, COL_XBC), halo(SSM_CONV_DIM, COL_XBC),
              pl.BlockSpec((ts, DT_W), lambda i, c: (i * nc + c, 0)),
              _per_layer((SSM_CONV_K, SSM_CONV_DIM), layer), _per_layer((1, SSM_CONV_DIM), layer),
              pl.BlockSpec((SSM_CONV_K - 1, CHUNK, 2 * CHUNK), lambda i, c: (0, 0, 0)),
              _per_layer((1, DT_W), layer), _per_layer((1, GROUP_W), layer), _per_layer((1, GROUP_W), layer),
              pl.BlockSpec((GROUP_W, GROUP_W), const)]
    assert (len(gmlp_in), len(pool_in), len(ssd_in)) == (N_GMLP_IN, N_POOL_IN, N_SSD_IN)
    return pl.pallas_call(
        _mixers_kernel,
        grid=(b, nc),
        in_specs=gmlp_in + pool_in + ssd_in,
        out_specs=[out, out, out],
        out_shape=[jax.ShapeDtypeStruct((m, GROUP_W), BF16)] * 3,
        scratch_shapes=[pltpu.VMEM((SSM_GROUPS, SSM_D_STATE, LANES), F32)],
        compiler_params=_cparams(("parallel", "arbitrary")),
        name="mixers",
    )(proj, gws, bsm, gmn, g64,
      proj, proj, band, fix, wbd, psc,
      proj, proj, proj, dt, cw, cb, shift, alog, dexp, nw, g128)


def _alibi_slope(h):
    return 2.0 ** (-8.0 * (h + 1) / N_HEADS)


def _head_slot(x, h):
    base = x[:, (h // 2) * LANES:(h // 2 + 1) * LANES]
    return pltpu.roll(base, HEAD_DIM, 1) if h % 2 else base


ATT_PAIR = 2


def _attn_kernel(q_ref, k_ref, v_ref, qnw_ref, knw_ref, lq1_ref, lk1_ref, lq2_ref, lk2_ref, subw_ref,
                 g_ref, out_ref, ka_ref, vt_ref, qs_ref, acc_ref, s0_ref, kmax_ref, qa_ref, *, lam_init):
    pj = pl.program_id(1)
    tq, tk = ATT_TQ, ATT_TK
    seq = k_ref.shape[0]

    def aug_lanes(pos, slope, key_side):
        lane = lax.broadcasted_iota(jnp.int32, pos.shape, 1) - ATT_LANE_ALIBI
        hi = (slope * LANES) * lax.shift_right_logical(pos, int(math.log2(LANES))).astype(F32)
        lo = slope * (pos & (LANES - 1)).astype(F32)
        one = jnp.ones_like(hi)
        c = (one, one, hi, lo) if key_side else (-hi, -lo, one, one)
        last = jnp.where(lane == ATT_LANE_BOUND - ATT_LANE_ALIBI, 1.0, 0.0) if key_side else 0.0
        return jnp.where(lane == 0, c[0], jnp.where(lane == 1, c[1],
                         jnp.where(lane == 2, c[2], jnp.where(lane == 3, c[3], last))))

    @pl.when(pj == 0)
    def _():
        k = k_ref[...].astype(F32)
        ms = _dot((k * k).astype(BF16), g_ref[...])
        kn = k * lax.rsqrt(ms + RMS_EPS) * knw_ref[...]
        ksq = _dot((kn * kn).astype(BF16), g_ref[...]) * DA_QK_DIM
        kmax_ref[...] = jnp.broadcast_to(jnp.max(ksq, axis=0, keepdims=True), kmax_ref.shape)
        pos = lax.broadcasted_iota(jnp.int32, (seq, LANES), 0)
        lane = lax.broadcasted_iota(jnp.int32, (seq, LANES), 1)
        for h in range(N_HEADS):
            aug = aug_lanes(pos, _alibi_slope(h), True)
            ka_ref[h] = jnp.where(lane < HEAD_DIM, _head_slot(kn, h), aug).astype(BF16)
            qa_ref[h] = aug_lanes(pos, _alibi_slope(h), False).astype(BF16)
        tail = (lax.broadcasted_iota(jnp.int32, (VT_ROWS - HEAD_DIM, tk), 0) == 0).astype(BF16)
        for j in range(seq // tk):
            vt = v_ref[j * tk:(j + 1) * tk, :].astype(F32).T.astype(BF16)
            for h in range(N_HEADS):
                vt_ref[j, h, 0:HEAD_DIM, :] = vt[h * HEAD_DIM:(h + 1) * HEAD_DIM, :]
                vt_ref[j, h, HEAD_DIM:VT_ROWS, :] = tail

    lane = lax.broadcasted_iota(jnp.int32, (1, LANES), 1)

    def normalised_q(t):
        q = q_ref[t * tq:(t + 1) * tq, :].astype(F32)
        ms = _dot((q * q).astype(BF16), g_ref[...])
        qn = q * lax.rsqrt(ms + RMS_EPS) * (qnw_ref[...] * (DA_QK_DIM ** -0.5))
        qsq = _dot((qn * qn).astype(BF16), g_ref[...]) * DA_QK_DIM
        b2 = qsq * kmax_ref[0:1, :]
        bound = b2 * lax.rsqrt(b2 + TINY) * ATT_BOUND_MARGIN
        return qn, bound, jnp.max(bound) <= ATT_ONE_PASS_BOUND

    def build_queries(t, qi, qn, bound):
        q0 = pl.multiple_of(qi * tq, tq)
        for h in range(N_HEADS):
            base = _head_slot(qn, h)
            bnd = _head_slot(bound, h)
            qa = qa_ref[h, pl.ds(q0, tq), :]
            for comp in range(2):
                own = (lane >= comp * DA_QK_DIM) & (lane < (comp + 1) * DA_QK_DIM)
                y = jnp.where(own, base, 0.0)
                y = jnp.where(lane == ATT_LANE_BOUND, -bnd[:, comp * DA_QK_DIM:comp * DA_QK_DIM + 1], y)
                qs_ref[t, h, comp * tq:(comp + 1) * tq, :] = y.astype(BF16) + qa
                yield

    qb = ATT_QB
    items = [(h, n) for h in range(N_HEADS) for n in range(2 * tq // qb)]

    def scores(t, item, j):
        h, n = item
        k0 = pl.multiple_of(j * tk, tk)
        return _dot_nt(ka_ref[h, pl.ds(k0, tk), :], qs_ref[t, h, n * qb:(n + 1) * qb, :])

    def causal_mask(s, n):
        kk = lax.broadcasted_iota(jnp.int32, (tk, qb), 0)
        qq = (lax.broadcasted_iota(jnp.int32, (tk, qb), 1) + n * qb) & (tq - 1)
        return jnp.where(kk <= qq, s, NEG_BIG)

    def tile(t, j, ms_, masked):
        new_m = []
        s_next = s0_ref[...]
        for i, (h, n) in enumerate(items):
            cols = slice(n * qb, (n + 1) * qb)
            s = s_next
            if i + 1 < len(items):
                s_next = scores(t, items[i + 1], j)
            elif not masked:
                s0_ref[...] = scores(t, items[0], j + 1)
            if masked:
                s = causal_mask(s, n)
            m_old = ms_[i]
            m_new = jnp.maximum(m_old, jnp.max(s, axis=0, keepdims=True))
            alpha = jnp.exp(m_old - m_new)
            p = jnp.exp(s - m_new)
            new_m.append(m_new)
            acc_ref[t, h, :, cols] = acc_ref[t, h, :, cols] * alpha + _dot(vt_ref[j, h], p.astype(BF16))
        return tuple(new_m)

    def tile_one_pass(t, j, masked, filler=None):
        s_next = s0_ref[...]
        for i, (h, n) in enumerate(items):
            cols = slice(n * qb, (n + 1) * qb)
            s = s_next
            if i + 1 < len(items):
                s_next = scores(t, items[i + 1], j)
            elif not masked:
                s0_ref[...] = scores(t, items[0], j + 1)
            if masked:
                s = causal_mask(s, n)
            acc_ref[t, h, :, cols] += _dot(vt_ref[j, h], jnp.exp(s).astype(BF16))
            if filler is not None:
                next(filler, None)

    def attend(t, qi, one_pass, other_work):
        acc_ref[t] = jnp.zeros(acc_ref.shape[1:], F32)
        s0_ref[...] = scores(t, items[0], 0)

        @pl.when(one_pass)
        def _():
            def body(j, carry):
                tile_one_pass(t, j, False)
                return carry
            lax.fori_loop(0, qi, body, 0)
            filler = other_work()
            tile_one_pass(t, qi, True, filler)
            for _ in filler:
                pass

        @pl.when(jnp.logical_not(one_pass))
        def _():
            m0 = tuple(jnp.full((1, qb), NEG_BIG, F32) for _ in items)
            m1 = lax.fori_loop(0, qi, lambda j, c: tile(t, j, c, False), m0)
            tile(t, qi, m1, True)
            for _ in other_work():
                pass

    def lane_sum(x):
        return jnp.broadcast_to(jnp.sum(x, axis=1, keepdims=True), x.shape)

    lam = (jnp.exp(lane_sum(lq1_ref[...] * lk1_ref[...]))
           - jnp.exp(lane_sum(lq2_ref[...] * lk2_ref[...])) + lam_init)
    lam = jnp.concatenate([lam] * (tq // LANES), axis=1)

    def finish(t):
        outs = []
        for h in range(N_HEADS):
            o = acc_ref[t, h, 0:HEAD_DIM, :] / acc_ref[t, h, HEAD_DIM:HEAD_DIM + 1, :]
            oh = o[:, 0:tq] - lam * o[:, tq:2 * tq]
            ms = jnp.mean(oh * oh, axis=0, keepdims=True)
            outs.append(oh * lax.rsqrt(ms + RMS_EPS) * (subw_ref[...] * (1.0 - lam_init)))
            yield
        out_ref[t * tq:(t + 1) * tq, :] = jnp.concatenate(outs, axis=0).T.astype(BF16)
        yield

    def chained(makers):
        def run():
            for make in makers:
                yield from make()
        return run

    prepared = [normalised_q(t) for t in range(ATT_PAIR)]
    for _ in build_queries(0, pj * ATT_PAIR, *prepared[0][:2]):
        pass
    for t in range(ATT_PAIR):
        qi = pj * ATT_PAIR + t
        work = []
        if t + 1 < ATT_PAIR:
            work.append(functools.partial(build_queries, t + 1, qi + 1, *prepared[t + 1][:2]))
        if t > 0:
            work.append(functools.partial(finish, t - 1))
        attend(t, qi, prepared[t][2], chained(work))
    for _ in finish(ATT_PAIR - 1):
        pass


def _attn(proj, qnw, knw, lq1, lk1, lq2, lk2, subw, g32, layer, b, seq, lam_init):
    m = proj.shape[0]
    assert ATT_TQ == ATT_TK and seq % (ATT_PAIR * ATT_TQ) == 0
    tp = ATT_PAIR * ATT_TQ
    nq = seq // tp
    const = lambda i, j: (0, 0)
    return pl.pallas_call(
        functools.partial(_attn_kernel, lam_init=lam_init),
        grid=(b, nq),
        in_specs=[
            pl.BlockSpec((tp, GROUP_W), lambda i, j: (i * nq + j, COL_Q // GROUP_W)),
            pl.BlockSpec((seq, GROUP_W), lambda i, j: (i, COL_K // GROUP_W)),
            pl.BlockSpec((seq, GROUP_W), lambda i, j: (i, COL_V // GROUP_W)),
            _per_layer((1, GROUP_W), layer),
            _per_layer((1, GROUP_W), layer),
            _per_layer((1, LANES), layer),
            _per_layer((1, LANES), layer),
            _per_layer((1, LANES), layer),
            _per_layer((1, LANES), layer),
            _per_layer((HEAD_DIM, ATT_TQ), layer),
            pl.BlockSpec((GROUP_W, GROUP_W), const),
        ],
        out_specs=pl.BlockSpec((tp, GROUP_W), lambda i, j: (i * nq + j, 0)),
        out_shape=jax.ShapeDtypeStruct((m, GROUP_W), BF16),
        scratch_shapes=[
            pltpu.VMEM((N_HEADS, seq, LANES), BF16),
            pltpu.VMEM((seq // ATT_TK, N_HEADS, VT_ROWS, ATT_TK), BF16),
            pltpu.VMEM((ATT_PAIR, N_HEADS, 2 * ATT_TQ, LANES), BF16),
            pltpu.VMEM((ATT_PAIR, N_HEADS, VT_ROWS, 2 * ATT_TQ), F32),
            pltpu.VMEM((ATT_TK, ATT_QB), F32),
            pltpu.VMEM((8, GROUP_W), F32),
            pltpu.VMEM((N_HEADS, seq, LANES), BF16),
        ],
        compiler_params=_cparams(("parallel", "arbitrary")),
        name="diffattn",
    )(proj, proj, proj, qnw, knw, lq1, lk1, lq2, lk2, subw, g32)


FFN_STAGE_WIDE = 256
FFN_STAGE_TALL = 704


def _ffn_weight_jobs(layer, wo_hbm, wg_hbm, wu_hbm, wd_hbm, wo_b, wg_b, wu_b, wd_b, wide, tall, sem_w, sem_t):
    jobs = []

    def add(src, dst, stage, sem, rows, n_rows):
        for k, r0 in enumerate(range(0, n_rows, rows)):
            r = min(rows, n_rows - r0)
            slot = k % 2
            view = stage.at[slot, 0:r, :]
            copy = pltpu.make_async_copy(src.at[layer, r0:r0 + r, :], view, sem.at[slot])
            jobs.append((copy, view, dst.at[r0:r0 + r, :]))

    add(wg_hbm, wg_b, wide, sem_w, FFN_STAGE_WIDE, D_MODEL)
    add(wu_hbm, wu_b, wide, sem_w, FFN_STAGE_WIDE, D_MODEL)
    add(wd_hbm, wd_b, tall, sem_t, FFN_STAGE_TALL, D_FF)
    add(wo_hbm, wo_b, tall, sem_t, FFN_STAGE_TALL, D_MODEL)
    return jobs


def _ffn_kernel(x_ref, ya_ref, yb_ref, yc_ref, yd_ref, nw_ref, wo_hbm, wg_hbm, wu_hbm, wd_hbm,
                out_ref, act_ref, wo_ref, wg_ref, wu_ref, wd_ref, wide_ref, tall_ref, sem_w, sem_t, *, layer):
    @pl.when(pl.program_id(0) == 0)
    def _():
        jobs = _ffn_weight_jobs(layer, wo_hbm, wg_hbm, wu_hbm, wd_hbm, wo_ref, wg_ref, wu_ref, wd_ref,
                                wide_ref, tall_ref, sem_w, sem_t)
        jobs[0][0].start()
        for k, (copy, view, dst) in enumerate(jobs):
            if k + 1 < len(jobs):
                jobs[k + 1][0].start()
            copy.wait()
            dst[...] = view[...].astype(BF16)

    x1 = x_ref[...]
    for i, y_ref in enumerate((ya_ref, yb_ref, yc_ref, yd_ref)):
        x1 = x1 + _dot(y_ref[...], wo_ref[i * GROUP_W:(i + 1) * GROUP_W, :])
    ms = jnp.mean(x1 * x1, axis=-1, keepdims=True)
    h = (x1 * lax.rsqrt(ms + RMS_EPS) * nw_ref[...]).astype(BF16)
    for c0 in range(0, D_FF, FFN_CHUNK):
        c1 = min(c0 + FFN_CHUNK, D_FF)
        g = _dot(h, wg_ref[:, c0:c1])
        u = _dot(h, wu_ref[:, c0:c1])
        act_ref[:, c0:c1] = (g * _sigmoid(g) * u).astype(BF16)
    out_ref[...] = x1 + _dot(act_ref[...], wd_ref[...])


def _ffn(x2, ya, yb, yc, yd, nw, wo, wg, wu, wd, layer, tm):
    m = x2.shape[0]
    row = lambda i: (i, 0)
    hbm = pl.BlockSpec(memory_space=pl.ANY)
    return pl.pallas_call(
        functools.partial(_ffn_kernel, layer=layer),
        grid=(m // tm,),
        in_specs=[
            pl.BlockSpec((tm, D_MODEL), row),
            pl.BlockSpec((tm, GROUP_W), row),
            pl.BlockSpec((tm, GROUP_W), row),
            pl.BlockSpec((tm, GROUP_W), row),
            pl.BlockSpec((tm, GROUP_W), row),
            _per_layer((1, D_MODEL), layer),
            hbm, hbm, hbm, hbm,
        ],
        out_specs=pl.BlockSpec((tm, D_MODEL), row),
        out_shape=jax.ShapeDtypeStruct((m, D_MODEL), F32),
        scratch_shapes=[
            pltpu.VMEM((tm, D_FF), BF16),
            pltpu.VMEM((D_MODEL, D_MODEL), BF16),
            pltpu.VMEM((D_MODEL, D_FF), BF16),
            pltpu.VMEM((D_MODEL, D_FF), BF16),
            pltpu.VMEM((D_FF, D_MODEL), BF16),
            pltpu.VMEM((2, FFN_STAGE_WIDE, D_FF), F32),
            pltpu.VMEM((2, FFN_STAGE_TALL, D_MODEL), F32),
            pltpu.SemaphoreType.DMA((2,)),
            pltpu.SemaphoreType.DMA((2,)),
        ],
        compiler_params=_cparams(("arbitrary",)),
        name="outproj_ffn",
    )(x2, ya, yb, yc, yd, nw, wo, wg, wu, wd)


def _block_diag_mean(width, group):
    idx = jnp.arange(width) // group
    return jnp.where(idx[:, None] == idx[None, :], 1.0 / group, 0.0).astype(BF16)


def _rows(v, width=None):
    v = v.reshape(v.shape[0], 1, -1).astype(F32)
    return v if width is None else jnp.pad(v, ((0, 0), (0, 0), (0, width - v.shape[2])))


def kernel(x, norm1_w, w_in, gm_norm_w, gm_ws, gm_bs, ssm_conv_w, ssm_conv_b, ssm_dt_bias, ssm_a_log, ssm_d, ssm_norm_w, pool_w, pool_scale, da_q_norm_w, da_k_norm_w, da_lambda_q1, da_lambda_k1, da_lambda_q2, da_lambda_k2, da_subln_w, w_out, norm2_w, ffn_w_gate, ffn_w_up, ffn_w_down):
    b, seq, d = x.shape
    depth = w_in.shape[0]
    assert d == D_MODEL and seq % ATT_TQ == 0 and seq % CHUNK == 0
    m = b * seq
    tm = 512 if m % 512 == 0 else ATT_TQ

    g64 = _block_diag_mean(GROUP_W, HEAD_DIM)
    g128 = _block_diag_mean(GROUP_W, SSM_D_STATE)
    g32 = _block_diag_mean(GROUP_W, DA_QK_DIM)

    shift = _conv_shift_matrices()
    band, fix = _pool_bands()

    assert w_in.shape[1:] == (D_MODEL, D_IN)

    n1, n2 = _rows(norm1_w), _rows(norm2_w)
    dtb, alog = _rows(ssm_dt_bias, DT_W), _rows(ssm_a_log, DT_W)
    bsm = jnp.repeat(jnp.swapaxes(gm_bs, 1, 2), HEAD_DIM, axis=2)
    gmn = _rows(gm_norm_w)
    cw, cb = jnp.swapaxes(ssm_conv_w, 1, 2), _rows(ssm_conv_b)
    dexp, ssn = _rows(jnp.repeat(ssm_d, HEAD_DIM, axis=1)), _rows(ssm_norm_w)
    eye = jnp.eye(len(POOL_WINDOWS), dtype=pool_w.dtype)
    wbd = jnp.einsum('lgab,gh->lgahb', pool_w, eye).reshape(depth, GROUP_W, GROUP_W).astype(BF16)
    psc = _rows(pool_scale)
    qnw = _rows(jnp.tile(da_q_norm_w, (1, GROUP_W // DA_QK_DIM)))
    knw = _rows(jnp.tile(da_k_norm_w, (1, GROUP_W // DA_QK_DIM)))
    lams = [_rows(v, LANES) for v in (da_lambda_q1, da_lambda_k1, da_lambda_q2, da_lambda_k2)]
    subw = jnp.broadcast_to(da_subln_w[:, :, None], (depth, HEAD_DIM, ATT_TQ))

    w_in_t = jnp.swapaxes(w_in, 1, 2)

    x2 = x.reshape(m, d)
    for i in range(depth):
        proj, dt = _inproj(x2, n1, w_in_t, i, dtb,2 * IN_ROWS if m % (2 * IN_ROWS) == 0 else IN_ROWS)
        ya, yc, yb = _mixers(proj, dt, gm_ws, bsm, gmn, g64, band, fix, wbd, psc,
                             cw, cb, shift, alog, dexp, ssn, g128, i, b, seq)
        lam_init = 0.8 - 0.6 * math.exp(-0.3 * i)
        yd = _attn(proj, qnw, knw, *lams, subw, g32, i, b, seq, lam_init)
        x2 = _ffn(x2, ya, yb, yc, yd, n2, w_out, ffn_w_gate, ffn_w_up, ffn_w_down, i, tm)
    return x2.reshape(b, seq, d)
```
